```python
import math
import jax, jax.numpy as jnp
from jax import lax
import numpy as np

D_MODEL = 1024
BATCH = 8
SEQ = 4096
DEPTH = 2

GRID_W = 64
CTX_LEN = 256
HEAD_DIM = 64
D_FF = 2816
N_MOD = 9
NORM_EPS = 1e-6
ROPE_THETA = 10000.0
Q_BLOCK = 128

HY_CH = 256
HY_EMB = 33
HY_ORDER = 64
HY_FAST_PCT = 0.3
HY_SLOW_PCT = 1.5
HY_TARGET = 1e-2

GQA_HEADS = 4
GQA_KV_HEADS = 2
GQA_GROUP = GQA_HEADS // GQA_KV_HEADS

MLA_HEADS = 4
MLA_NOPE = 64
MLA_ROPE = 32
MLA_QK = MLA_NOPE + MLA_ROPE
MLA_V = 64
MLA_Q_RANK = 256
MLA_KV_RANK = 128

RW_HEADS = 4
RW_N = 64
RW_C = RW_HEADS * RW_N
RW_DECAY_LORA = 64
RW_AAA_LORA = 64
RW_GATE_LORA = 128
RW_GN_EPS = 64e-5

HY_COLS = 3 * HY_CH
GQA_COLS = (GQA_HEADS + 2 * GQA_KV_HEADS) * HEAD_DIM
MLA_COLS = MLA_Q_RANK + MLA_KV_RANK + MLA_ROPE
RW_COLS = 3 * RW_C + 2 * RW_DECAY_LORA + 2 * RW_AAA_LORA + RW_GATE_LORA
D_IN = HY_COLS + GQA_COLS + MLA_COLS + RW_COLS
D_MIX = HY_CH + GQA_HEADS * HEAD_DIM + MLA_HEADS * MLA_V + RW_C
IN_SPLITS = [HY_COLS, HY_COLS + GQA_COLS, HY_COLS + GQA_COLS + MLA_COLS]
RW_SPLITS = [RW_C, 2 * RW_C, 3 * RW_C, 3 * RW_C + RW_DECAY_LORA, 3 * RW_C + 2 * RW_DECAY_LORA,
             3 * RW_C + 2 * RW_DECAY_LORA + RW_AAA_LORA, 3 * RW_C + 2 * RW_DECAY_LORA + 2 * RW_AAA_LORA]

kernel_name = 'hybrid_head_group_flow_block'


def rms_norm(x, g):
    xf = x.astype(jnp.float32)
    y = xf * lax.rsqrt(jnp.mean(xf * xf, axis=-1, keepdims=True) + NORM_EPS)
    return (y * g.astype(jnp.float32)).astype(x.dtype)


def adaln(x, g, mod, i):
    return rms_norm(x, g) * (1.0 + mod[:, 3 * i + 1][:, None]) + mod[:, 3 * i][:, None]


def res_gate(mod, i):
    return mod[:, 3 * i + 2][:, None]


def swiglu(h, w_gate, w_up, w_down):
    return (jax.nn.silu(h @ w_gate) * (h @ w_up)) @ w_down


def centred_conv3(u, w, b):
    up = jnp.pad(u, ((0, 0), (1, 1), (0, 0)))
    return up[:, :-2] * w[0] + up[:, 1:-1] * w[1] + up[:, 2:] * w[2] + b


def token_shift(u, mu):
    up = jnp.pad(u, ((0, 0), (1, 1), (0, 0)))
    return u + mu * (0.5 * (up[:, :-2] + up[:, 2:]) - u)


def rope_tables(rows, d_rot):
    row = jnp.repeat(jnp.arange(rows, dtype=jnp.float32), GRID_W)
    col = jnp.tile(jnp.arange(GRID_W, dtype=jnp.float32), rows)
    n_freq = d_rot // 4
    inv = ROPE_THETA ** (-jnp.arange(n_freq, dtype=jnp.float32) / n_freq)
    ang = jnp.concatenate([row[:, None] * inv, col[:, None] * inv], axis=-1)
    return jnp.cos(ang), jnp.sin(ang)


def apply_rope(x, cos, sin):
    shp = (1, x.shape[1]) + (1,) * (x.ndim - 3) + (cos.shape[-1],)
    cos = cos.reshape(shp)
    sin = sin.reshape(shp)
    xf = x.astype(jnp.float32).reshape(x.shape[:-1] + (-1, 2))
    x0, x1 = xf[..., 0], xf[..., 1]
    out = jnp.stack([x0 * cos - x1 * sin, x0 * sin + x1 * cos], axis=-1)
    return out.reshape(x.shape).astype(x.dtype)


def attention(q, k, v, scale):
    s = jnp.einsum('bqhgd,bkhd->bhgqk', q, k, preferred_element_type=jnp.float32) * scale
    p = jax.nn.softmax(s, axis=-1)
    return jnp.einsum('bhgqk,bkhd->bqhgd', p.astype(v.dtype), v)


def blocked_attention(q, k, v, scale):
    B, L = q.shape[:2]
    nb = L // Q_BLOCK
    qb = jnp.moveaxis(q.reshape((B, nb, Q_BLOCK) + q.shape[2:]), 1, 0)
    ob = lax.map(lambda qi: attention(qi, k, v, scale), qb)
    return jnp.moveaxis(ob, 0, 1).reshape((B, L) + ob.shape[3:])


def hyena_filters(L, w1, b1, w2, b2, w3, b3, w4, freq):
    t01 = jnp.linspace(0.0, 1.0, L, dtype=jnp.float32)[:, None]
    bands = (HY_EMB - 1) // 2
    w_ang = 2.0 * math.pi * jnp.arange(L, dtype=jnp.float32)[:, None] / L
    f = jnp.linspace(1e-4, bands - 1, bands, dtype=jnp.float32)[None]
    z = jnp.concatenate([t01, jnp.cos(f * w_ang), -jnp.sin(f * w_ang)], axis=-1)
    h = jnp.sin(freq * (z @ w1 + b1))
    h = jnp.sin(freq * (h @ w2 + b2))
    h = jnp.sin(freq * (h @ w3 + b3))
    h = (h @ w4).astype(jnp.float32)
    max_decay = math.log(HY_TARGET) / HY_FAST_PCT
    min_decay = math.log(HY_TARGET) / HY_SLOW_PCT
    deltas = jnp.tile(jnp.abs(jnp.linspace(min_decay, max_decay, HY_CH, dtype=jnp.float32)), 2)
    return h * jnp.exp(-t01 * deltas)


def fft_long_conv(u, h_fwd, h_bwd):
    L = u.shape[1]
    n = 2 * L
    hf = jnp.fft.rfft(h_fwd, n=n, axis=0)
    hb = jnp.fft.rfft(h_bwd, n=n, axis=0)
    y_f = jnp.fft.irfft(jnp.fft.rfft(u, n=n, axis=1) * hf, n=n, axis=1)[:, :L]
    y_b = jnp.fft.irfft(jnp.fft.rfft(u[:, ::-1], n=n, axis=1) * hb, n=n, axis=1)[:, :L][:, ::-1]
    return y_f + y_b


def hyena_mixer(p, conv_w, conv_b, f_w1, f_b1, f_w2, f_b2, f_w3, f_b3, f_w4, f_freq, bias):
    L = p.shape[1]
    x1, x2, v = jnp.split(centred_conv3(p, conv_w, conv_b), [HY_CH, 2 * HY_CH], axis=-1)
    h = hyena_filters(L, f_w1, f_b1, f_w2, f_b2, f_w3, f_b3, f_w4, f_freq)
    u = (x1 * v).astype(jnp.float32)
    y = fft_long_conv(u, h[:, :HY_CH], h[:, HY_CH:]) + u * bias
    return (x2 * y).astype(p.dtype)


def gqa_mixer(p_lat, p_ctx, q_norm, k_norm, cos, sin, ctx_out):
    def heads(p):
        B, L = p.shape[:2]
        q, k, v = jnp.split(p, [GQA_HEADS * HEAD_DIM, (GQA_HEADS + GQA_KV_HEADS) * HEAD_DIM], axis=-1)
        q = rms_norm(q.reshape(B, L, GQA_KV_HEADS, GQA_GROUP, HEAD_DIM), q_norm)
        k = rms_norm(k.reshape(B, L, GQA_KV_HEADS, HEAD_DIM), k_norm)
        return q, k, v.reshape(B, L, GQA_KV_HEADS, HEAD_DIM)
    B, L = p_lat.shape[:2]
    q_l, k_l, v_l = heads(p_lat)
    q_c, k_c, v_c = heads(p_ctx)
    q_l = apply_rope(q_l, cos, sin)
    k_l = apply_rope(k_l, cos, sin)
    scale = HEAD_DIM ** -0.5
    y_lat = blocked_attention(q_l, jnp.concatenate([k_c, k_l], axis=1),
                              jnp.concatenate([v_c, v_l], axis=1), scale).reshape(B, L, -1)
    y_ctx = None
    if ctx_out:
        y_ctx = attention(q_c, k_c, v_c, scale).reshape(B, p_ctx.shape[1], -1)
    return y_lat, y_ctx


def mla_mixer(p_lat, p_ctx, cq_norm, ckv_norm, w_uq, w_ukv, q_norm, k_norm, cos, sin, ctx_out):
    def heads(p):
        B, L = p.shape[:2]
        c_q, c_kv, k_rope = jnp.split(p, [MLA_Q_RANK, MLA_Q_RANK + MLA_KV_RANK], axis=-1)
        q = (rms_norm(c_q, cq_norm) @ w_uq).reshape(B, L, MLA_HEADS, 1, MLA_QK)
        kv = (rms_norm(c_kv, ckv_norm) @ w_ukv).reshape(B, L, MLA_HEADS, MLA_NOPE + MLA_V)
        k_nope, v = jnp.split(kv, [MLA_NOPE], axis=-1)
        k_rope = jnp.broadcast_to(k_rope[:, :, None, :], (B, L, MLA_HEADS, MLA_ROPE))
        k = jnp.concatenate([k_nope, k_rope], axis=-1)
        return rms_norm(q, q_norm), rms_norm(k, k_norm), v
    def rot(t):
        return jnp.concatenate([t[..., :MLA_NOPE], apply_rope(t[..., MLA_NOPE:], cos, sin)], axis=-1)
    B, L = p_lat.shape[:2]
    q_l, k_l, v_l = heads(p_lat)
    q_c, k_c, v_c = heads(p_ctx)
    q_l = rot(q_l)
    k_l = rot(k_l)
    scale = MLA_QK ** -0.5
    y_lat = blocked_attention(q_l, jnp.concatenate([k_c, k_l], axis=1),
                              jnp.concatenate([v_c, v_l], axis=1), scale).reshape(B, L, -1)
    y_ctx = None
    if ctx_out:
        y_ctx = attention(q_c, k_c, v_c, scale).reshape(B, p_ctx.shape[1], -1)
    return y_lat, y_ctx


def rwkv_prepare(p, mu, w0, w2, a0, a2, g2, k_k, k_a):
    B, L = p.shape[:2]
    r, k, v, xw_f, xw_b, xa_f, xa_b, xg = jnp.split(token_shift(p, mu), RW_SPLITS, axis=-1)
    def heads(t):
        return t.reshape(B, L, RW_HEADS, RW_N)
    kk = heads(k * k_k).astype(jnp.float32)
    kk = kk / jnp.maximum(jnp.sqrt(jnp.sum(kk * kk, axis=-1, keepdims=True)), 1e-12)
    dirs = []
    for d, (xw, xa) in enumerate(((xw_f, xa_f), (xw_b, xa_b))):
        w_log = -jax.nn.softplus(-(w0[d] + jnp.tanh(xw) @ w2[d])) - 0.5
        decay = jnp.exp(-jnp.exp(w_log.astype(jnp.float32)))
        a = jax.nn.sigmoid(a0[d] + xa @ a2[d])
        k_d = k * (1.0 + (a - 1.0) * k_a)
        dirs.append((heads(decay), heads(k_d), heads(a)))
    g = jax.nn.sigmoid(xg) @ g2
    return heads(r), heads(v), kk, dirs, g


def rwkv_scan(r, w, k, v, kk, a, s0, reverse):
    def step(S, inp):
        r_t, w_t, k_t, v_t, kk_t, a_t = inp
        sa = jnp.einsum('bhvk,bhk->bhv', S, -kk_t)
        S = S * w_t[:, :, None, :] + sa[..., None] * (kk_t * a_t)[:, :, None, :] + v_t[..., None] * k_t[:, :, None, :]
        return S, jnp.einsum('bhvk,bhk->bhv', S, r_t)
    xs = tuple(jnp.moveaxis(t.astype(jnp.float32), 1, 0) for t in (r, w, k, v, kk, a))
    s_fin, ys = lax.scan(step, s0, xs, reverse=reverse)
    return jnp.moveaxis(ys, 0, 1), s_fin


def rwkv_directions(prep, s_init):
    r, v, kk, dirs, g = prep
    ys, finals = [], []
    for d, (decay, k_d, a) in enumerate(dirs):
        y_d, s_d = rwkv_scan(r, decay, k_d, v, kk, a, s_init[d], reverse=(d == 1))
        ys.append(y_d)
        finals.append(s_d)
    return ys, finals


def rwkv_readout(prep, ys, r_k, ln_w, ln_b, dtype):
    r, v, kk, dirs, g = prep
    B, L = r.shape[:2]
    y = ys[0] + ys[1]
    mean = jnp.mean(y, axis=-1, keepdims=True)
    var = jnp.mean(jnp.square(y - mean), axis=-1, keepdims=True)
    y = ((y - mean) * lax.rsqrt(var + RW_GN_EPS)).reshape(B, L, RW_C) * ln_w + ln_b
    bonus = sum(jnp.sum(r * k_d * r_k, axis=-1, keepdims=True) * v for (_, k_d, _) in dirs)
    return ((y + bonus.reshape(B, L, RW_C)) * g).astype(dtype)


def rwkv_mixer(p_lat, p_ctx, mu, w0, w2, a0, a2, g2, k_k, k_a, r_k, ln_w, ln_b, ctx_out):
    prep_c = rwkv_prepare(p_ctx, mu, w0, w2, a0, a2, g2, k_k, k_a)
    prep_l = rwkv_prepare(p_lat, mu, w0, w2, a0, a2, g2, k_k, k_a)
    zeros = jnp.zeros((p_ctx.shape[0], RW_HEADS, RW_N, RW_N), jnp.float32)
    ys_c, s_ctx = rwkv_directions(prep_c, (zeros, zeros))
    ys_l, _ = rwkv_directions(prep_l, s_ctx)
    y_lat = rwkv_readout(prep_l, ys_l, r_k, ln_w, ln_b, p_lat.dtype)
    y_ctx = rwkv_readout(prep_c, ys_c, r_k, ln_w, ln_b, p_ctx.dtype) if ctx_out else None
    return y_lat, y_ctx


def setup_inputs(seed: int = 0) -> dict:
    key = jax.random.key(seed)
    ks = iter(jax.random.split(key, 64))
    def nrm(shape, scale):
        return jax.random.normal(next(ks), shape, jnp.float32) * scale
    def gain(shape):
        return 1.0 + nrm(shape, 0.02)
    L = DEPTH
    return {
        'x': nrm((BATCH, SEQ, D_MODEL), 1.0),
        'c': nrm((BATCH, D_MODEL), 1.0),
        'ctx': nrm((BATCH, CTX_LEN, D_MODEL), 1.0),
        'c_ctx': nrm((D_MODEL,), 1.0),
        'ada_w': nrm((L, D_MODEL, N_MOD * D_MODEL), 0.5 * D_MODEL ** -0.5),
        'ada_b': nrm((L, N_MOD * D_MODEL), 0.02),
        'norm_ffn1': gain((L, D_MODEL)),
        'norm_mix': gain((L, D_MODEL)),
        'norm_ffn2': gain((L, D_MODEL)),
        'ffn1_gate': nrm((L, D_MODEL, D_FF), D_MODEL ** -0.5),
        'ffn1_up': nrm((L, D_MODEL, D_FF), D_MODEL ** -0.5),
        'ffn1_down': nrm((L, D_FF, D_MODEL), D_FF ** -0.5),
        'ffn2_gate': nrm((L, D_MODEL, D_FF), D_MODEL ** -0.5),
        'ffn2_up': nrm((L, D_MODEL, D_FF), D_MODEL ** -0.5),
        'ffn2_down': nrm((L, D_FF, D_MODEL), D_FF ** -0.5),
        'w_in': nrm((L, D_MODEL, D_IN), D_MODEL ** -0.5),
        'w_out': nrm((L, D_MIX, D_MODEL), D_MIX ** -0.5),
        'hy_conv_w': nrm((L, 3, HY_COLS), 3 ** -0.5),
        'hy_conv_b': nrm((L, HY_COLS), 0.02),
        'hy_f_w1': nrm((L, HY_EMB, HY_ORDER), HY_EMB ** -0.5),
        'hy_f_b1': nrm((L, HY_ORDER), 0.02),
        'hy_f_w2': nrm((L, HY_ORDER, HY_ORDER), HY_ORDER ** -0.5),
        'hy_f_b2': nrm((L, HY_ORDER), 0.02),
        'hy_f_w3': nrm((L, HY_ORDER, HY_ORDER), HY_ORDER ** -0.5),
        'hy_f_b3': nrm((L, HY_ORDER), 0.02),
        'hy_f_w4': nrm((L, HY_ORDER, 2 * HY_CH), 0.02),
        'hy_f_freq': gain((L, HY_ORDER)),
        'hy_bias': nrm((L, HY_CH), 0.5),
        'gqa_q_norm': gain((L, HEAD_DIM)),
        'gqa_k_norm': gain((L, HEAD_DIM)),
        'mla_cq_norm': gain((L, MLA_Q_RANK)),
        'mla_ckv_norm': gain((L, MLA_KV_RANK)),
        'mla_w_uq': nrm((L, MLA_Q_RANK, MLA_HEADS * MLA_QK), MLA_Q_RANK ** -0.5),
        'mla_w_ukv': nrm((L, MLA_KV_RANK, MLA_HEADS * (MLA_NOPE + MLA_V)), MLA_KV_RANK ** -0.5),
        'mla_q_norm': gain((L, MLA_QK)),
        'mla_k_norm': gain((L, MLA_QK)),
        'rw_mu': jax.random.uniform(next(ks), (L, RW_COLS), jnp.float32),
        'rw_w0': jnp.broadcast_to(jnp.linspace(-6.0, -1.0, RW_C, dtype=jnp.float32), (L, 2, RW_C)) + nrm((L, 2, RW_C), 0.1),
        'rw_w2': nrm((L, 2, RW_DECAY_LORA, RW_C), 0.1 * RW_DECAY_LORA ** -0.5),
        'rw_a0': nrm((L, 2, RW_C), 0.1),
        'rw_a2': nrm((L, 2, RW_AAA_LORA, RW_C), RW_AAA_LORA ** -0.5),
        'rw_g2': nrm((L, RW_GATE_LORA, RW_C), RW_GATE_LORA ** -0.5),
        'rw_k_k': 0.85 + nrm((L, RW_C), 0.02),
        'rw_k_a': gain((L, RW_C)),
        'rw_r_k': nrm((L, RW_HEADS, RW_N), 0.1),
        'rw_ln_w': gain((L, RW_C)),
        'rw_ln_b': nrm((L, RW_C), 0.02),
    }


def reference(x, c, ctx, c_ctx, ada_w, ada_b, norm_ffn1, norm_mix, norm_ffn2,
              ffn1_gate, ffn1_up, ffn1_down, ffn2_gate, ffn2_up, ffn2_down, w_in, w_out,
              hy_conv_w, hy_conv_b, hy_f_w1, hy_f_b1, hy_f_w2, hy_f_b2, hy_f_w3, hy_f_b3, hy_f_w4,
              hy_f_freq, hy_bias, gqa_q_norm, gqa_k_norm, mla_cq_norm, mla_ckv_norm, mla_w_uq, mla_w_ukv,
              mla_q_norm, mla_k_norm, rw_mu, rw_w0, rw_w2, rw_a0, rw_a2, rw_g2, rw_k_k, rw_k_a, rw_r_k,
              rw_ln_w, rw_ln_b):
    B, L, D = x.shape
    rows = L // GRID_W
    cos_g, sin_g = rope_tables(rows, HEAD_DIM)
    cos_m, sin_m = rope_tables(rows, MLA_ROPE)
    silu_c = jax.nn.silu(c)
    silu_cc = jax.nn.silu(c_ctx)[None]
    for l in range(DEPTH):
        ctx_out = l < DEPTH - 1
        mod_x = (silu_c @ ada_w[l] + ada_b[l]).reshape(B, N_MOD, D)
        mod_c = (silu_cc @ ada_w[l] + ada_b[l]).reshape(1, N_MOD, D)

        x = x + 0.5 * res_gate(mod_x, 0) * swiglu(adaln(x, norm_ffn1[l], mod_x, 0), ffn1_gate[l], ffn1_up[l], ffn1_down[l])
        ctx = ctx + 0.5 * res_gate(mod_c, 0) * swiglu(adaln(ctx, norm_ffn1[l], mod_c, 0), ffn1_gate[l], ffn1_up[l], ffn1_down[l])

        p_x = adaln(x, norm_mix[l], mod_x, 1) @ w_in[l]
        p_c = adaln(ctx, norm_mix[l], mod_c, 1) @ w_in[l]
        hy_x, gq_x, ml_x, rw_x = jnp.split(p_x, IN_SPLITS, axis=-1)
        hy_c, gq_c, ml_c, rw_c = jnp.split(p_c, IN_SPLITS, axis=-1)
        hy_par = (hy_conv_w[l], hy_conv_b[l], hy_f_w1[l], hy_f_b1[l], hy_f_w2[l], hy_f_b2[l],
                  hy_f_w3[l], hy_f_b3[l], hy_f_w4[l], hy_f_freq[l], hy_bias[l])
        y_hy_x = hyena_mixer(hy_x, *hy_par)
        y_gq_x, y_gq_c = gqa_mixer(gq_x, gq_c, gqa_q_norm[l], gqa_k_norm[l], cos_g, sin_g, ctx_out)
        y_ml_x, y_ml_c = mla_mixer(ml_x, ml_c, mla_cq_norm[l], mla_ckv_norm[l], mla_w_uq[l], mla_w_ukv[l],
                                   mla_q_norm[l], mla_k_norm[l], cos_m, sin_m, ctx_out)
        y_rw_x, y_rw_c = rwkv_mixer(rw_x, rw_c, rw_mu[l], rw_w0[l], rw_w2[l], rw_a0[l], rw_a2[l], rw_g2[l],
                                    rw_k_k[l], rw_k_a[l], rw_r_k[l], rw_ln_w[l], rw_ln_b[l], ctx_out)
        y_x = jnp.concatenate([y_hy_x, y_gq_x, y_ml_x, y_rw_x], axis=-1) @ w_out[l]
        x = x + res_gate(mod_x, 1) * y_x
        if ctx_out:
            y_hy_c = hyena_mixer(hy_c, *hy_par)
            y_c = jnp.concatenate([y_hy_c, y_gq_c, y_ml_c, y_rw_c], axis=-1) @ w_out[l]
            ctx = ctx + res_gate(mod_c, 1) * y_c

        x = x + 0.5 * res_gate(mod_x, 2) * swiglu(adaln(x, norm_ffn2[l], mod_x, 2), ffn2_gate[l], ffn2_up[l], ffn2_down[l])
        if ctx_out:
            ctx = ctx + 0.5 * res_gate(mod_c, 2) * swiglu(adaln(ctx, norm_ffn2[l], mod_c, 2), ffn2_gate[l], ffn2_up[l], ffn2_down[l])
    return x
```

```python
import functools
import math

import numpy as np
import jax
import jax.numpy as jnp
from jax import lax
from jax.experimental import pallas as pl
from jax.experimental.pallas import tpu as pltpu

F32 = jnp.float32
BF16 = jnp.bfloat16
HI = lax.Precision.HIGHEST

D_MODEL = 1024
GRID_W = 64
HEAD_DIM = 64
D_FF = 2816
N_MOD = 9
NORM_EPS = 1e-6
ROPE_THETA = 10000.0

HY_CH = 256
HY_EMB = 33
HY_ORDER = 64
HY_FAST_PCT = 0.3
HY_SLOW_PCT = 1.5
HY_TARGET = 1e-2

GQA_HEADS = 4
GQA_KV_HEADS = 2
MLA_HEADS = 4
MLA_NOPE = 64
MLA_ROPE = 32
MLA_QK = MLA_NOPE + MLA_ROPE
MLA_V = 64
MLA_Q_RANK = 256
MLA_KV_RANK = 128

RW_HEADS = 4
RW_N = 64
RW_C = RW_HEADS * RW_N
RW_DECAY_LORA = 64
RW_AAA_LORA = 64
RW_GATE_LORA = 128
RW_GN_EPS = 64e-5

HY_COLS = 3 * HY_CH
GQA_COLS = (GQA_HEADS + 2 * GQA_KV_HEADS) * HEAD_DIM
MLA_COLS = MLA_Q_RANK + MLA_KV_RANK + MLA_ROPE
MLA_COLS_PAD = 512
RW_COLS = 3 * RW_C + 2 * RW_DECAY_LORA + 2 * RW_AAA_LORA + RW_GATE_LORA
D_IN_PAD = HY_COLS + GQA_COLS + MLA_COLS_PAD + RW_COLS

LANES = 128
ROW_TILE = 512
Q_TILE = 256
RW_CHUNK = 64
RW_BLOCK = 512
VMEM_LIMIT = 56 * 2 ** 20


def _cparams(sem):
    return pltpu.CompilerParams(dimension_semantics=sem, vmem_limit_bytes=VMEM_LIMIT)


def _const_spec(shape):
    nd = len(shape)
    return pl.BlockSpec(shape, lambda *_: (0,) * nd, pipeline_mode=pl.Buffered(1))


def _dot(a, b, precision=None):
    return jnp.dot(a, b, preferred_element_type=F32, precision=precision)


def _dot_nt(a, b, precision=None):
    return lax.dot_general(a, b, (((1,), (1,)), ((), ())), preferred_element_type=F32,
                           precision=precision)


def _dot_tn(a, b, precision=None):
    return lax.dot_general(a, b, (((0,), (0,)), ((), ())), preferred_element_type=F32,
                           precision=precision)


def _sigmoid(x):
    return 1.0 / (1.0 + jnp.exp(-x))


def _adaln(x, g, mod_ref, i):
    shift = mod_ref[pl.ds(3 * i, 1), :]
    scale = mod_ref[pl.ds(3 * i + 1, 1), :]
    r = lax.rsqrt(jnp.mean(x * x, axis=-1, keepdims=True) + NORM_EPS)
    return (x * r) * (g * (1.0 + scale)) + shift


def _mod_kernel(c_ref, w_ref, b_ref, o_ref):
    c = c_ref[...]
    s = c * _sigmoid(c)
    o_ref[...] = _dot(s, w_ref[...], HI) + b_ref[...]


def _compute_mod(c_all, ada_w, ada_b):
    depth, d, n = ada_w.shape
    rows = c_all.shape[0]
    tn = 1024
    return pl.pallas_call(
        _mod_kernel,
        grid=(depth, n // tn),
        in_specs=[
            pl.BlockSpec((rows, d), lambda l, j: (0, 0)),
            pl.BlockSpec((None, d, tn), lambda l, j: (l, 0, j)),
            pl.BlockSpec((None, 1, tn), lambda l, j: (l, 0, j)),
        ],
        out_specs=pl.BlockSpec((None, rows, tn), lambda l, j: (l, 0, j)),
        out_shape=jax.ShapeDtypeStruct((depth, rows, n), F32),
        compiler_params=_cparams(("arbitrary", "arbitrary")),
        name="mod",
    )(c_all, ada_w, ada_b.reshape(depth, 1, n))


def _ffn_kernel(x_ref, mod_ref, g_ref, wg_ref, wu_ref, wd_ref, o_ref, *, sub):
    x = x_ref[...]
    h = _adaln(x, g_ref[...], mod_ref, sub).astype(BF16)
    a = _dot(h, wg_ref[...])
    u = _dot(h, wu_ref[...])
    z = (a * _sigmoid(a) * u).astype(BF16)
    y = _dot(z, wd_ref[...])
    gate = mod_ref[pl.ds(3 * sub + 2, 1), :]
    o_ref[...] = x + (0.5 * gate) * y


def _ffn(x, mod, g, wg, wu, wd, sub):
    bm, rows, d = x.shape
    tm = min(ROW_TILE, rows)
    f = wg.shape[1]
    return pl.pallas_call(
        functools.partial(_ffn_kernel, sub=sub),
        grid=(bm, rows // tm),
        in_specs=[
            pl.BlockSpec((None, tm, d), lambda b, i: (b, i, 0)),
            pl.BlockSpec((None, N_MOD, d), lambda b, i: (b, 0, 0)),
            _const_spec((1, d)),
            _const_spec((d, f)),
            _const_spec((d, f)),
            _const_spec((f, d)),
        ],
        out_specs=pl.BlockSpec((None, tm, d), lambda b, i: (b, i, 0)),
        out_shape=jax.ShapeDtypeStruct(x.shape, F32),
        compiler_params=_cparams(("arbitrary", "arbitrary")),
        name="ffn",
    )(x, mod, g, wg, wu, wd)


_IN_OFFS = (0, HY_COLS, HY_COLS + GQA_COLS, HY_COLS + GQA_COLS + MLA_COLS_PAD, D_IN_PAD)


def _inproj_kernel(x_ref, mod_ref, g_ref, w_ref, hy_ref, gq_ref, ml_ref, rw_ref):
    h = _adaln(x_ref[...], g_ref[...], mod_ref, 1).astype(BF16)
    p = _dot(h, w_ref[...])
    for ref, lo, hi in zip((hy_ref, gq_ref, ml_ref, rw_ref), _IN_OFFS[:-1], _IN_OFFS[1:]):
        ref[...] = p[:, lo:hi]


def _inproj(x, mod, g, w):
    bm, rows, d = x.shape
    tm = min(ROW_TILE, rows)
    widths = [hi - lo for lo, hi in zip(_IN_OFFS[:-1], _IN_OFFS[1:])]
    return pl.pallas_call(
        _inproj_kernel,
        grid=(bm, rows // tm),
        in_specs=[
            pl.BlockSpec((None, tm, d), lambda b, i: (b, i, 0)),
            pl.BlockSpec((None, N_MOD, d), lambda b, i: (b, 0, 0)),
            _const_spec((1, d)),
            _const_spec((d, D_IN_PAD)),
        ],
        out_specs=[pl.BlockSpec((None, tm, wd), lambda b, i: (b, i, 0)) for wd in widths],
        out_shape=[jax.ShapeDtypeStruct((bm, rows, wd), F32) for wd in widths],
        compiler_params=_cparams(("arbitrary", "arbitrary")),
        name="inproj",
    )(x, mod, g, w)


def _outproj_kernel(x_ref, mod_ref, yh_ref, yg_ref, ym_ref, yr_ref, w_ref, o_ref):
    y = jnp.concatenate([yh_ref[...], yg_ref[...], ym_ref[...], yr_ref[...]], axis=-1)
    gate = mod_ref[pl.ds(5, 1), :]
    o_ref[...] = x_ref[...] + gate * _dot(y, w_ref[...])


def _outproj(x, mod, ys, w):
    bm, rows, d = x.shape
    tm = min(ROW_TILE, rows)
    row_spec = pl.BlockSpec((None, tm, d), lambda b, i: (b, i, 0))
    y_spec = pl.BlockSpec((None, tm, 256), lambda b, i: (b, i, 0))
    return pl.pallas_call(
        _outproj_kernel,
        grid=(bm, rows // tm),
        in_specs=[row_spec, pl.BlockSpec((None, N_MOD, d), lambda b, i: (b, 0, 0)),
                  y_spec, y_spec, y_spec, y_spec, _const_spec(w.shape)],
        out_specs=row_spec,
        out_shape=jax.ShapeDtypeStruct(x.shape, F32),
        compiler_params=_cparams(("arbitrary", "arbitrary")),
        name="outproj",
    )(x, mod, *ys, w)


def _hy_features(length):
    t01 = np.linspace(0.0, 1.0, length, dtype=np.float32)[:, None]
    bands = (HY_EMB - 1) // 2
    w_ang = (np.float32(2.0 * math.pi) * np.arange(length, dtype=np.float32)[:, None]
             / np.float32(length)).astype(np.float32)
    f = np.linspace(1e-4, bands - 1, bands, dtype=np.float32)[None]
    arg = (f * w_ang).astype(np.float32)
    z = np.concatenate([t01, np.cos(arg), -np.sin(arg)], axis=-1).astype(np.float32)
    zp = np.zeros((length, LANES), np.float32)
    zp[:, :HY_EMB] = z
    return zp


def _hy_deltas():
    max_decay = math.log(HY_TARGET) / HY_FAST_PCT
    min_decay = math.log(HY_TARGET) / HY_SLOW_PCT
    d = np.abs(np.linspace(min_decay, max_decay, HY_CH, dtype=np.float32))
    return np.tile(d, 2)[None].astype(np.float32)


def _hyfilt_kernel(z_ref, w1_ref, b1_ref, w2_ref, b2_ref, w3_ref, b3_ref, w4_ref, fr_ref,
                   dl_ref, o_ref):
    z = z_ref[...]
    fr = fr_ref[...]
    h = jnp.sin(fr * (_dot(z, w1_ref[...], HI) + b1_ref[...]))
    h = jnp.sin(fr * (_dot(h, w2_ref[...], HI) + b2_ref[...]))
    h = jnp.sin(fr * (_dot(h, w3_ref[...], HI) + b3_ref[...]))
    h = _dot(h, w4_ref[...], HI)
    o_ref[...] = h * jnp.exp(-z[:, 0:1] * dl_ref[...])


def _hy_filters(length, w1p, b1, w2, b2, w3, b3, w4, freq):
    z = jnp.asarray(_hy_features(length))
    dl = jnp.asarray(_hy_deltas())
    tl = min(ROW_TILE, length)
    consts = (w1p, b1, w2, b2, w3, b3, w4, freq, dl)
    return pl.pallas_call(
        _hyfilt_kernel,
        grid=(length // tl,),
        in_specs=[pl.BlockSpec((tl, LANES), lambda i: (i, 0))] + [_const_spec(a.shape) for a in consts],
        out_specs=pl.BlockSpec((tl, 2 * HY_CH), lambda i: (i, 0)),
        out_shape=jax.ShapeDtypeStruct((length, 2 * HY_CH), F32),
        compiler_params=_cparams(("arbitrary",)),
        name="hyena_filters",
    )(z, *consts)


def _fft_factors(length):
    return (128, 64) if length >= 2048 else (2 * length, 1)


@functools.lru_cache(maxsize=None)
def _fft_tables(length):
    n1, n2 = _fft_factors(length)
    n = n1 * n2
    assert n == 2 * length
    a_n2 = np.arange(n2)[:, None, None]
    a_k1 = np.arange(n1)[None, :, None]
    a_n1 = np.arange(n1 // 2)[None, None, :]
    ang = 2.0 * np.pi * ((a_n1 * a_k1 % n1) / n1 + (a_n2 * a_k1 % n) / n)
    t_re, t_im = np.cos(ang), -np.sin(ang)
    ta = np.concatenate([t_re, t_im], axis=1)
    tai = np.concatenate([np.transpose(t_re, (0, 2, 1)), np.transpose(t_im, (0, 2, 1))], axis=2) / n
    jj = np.arange(n2)
    ang2 = 2.0 * np.pi * (np.outer(jj, jj) % n2) / n2
    c, s = np.cos(ang2), np.sin(ang2)
    fb = np.block([[c, s], [-s, c]])
    fbi = np.block([[c, -s], [s, c]])
    return tuple(np.asarray(t, np.float32) for t in (ta, tai, fb, fbi))


def _fft_layout(n1, n2):
    return (n1, n2) if n2 > 1 else (1, n1)


def _fft_stage_a(src_ref, fbuf_ref, ta_ref, n1, n2):
    if n2 == 1:
        fbuf_ref[...] = _dot(ta_ref[0], src_ref[...], HI)
        return

    def body(j, carry):
        rows = src_ref[pl.ds(j, n1 // 2, stride=n2), :]
        t = _dot(ta_ref[j], rows, HI)
        fbuf_ref[pl.ds(j, n1, stride=2 * n2), :] = t[:n1]
        fbuf_ref[pl.ds(n2 + j, n1, stride=2 * n2), :] = t[n1:]
        return carry

    lax.fori_loop(0, n2, body, 0)


def _spec_kernel(hf_ref, hb_ref, ta_ref, fb_ref, o_ref, fbuf_ref, *, n1, n2):
    _fft_stage_a(hf_ref, o_ref, ta_ref, n1, n2)
    _fft_stage_a(hb_ref, fbuf_ref, ta_ref, n1, n2)
    nslab, hs = _fft_layout(n1, n2)
    sgn = jnp.where(lax.broadcasted_iota(jnp.int32, (2 * hs, 1), 0) < hs, 1.0, -1.0).astype(F32)

    def stage_b(k, carry):
        sl = pl.ds(pl.multiple_of(k * 2 * hs, 2 * hs), 2 * hs)
        a, b = o_ref[sl, :], fbuf_ref[sl, :]
        if n2 > 1:
            a, b = _dot(fb_ref[...], a, HI), _dot(fb_ref[...], b, HI)
        o_ref[sl, :] = a + sgn * b
        return carry

    lax.fori_loop(0, nslab, stage_b, 0)


def _hy_spectrum(h):
    length = h.shape[0]
    n1, n2 = _fft_factors(length)
    ta, _, fb, _ = (jnp.asarray(t) for t in _fft_tables(length))
    nblk = HY_CH // LANES
    return pl.pallas_call(
        functools.partial(_spec_kernel, n1=n1, n2=n2),
        grid=(nblk,),
        in_specs=[pl.BlockSpec((length, LANES), lambda j: (0, j)),
                  pl.BlockSpec((length, LANES), lambda j: (0, nblk + j)),
                  _const_spec(ta.shape), _const_spec(fb.shape)],
        out_specs=pl.BlockSpec((4 * length, LANES), lambda j: (0, j)),
        out_shape=jax.ShapeDtypeStruct((4 * length, HY_CH), F32),
        scratch_shapes=[pltpu.VMEM((4 * length, LANES), F32)],
        compiler_params=_cparams(("arbitrary",)),
        name="hyena_spectrum",
    )(h, h, ta, fb)


def _shift_rows(x, prev_row, next_row):
    n = x.shape[0]
    row = lax.broadcasted_iota(jnp.int32, x.shape, 0)
    up = jnp.where(row == 0, prev_row, pltpu.roll(x, 1, 0))
    dn = jnp.where(row == n - 1, next_row, pltpu.roll(x, n - 1, 0))
    return up, dn


def _hyconv_kernel(x1_ref, x2_ref, v_ref, cw1_ref, cw2_ref, cwv_ref, cb_ref, bias_ref, g_ref,
                   ta_ref, tai_ref, fb_ref, fbi_ref, o_ref, u_ref, y_ref, fbuf_ref, *, n1, n2):
    def conv3(ref, w_ref, b):
        x = ref[...]
        up, dn = _shift_rows(x, 0.0, 0.0)
        return up * w_ref[0:1, :] + x * w_ref[1:2, :] + dn * w_ref[2:3, :] + b

    cb = cb_ref[...]
    u_ref[...] = conv3(x1_ref, cw1_ref, cb[0:1, :]) * conv3(v_ref, cwv_ref, cb[2:3, :])

    _fft_stage_a(u_ref, fbuf_ref, ta_ref, n1, n2)
    nslab, hs = _fft_layout(n1, n2)

    def stage_b(k, carry):
        sl = pl.ds(pl.multiple_of(k * 2 * hs, 2 * hs), 2 * hs)
        a = fbuf_ref[sl, :]
        if n2 > 1:
            a = _dot(fb_ref[...], a, HI)
        g = g_ref[sl, :]
        ar, ai, gr, gi = a[:hs], a[hs:], g[:hs], g[hs:]
        a = jnp.concatenate([ar * gr - ai * gi, ar * gi + ai * gr], axis=0)
        if n2 > 1:
            a = _dot(fbi_ref[...], a, HI)
        fbuf_ref[sl, :] = a
        return carry

    lax.fori_loop(0, nslab, stage_b, 0)

    if n2 == 1:
        y_ref[...] = _dot(tai_ref[0], fbuf_ref[...], HI)
    else:
        def stage_a_inv(j, carry):
            cre = fbuf_ref[pl.ds(j, n1, stride=2 * n2), :]
            cim = fbuf_ref[pl.ds(n2 + j, n1, stride=2 * n2), :]
            y = _dot(tai_ref[j], jnp.concatenate([cre, cim], axis=0), HI)
            y_ref[pl.ds(j, n1 // 2, stride=n2), :] = y
            return carry

        lax.fori_loop(0, n2, stage_a_inv, 0)

    u = u_ref[...]
    o_ref[...] = (conv3(x2_ref, cw2_ref, cb[1:2, :]) * (y_ref[...] + u * bias_ref[...])).astype(o_ref.dtype)


def _hy_conv(p, conv_w, conv_b, bias, spec):
    b, length, _ = p.shape
    n1, n2 = _fft_factors(length)
    ta, tai, fb, fbi = (jnp.asarray(t) for t in _fft_tables(length))
    nblk = HY_CH // LANES
    cb3 = conv_b.reshape(3, HY_CH)
    col = lambda g: pl.BlockSpec((None, length, LANES), lambda j, i, g=g: (i, 0, g * nblk + j))
    wcol = lambda g: pl.BlockSpec((3, LANES), lambda j, i, g=g: (0, g * nblk + j))
    return pl.pallas_call(
        functools.partial(_hyconv_kernel, n1=n1, n2=n2),
        grid=(nblk, b),
        in_specs=[col(0), col(1), col(2), wcol(0), wcol(1), wcol(2),
                  pl.BlockSpec((3, LANES), lambda j, i: (0, j)),
                  pl.BlockSpec((1, LANES), lambda j, i: (0, j)),
                  pl.BlockSpec((4 * length, LANES), lambda j, i: (0, j), pipeline_mode=pl.Buffered(1)),
                  _const_spec(ta.shape), _const_spec(tai.shape), _const_spec(fb.shape),
                  _const_spec(fbi.shape)],
        out_specs=pl.BlockSpec((None, length, LANES), lambda j, i: (i, 0, j)),
        out_shape=jax.ShapeDtypeStruct((b, length, HY_CH), BF16),
        scratch_shapes=[pltpu.VMEM((length, LANES), F32), pltpu.VMEM((length, LANES), F32),
                        pltpu.VMEM((4 * length, LANES), F32)],
        compiler_params=_cparams(("arbitrary", "arbitrary")),
        name="hyena_conv",
    )(p, p, p, conv_w, conv_w, conv_w, cb3, bias, spec, ta, tai, fb, fbi)


def _rope_tables(length, d_rot, lane_lo, head_w):
    rows = length // GRID_W
    row = np.repeat(np.arange(rows, dtype=np.float32), GRID_W)
    colv = np.tile(np.arange(GRID_W, dtype=np.float32), rows)
    n_freq = d_rot // 4
    inv = (np.float32(ROPE_THETA) ** (-np.arange(n_freq, dtype=np.float32) / np.float32(n_freq))).astype(np.float32)
    ang = np.concatenate([row[:, None] * inv, colv[:, None] * inv], axis=-1).astype(np.float32)
    cos_t = np.ones((length, LANES), np.float32)
    sin_t = np.zeros((length, LANES), np.float32)
    c, s = np.cos(ang), np.sin(ang)
    for base in range(0, LANES, head_w):
        for i in range(d_rot // 2):
            cos_t[:, base + lane_lo + 2 * i] = c[:, i]
            cos_t[:, base + lane_lo + 2 * i + 1] = c[:, i]
            sin_t[:, base + lane_lo + 2 * i] = -s[:, i]
            sin_t[:, base + lane_lo + 2 * i + 1] = s[:, i]
    return cos_t, sin_t


def _rope(x, cos_t, sin_t):
    lane = lax.broadcasted_iota(jnp.int32, x.shape, 1)
    w = x.shape[1]
    partner = jnp.where(jnp.bitwise_and(lane, 1) == 0, pltpu.roll(x, w - 1, 1), pltpu.roll(x, 1, 1))
    return x * cos_t + partner * sin_t


def _gqa_prep_kernel(p_ref, cos_ref, sin_ref, gq_ref, gk_ref, seg_ref, q_ref, k_ref, v_ref, *, rope):
    p = p_ref[...]
    seg = seg_ref[...]
    lane = lax.broadcasted_iota(jnp.int32, (p.shape[0], LANES), 1)
    low = lane < HEAD_DIM

    def hnorm(x, g):
        ms = _dot(x * x, seg, HI)
        return x * lax.rsqrt(ms + NORM_EPS) * g

    scale = HEAD_DIM ** -0.5
    for c in range(2):
        q = hnorm(p[:, c * LANES:(c + 1) * LANES], gq_ref[...])
        if rope:
            q = _rope(q, cos_ref[...], sin_ref[...])
        q = q * scale
        qs = pltpu.roll(q, HEAD_DIM, 1)
        q_ref[:, (2 * c) * LANES:(2 * c + 1) * LANES] = jnp.where(low, q, 0.0).astype(BF16)
        q_ref[:, (2 * c + 1) * LANES:(2 * c + 2) * LANES] = jnp.where(low, qs, 0.0).astype(BF16)
    k = hnorm(p[:, 256:384], gk_ref[...])
    if rope:
        k = _rope(k, cos_ref[...], sin_ref[...])
    ks = pltpu.roll(k, HEAD_DIM, 1)
    k_ref[:, 0:LANES] = jnp.where(low, k, 0.0).astype(BF16)
    k_ref[:, LANES:2 * LANES] = jnp.where(low, ks, 0.0).astype(BF16)
    v = p[:, 384:512]
    vs = pltpu.roll(v, HEAD_DIM, 1)
    v_ref[:, 0:LANES] = jnp.where(low, v, vs).astype(BF16)
    v_ref[:, LANES:2 * LANES] = jnp.where(low, vs, v).astype(BF16)


def _seg_matrix(width, seg, value):
    i = np.arange(width)
    return ((i[:, None] // seg) == (i[None, :] // seg)).astype(np.float32) * np.float32(value)


def _gqa_prep(p, gq, gk, rope):
    b, length, _ = p.shape
    tm = min(ROW_TILE, length)
    if rope:
        cos_t, sin_t = (jnp.asarray(t) for t in _rope_tables(length, HEAD_DIM, 0, HEAD_DIM))
    else:
        cos_t = sin_t = jnp.zeros((length, LANES), F32)
    seg = jnp.asarray(_seg_matrix(LANES, HEAD_DIM, 1.0 / HEAD_DIM))
    tab = pl.BlockSpec((tm, LANES), lambda i, bb: (i, 0))
    outw = (512, 256, 256)
    return pl.pallas_call(
        functools.partial(_gqa_prep_kernel, rope=rope),
        grid=(length // tm, b),
        in_specs=[pl.BlockSpec((None, tm, GQA_COLS), lambda i, bb: (bb, i, 0)), tab, tab,
                  _const_spec((1, LANES)), _const_spec((1, LANES)), _const_spec((LANES, LANES))],
        out_specs=[pl.BlockSpec((None, tm, w), lambda i, bb: (bb, i, 0)) for w in outw],
        out_shape=[jax.ShapeDtypeStruct((b, length, w), BF16) for w in outw],
        compiler_params=_cparams(("arbitrary", "arbitrary")),
        name="gqa_prep",
    )(p, cos_t, sin_t, gq, gk, seg)


def _mla_prep_kernel(p_ref, cos_ref, sin_ref, cqn_ref, ckvn_ref, wuq_ref, wuk_ref, wuv_ref,
                     qn_ref, kn_ref, q_ref, k_ref, v_ref, *, rope):
    p = p_ref[...]

    def rms(x, g):
        return x * lax.rsqrt(jnp.mean(x * x, axis=-1, keepdims=True) + NORM_EPS) * g

    cq = rms(p[:, 0:MLA_Q_RANK], cqn_ref[...]).astype(BF16)
    ckv = rms(p[:, MLA_Q_RANK:MLA_Q_RANK + MLA_KV_RANK], ckvn_ref[...]).astype(BF16)
    q = _dot(cq, wuq_ref[...])
    kn = _dot(ckv, wuk_ref[...])
    v_ref[...] = _dot(ckv, wuv_ref[...]).astype(BF16)
    lane = lax.broadcasted_iota(jnp.int32, (p.shape[0], LANES), 1)
    in_rope = jnp.logical_and(lane >= MLA_NOPE, lane < MLA_QK)
    kr = jnp.where(in_rope, pltpu.roll(p[:, 384:512], MLA_NOPE, 1), 0.0)
    scale = MLA_QK ** -0.5

    def hnorm(x, g):
        ms = jnp.sum(x * x, axis=-1, keepdims=True) * (1.0 / MLA_QK)
        return x * lax.rsqrt(ms + NORM_EPS) * g

    for h in range(MLA_HEADS):
        sl = slice(h * LANES, (h + 1) * LANES)
        qh = hnorm(q[:, sl], qn_ref[...])
        kh = hnorm(kn[:, sl] + kr, kn_ref[...])
        if rope:
            qh = _rope(qh, cos_ref[...], sin_ref[...])
            kh = _rope(kh, cos_ref[...], sin_ref[...])
        q_ref[:, sl] = (qh * scale).astype(BF16)
        k_ref[:, sl] = kh.astype(BF16)


def _mla_prep(p, cqn, ckvn, wuq, wuk, wuv, qn, kn, rope):
    b, length, _ = p.shape
    tm = min(ROW_TILE, length)
    if rope:
        cos_t, sin_t = (jnp.asarray(t) for t in _rope_tables(length, MLA_ROPE, MLA_NOPE, LANES))
    else:
        cos_t = sin_t = jnp.zeros((length, LANES), F32)
    tab = pl.BlockSpec((tm, LANES), lambda i, bb: (i, 0))
    consts = (cqn, ckvn, wuq, wuk, wuv, qn, kn)
    outw = (512, 512, 256)
    return pl.pallas_call(
        functools.partial(_mla_prep_kernel, rope=rope),
        grid=(length // tm, b),
        in_specs=[pl.BlockSpec((None, tm, MLA_COLS_PAD), lambda i, bb: (bb, i, 0)), tab, tab]
                 + [_const_spec(a.shape) for a in consts],
        out_specs=[pl.BlockSpec((None, tm, w), lambda i, bb: (bb, i, 0)) for w in outw],
        out_shape=[jax.ShapeDtypeStruct((b, length, w), BF16) for w in outw],
        compiler_params=_cparams(("arbitrary", "arbitrary")),
        name="mla_prep",
    )(p, cos_t, sin_t, *consts)


def _attn_kernel(*refs, n_seg):
    q_ref, o_ref = refs[0], refs[-1]
    segs = [refs[1 + 3 * s:4 + 3 * s] for s in range(n_seg)]
    q = q_ref[...]
    outs = []
    for side in range(2):
        qh = q[:, side * LANES:(side + 1) * LANES]
        scores = [_dot_nt(qh, seg[side][...]) for seg in segs]
        m = functools.reduce(jnp.maximum, [jnp.max(s, axis=-1, keepdims=True) for s in scores])
        ps = [jnp.exp(s - m) for s in scores]
        denom = functools.reduce(jnp.add, [jnp.sum(e, axis=-1, keepdims=True) for e in ps])
        o = functools.reduce(jnp.add, [_dot(e.astype(BF16), seg[2][...]) for e, seg in zip(ps, segs)])
        outs.append(o / denom)
    lane = lax.broadcasted_iota(jnp.int32, outs[0].shape, 1)
    o_ref[...] = jnp.where(lane < HEAD_DIM, outs[0], outs[1]).astype(o_ref.dtype)


def _attention(q, kv_segs, k_heads):
    b, lq, _ = q.shape
    tq = min(Q_TILE, lq)
    ka = (lambda pr: pr) if k_heads == 2 else (lambda pr: 2 * pr)
    kb = (lambda pr: pr) if k_heads == 2 else (lambda pr: 2 * pr + 1)
    in_specs = [pl.BlockSpec((None, tq, 2 * LANES), lambda bb, pr, i: (bb, i, pr))]
    args = [q]
    for k, v in kv_segs:
        lk = k.shape[1]
        in_specs += [pl.BlockSpec((None, lk, LANES), lambda bb, pr, i, f=ka: (bb, 0, f(pr))),
                     pl.BlockSpec((None, lk, LANES), lambda bb, pr, i, f=kb: (bb, 0, f(pr))),
                     pl.BlockSpec((None, lk, LANES), lambda bb, pr, i: (bb, 0, pr))]
        args += [k, k, v]
    return pl.pallas_call(
        functools.partial(_attn_kernel, n_seg=len(kv_segs)),
        grid=(b, 2, lq // tq),
        in_specs=in_specs,
        out_specs=pl.BlockSpec((None, tq, LANES), lambda bb, pr, i: (bb, i, pr)),
        out_shape=jax.ShapeDtypeStruct((b, lq, 2 * LANES), BF16),
        compiler_params=_cparams(("arbitrary", "arbitrary", "arbitrary")),
        name="attention",
    )(*args)


def _rw_prep_kernel(p_ref, prev_ref, next_ref, mu_ref, kk_ref_w, ka_ref, rk_ref, w0_ref, w2_ref,
                    a0_ref, a2_ref, g2_ref, seg_ref,
                    r_ref, v_ref, kk_ref, lw_ref, kd_ref, bd_ref, bonus_ref, g_ref):
    i = pl.program_id(1)
    n = pl.num_programs(1)
    x = p_ref[...]
    prev_row = jnp.where(i > 0, prev_ref[7:8, :], 0.0)
    next_row = jnp.where(i < n - 1, next_ref[0:1, :], 0.0)
    up, dn = _shift_rows(x, prev_row, next_row)
    xs = x + mu_ref[...] * (0.5 * (up + dn) - x)
    c = RW_C
    r, k, v = xs[:, 0:c], xs[:, c:2 * c], xs[:, 2 * c:3 * c]
    xw, xa, xg = xs[:, 3 * c:3 * c + 128], xs[:, 3 * c + 128:3 * c + 256], xs[:, 3 * c + 256:3 * c + 384]
    seg = seg_ref[...]
    kk = k * kk_ref_w[...]
    nrm = jnp.sqrt(_dot(kk * kk, seg, HI))
    kk = kk / jnp.maximum(nrm, 1e-12)
    u = w0_ref[...] + _dot(jnp.tanh(xw), w2_ref[...], HI)
    z = -u
    softplus = jnp.maximum(z, 0.0) + jnp.log(1.0 + jnp.exp(-jnp.abs(z)))
    lw_ref[...] = -jnp.exp(-softplus - 0.5)
    a = _sigmoid(a0_ref[...] + _dot(xa, a2_ref[...], HI))
    bonus = jnp.zeros_like(r)
    for d in range(2):
        a_d = a[:, d * c:(d + 1) * c]
        k_d = k * (1.0 + (a_d - 1.0) * ka_ref[...])
        kd_ref[:, d * c:(d + 1) * c] = k_d
        bd_ref[:, d * c:(d + 1) * c] = kk * a_d
        bonus = bonus + _dot(r * k_d * rk_ref[...], seg, HI)
    r_ref[...] = r
    v_ref[...] = v
    kk_ref[...] = kk
    bonus_ref[...] = bonus * v
    g_ref[...] = _dot(_sigmoid(xg), g2_ref[...], HI)


def _rw_prep(p, mu, k_k, k_a, r_k, w0, w2bd, a0, a2bd, g2):
    b, length, cols = p.shape
    tm = min(ROW_TILE, length)
    nb = tm // 8
    last = length // 8 - 1
    seg = jnp.asarray(_seg_matrix(RW_C, RW_N, 1.0))
    consts = (mu, k_k, k_a, r_k, w0, w2bd, a0, a2bd, g2, seg)
    outw = (RW_C, RW_C, RW_C, 2 * RW_C, 2 * RW_C, 2 * RW_C, RW_C, RW_C)
    return pl.pallas_call(
        _rw_prep_kernel,
        grid=(b, length // tm),
        in_specs=[pl.BlockSpec((None, tm, cols), lambda bb, i: (bb, i, 0)),
                  pl.BlockSpec((None, 8, cols), lambda bb, i: (bb, jnp.maximum(i * nb - 1, 0), 0)),
                  pl.BlockSpec((None, 8, cols), lambda bb, i: (bb, jnp.minimum((i + 1) * nb, last), 0))]
                 + [_const_spec(a.shape) for a in consts],
        out_specs=[pl.BlockSpec((None, tm, w), lambda bb, i: (bb, i, 0)) for w in outw],
        out_shape=[jax.ShapeDtypeStruct((b, length, w), F32) for w in outw],
        compiler_params=_cparams(("arbitrary", "arbitrary")),
        name="rwkv_prep",
    )(p, p, p, *consts)


def _rw_scan_kernel(r_ref, v_ref, kk_ref, lw_ref, kd_ref, bd_ref, s0_ref, y_ref, sfin_ref, s_ref,
                    *, n_chunks):
    d = pl.program_id(1)
    i = pl.program_id(2)
    cs = RW_CHUNK
    n = RW_C

    @pl.when(i == 0)
    def _():
        s_ref[...] = s0_ref[...]

    sgn = 1 - 2 * d
    row = lax.broadcasted_iota(jnp.int32, (n, n), 0)
    col = lax.broadcasted_iota(jnp.int32, (n, n), 1)
    same_head = jnp.right_shift(row, 6) == jnp.right_shift(col, 6)
    rel = (jnp.bitwise_and(col, cs - 1) - jnp.bitwise_and(row, cs - 1)) * sgn
    strict = jnp.logical_and(same_head, rel < 0)
    incl = jnp.logical_and(same_head, rel <= 0)
    eye = (row == col).astype(F32)
    trow = lax.broadcasted_iota(jnp.int32, (cs, cs), 0)
    tcol = lax.broadcasted_iota(jnp.int32, (cs, cs), 1)
    tri = (((tcol - trow) * sgn) <= 0).astype(F32)

    def bd(x):
        return jnp.where(same_head, jnp.concatenate([x, x, x, x], axis=0), 0.0)

    def tile(x):
        return jnp.concatenate([x, x, x, x], axis=0)

    def chunk(c, carry):
        cc = c + d * (n_chunks - 1 - 2 * c)
        sl = pl.ds(pl.multiple_of(cc * cs, cs), cs)
        r, v, kk = r_ref[sl, :], v_ref[sl, :], kk_ref[sl, :]
        lw, kd, b = lw_ref[sl, :], kd_ref[sl, :], bd_ref[sl, :]
        cl = _dot(tri, lw, HI)
        w_cum, w_prev, w_inv = jnp.exp(cl), jnp.exp(cl - lw), jnp.exp(-cl)
        w_all = jnp.exp(jnp.sum(lw, axis=0, keepdims=True))
        kt, bt = kd * w_inv, b * w_inv
        qk_s, rt_s = bd(kk * w_prev), bd(r * w_cum)
        kt_t, bt_t = tile(kt), tile(bt)
        a_kk = jnp.where(strict, _dot_nt(qk_s, kt_t), 0.0)
        a_kb = jnp.where(strict, _dot_nt(qk_s, bt_t), 0.0)
        a_rk = jnp.where(incl, _dot_nt(rt_s, kt_t), 0.0)
        a_rb = jnp.where(incl, _dot_nt(rt_s, bt_t), 0.0)
        lp = -a_kb
        t_inv = eye + lp
        m = 1
        while 2 * m < cs:
            lp = _dot(lp, lp)
            t_inv = t_inv + _dot(t_inv, lp)
            m *= 2
        v_s = bd(v)
        x1 = _dot(t_inv, qk_s)
        x2 = _dot(t_inv, _dot(a_kk, v_s))
        y1 = rt_s - _dot(a_rb, x1)
        y2 = _dot(a_rk, v_s) - _dot(a_rb, x2)
        bh_s, kh_s = bd(bt * w_all), bd(kt * w_all)
        m_c = eye * w_all - _dot_tn(x1, bh_s)
        n_c = _dot_tn(v_s, kh_s) - _dot_tn(x2, bh_s)
        s0 = s_ref[...]
        y_bd = _dot_nt(y1, s0) + y2
        y_ref[sl, :] = y_bd[0:cs] + y_bd[cs:2 * cs] + y_bd[2 * cs:3 * cs] + y_bd[3 * cs:4 * cs]
        s_ref[...] = _dot(s0, m_c) + n_c
        return carry

    lax.fori_loop(0, n_chunks, chunk, 0)
    sfin_ref[...] = s_ref[...]


def _rw_scan(r, v, kk, lw, kd, bdir, s0):
    b, length, c = r.shape
    tc = min(RW_BLOCK, length)
    nblk = length // tc
    tblk = lambda d, i: i + d * (nblk - 1 - 2 * i)
    shared = pl.BlockSpec((None, tc, c), lambda bb, d, i: (bb, tblk(d, i), 0))
    perdir = pl.BlockSpec((None, tc, c), lambda bb, d, i: (bb, tblk(d, i), d))
    state = pl.BlockSpec((None, None, c, c), lambda bb, d, i: (bb, d, 0, 0))
    return pl.pallas_call(
        functools.partial(_rw_scan_kernel, n_chunks=tc // RW_CHUNK),
        grid=(b, 2, nblk),
        in_specs=[shared, shared, shared, perdir, perdir, perdir, state],
        out_specs=[pl.BlockSpec((None, None, tc, c), lambda bb, d, i: (d, bb, tblk(d, i), 0)), state],
        out_shape=[jax.ShapeDtypeStruct((2, b, length, c), F32),
                   jax.ShapeDtypeStruct((b, 2, c, c), F32)],
        scratch_shapes=[pltpu.VMEM((c, c), F32)],
        compiler_params=_cparams(("arbitrary", "arbitrary", "arbitrary")),
        name="rwkv_scan",
    )(r, v, kk, lw, kd, bdir, s0)


def _rw_readout_kernel(yf_ref, yb_ref, bonus_ref, g_ref, lnw_ref, lnb_ref, seg_ref, o_ref):
    y = yf_ref[...] + yb_ref[...]
    seg = seg_ref[...]
    mean = _dot(y, seg, HI)
    yc = y - mean
    var = _dot(yc * yc, seg, HI)
    yn = yc * lax.rsqrt(var + RW_GN_EPS) * lnw_ref[...] + lnb_ref[...]
    o_ref[...] = ((yn + bonus_ref[...]) * g_ref[...]).astype(o_ref.dtype)


def _rw_readout(ys, bonus, g, ln_w, ln_b):
    _, b, length, c = ys.shape
    tm = min(ROW_TILE, length)
    seg = jnp.asarray(_seg_matrix(RW_C, RW_N, 1.0 / RW_N))
    rowspec = pl.BlockSpec((None, tm, c), lambda bb, i: (bb, i, 0))
    return pl.pallas_call(
        _rw_readout_kernel,
        grid=(b, length // tm),
        in_specs=[pl.BlockSpec((None, None, tm, c), lambda bb, i: (0, bb, i, 0)),
                  pl.BlockSpec((None, None, tm, c), lambda bb, i: (1, bb, i, 0)),
                  rowspec, rowspec, _const_spec((1, c)), _const_spec((1, c)), _const_spec((c, c))],
        out_specs=rowspec,
        out_shape=jax.ShapeDtypeStruct((b, length, c), BF16),
        compiler_params=_cparams(("arbitrary", "arbitrary")),
        name="rwkv_readout",
    )(ys, ys, bonus, g, ln_w, ln_b, seg)


def _blockdiag2(w):
    k, n = w.shape[1:]
    z = jnp.zeros((k, n), w.dtype)
    return jnp.concatenate([jnp.concatenate([w[0], z], axis=1), jnp.concatenate([z, w[1]], axis=1)], axis=0)


def _mla_head_pad(w, width):
    k = w.shape[0]
    w = w.reshape(k, MLA_HEADS, width)
    return jnp.pad(w, ((0, 0), (0, 0), (0, LANES - width))).reshape(k, MLA_HEADS * LANES)


def kernel(x, c, ctx, c_ctx, ada_w, ada_b, norm_ffn1, norm_mix, norm_ffn2, ffn1_gate, ffn1_up, ffn1_down, ffn2_gate, ffn2_up, ffn2_down, w_in, w_out, hy_conv_w, hy_conv_b, hy_f_w1, hy_f_b1, hy_f_w2, hy_f_b2, hy_f_w3, hy_f_b3, hy_f_w4, hy_f_freq, hy_bias, gqa_q_norm, gqa_k_norm, mla_cq_norm, mla_ckv_norm, mla_w_uq, mla_w_ukv, mla_q_norm, mla_k_norm, rw_mu, rw_w0, rw_w2, rw_a0, rw_a2, rw_g2, rw_k_k, rw_k_a, rw_r_k, rw_ln_w, rw_ln_b):
    b, length, d = x.shape
    lc = ctx.shape[1]
    depth = ada_w.shape[0]

    c_all = jnp.zeros((16, d), F32).at[:b].set(c).at[b].set(c_ctx)
    mod = _compute_mod(c_all, ada_w, ada_b)

    xc = ctx.reshape(1, b * lc, d)
    for l in range(depth):
        ctx_out = l < depth - 1
        mod_x = mod[l, :b].reshape(b, N_MOD, d)
        mod_c = mod[l, b:b + 1].reshape(1, N_MOD, d)
        row = lambda a: a[l].reshape(1, -1)

        wg1, wu1, wd1 = (w[l].astype(BF16) for w in (ffn1_gate, ffn1_up, ffn1_down))
        wg2, wu2, wd2 = (w[l].astype(BF16) for w in (ffn2_gate, ffn2_up, ffn2_down))
        wi = w_in[l]
        o1, o2, o3 = HY_COLS, HY_COLS + GQA_COLS, HY_COLS + GQA_COLS + MLA_COLS
        wi = jnp.concatenate([wi[:, :o3], jnp.zeros((d, MLA_COLS_PAD - MLA_COLS), F32), wi[:, o3:]],
                             axis=1).astype(BF16)
        wo = w_out[l].astype(BF16)

        x = _ffn(x, mod_x, row(norm_ffn1), wg1, wu1, wd1, 0)
        xc = _ffn(xc, mod_c, row(norm_ffn1), wg1, wu1, wd1, 0)

        hy_x, gq_x, ml_x, rw_x = _inproj(x, mod_x, row(norm_mix), wi)
        hy_c, gq_c, ml_c, rw_c = (t.reshape(b, lc, -1) for t in _inproj(xc, mod_c, row(norm_mix), wi))

        w1p = jnp.zeros((LANES, HY_ORDER), F32).at[:HY_EMB].set(hy_f_w1[l])
        filt = (w1p, row(hy_f_b1), hy_f_w2[l], row(hy_f_b2), hy_f_w3[l], row(hy_f_b3), hy_f_w4[l],
                row(hy_f_freq))
        spec_x = _hy_spectrum(_hy_filters(length, *filt))
        y_hy_x = _hy_conv(hy_x, hy_conv_w[l], hy_conv_b[l], row(hy_bias), spec_x)

        gq = jnp.tile(row(gqa_q_norm), (1, 2))
        gk = jnp.tile(row(gqa_k_norm), (1, 2))
        q_l, k_l, v_l = _gqa_prep(gq_x, gq, gk, True)
        q_c, k_c, v_c = _gqa_prep(gq_c, gq, gk, False)
        y_gq_x = _attention(q_l, [(k_c, v_c), (k_l, v_l)], GQA_KV_HEADS)

        wuq = _mla_head_pad(mla_w_uq[l], MLA_QK).astype(BF16)
        wukv = mla_w_ukv[l].reshape(MLA_KV_RANK, MLA_HEADS, MLA_NOPE + MLA_V)
        wuk = _mla_head_pad(wukv[:, :, :MLA_NOPE].reshape(MLA_KV_RANK, -1), MLA_NOPE).astype(BF16)
        wuv = wukv[:, :, MLA_NOPE:].reshape(MLA_KV_RANK, -1).astype(BF16)
        pad_n = lambda g: jnp.pad(g[l], (0, LANES - MLA_QK)).reshape(1, LANES)
        mla_w = (row(mla_cq_norm), row(mla_ckv_norm), wuq, wuk, wuv, pad_n(mla_q_norm), pad_n(mla_k_norm))
        mq_l, mk_l, mv_l = _mla_prep(ml_x, *mla_w, True)
        mq_c, mk_c, mv_c = _mla_prep(ml_c, *mla_w, False)
        y_ml_x = _attention(mq_l, [(mk_c, mv_c), (mk_l, mv_l)], MLA_HEADS)

        rw_w = (row(rw_mu), row(rw_k_k), row(rw_k_a), rw_r_k[l].reshape(1, RW_C),
                rw_w0[l].reshape(1, 2 * RW_C), _blockdiag2(rw_w2[l]),
                rw_a0[l].reshape(1, 2 * RW_C), _blockdiag2(rw_a2[l]), rw_g2[l])
        pc = _rw_prep(rw_c, *rw_w)
        px = _rw_prep(rw_x, *rw_w)
        zeros = jnp.zeros((b, 2, RW_C, RW_C), F32)
        ys_c, s_ctx = _rw_scan(*pc[:6], zeros)
        ys_x, _ = _rw_scan(*px[:6], s_ctx)
        y_rw_x = _rw_readout(ys_x, px[6], px[7], row(rw_ln_w), row(rw_ln_b))

        x = _outproj(x, mod_x, (y_hy_x, y_gq_x, y_ml_x, y_rw_x), wo)
        if ctx_out:
            spec_c = _hy_spectrum(_hy_filters(lc, *filt))
            y_hy_c = _hy_conv(hy_c, hy_conv_w[l], hy_conv_b[l], row(hy_bias), spec_c)
            y_gq_c = _attention(q_c, [(k_c, v_c)], GQA_KV_HEADS)
            y_ml_c = _attention(mq_c, [(mk_c, mv_c)], MLA_HEADS)
            y_rw_c = _rw_readout(ys_c, pc[6], pc[7], row(rw_ln_w), row(rw_ln_b))
            flat = lambda t: t.reshape(1, b * lc, -1)
            xc = _outproj(xc, mod_c, tuple(flat(t) for t in (y_hy_c, y_gq_c, y_ml_c, y_rw_c)), wo)

        x = _ffn(x, mod_x, row(norm_ffn2), wg2, wu2, wd2, 2)
        if ctx_out:
            xc = _ffn(xc, mod_c, row(norm_ffn2), wg2, wu2, wd2, 2)
    return x
```

```python
import functools
import math

import numpy as np
import jax
import jax.numpy as jnp
from jax import lax
from jax.experimental import pallas as pl
from jax.experimental.pallas import tpu as pltpu

F32 = jnp.float32
BF16 = jnp.bfloat16
HI = lax.Precision.HIGHEST

D_MODEL = 1024
GRID_W = 64
HEAD_DIM = 64
D_FF = 2816
N_MOD = 9
NORM_EPS = 1e-6
LOG2E = math.log2(math.e)
ROPE_THETA = 10000.0

HY_CH = 256
HY_EMB = 33
HY_ORDER = 64
HY_FAST_PCT = 0.3
HY_SLOW_PCT = 1.5
HY_TARGET = 1e-2

GQA_HEADS = 4
GQA_KV_HEADS = 2
MLA_HEADS = 4
MLA_NOPE = 64
MLA_ROPE = 32
MLA_QK = MLA_NOPE + MLA_ROPE
MLA_V = 64
MLA_Q_RANK = 256
MLA_KV_RANK = 128

RW_HEADS = 4
RW_N = 64
RW_C = RW_HEADS * RW_N
RW_DECAY_LORA = 64
RW_AAA_LORA = 64
RW_GATE_LORA = 128
RW_GN_EPS = 64e-5

HY_COLS = 3 * HY_CH
GQA_COLS = (GQA_HEADS + 2 * GQA_KV_HEADS) * HEAD_DIM
MLA_COLS = MLA_Q_RANK + MLA_KV_RANK + MLA_ROPE
MLA_COLS_PAD = 512
RW_COLS = 3 * RW_C + 2 * RW_DECAY_LORA + 2 * RW_AAA_LORA + RW_GATE_LORA
D_IN_PAD = HY_COLS + GQA_COLS + MLA_COLS_PAD + RW_COLS

LANES = 128
ROW_TILE = 512
Q_TILE = 256
KEY_CHUNK = 256
RW_CHUNK = 64
RW_BLOCK = 512
RW_SEQS = 2
VMEM_LIMIT = 56 * 2 ** 20


def _cparams(sem):
    return pltpu.CompilerParams(dimension_semantics=sem, vmem_limit_bytes=VMEM_LIMIT)


def _const_spec(shape):
    nd = len(shape)
    return pl.BlockSpec(shape, lambda *_: (0,) * nd, pipeline_mode=pl.Buffered(1))


def _dot(a, b, precision=None):
    return jnp.dot(a, b, preferred_element_type=F32, precision=precision)


def _dot_nt(a, b, precision=None):
    return lax.dot_general(a, b, (((1,), (1,)), ((), ())), preferred_element_type=F32,
                           precision=precision)


def _dot_tn(a, b, precision=None):
    return lax.dot_general(a, b, (((0,), (0,)), ((), ())), preferred_element_type=F32,
                           precision=precision)


def _sigmoid(x):
    return 1.0 / (1.0 + jnp.exp(-x))


def _adaln(x, g, mod_ref, i):
    shift = mod_ref[pl.ds(3 * i, 1), :]
    scale = mod_ref[pl.ds(3 * i + 1, 1), :]
    r = lax.rsqrt(jnp.mean(x * x, axis=-1, keepdims=True) + NORM_EPS)
    return (x * r) * (g * (1.0 + scale)) + shift


def _mod_kernel(c_ref, w_ref, b_ref, o_ref):
    c = c_ref[...]
    s = c * _sigmoid(c)
    o_ref[...] = _dot(s, w_ref[...], HI) + b_ref[...]


def _compute_mod(c_all, ada_w, ada_b):
    depth, d, n = ada_w.shape
    rows = c_all.shape[0]
    tn = 1024
    return pl.pallas_call(
        _mod_kernel,
        grid=(depth, n // tn),
        in_specs=[
            pl.BlockSpec((rows, d), lambda l, j: (0, 0)),
            pl.BlockSpec((None, d, tn), lambda l, j: (l, 0, j)),
            pl.BlockSpec((None, 1, tn), lambda l, j: (l, 0, j)),
        ],
        out_specs=pl.BlockSpec((None, rows, tn), lambda l, j: (l, 0, j)),
        out_shape=jax.ShapeDtypeStruct((depth, rows, n), F32),
        compiler_params=_cparams(("arbitrary", "arbitrary")),
        name="mod",
    )(c_all, ada_w, ada_b.reshape(depth, 1, n))


def _ffn_kernel(x_ref, mod_ref, g_ref, wg_ref, wu_ref, wd_ref, o_ref, *, sub):
    x = x_ref[...]
    h = _adaln(x, g_ref[...], mod_ref, sub).astype(BF16)
    a = _dot(h, wg_ref[...])
    u = _dot(h, wu_ref[...])
    z = (a * _sigmoid(a) * u).astype(BF16)
    y = _dot(z, wd_ref[...])
    gate = mod_ref[pl.ds(3 * sub + 2, 1), :]
    o_ref[...] = x + (0.5 * gate) * y


def _ffn(x, mod, g, wg, wu, wd, sub):
    bm, rows, d = x.shape
    tm = min(ROW_TILE, rows)
    f = wg.shape[1]
    return pl.pallas_call(
        functools.partial(_ffn_kernel, sub=sub),
        grid=(bm, rows // tm),
        in_specs=[
            pl.BlockSpec((None, tm, d), lambda b, i: (b, i, 0)),
            pl.BlockSpec((None, N_MOD, d), lambda b, i: (b, 0, 0)),
            _const_spec((1, d)),
            _const_spec((d, f)),
            _const_spec((d, f)),
            _const_spec((f, d)),
        ],
        out_specs=pl.BlockSpec((None, tm, d), lambda b, i: (b, i, 0)),
        out_shape=jax.ShapeDtypeStruct(x.shape, F32),
        compiler_params=_cparams(("arbitrary", "arbitrary")),
        name="ffn",
    )(x, mod, g, wg, wu, wd)


_IN_OFFS = (0, HY_COLS, HY_COLS + GQA_COLS, HY_COLS + GQA_COLS + MLA_COLS_PAD, D_IN_PAD)


def _inproj_kernel(x_ref, mod_ref, g_ref, w_ref, hy_ref, gq_ref, ml_ref, rw_ref):
    h = _adaln(x_ref[...], g_ref[...], mod_ref, 1).astype(BF16)
    p = _dot(h, w_ref[...])
    for ref, lo, hi in zip((hy_ref, gq_ref, ml_ref, rw_ref), _IN_OFFS[:-1], _IN_OFFS[1:]):
        ref[...] = p[:, lo:hi]


def _inproj(x, mod, g, w):
    bm, rows, d = x.shape
    tm = min(ROW_TILE, rows)
    widths = [hi - lo for lo, hi in zip(_IN_OFFS[:-1], _IN_OFFS[1:])]
    return pl.pallas_call(
        _inproj_kernel,
        grid=(bm, rows // tm),
        in_specs=[
            pl.BlockSpec((None, tm, d), lambda b, i: (b, i, 0)),
            pl.BlockSpec((None, N_MOD, d), lambda b, i: (b, 0, 0)),
            _const_spec((1, d)),
            _const_spec((d, D_IN_PAD)),
        ],
        out_specs=[pl.BlockSpec((None, tm, wd), lambda b, i: (b, i, 0)) for wd in widths],
        out_shape=[jax.ShapeDtypeStruct((bm, rows, wd), F32) for wd in widths],
        compiler_params=_cparams(("arbitrary", "arbitrary")),
        name="inproj",
    )(x, mod, g, w)


def _outproj_kernel(x_ref, mod_ref, yh_ref, yg_ref, ym_ref, yr_ref, w_ref, o_ref):
    y = jnp.concatenate([yh_ref[...], yg_ref[...], ym_ref[...], yr_ref[...]], axis=-1)
    gate = mod_ref[pl.ds(5, 1), :]
    o_ref[...] = x_ref[...] + gate * _dot(y, w_ref[...])


def _outproj(x, mod, ys, w):
    bm, rows, d = x.shape
    tm = min(ROW_TILE, rows)
    row_spec = pl.BlockSpec((None, tm, d), lambda b, i: (b, i, 0))
    y_spec = pl.BlockSpec((None, tm, 256), lambda b, i: (b, i, 0))
    return pl.pallas_call(
        _outproj_kernel,
        grid=(bm, rows // tm),
        in_specs=[row_spec, pl.BlockSpec((None, N_MOD, d), lambda b, i: (b, 0, 0)),
                  y_spec, y_spec, y_spec, y_spec, _const_spec(w.shape)],
        out_specs=row_spec,
        out_shape=jax.ShapeDtypeStruct(x.shape, F32),
        compiler_params=_cparams(("arbitrary", "arbitrary")),
        name="outproj",
    )(x, mod, *ys, w)


def _hy_features(length):
    t01 = np.linspace(0.0, 1.0, length, dtype=np.float32)[:, None]
    bands = (HY_EMB - 1) // 2
    w_ang = (np.float32(2.0 * math.pi) * np.arange(length, dtype=np.float32)[:, None]
             / np.float32(length)).astype(np.float32)
    f = np.linspace(1e-4, bands - 1, bands, dtype=np.float32)[None]
    arg = (f * w_ang).astype(np.float32)
    z = np.concatenate([t01, np.cos(arg), -np.sin(arg)], axis=-1).astype(np.float32)
    zp = np.zeros((length, LANES), np.float32)
    zp[:, :HY_EMB] = z
    return zp


def _hy_deltas():
    max_decay = math.log(HY_TARGET) / HY_FAST_PCT
    min_decay = math.log(HY_TARGET) / HY_SLOW_PCT
    d = np.abs(np.linspace(min_decay, max_decay, HY_CH, dtype=np.float32))
    return np.tile(d, 2)[None].astype(np.float32)


def _hyfilt_kernel(z_ref, w1_ref, b1_ref, w2_ref, b2_ref, w3_ref, b3_ref, w4_ref, fr_ref,
                   dl_ref, o_ref):
    z = z_ref[...]
    fr = fr_ref[...]
    h = jnp.sin(fr * (_dot(z, w1_ref[...], HI) + b1_ref[...]))
    h = jnp.sin(fr * (_dot(h, w2_ref[...], HI) + b2_ref[...]))
    h = jnp.sin(fr * (_dot(h, w3_ref[...], HI) + b3_ref[...]))
    h = _dot(h, w4_ref[...], HI)
    o_ref[...] = h * jnp.exp(-z[:, 0:1] * dl_ref[...])


def _hy_filters(length, w1p, b1, w2, b2, w3, b3, w4, freq):
    z = jnp.asarray(_hy_features(length))
    dl = jnp.asarray(_hy_deltas())
    tl = min(ROW_TILE, length)
    consts = (w1p, b1, w2, b2, w3, b3, w4, freq, dl)
    return pl.pallas_call(
        _hyfilt_kernel,
        grid=(length // tl,),
        in_specs=[pl.BlockSpec((tl, LANES), lambda i: (i, 0))] + [_const_spec(a.shape) for a in consts],
        out_specs=pl.BlockSpec((tl, 2 * HY_CH), lambda i: (i, 0)),
        out_shape=jax.ShapeDtypeStruct((length, 2 * HY_CH), F32),
        compiler_params=_cparams(("arbitrary",)),
        name="hyena_filters",
    )(z, *consts)


FFT_SLAB_PAD = 8
FFT_GROUP = 4


def _fft_factors(length):
    return (128, 64) if length >= 2048 else (2 * length, 1)


def _fft_layout(n1, n2):
    return (n1, n2, 2 * n2 + FFT_SLAB_PAD) if n2 > 1 else (1, n1, 2 * n1)


def _stack3(t):
    hi = t.astype(jnp.bfloat16)
    lo = (t - hi.astype(np.float64)).astype(jnp.bfloat16)
    return np.concatenate([hi, lo, hi], axis=-1)


def _rows3(d):
    hi = d.astype(BF16)
    lo = (d - hi.astype(F32)).astype(BF16)
    return jnp.concatenate([hi, hi, lo], axis=0)


@functools.lru_cache(maxsize=None)
def _fft_tables(length):
    n1, n2 = _fft_factors(length)
    n = n1 * n2
    assert n == 2 * length
    a_n2 = np.arange(n2)[:, None, None]
    a_k1 = np.arange(n1)[None, :, None]
    a_n1 = np.arange(n1 // 2)[None, None, :]
    ang = 2.0 * np.pi * ((a_n1 * a_k1 % n1) / n1 + (a_n2 * a_k1 % n) / n)
    t_re, t_im = np.cos(ang), -np.sin(ang)
    ta = np.concatenate([t_re, t_im], axis=1)
    tai = np.concatenate([np.transpose(t_re, (0, 2, 1)), np.transpose(t_im, (0, 2, 1))], axis=2) / n
    jj = np.arange(n2)
    ang2 = 2.0 * np.pi * (np.outer(jj, jj) % n2) / n2
    c, s = np.cos(ang2), np.sin(ang2)
    fb = np.block([[c, s], [-s, c]])
    fbi = np.block([[c, -s], [s, c]])
    return tuple(_stack3(t) for t in (ta, tai, fb, fbi))


def _fft_stage_a(src_ref, fbuf_ref, ta_ref, n1, n2):
    if n2 == 1:
        fbuf_ref[...] = _dot(ta_ref[0], _rows3(src_ref[...]))
        return
    pitch = _fft_layout(n1, n2)[2]

    def body(i, carry):
        js = [i * FFT_GROUP + g for g in range(FFT_GROUP)]
        rows = [_rows3(src_ref[pl.ds(j, n1 // 2, stride=n2), :]) for j in js]
        ts = [_dot(ta_ref[j], r) for j, r in zip(js, rows)]
        for j, t in zip(js, ts):
            fbuf_ref[pl.ds(j, n1, stride=pitch), :] = t[:n1]
            fbuf_ref[pl.ds(n2 + j, n1, stride=pitch), :] = t[n1:]
        return carry

    lax.fori_loop(0, n2 // FFT_GROUP, body, 0)


def _spec_kernel(hf_ref, hb_ref, bias_ref, ta_ref, fb_ref, o_ref, fbuf_ref, *, n1, n2):
    nslab, hs, pitch = _fft_layout(n1, n2)
    if pitch > 2 * hs:
        o_ref[...] = jnp.zeros(o_ref.shape, F32)
    _fft_stage_a(hf_ref, o_ref, ta_ref, n1, n2)
    _fft_stage_a(hb_ref, fbuf_ref, ta_ref, n1, n2)
    is_re = lax.broadcasted_iota(jnp.int32, (2 * hs, 1), 0) < hs
    sgn = jnp.where(is_re, 1.0, -1.0).astype(F32)
    skip = jnp.where(is_re, bias_ref[...], 0.0)

    grp = FFT_GROUP if n2 > 1 else 1

    def stage_b(i, carry):
        sls = [pl.ds(pl.multiple_of((i * grp + g) * pitch, 8), 2 * hs) for g in range(grp)]
        a = [o_ref[sl, :] for sl in sls]
        b = [fbuf_ref[sl, :] for sl in sls]
        if n2 > 1:
            a = [_dot(fb_ref[...], _rows3(x)) for x in a]
            b = [_dot(fb_ref[...], _rows3(x)) for x in b]
        for sl, x, y in zip(sls, a, b):
            o_ref[sl, :] = x + sgn * y + skip
        return carry

    lax.fori_loop(0, nslab // grp, stage_b, 0)


def _hy_spectrum(h, bias):
    length = h.shape[0]
    n1, n2 = _fft_factors(length)
    nslab, _, pitch = _fft_layout(n1, n2)
    ta, _, fb, _ = (jnp.asarray(t) for t in _fft_tables(length))
    nblk = HY_CH // LANES
    return pl.pallas_call(
        functools.partial(_spec_kernel, n1=n1, n2=n2),
        grid=(nblk,),
        in_specs=[pl.BlockSpec((length, LANES), lambda j: (0, j)),
                  pl.BlockSpec((length, LANES), lambda j: (0, nblk + j)),
                  pl.BlockSpec((1, LANES), lambda j: (0, j)),
                  _const_spec(ta.shape), _const_spec(fb.shape)],
        out_specs=pl.BlockSpec((nslab * pitch, LANES), lambda j: (0, j)),
        out_shape=jax.ShapeDtypeStruct((nslab * pitch, HY_CH), F32),
        scratch_shapes=[pltpu.VMEM((nslab * pitch, LANES), F32)],
        compiler_params=_cparams(("arbitrary",)),
        name="hyena_spectrum",
    )(h, h, bias, ta, fb)


def _shift_rows(x, prev_row, next_row):
    n = x.shape[0]
    row = lax.broadcasted_iota(jnp.int32, x.shape, 0)
    up = jnp.where(row == 0, prev_row, pltpu.roll(x, 1, 0))
    dn = jnp.where(row == n - 1, next_row, pltpu.roll(x, n - 1, 0))
    return up, dn


def _hyconv_kernel(x1_ref, x2_ref, v_ref, cw1_ref, cw2_ref, cwv_ref, cb_ref, g_ref,
                   ta_ref, tai_ref, fb_ref, fbi_ref, o_ref, u_ref, fbuf_ref, *, n1, n2):
    def conv3(ref, w_ref, b):
        x = ref[...]
        up, dn = _shift_rows(x, 0.0, 0.0)
        return up * w_ref[0:1, :] + x * w_ref[1:2, :] + dn * w_ref[2:3, :] + b

    cb = cb_ref[...]
    u_ref[...] = conv3(x1_ref, cw1_ref, cb[0:1, :]) * conv3(v_ref, cwv_ref, cb[2:3, :])

    _fft_stage_a(u_ref, fbuf_ref, ta_ref, n1, n2)
    nslab, hs, pitch = _fft_layout(n1, n2)

    grp = FFT_GROUP if n2 > 1 else 1

    def cmul(a, g):
        ar, ai, gr, gi = a[:hs], a[hs:], g[:hs], g[hs:]
        return jnp.concatenate([ar * gr - ai * gi, ar * gi + ai * gr], axis=0)

    def stage_b(i, carry):
        sls = [pl.ds(pl.multiple_of((i * grp + g) * pitch, 8), 2 * hs) for g in range(grp)]
        a = [fbuf_ref[sl, :] for sl in sls]
        if n2 > 1:
            a = [_dot(fb_ref[...], _rows3(x)) for x in a]
        a = [cmul(x, g_ref[sl, :]) for x, sl in zip(a, sls)]
        if n2 > 1:
            a = [_dot(fbi_ref[...], _rows3(x)) for x in a]
        for sl, x in zip(sls, a):
            fbuf_ref[sl, :] = x
        return carry

    lax.fori_loop(0, nslab // grp, stage_b, 0)

    if n2 == 1:
        u_ref[...] = _dot(tai_ref[0], _rows3(fbuf_ref[...]))
    else:
        def stage_a_inv(i, carry):
            js = [i * FFT_GROUP + g for g in range(FFT_GROUP)]
            cs = [_rows3(jnp.concatenate([fbuf_ref[pl.ds(j, n1, stride=pitch), :],
                                          fbuf_ref[pl.ds(n2 + j, n1, stride=pitch), :]], axis=0))
                  for j in js]
            ys = [_dot(tai_ref[j], c) for j, c in zip(js, cs)]
            for j, y in zip(js, ys):
                u_ref[pl.ds(j, n1 // 2, stride=n2), :] = y
            return carry

        lax.fori_loop(0, n2 // FFT_GROUP, stage_a_inv, 0)

    o_ref[...] = (conv3(x2_ref, cw2_ref, cb[1:2, :]) * u_ref[...]).astype(o_ref.dtype)


def _hy_conv(p, conv_w, conv_b, spec):
    b, length, _ = p.shape
    n1, n2 = _fft_factors(length)
    nslab, _, pitch = _fft_layout(n1, n2)
    ta, tai, fb, fbi = (jnp.asarray(t) for t in _fft_tables(length))
    nblk = HY_CH // LANES
    cb3 = conv_b.reshape(3, HY_CH)
    col = lambda g: pl.BlockSpec((None, length, LANES), lambda j, i, g=g: (i, 0, g * nblk + j))
    wcol = lambda g: pl.BlockSpec((3, LANES), lambda j, i, g=g: (0, g * nblk + j))
    return pl.pallas_call(
        functools.partial(_hyconv_kernel, n1=n1, n2=n2),
        grid=(nblk, b),
        in_specs=[col(0), col(1), col(2), wcol(0), wcol(1), wcol(2),
                  pl.BlockSpec((3, LANES), lambda j, i: (0, j)),
                  pl.BlockSpec((nslab * pitch, LANES), lambda j, i: (0, j), pipeline_mode=pl.Buffered(1)),
                  _const_spec(ta.shape), _const_spec(tai.shape), _const_spec(fb.shape),
                  _const_spec(fbi.shape)],
        out_specs=pl.BlockSpec((None, length, LANES), lambda j, i: (i, 0, j)),
        out_shape=jax.ShapeDtypeStruct((b, length, HY_CH), BF16),
        scratch_shapes=[pltpu.VMEM((length, LANES), F32), pltpu.VMEM((nslab * pitch, LANES), F32)],
        compiler_params=_cparams(("arbitrary", "arbitrary")),
        name="hyena_conv",
    )(p, p, p, conv_w, conv_w, conv_w, cb3, spec, ta, tai, fb, fbi)


def _rope_tables(length, d_rot, lane_lo, head_w):
    rows = length // GRID_W
    row = np.repeat(np.arange(rows, dtype=np.float32), GRID_W)
    colv = np.tile(np.arange(GRID_W, dtype=np.float32), rows)
    n_freq = d_rot // 4
    inv = (np.float32(ROPE_THETA) ** (-np.arange(n_freq, dtype=np.float32) / np.float32(n_freq))).astype(np.float32)
    ang = np.concatenate([row[:, None] * inv, colv[:, None] * inv], axis=-1).astype(np.float32)
    cos_t = np.ones((length, LANES), np.float32)
    sin_t = np.zeros((length, LANES), np.float32)
    c, s = np.cos(ang), np.sin(ang)
    for base in range(0, LANES, head_w):
        for i in range(d_rot // 2):
            cos_t[:, base + lane_lo + 2 * i] = c[:, i]
            cos_t[:, base + lane_lo + 2 * i + 1] = c[:, i]
            sin_t[:, base + lane_lo + 2 * i] = -s[:, i]
            sin_t[:, base + lane_lo + 2 * i + 1] = s[:, i]
    return cos_t, sin_t


def _rope(x, cos_t, sin_t):
    lane = lax.broadcasted_iota(jnp.int32, x.shape, 1)
    w = x.shape[1]
    partner = jnp.where(jnp.bitwise_and(lane, 1) == 0, pltpu.roll(x, w - 1, 1), pltpu.roll(x, 1, 1))
    return x * cos_t + partner * sin_t


def _store_vt(vt_ref, h, vt):
    row = lax.broadcasted_iota(jnp.int32, vt.shape, 0)
    vt_ref[h, 0:HEAD_DIM, :] = vt.astype(BF16)
    vt_ref[h, HEAD_DIM:2 * HEAD_DIM, :] = jnp.where(row == 0, 1.0, 0.0).astype(BF16)


def _gqa_prep_kernel(p_ref, cos_ref, sin_ref, gq_ref, gk_ref, seg_ref, q_ref, k_ref, vt_ref, *, rope):
    p = p_ref[...]
    seg = seg_ref[...]
    lane = lax.broadcasted_iota(jnp.int32, (p.shape[0], LANES), 1)
    low = lane < HEAD_DIM

    def hnorm(x, g):
        ms = _dot(x * x, seg, HI)
        return x * lax.rsqrt(ms + NORM_EPS) * g

    scale = HEAD_DIM ** -0.5 * LOG2E
    for c in range(2):
        q = hnorm(p[:, c * LANES:(c + 1) * LANES], gq_ref[...])
        if rope:
            q = _rope(q, cos_ref[...], sin_ref[...])
        q = q * scale
        qs = pltpu.roll(q, HEAD_DIM, 1)
        q_ref[:, (2 * c) * LANES:(2 * c + 1) * LANES] = jnp.where(low, q, 0.0).astype(BF16)
        q_ref[:, (2 * c + 1) * LANES:(2 * c + 2) * LANES] = jnp.where(low, qs, 0.0).astype(BF16)
    k = hnorm(p[:, 256:384], gk_ref[...])
    if rope:
        k = _rope(k, cos_ref[...], sin_ref[...])
    ks = pltpu.roll(k, HEAD_DIM, 1)
    k_ref[:, 0:LANES] = jnp.where(low, k, 0.0).astype(BF16)
    k_ref[:, LANES:2 * LANES] = jnp.where(low, ks, 0.0).astype(BF16)
    vt = p[:, 384:512].T
    _store_vt(vt_ref, 0, vt[0:HEAD_DIM])
    _store_vt(vt_ref, 1, vt[HEAD_DIM:2 * HEAD_DIM])


def _seg_matrix(width, seg, value):
    i = np.arange(width)
    return ((i[:, None] // seg) == (i[None, :] // seg)).astype(np.float32) * np.float32(value)


def _gqa_prep(p, gq, gk, rope):
    b, length, _ = p.shape
    tm = min(ROW_TILE, length)
    if rope:
        cos_t, sin_t = (jnp.asarray(t) for t in _rope_tables(length, HEAD_DIM, 0, HEAD_DIM))
    else:
        cos_t = sin_t = jnp.zeros((length, LANES), F32)
    seg = jnp.asarray(_seg_matrix(LANES, HEAD_DIM, 1.0 / HEAD_DIM))
    tab = pl.BlockSpec((tm, LANES), lambda i, bb: (i, 0))
    outw = (512, 256)
    return pl.pallas_call(
        functools.partial(_gqa_prep_kernel, rope=rope),
        grid=(length // tm, b),
        in_specs=[pl.BlockSpec((None, tm, GQA_COLS), lambda i, bb: (bb, i, 0)), tab, tab,
                  _const_spec((1, LANES)), _const_spec((1, LANES)), _const_spec((LANES, LANES))],
        out_specs=[pl.BlockSpec((None, tm, w), lambda i, bb: (bb, i, 0)) for w in outw]
                  + [pl.BlockSpec((None, GQA_KV_HEADS, LANES, tm), lambda i, bb: (bb, 0, 0, i))],
        out_shape=[jax.ShapeDtypeStruct((b, length, w), BF16) for w in outw]
                  + [jax.ShapeDtypeStruct((b, GQA_KV_HEADS, LANES, length), BF16)],
        compiler_params=_cparams(("arbitrary", "arbitrary")),
        name="gqa_prep",
    )(p, cos_t, sin_t, gq, gk, seg)


def _mla_prep_kernel(p_ref, cos_ref, sin_ref, cqn_ref, ckvn_ref, wuq_ref, wuk_ref, wuv_ref,
                     qn_ref, kn_ref, q_ref, k_ref, vt_ref, *, rope):
    p = p_ref[...]

    def rms(x, g):
        return x * lax.rsqrt(jnp.mean(x * x, axis=-1, keepdims=True) + NORM_EPS) * g

    cq = rms(p[:, 0:MLA_Q_RANK], cqn_ref[...]).astype(BF16)
    ckv = rms(p[:, MLA_Q_RANK:MLA_Q_RANK + MLA_KV_RANK], ckvn_ref[...]).astype(BF16)
    q = _dot(cq, wuq_ref[...])
    kn = _dot(ckv, wuk_ref[...])
    v = _dot(ckv, wuv_ref[...])
    for c in range(MLA_HEADS // 2):
        vt = v[:, c * LANES:(c + 1) * LANES].T
        _store_vt(vt_ref, 2 * c, vt[0:MLA_V])
        _store_vt(vt_ref, 2 * c + 1, vt[MLA_V:2 * MLA_V])
    lane = lax.broadcasted_iota(jnp.int32, (p.shape[0], LANES), 1)
    in_rope = jnp.logical_and(lane >= MLA_NOPE, lane < MLA_QK)
    kr = jnp.where(in_rope, pltpu.roll(p[:, 384:512], MLA_NOPE, 1), 0.0)
    scale = MLA_QK ** -0.5 * LOG2E

    def hnorm(x, g):
        ms = jnp.sum(x * x, axis=-1, keepdims=True) * (1.0 / MLA_QK)
        return x * lax.rsqrt(ms + NORM_EPS) * g

    for h in range(MLA_HEADS):
        sl = slice(h * LANES, (h + 1) * LANES)
        qh = hnorm(q[:, sl], qn_ref[...])
        kh = hnorm(kn[:, sl] + kr, kn_ref[...])
        if rope:
            qh = _rope(qh, cos_ref[...], sin_ref[...])
            kh = _rope(kh, cos_ref[...], sin_ref[...])
        q_ref[:, sl] = (qh * scale).astype(BF16)
        k_ref[:, sl] = kh.astype(BF16)


def _mla_prep(p, cqn, ckvn, wuq, wuk, wuv, qn, kn, rope):
    b, length, _ = p.shape
    tm = min(ROW_TILE, length)
    if rope:
        cos_t, sin_t = (jnp.asarray(t) for t in _rope_tables(length, MLA_ROPE, MLA_NOPE, LANES))
    else:
        cos_t = sin_t = jnp.zeros((length, LANES), F32)
    tab = pl.BlockSpec((tm, LANES), lambda i, bb: (i, 0))
    consts = (cqn, ckvn, wuq, wuk, wuv, qn, kn)
    outw = (512, 512)
    return pl.pallas_call(
        functools.partial(_mla_prep_kernel, rope=rope),
        grid=(length // tm, b),
        in_specs=[pl.BlockSpec((None, tm, MLA_COLS_PAD), lambda i, bb: (bb, i, 0)), tab, tab]
                 + [_const_spec(a.shape) for a in consts],
        out_specs=[pl.BlockSpec((None, tm, w), lambda i, bb: (bb, i, 0)) for w in outw]
                  + [pl.BlockSpec((None, MLA_HEADS, LANES, tm), lambda i, bb: (bb, 0, 0, i))],
        out_shape=[jax.ShapeDtypeStruct((b, length, w), BF16) for w in outw]
                  + [jax.ShapeDtypeStruct((b, MLA_HEADS, LANES, length), BF16)],
        compiler_params=_cparams(("arbitrary", "arbitrary")),
        name="mla_prep",
    )(p, cos_t, sin_t, *consts)


def _attn_kernel(*refs, n_seg):
    q_ref, o_ref = refs[0], refs[-1]
    segs = [refs[1 + 4 * s:5 + 4 * s] for s in range(n_seg)]
    q = q_ref[...]
    qh = [q[:, side * LANES:(side + 1) * LANES] for side in range(2)]
    pieces = [(side, seg, c0, min(KEY_CHUNK, seg[0].shape[0]))
              for side in range(2) for seg in segs
              for c0 in range(0, seg[0].shape[0], min(KEY_CHUNK, seg[0].shape[0]))]
    scores = [_dot_nt(seg[side][c0:c0 + n, :], qh[side]) for side, seg, c0, n in pieces]
    mx = [jnp.max(s, axis=0, keepdims=True) for s in scores]
    ps = [jnp.exp2(s - m).astype(BF16) for s, m in zip(scores, mx)]
    os_ = [_dot(seg[2 + side][:, c0:c0 + n], e) for (side, seg, c0, n), e in zip(pieces, ps)]
    outs = []
    for side in range(2):
        idx = [i for i, pc in enumerate(pieces) if pc[0] == side]
        m = functools.reduce(jnp.maximum, [mx[i] for i in idx])
        o = functools.reduce(jnp.add, [os_[i] * jnp.exp2(mx[i] - m) for i in idx])
        outs.append(o[0:HEAD_DIM] / o[HEAD_DIM:HEAD_DIM + 1])
    o_ref[...] = jnp.concatenate(outs, axis=0).T.astype(o_ref.dtype)


def _attention(q, kv_segs, k_heads):
    b, lq, _ = q.shape
    tq = min(Q_TILE, lq)
    ka = (lambda pr: pr) if k_heads == 2 else (lambda pr: 2 * pr)
    kb = (lambda pr: pr) if k_heads == 2 else (lambda pr: 2 * pr + 1)
    in_specs = [pl.BlockSpec((None, tq, 2 * LANES), lambda bb, pr, i: (bb, i, pr))]
    args = [q]
    for k, vt in kv_segs:
        lk = k.shape[1]
        in_specs += [pl.BlockSpec((None, lk, LANES), lambda bb, pr, i, f=ka: (bb, 0, f(pr))),
                     pl.BlockSpec((None, lk, LANES), lambda bb, pr, i, f=kb: (bb, 0, f(pr))),
                     pl.BlockSpec((None, None, LANES, lk), lambda bb, pr, i, f=ka: (bb, f(pr), 0, 0)),
                     pl.BlockSpec((None, None, LANES, lk), lambda bb, pr, i, f=kb: (bb, f(pr), 0, 0))]
        args += [k, k, vt, vt]
    return pl.pallas_call(
        functools.partial(_attn_kernel, n_seg=len(kv_segs)),
        grid=(b, 2, lq // tq),
        in_specs=in_specs,
        out_specs=pl.BlockSpec((None, tq, LANES), lambda bb, pr, i: (bb, i, pr)),
        out_shape=jax.ShapeDtypeStruct((b, lq, 2 * LANES), BF16),
        compiler_params=_cparams(("arbitrary", "arbitrary", "arbitrary")),
        name="attention",
    )(*args)


def _rw_prep_kernel(p_ref, prev_ref, next_ref, mu_ref, kk_ref_w, ka_ref, rk_ref, w0_ref, w2_ref,
                    a0_ref, a2_ref, g2_ref, seg_ref,
                    r_ref, v_ref, kk_ref, lw_ref, kd_ref, bd_ref, bonus_ref, g_ref):
    i = pl.program_id(1)
    n = pl.num_programs(1)
    x = p_ref[...]
    prev_row = jnp.where(i > 0, prev_ref[7:8, :], 0.0)
    next_row = jnp.where(i < n - 1, next_ref[0:1, :], 0.0)
    up, dn = _shift_rows(x, prev_row, next_row)
    xs = x + mu_ref[...] * (0.5 * (up + dn) - x)
    c = RW_C
    r, k, v = xs[:, 0:c], xs[:, c:2 * c], xs[:, 2 * c:3 * c]
    xw, xa, xg = xs[:, 3 * c:3 * c + 128], xs[:, 3 * c + 128:3 * c + 256], xs[:, 3 * c + 256:3 * c + 384]
    seg = seg_ref[...]
    kk = k * kk_ref_w[...]
    nrm = jnp.sqrt(_dot(kk * kk, seg, HI))
    kk = kk / jnp.maximum(nrm, 1e-12)
    u = w0_ref[...] + _dot(jnp.tanh(xw), w2_ref[...], HI)
    z = -u
    softplus = jnp.maximum(z, 0.0) + jnp.log(1.0 + jnp.exp(-jnp.abs(z)))
    lw_ref[...] = -jnp.exp(-softplus - 0.5)
    a = _sigmoid(a0_ref[...] + _dot(xa, a2_ref[...], HI))
    bonus = jnp.zeros_like(r)
    for d in range(2):
        a_d = a[:, d * c:(d + 1) * c]
        k_d = k * (1.0 + (a_d - 1.0) * ka_ref[...])
        kd_ref[:, d * c:(d + 1) * c] = k_d
        bd_ref[:, d * c:(d + 1) * c] = kk * a_d
        bonus = bonus + _dot(r * k_d * rk_ref[...], seg, HI)
    r_ref[...] = r
    v_ref[...] = v
    kk_ref[...] = kk
    bonus_ref[...] = bonus * v
    g_ref[...] = _dot(_sigmoid(xg), g2_ref[...], HI)


def _rw_prep(p, mu, k_k, k_a, r_k, w0, w2bd, a0, a2bd, g2):
    b, length, cols = p.shape
    tm = min(ROW_TILE, length)
    nb = tm // 8
    last = length // 8 - 1
    seg = jnp.asarray(_seg_matrix(RW_C, RW_N, 1.0))
    consts = (mu, k_k, k_a, r_k, w0, w2bd, a0, a2bd, g2, seg)
    outw = (RW_C, RW_C, RW_C, 2 * RW_C, 2 * RW_C, 2 * RW_C, RW_C, RW_C)
    return pl.pallas_call(
        _rw_prep_kernel,
        grid=(b, length // tm),
        in_specs=[pl.BlockSpec((None, tm, cols), lambda bb, i: (bb, i, 0)),
                  pl.BlockSpec((None, 8, cols), lambda bb, i: (bb, jnp.maximum(i * nb - 1, 0), 0)),
                  pl.BlockSpec((None, 8, cols), lambda bb, i: (bb, jnp.minimum((i + 1) * nb, last), 0))]
                 + [_const_spec(a.shape) for a in consts],
        out_specs=[pl.BlockSpec((None, tm, w), lambda bb, i: (bb, i, 0)) for w in outw],
        out_shape=[jax.ShapeDtypeStruct((b, length, w), F32) for w in outw],
        compiler_params=_cparams(("arbitrary", "arbitrary")),
        name="rwkv_prep",
    )(p, p, p, *consts)


@functools.lru_cache(maxsize=None)
def _rw_masks():
    cs, n = RW_CHUNK, RW_C
    i = np.arange(n)
    same = (i[:, None] // cs) == (i[None, :] // cs)
    rel = (i[None, :] % cs) - (i[:, None] % cs)
    masks = np.stack([same & (rel < 0), same & (rel <= 0), same & (rel > 0), same & (rel >= 0),
                      same, i[:, None] == i[None, :]]).astype(np.float32)
    t = np.arange(cs)
    tri = np.stack([t[None, :] <= t[:, None], t[None, :] >= t[:, None]]).astype(np.float32)
    return masks, tri


def _rw_scan_kernel(rf_ref, rb_ref, vf_ref, vb_ref, kkf_ref, kkb_ref, lwf_ref, lwb_ref, kdf_ref, kdb_ref,
                    bf_ref, bb_ref, s0_ref, mask_ref, tri_ref, yf_ref, yb_ref, sfin_ref, s_ref,
                    *, n_chunks, n_batch):
    cs = RW_CHUNK

    @pl.when(pl.program_id(1) == 0)
    def _():
        s_ref[...] = s0_ref[...]

    same_head = mask_ref[4].astype(BF16)
    eye = mask_ref[5]
    dirs = ((rf_ref, vf_ref, kkf_ref, lwf_ref, kdf_ref, bf_ref, yf_ref),
            (rb_ref, vb_ref, kkb_ref, lwb_ref, kdb_ref, bb_ref, yb_ref))

    def tile(x):
        xb = x.astype(BF16)
        return jnp.concatenate([xb, xb, xb, xb], axis=0)

    def bd(x):
        return tile(x) * same_head

    streams = [(bi, d) for bi in range(n_batch) for d in range(2)]

    def each(f, *cols):
        return [f(*a) for a in zip(*cols)] if cols else [f(bi, d) for bi, d in streams]

    bf = lambda xs: [x.astype(BF16) for x in xs]

    def body(c, carry):
        def load(bi, d):
            cc = c if d == 0 else n_chunks - 1 - c
            sl = pl.ds(pl.multiple_of(cc * cs, cs), cs)
            return [ref[bi, sl, :] for ref in dirs[d][:6]] + [sl]

        r, v, kk, lw, kd, b, sl = zip(*each(load))
        strict = [mask_ref[2 * d] for _, d in streams]
        incl = [mask_ref[2 * d + 1] for _, d in streams]
        cl = [_dot(tri_ref[d], x, HI) for (_, d), x in zip(streams, lw)]
        w_inv = [jnp.exp(-x) for x in cl]
        w_all = [jnp.exp(jnp.sum(x, axis=0, keepdims=True)) for x in lw]
        kt = each(lambda a, w: a * w, kd, w_inv)
        bt = each(lambda a, w: a * w, b, w_inv)
        qk_s = each(lambda a, x, y: bd(a * jnp.exp(x - y)), kk, cl, lw)
        rt_s = each(lambda a, x: bd(a * jnp.exp(x)), r, cl)
        kt_t, bt_t = each(tile, kt), each(tile, bt)
        a_kb = each(lambda a, t, m: _dot_nt(a, t) * m, qk_s, bt_t, strict)
        a_kk = each(lambda a, t, m: (_dot_nt(a, t) * m).astype(BF16), qk_s, kt_t, strict)
        a_rk = each(lambda a, t, m: (_dot_nt(a, t) * m).astype(BF16), rt_s, kt_t, incl)
        a_rb = each(lambda a, t, m: (_dot_nt(a, t) * m).astype(BF16), rt_s, bt_t, incl)
        lp = [-a for a in a_kb]
        t_inv = [eye + a for a in lp]
        m = 1
        while 2 * m < cs:
            lp = each(lambda a: _dot(a, a), bf(lp))
            t_inv = each(lambda t, tb, a: t + _dot(tb, a), t_inv, bf(t_inv), bf(lp))
            m *= 2
        t_inv = bf(t_inv)
        v_s = each(bd, v)
        akv = bf(each(_dot, a_kk, v_s))
        x1 = bf(each(_dot, t_inv, qk_s))
        x2 = bf(each(_dot, t_inv, akv))
        y1 = bf(each(lambda a, t, x: a.astype(F32) - _dot(t, x), rt_s, a_rb, x1))
        y2 = each(lambda ak, vs, ab, x: _dot(ak, vs) - _dot(ab, x), a_rk, v_s, a_rb, x2)
        bh_s = each(lambda a, w: bd(a * w), bt, w_all)
        kh_s = each(lambda a, w: bd(a * w), kt, w_all)
        m_c = bf(each(lambda w, x, bh: eye * w - _dot_tn(x, bh), w_all, x1, bh_s))
        n_c = each(lambda vs, kh, x, bh: _dot_tn(vs, kh) - _dot_tn(x, bh), v_s, kh_s, x2, bh_s)
        s0 = [s_ref[bi, d].astype(BF16) for bi, d in streams]
        y_bd = each(lambda a, s, y: _dot_nt(a, s) + y, y1, s0, y2)
        s1 = each(lambda s, mc, nc: _dot(s, mc) + nc, s0, m_c, n_c)
        for (bi, d), y, s, rows in zip(streams, y_bd, s1, sl):
            dirs[d][6][bi, rows, :] = y[0:cs] + y[cs:2 * cs] + y[2 * cs:3 * cs] + y[3 * cs:4 * cs]
            s_ref[bi, d] = s
        return carry

    lax.fori_loop(0, n_chunks, body, 0)
    sfin_ref[...] = s_ref[...]


def _rw_scan(r, v, kk, lw, kd, bdir, s0):
    b, length, c = r.shape
    tc = min(RW_BLOCK, length)
    nblk = length // tc
    nb = RW_SEQS
    masks, tri = (jnp.asarray(t) for t in _rw_masks())
    fwd = lambda col: pl.BlockSpec((nb, tc, c), lambda g, i: (g, i, col))
    bwd = lambda col: pl.BlockSpec((nb, tc, c), lambda g, i: (g, nblk - 1 - i, col))
    state = pl.BlockSpec((nb, 2, c, c), lambda g, i: (g, 0, 0, 0))
    return pl.pallas_call(
        functools.partial(_rw_scan_kernel, n_chunks=tc // RW_CHUNK, n_batch=nb),
        grid=(b // nb, nblk),
        in_specs=[fwd(0), bwd(0)] * 3 + [fwd(0), bwd(1)] * 3
                 + [state, _const_spec(masks.shape), _const_spec(tri.shape)],
        out_specs=[fwd(0), bwd(0), state],
        out_shape=[jax.ShapeDtypeStruct((b, length, c), F32), jax.ShapeDtypeStruct((b, length, c), F32),
                   jax.ShapeDtypeStruct((b, 2, c, c), F32)],
        scratch_shapes=[pltpu.VMEM((nb, 2, c, c), F32)],
        compiler_params=_cparams(("arbitrary", "arbitrary")),
        name="rwkv_scan",
    )(r, r, v, v, kk, kk, lw, lw, kd, kd, bdir, bdir, s0, masks, tri)


def _rw_readout_kernel(yf_ref, yb_ref, bonus_ref, g_ref, lnw_ref, lnb_ref, seg_ref, o_ref):
    y = yf_ref[...] + yb_ref[...]
    seg = seg_ref[...]
    mean = _dot(y, seg, HI)
    yc = y - mean
    var = _dot(yc * yc, seg, HI)
    yn = yc * lax.rsqrt(var + RW_GN_EPS) * lnw_ref[...] + lnb_ref[...]
    o_ref[...] = ((yn + bonus_ref[...]) * g_ref[...]).astype(o_ref.dtype)


def _rw_readout(y_f, y_b, bonus, g, ln_w, ln_b):
    b, length, c = y_f.shape
    tm = min(ROW_TILE, length)
    seg = jnp.asarray(_seg_matrix(RW_C, RW_N, 1.0 / RW_N))
    rowspec = pl.BlockSpec((None, tm, c), lambda bb, i: (bb, i, 0))
    return pl.pallas_call(
        _rw_readout_kernel,
        grid=(b, length // tm),
        in_specs=[rowspec, rowspec, rowspec, rowspec,
                  _const_spec((1, c)), _const_spec((1, c)), _const_spec((c, c))],
        out_specs=rowspec,
        out_shape=jax.ShapeDtypeStruct((b, length, c), BF16),
        compiler_params=_cparams(("arbitrary", "arbitrary")),
        name="rwkv_readout",
    )(y_f, y_b, bonus, g, ln_w, ln_b, seg)


def _blockdiag2(w):
    k, n = w.shape[1:]
    z = jnp.zeros((k, n), w.dtype)
    return jnp.concatenate([jnp.concatenate([w[0], z], axis=1), jnp.concatenate([z, w[1]], axis=1)], axis=0)


def _mla_head_pad(w, width):
    k = w.shape[0]
    w = w.reshape(k, MLA_HEADS, width)
    return jnp.pad(w, ((0, 0), (0, 0), (0, LANES - width))).reshape(k, MLA_HEADS * LANES)


def kernel(x, c, ctx, c_ctx, ada_w, ada_b, norm_ffn1, norm_mix, norm_ffn2, ffn1_gate, ffn1_up, ffn1_down, ffn2_gate, ffn2_up, ffn2_down, w_in, w_out, hy_conv_w, hy_conv_b, hy_f_w1, hy_f_b1, hy_f_w2, hy_f_b2, hy_f_w3, hy_f_b3, hy_f_w4, hy_f_freq, hy_bias, gqa_q_norm, gqa_k_norm, mla_cq_norm, mla_ckv_norm, mla_w_uq, mla_w_ukv, mla_q_norm, mla_k_norm, rw_mu, rw_w0, rw_w2, rw_a0, rw_a2, rw_g2, rw_k_k, rw_k_a, rw_r_k, rw_ln_w, rw_ln_b):
    b, length, d = x.shape
    lc = ctx.shape[1]
    depth = ada_w.shape[0]

    c_all = jnp.zeros((16, d), F32).at[:b].set(c).at[b].set(c_ctx)
    mod = _compute_mod(c_all, ada_w, ada_b)

    xc = ctx.reshape(1, b * lc, d)
    for l in range(depth):
        ctx_out = l < depth - 1
        mod_x = mod[l, :b].reshape(b, N_MOD, d)
        mod_c = mod[l, b:b + 1].reshape(1, N_MOD, d)
        row = lambda a: a[l].reshape(1, -1)

        wg1, wu1, wd1 = (w[l].astype(BF16) for w in (ffn1_gate, ffn1_up, ffn1_down))
        wg2, wu2, wd2 = (w[l].astype(BF16) for w in (ffn2_gate, ffn2_up, ffn2_down))
        wi = w_in[l]
        o1, o2, o3 = HY_COLS, HY_COLS + GQA_COLS, HY_COLS + GQA_COLS + MLA_COLS
        wi = jnp.concatenate([wi[:, :o3], jnp.zeros((d, MLA_COLS_PAD - MLA_COLS), F32), wi[:, o3:]],
                             axis=1).astype(BF16)
        wo = w_out[l].astype(BF16)

        x = _ffn(x, mod_x, row(norm_ffn1), wg1, wu1, wd1, 0)
        xc = _ffn(xc, mod_c, row(norm_ffn1), wg1, wu1, wd1, 0)

        hy_x, gq_x, ml_x, rw_x = _inproj(x, mod_x, row(norm_mix), wi)
        hy_c, gq_c, ml_c, rw_c = (t.reshape(b, lc, -1) for t in _inproj(xc, mod_c, row(norm_mix), wi))

        w1p = jnp.zeros((LANES, HY_ORDER), F32).at[:HY_EMB].set(hy_f_w1[l])
        filt = (w1p, row(hy_f_b1), hy_f_w2[l], row(hy_f_b2), hy_f_w3[l], row(hy_f_b3), hy_f_w4[l],
                row(hy_f_freq))
        spec_x = _hy_spectrum(_hy_filters(length, *filt), row(hy_bias))
        y_hy_x = _hy_conv(hy_x, hy_conv_w[l], hy_conv_b[l], spec_x)

        gq = jnp.tile(row(gqa_q_norm), (1, 2))
        gk = jnp.tile(row(gqa_k_norm), (1, 2))
        q_l, k_l, v_l = _gqa_prep(gq_x, gq, gk, True)
        q_c, k_c, v_c = _gqa_prep(gq_c, gq, gk, False)
        y_gq_x = _attention(q_l, [(k_c, v_c), (k_l, v_l)], GQA_KV_HEADS)

        wuq = _mla_head_pad(mla_w_uq[l], MLA_QK).astype(BF16)
        wukv = mla_w_ukv[l].reshape(MLA_KV_RANK, MLA_HEADS, MLA_NOPE + MLA_V)
        wuk = _mla_head_pad(wukv[:, :, :MLA_NOPE].reshape(MLA_KV_RANK, -1), MLA_NOPE).astype(BF16)
        wuv = wukv[:, :, MLA_NOPE:].reshape(MLA_KV_RANK, -1).astype(BF16)
        pad_n = lambda g: jnp.pad(g[l], (0, LANES - MLA_QK)).reshape(1, LANES)
        mla_w = (row(mla_cq_norm), row(mla_ckv_norm), wuq, wuk, wuv, pad_n(mla_q_norm), pad_n(mla_k_norm))
        mq_l, mk_l, mv_l = _mla_prep(ml_x, *mla_w, True)
        mq_c, mk_c, mv_c = _mla_prep(ml_c, *mla_w, False)
        y_ml_x = _attention(mq_l, [(mk_c, mv_c), (mk_l, mv_l)], MLA_HEADS)

        rw_w = (row(rw_mu), row(rw_k_k), row(rw_k_a), rw_r_k[l].reshape(1, RW_C),
                rw_w0[l].reshape(1, 2 * RW_C), _blockdiag2(rw_w2[l]),
                rw_a0[l].reshape(1, 2 * RW_C), _blockdiag2(rw_a2[l]), rw_g2[l])
        pc = _rw_prep(rw_c, *rw_w)
        px = _rw_prep(rw_x, *rw_w)
        zeros = jnp.zeros((b, 2, RW_C, RW_C), F32)
        yf_c, yb_c, s_ctx = _rw_scan(*pc[:6], zeros)
        yf_x, yb_x, _ = _rw_scan(*px[:6], s_ctx)
        y_rw_x = _rw_readout(yf_x, yb_x, px[6], px[7], row(rw_ln_w), row(rw_ln_b))

        x = _outproj(x, mod_x, (y_hy_x, y_gq_x, y_ml_x, y_rw_x), wo)
        if ctx_out:
            spec_c = _hy_spectrum(_hy_filters(lc, *filt), row(hy_bias))
            y_hy_c = _hy_conv(hy_c, hy_conv_w[l], hy_conv_b[l], spec_c)
            y_gq_c = _attention(q_c, [(k_c, v_c)], GQA_KV_HEADS)
            y_ml_c = _attention(mq_c, [(mk_c, mv_c)], MLA_HEADS)
            y_rw_c = _rw_readout(yf_c, yb_c, pc[6], pc[7], row(rw_ln_w), row(rw_ln_b))
            flat = lambda t: t.reshape(1, b * lc, -1)
            xc = _outproj(xc, mod_c, tuple(flat(t) for t in (y_hy_c, y_gq_c, y_ml_c, y_rw_c)), wo)

        x = _ffn(x, mod_x, row(norm_ffn2), wg2, wu2, wd2, 2)
        if ctx_out:
            xc = _ffn(xc, mod_c, row(norm_ffn2), wg2, wu2, wd2, 2)
    return x
```

```python
import functools
import math

import numpy as np
import jax
import jax.numpy as jnp
from jax import lax
from jax.experimental import pallas as pl
from jax.experimental.pallas import tpu as pltpu

F32 = jnp.float32
BF16 = jnp.bfloat16
HI = lax.Precision.HIGHEST

D_MODEL = 1024
GRID_W = 64
HEAD_DIM = 64
D_FF = 2816
N_MOD = 9
NORM_EPS = 1e-6
LOG2E = math.log2(math.e)
ROPE_THETA = 10000.0

HY_CH = 256
HY_EMB = 33
HY_ORDER = 64
HY_FAST_PCT = 0.3
HY_SLOW_PCT = 1.5
HY_TARGET = 1e-2

GQA_HEADS = 4
GQA_KV_HEADS = 2
MLA_HEADS = 4
MLA_NOPE = 64
MLA_ROPE = 32
MLA_QK = MLA_NOPE + MLA_ROPE
MLA_V = 64
MLA_Q_RANK = 256
MLA_KV_RANK = 128

RW_HEADS = 4
RW_N = 64
RW_C = RW_HEADS * RW_N
RW_DECAY_LORA = 64
RW_AAA_LORA = 64
RW_GATE_LORA = 128
RW_GN_EPS = 64e-5

HY_COLS = 3 * HY_CH
GQA_COLS = (GQA_HEADS + 2 * GQA_KV_HEADS) * HEAD_DIM
MLA_COLS = MLA_Q_RANK + MLA_KV_RANK + MLA_ROPE
MLA_COLS_PAD = 512
RW_COLS = 3 * RW_C + 2 * RW_DECAY_LORA + 2 * RW_AAA_LORA + RW_GATE_LORA
D_IN_PAD = HY_COLS + GQA_COLS + MLA_COLS_PAD + RW_COLS

LANES = 128
ROW_TILE = 512
Q_TILE = 512
KEY_CHUNK = 256
RW_CHUNK = 64
RW_BASE = 4
RW_LEVELS = 4
RW_BLOCK = 128
RW_SEQS = 4
VMEM_LIMIT = 56 * 2 ** 20


def _cparams(sem):
    return pltpu.CompilerParams(dimension_semantics=sem, vmem_limit_bytes=VMEM_LIMIT)


def _const_spec(shape):
    nd = len(shape)
    return pl.BlockSpec(shape, lambda *_: (0,) * nd, pipeline_mode=pl.Buffered(1))


def _dot(a, b, precision=None):
    return jnp.dot(a, b, preferred_element_type=F32, precision=precision)


def _dot_nt(a, b, precision=None):
    return lax.dot_general(a, b, (((1,), (1,)), ((), ())), preferred_element_type=F32,
                           precision=precision)


def _dot_tn(a, b, precision=None):
    return lax.dot_general(a, b, (((0,), (0,)), ((), ())), preferred_element_type=F32,
                           precision=precision)


def _pieces(x, n):
    out = []
    for _ in range(n):
        p = x.astype(BF16)
        out.append(p)
        x = x - p.astype(F32)
    return out


def _segsum(x, seg):
    return functools.reduce(jnp.add, [_dot(p, seg) for p in _pieces(x, 2)])


def _sigmoid(x):
    return 1.0 / (1.0 + jnp.exp(-x))


def _adaln(x, g, mod_ref, i):
    shift = mod_ref[pl.ds(3 * i, 1), :]
    scale = mod_ref[pl.ds(3 * i + 1, 1), :]
    r = lax.rsqrt(jnp.mean(x * x, axis=-1, keepdims=True) + NORM_EPS)
    return (x * r) * (g * (1.0 + scale)) + shift


def _mod_kernel(c_ref, w_ref, b_ref, o_ref):
    c = c_ref[...]
    s = c * _sigmoid(c)
    o_ref[...] = _dot(s, w_ref[...], HI) + b_ref[...]


def _compute_mod(c_all, ada_w, ada_b):
    depth, d, n = ada_w.shape
    rows = c_all.shape[0]
    tn = 1024
    return pl.pallas_call(
        _mod_kernel,
        grid=(depth, n // tn),
        in_specs=[
            pl.BlockSpec((rows, d), lambda l, j: (0, 0)),
            pl.BlockSpec((None, d, tn), lambda l, j: (l, 0, j)),
            pl.BlockSpec((None, 1, tn), lambda l, j: (l, 0, j)),
        ],
        out_specs=pl.BlockSpec((None, rows, tn), lambda l, j: (l, 0, j)),
        out_shape=jax.ShapeDtypeStruct((depth, rows, n), F32),
        compiler_params=_cparams(("arbitrary", "arbitrary")),
        name="mod",
    )(c_all, ada_w, ada_b.reshape(depth, 1, n))


def _ffn_kernel(x_ref, mod_ref, g_ref, wg_ref, wu_ref, wd_ref, o_ref, *, sub):
    x = x_ref[...]
    h = _adaln(x, g_ref[...], mod_ref, sub).astype(BF16)
    a = _dot(h, wg_ref[...])
    u = _dot(h, wu_ref[...])
    z = (a * _sigmoid(a) * u).astype(BF16)
    y = _dot(z, wd_ref[...])
    gate = mod_ref[pl.ds(3 * sub + 2, 1), :]
    o_ref[...] = x + (0.5 * gate) * y


def _ffn(x, mod, g, wg, wu, wd, sub):
    bm, rows, d = x.shape
    tm = min(ROW_TILE, rows)
    f = wg.shape[1]
    return pl.pallas_call(
        functools.partial(_ffn_kernel, sub=sub),
        grid=(bm, rows // tm),
        in_specs=[
            pl.BlockSpec((None, tm, d), lambda b, i: (b, i, 0)),
            pl.BlockSpec((None, N_MOD, d), lambda b, i: (b, 0, 0)),
            _const_spec((1, d)),
            _const_spec((d, f)),
            _const_spec((d, f)),
            _const_spec((f, d)),
        ],
        out_specs=pl.BlockSpec((None, tm, d), lambda b, i: (b, i, 0)),
        out_shape=jax.ShapeDtypeStruct(x.shape, F32),
        compiler_params=_cparams(("arbitrary", "arbitrary")),
        name="ffn",
    )(x, mod, g, wg, wu, wd)


_IN_OFFS = (0, HY_COLS, HY_COLS + GQA_COLS, HY_COLS + GQA_COLS + MLA_COLS_PAD, D_IN_PAD)


def _inproj_kernel(x_ref, mod_ref, g_ref, w_ref, hy_ref, gq_ref, ml_ref, rw_ref):
    h = _adaln(x_ref[...], g_ref[...], mod_ref, 1).astype(BF16)
    p = _dot(h, w_ref[...])
    for ref, lo, hi in zip((hy_ref, gq_ref, ml_ref, rw_ref), _IN_OFFS[:-1], _IN_OFFS[1:]):
        ref[...] = p[:, lo:hi]


def _inproj(x, mod, g, w):
    bm, rows, d = x.shape
    tm = min(ROW_TILE, rows)
    widths = [hi - lo for lo, hi in zip(_IN_OFFS[:-1], _IN_OFFS[1:])]
    return pl.pallas_call(
        _inproj_kernel,
        grid=(bm, rows // tm),
        in_specs=[
            pl.BlockSpec((None, tm, d), lambda b, i: (b, i, 0)),
            pl.BlockSpec((None, N_MOD, d), lambda b, i: (b, 0, 0)),
            _const_spec((1, d)),
            _const_spec((d, D_IN_PAD)),
        ],
        out_specs=[pl.BlockSpec((None, tm, wd), lambda b, i: (b, i, 0)) for wd in widths],
        out_shape=[jax.ShapeDtypeStruct((bm, rows, wd), F32) for wd in widths],
        compiler_params=_cparams(("arbitrary", "arbitrary")),
        name="inproj",
    )(x, mod, g, w)


def _outproj_kernel(x_ref, mod_ref, yh_ref, yg_ref, ym_ref, yr_ref, w_ref, o_ref):
    y = jnp.concatenate([yh_ref[...], yg_ref[...], ym_ref[...], yr_ref[...]], axis=-1)
    gate = mod_ref[pl.ds(5, 1), :]
    o_ref[...] = x_ref[...] + gate * _dot(y, w_ref[...])


def _outproj(x, mod, ys, w):
    bm, rows, d = x.shape
    tm = min(ROW_TILE, rows)
    row_spec = pl.BlockSpec((None, tm, d), lambda b, i: (b, i, 0))
    y_spec = pl.BlockSpec((None, tm, 256), lambda b, i: (b, i, 0))
    return pl.pallas_call(
        _outproj_kernel,
        grid=(bm, rows // tm),
        in_specs=[row_spec, pl.BlockSpec((None, N_MOD, d), lambda b, i: (b, 0, 0)),
                  y_spec, y_spec, y_spec, y_spec, _const_spec(w.shape)],
        out_specs=row_spec,
        out_shape=jax.ShapeDtypeStruct(x.shape, F32),
        compiler_params=_cparams(("arbitrary", "arbitrary")),
        name="outproj",
    )(x, mod, *ys, w)


def _hy_features(length):
    t01 = np.linspace(0.0, 1.0, length, dtype=np.float32)[:, None]
    bands = (HY_EMB - 1) // 2
    w_ang = (np.float32(2.0 * math.pi) * np.arange(length, dtype=np.float32)[:, None]
             / np.float32(length)).astype(np.float32)
    f = np.linspace(1e-4, bands - 1, bands, dtype=np.float32)[None]
    arg = (f * w_ang).astype(np.float32)
    z = np.concatenate([t01, np.cos(arg), -np.sin(arg)], axis=-1).astype(np.float32)
    zp = np.zeros((length, LANES), np.float32)
    zp[:, :HY_EMB] = z
    return zp


def _hy_deltas():
    max_decay = math.log(HY_TARGET) / HY_FAST_PCT
    min_decay = math.log(HY_TARGET) / HY_SLOW_PCT
    d = np.abs(np.linspace(min_decay, max_decay, HY_CH, dtype=np.float32))
    return np.tile(d, 2)[None].astype(np.float32)


def _hyfilt_kernel(z_ref, w1_ref, b1_ref, w2_ref, b2_ref, w3_ref, b3_ref, w4_ref, fr_ref,
                   dl_ref, o_ref):
    z = z_ref[...]
    fr = fr_ref[...]
    h = jnp.sin(fr * (_dot(z, w1_ref[...], HI) + b1_ref[...]))
    h = jnp.sin(fr * (_dot(h, w2_ref[...], HI) + b2_ref[...]))
    h = jnp.sin(fr * (_dot(h, w3_ref[...], HI) + b3_ref[...]))
    h = _dot(h, w4_ref[...], HI)
    o_ref[...] = h * jnp.exp(-z[:, 0:1] * dl_ref[...])


def _hy_filters(length, w1p, b1, w2, b2, w3, b3, w4, freq):
    z = jnp.asarray(_hy_features(length))
    dl = jnp.asarray(_hy_deltas())
    tl = min(ROW_TILE, length)
    consts = (w1p, b1, w2, b2, w3, b3, w4, freq, dl)
    return pl.pallas_call(
        _hyfilt_kernel,
        grid=(length // tl,),
        in_specs=[pl.BlockSpec((tl, LANES), lambda i: (i, 0))] + [_const_spec(a.shape) for a in consts],
        out_specs=pl.BlockSpec((tl, 2 * HY_CH), lambda i: (i, 0)),
        out_shape=jax.ShapeDtypeStruct((length, 2 * HY_CH), F32),
        compiler_params=_cparams(("arbitrary",)),
        name="hyena_filters",
    )(z, *consts)


FFT_SLAB_PAD = 8
FFT_GROUP = 8


def _fft_factors(length):
    return (128, 64) if length >= 2048 else (2 * length, 1)


def _fft_layout(n1, n2):
    return (n1, n2, 2 * n2 + FFT_SLAB_PAD) if n2 > 1 else (1, n1, 2 * n1)


def _stack3(t):
    hi = t.astype(jnp.bfloat16)
    lo = (t - hi.astype(np.float64)).astype(jnp.bfloat16)
    return np.concatenate([hi, lo, hi], axis=-1)


def _rows3(d):
    hi = d.astype(BF16)
    lo = (d - hi.astype(F32)).astype(BF16)
    return jnp.concatenate([hi, hi, lo], axis=0)


@functools.lru_cache(maxsize=None)
def _fft_tables(length):
    n1, n2 = _fft_factors(length)
    n = n1 * n2
    assert n == 2 * length
    a_n2 = np.arange(n2)[:, None, None]
    a_k1 = np.arange(n1)[None, :, None]
    a_n1 = np.arange(n1 // 2)[None, None, :]
    ang = 2.0 * np.pi * ((a_n1 * a_k1 % n1) / n1 + (a_n2 * a_k1 % n) / n)
    t_re, t_im = np.cos(ang), -np.sin(ang)
    ta = np.concatenate([t_re, t_im], axis=1)
    tai = np.concatenate([np.transpose(t_re, (0, 2, 1)), np.transpose(t_im, (0, 2, 1))], axis=2) / n
    jj = np.arange(n2)
    ang2 = 2.0 * np.pi * (np.outer(jj, jj) % n2) / n2
    c, s = np.cos(ang2), np.sin(ang2)
    fb = np.block([[c, s], [-s, c]])
    fbi = np.block([[c, -s], [s, c]])
    return tuple(_stack3(t) for t in (ta, tai, fb, fbi))


def _fft_stage_a(src_ref, fbuf_ref, ta_ref, n1, n2):
    if n2 == 1:
        fbuf_ref[...] = _dot(ta_ref[0], _rows3(src_ref[...]))
        return
    pitch = _fft_layout(n1, n2)[2]

    def body(i, carry):
        js = [i * FFT_GROUP + g for g in range(FFT_GROUP)]
        rows = [_rows3(src_ref[pl.ds(j, n1 // 2, stride=n2), :]) for j in js]
        ts = [_dot(ta_ref[j], r) for j, r in zip(js, rows)]
        for j, t in zip(js, ts):
            fbuf_ref[pl.ds(j, n1, stride=pitch), :] = t[:n1]
            fbuf_ref[pl.ds(n2 + j, n1, stride=pitch), :] = t[n1:]
        return carry

    lax.fori_loop(0, n2 // FFT_GROUP, body, 0)


def _spec_kernel(hf_ref, hb_ref, bias_ref, ta_ref, fb_ref, o_ref, fbuf_ref, *, n1, n2):
    nslab, hs, pitch = _fft_layout(n1, n2)
    if pitch > 2 * hs:
        o_ref[...] = jnp.zeros(o_ref.shape, F32)
    _fft_stage_a(hf_ref, o_ref, ta_ref, n1, n2)
    _fft_stage_a(hb_ref, fbuf_ref, ta_ref, n1, n2)
    is_re = lax.broadcasted_iota(jnp.int32, (2 * hs, 1), 0) < hs
    sgn = jnp.where(is_re, 1.0, -1.0).astype(F32)
    skip = jnp.where(is_re, bias_ref[...], 0.0)

    grp = FFT_GROUP if n2 > 1 else 1

    def stage_b(i, carry):
        sls = [pl.ds(pl.multiple_of((i * grp + g) * pitch, 8), 2 * hs) for g in range(grp)]
        a = [o_ref[sl, :] for sl in sls]
        b = [fbuf_ref[sl, :] for sl in sls]
        if n2 > 1:
            a = [_dot(fb_ref[...], _rows3(x)) for x in a]
            b = [_dot(fb_ref[...], _rows3(x)) for x in b]
        for sl, x, y in zip(sls, a, b):
            o_ref[sl, :] = x + sgn * y + skip
        return carry

    lax.fori_loop(0, nslab // grp, stage_b, 0)


def _hy_spectrum(h, bias):
    length = h.shape[0]
    n1, n2 = _fft_factors(length)
    nslab, _, pitch = _fft_layout(n1, n2)
    ta, _, fb, _ = (jnp.asarray(t) for t in _fft_tables(length))
    nblk = HY_CH // LANES
    return pl.pallas_call(
        functools.partial(_spec_kernel, n1=n1, n2=n2),
        grid=(nblk,),
        in_specs=[pl.BlockSpec((length, LANES), lambda j: (0, j)),
                  pl.BlockSpec((length, LANES), lambda j: (0, nblk + j)),
                  pl.BlockSpec((1, LANES), lambda j: (0, j)),
                  _const_spec(ta.shape), _const_spec(fb.shape)],
        out_specs=pl.BlockSpec((nslab * pitch, LANES), lambda j: (0, j)),
        out_shape=jax.ShapeDtypeStruct((nslab * pitch, HY_CH), F32),
        scratch_shapes=[pltpu.VMEM((nslab * pitch, LANES), F32)],
        compiler_params=_cparams(("arbitrary",)),
        name="hyena_spectrum",
    )(h, h, bias, ta, fb)


def _shift_rows(x, prev_row, next_row):
    n = x.shape[0]
    row = lax.broadcasted_iota(jnp.int32, x.shape, 0)
    up = jnp.where(row == 0, prev_row, pltpu.roll(x, 1, 0))
    dn = jnp.where(row == n - 1, next_row, pltpu.roll(x, n - 1, 0))
    return up, dn


def _hyconv_kernel(x1_ref, x2_ref, v_ref, cw1_ref, cw2_ref, cwv_ref, cb_ref, g_ref,
                   ta_ref, tai_ref, fb_ref, fbi_ref, o_ref, u_ref, fbuf_ref, *, n1, n2):
    def conv3(ref, w_ref, b):
        x = ref[...]
        up, dn = _shift_rows(x, 0.0, 0.0)
        return up * w_ref[0:1, :] + x * w_ref[1:2, :] + dn * w_ref[2:3, :] + b

    cb = cb_ref[...]
    u_ref[...] = conv3(x1_ref, cw1_ref, cb[0:1, :]) * conv3(v_ref, cwv_ref, cb[2:3, :])

    _fft_stage_a(u_ref, fbuf_ref, ta_ref, n1, n2)
    nslab, hs, pitch = _fft_layout(n1, n2)

    grp = FFT_GROUP if n2 > 1 else 1

    def cmul(a, g):
        ar, ai, gr, gi = a[:hs], a[hs:], g[:hs], g[hs:]
        return jnp.concatenate([ar * gr - ai * gi, ar * gi + ai * gr], axis=0)

    def stage_b(i, carry):
        sls = [pl.ds(pl.multiple_of((i * grp + g) * pitch, 8), 2 * hs) for g in range(grp)]
        a = [fbuf_ref[sl, :] for sl in sls]
        if n2 > 1:
            a = [_dot(fb_ref[...], _rows3(x)) for x in a]
        a = [cmul(x, g_ref[sl, :]) for x, sl in zip(a, sls)]
        if n2 > 1:
            a = [_dot(fbi_ref[...], _rows3(x)) for x in a]
        for sl, x in zip(sls, a):
            fbuf_ref[sl, :] = x
        return carry

    lax.fori_loop(0, nslab // grp, stage_b, 0)

    if n2 == 1:
        u_ref[...] = _dot(tai_ref[0], _rows3(fbuf_ref[...]))
    else:
        def stage_a_inv(i, carry):
            js = [i * FFT_GROUP + g for g in range(FFT_GROUP)]
            cs = [_rows3(jnp.concatenate([fbuf_ref[pl.ds(j, n1, stride=pitch), :],
                                          fbuf_ref[pl.ds(n2 + j, n1, stride=pitch), :]], axis=0))
                  for j in js]
            ys = [_dot(tai_ref[j], c) for j, c in zip(js, cs)]
            for j, y in zip(js, ys):
                u_ref[pl.ds(j, n1 // 2, stride=n2), :] = y
            return carry

        lax.fori_loop(0, n2 // FFT_GROUP, stage_a_inv, 0)

    o_ref[...] = (conv3(x2_ref, cw2_ref, cb[1:2, :]) * u_ref[...]).astype(o_ref.dtype)


def _hy_conv(p, conv_w, conv_b, spec):
    b, length, _ = p.shape
    n1, n2 = _fft_factors(length)
    nslab, _, pitch = _fft_layout(n1, n2)
    ta, tai, fb, fbi = (jnp.asarray(t) for t in _fft_tables(length))
    nblk = HY_CH // LANES
    cb3 = conv_b.reshape(3, HY_CH)
    col = lambda g: pl.BlockSpec((None, length, LANES), lambda j, i, g=g: (i, 0, g * nblk + j))
    wcol = lambda g: pl.BlockSpec((3, LANES), lambda j, i, g=g: (0, g * nblk + j))
    return pl.pallas_call(
        functools.partial(_hyconv_kernel, n1=n1, n2=n2),
        grid=(nblk, b),
        in_specs=[col(0), col(1), col(2), wcol(0), wcol(1), wcol(2),
                  pl.BlockSpec((3, LANES), lambda j, i: (0, j)),
                  pl.BlockSpec((nslab * pitch, LANES), lambda j, i: (0, j), pipeline_mode=pl.Buffered(1)),
                  _const_spec(ta.shape), _const_spec(tai.shape), _const_spec(fb.shape),
                  _const_spec(fbi.shape)],
        out_specs=pl.BlockSpec((None, length, LANES), lambda j, i: (i, 0, j)),
        out_shape=jax.ShapeDtypeStruct((b, length, HY_CH), BF16),
        scratch_shapes=[pltpu.VMEM((length, LANES), F32), pltpu.VMEM((nslab * pitch, LANES), F32)],
        compiler_params=_cparams(("arbitrary", "arbitrary")),
        name="hyena_conv",
    )(p, p, p, conv_w, conv_w, conv_w, cb3, spec, ta, tai, fb, fbi)


def _rope_tables(length, d_rot, lane_lo, head_w):
    rows = length // GRID_W
    row = np.repeat(np.arange(rows, dtype=np.float32), GRID_W)
    colv = np.tile(np.arange(GRID_W, dtype=np.float32), rows)
    n_freq = d_rot // 4
    inv = (np.float32(ROPE_THETA) ** (-np.arange(n_freq, dtype=np.float32) / np.float32(n_freq))).astype(np.float32)
    ang = np.concatenate([row[:, None] * inv, colv[:, None] * inv], axis=-1).astype(np.float32)
    cos_t = np.ones((length, LANES), np.float32)
    sin_t = np.zeros((length, LANES), np.float32)
    c, s = np.cos(ang), np.sin(ang)
    for base in range(0, LANES, head_w):
        for i in range(d_rot // 2):
            cos_t[:, base + lane_lo + 2 * i] = c[:, i]
            cos_t[:, base + lane_lo + 2 * i + 1] = c[:, i]
            sin_t[:, base + lane_lo + 2 * i] = -s[:, i]
            sin_t[:, base + lane_lo + 2 * i + 1] = s[:, i]
    return cos_t, sin_t


def _rope(x, cos_t, sin_t):
    lane = lax.broadcasted_iota(jnp.int32, x.shape, 1)
    w = x.shape[1]
    partner = jnp.where(jnp.bitwise_and(lane, 1) == 0, pltpu.roll(x, w - 1, 1), pltpu.roll(x, 1, 1))
    return x * cos_t + partner * sin_t


def _store_vt(vt_ref, h, vt):
    row = lax.broadcasted_iota(jnp.int32, vt.shape, 0)
    vt_ref[h, 0:HEAD_DIM, :] = vt.astype(BF16)
    vt_ref[h, HEAD_DIM:2 * HEAD_DIM, :] = jnp.where(row == 0, 1.0, 0.0).astype(BF16)


def _gqa_prep_kernel(p_ref, cos_ref, sin_ref, gq_ref, gk_ref, seg_ref, q_ref, k_ref, vt_ref, *, rope):
    p = p_ref[...]
    seg = seg_ref[...]
    lane = lax.broadcasted_iota(jnp.int32, (p.shape[0], LANES), 1)
    low = lane < HEAD_DIM

    def hnorm(x, g):
        ms = _segsum(x * x, seg)
        return x * lax.rsqrt(ms + NORM_EPS) * g

    scale = HEAD_DIM ** -0.5 * LOG2E
    for c in range(2):
        q = hnorm(p[:, c * LANES:(c + 1) * LANES], gq_ref[...])
        if rope:
            q = _rope(q, cos_ref[...], sin_ref[...])
        q = q * scale
        qs = pltpu.roll(q, HEAD_DIM, 1)
        q_ref[:, (2 * c) * LANES:(2 * c + 1) * LANES] = jnp.where(low, q, 0.0).astype(BF16)
        q_ref[:, (2 * c + 1) * LANES:(2 * c + 2) * LANES] = jnp.where(low, qs, 0.0).astype(BF16)
    k = hnorm(p[:, 256:384], gk_ref[...])
    if rope:
        k = _rope(k, cos_ref[...], sin_ref[...])
    ks = pltpu.roll(k, HEAD_DIM, 1)
    k_ref[:, 0:LANES] = jnp.where(low, k, 0.0).astype(BF16)
    k_ref[:, LANES:2 * LANES] = jnp.where(low, ks, 0.0).astype(BF16)
    vt = p[:, 384:512].T
    _store_vt(vt_ref, 0, vt[0:HEAD_DIM])
    _store_vt(vt_ref, 1, vt[HEAD_DIM:2 * HEAD_DIM])


def _seg_matrix(width, seg, value):
    i = np.arange(width)
    return ((i[:, None] // seg) == (i[None, :] // seg)).astype(np.float32) * np.float32(value)


def _gqa_prep(p, gq, gk, rope):
    b, length, _ = p.shape
    tm = min(ROW_TILE, length)
    if rope:
        cos_t, sin_t = (jnp.asarray(t) for t in _rope_tables(length, HEAD_DIM, 0, HEAD_DIM))
    else:
        cos_t = sin_t = jnp.zeros((length, LANES), F32)
    seg = jnp.asarray(_seg_matrix(LANES, HEAD_DIM, 1.0 / HEAD_DIM), BF16)
    tab = pl.BlockSpec((tm, LANES), lambda i, bb: (i, 0))
    outw = (512, 256)
    return pl.pallas_call(
        functools.partial(_gqa_prep_kernel, rope=rope),
        grid=(length // tm, b),
        in_specs=[pl.BlockSpec((None, tm, GQA_COLS), lambda i, bb: (bb, i, 0)), tab, tab,
                  _const_spec((1, LANES)), _const_spec((1, LANES)), _const_spec((LANES, LANES))],
        out_specs=[pl.BlockSpec((None, tm, w), lambda i, bb: (bb, i, 0)) for w in outw]
                  + [pl.BlockSpec((None, GQA_KV_HEADS, LANES, tm), lambda i, bb: (bb, 0, 0, i))],
        out_shape=[jax.ShapeDtypeStruct((b, length, w), BF16) for w in outw]
                  + [jax.ShapeDtypeStruct((b, GQA_KV_HEADS, LANES, length), BF16)],
        compiler_params=_cparams(("arbitrary", "arbitrary")),
        name="gqa_prep",
    )(p, cos_t, sin_t, gq, gk, seg)


def _mla_prep_kernel(p_ref, cos_ref, sin_ref, cqn_ref, ckvn_ref, wuq_ref, wuk_ref, wuv_ref,
                     qn_ref, kn_ref, q_ref, k_ref, vt_ref, *, rope):
    p = p_ref[...]

    def rms(x, g):
        return x * lax.rsqrt(jnp.mean(x * x, axis=-1, keepdims=True) + NORM_EPS) * g

    cq = rms(p[:, 0:MLA_Q_RANK], cqn_ref[...]).astype(BF16)
    ckv = rms(p[:, MLA_Q_RANK:MLA_Q_RANK + MLA_KV_RANK], ckvn_ref[...]).astype(BF16)
    q = _dot(cq, wuq_ref[...])
    kn = _dot(ckv, wuk_ref[...])
    v = _dot(ckv, wuv_ref[...])
    for c in range(MLA_HEADS // 2):
        vt = v[:, c * LANES:(c + 1) * LANES].T
        _store_vt(vt_ref, 2 * c, vt[0:MLA_V])
        _store_vt(vt_ref, 2 * c + 1, vt[MLA_V:2 * MLA_V])
    lane = lax.broadcasted_iota(jnp.int32, (p.shape[0], LANES), 1)
    in_rope = jnp.logical_and(lane >= MLA_NOPE, lane < MLA_QK)
    kr = jnp.where(in_rope, pltpu.roll(p[:, 384:512], MLA_NOPE, 1), 0.0)
    scale = MLA_QK ** -0.5 * LOG2E

    def hnorm(x, g):
        ms = jnp.sum(x * x, axis=-1, keepdims=True) * (1.0 / MLA_QK)
        return x * lax.rsqrt(ms + NORM_EPS) * g

    for h in range(MLA_HEADS):
        sl = slice(h * LANES, (h + 1) * LANES)
        qh = hnorm(q[:, sl], qn_ref[...])
        kh = hnorm(kn[:, sl] + kr, kn_ref[...])
        if rope:
            qh = _rope(qh, cos_ref[...], sin_ref[...])
            kh = _rope(kh, cos_ref[...], sin_ref[...])
        q_ref[:, sl] = (qh * scale).astype(BF16)
        k_ref[:, sl] = kh.astype(BF16)


def _mla_prep(p, cqn, ckvn, wuq, wuk, wuv, qn, kn, rope):
    b, length, _ = p.shape
    tm = min(ROW_TILE, length)
    if rope:
        cos_t, sin_t = (jnp.asarray(t) for t in _rope_tables(length, MLA_ROPE, MLA_NOPE, LANES))
    else:
        cos_t = sin_t = jnp.zeros((length, LANES), F32)
    tab = pl.BlockSpec((tm, LANES), lambda i, bb: (i, 0))
    consts = (cqn, ckvn, wuq, wuk, wuv, qn, kn)
    outw = (512, 512)
    return pl.pallas_call(
        functools.partial(_mla_prep_kernel, rope=rope),
        grid=(length // tm, b),
        in_specs=[pl.BlockSpec((None, tm, MLA_COLS_PAD), lambda i, bb: (bb, i, 0)), tab, tab]
                 + [_const_spec(a.shape) for a in consts],
        out_specs=[pl.BlockSpec((None, tm, w), lambda i, bb: (bb, i, 0)) for w in outw]
                  + [pl.BlockSpec((None, MLA_HEADS, LANES, tm), lambda i, bb: (bb, 0, 0, i))],
        out_shape=[jax.ShapeDtypeStruct((b, length, w), BF16) for w in outw]
                  + [jax.ShapeDtypeStruct((b, MLA_HEADS, LANES, length), BF16)],
        compiler_params=_cparams(("arbitrary", "arbitrary")),
        name="mla_prep",
    )(p, cos_t, sin_t, *consts)


def _attn_kernel(*refs, n_seg):
    q_ref, o_ref = refs[0], refs[-1]
    segs = [refs[1 + 4 * s:5 + 4 * s] for s in range(n_seg)]
    q = q_ref[...]
    qh = [q[:, side * LANES:(side + 1) * LANES] for side in range(2)]
    pieces = [(side, seg, c0, min(KEY_CHUNK, seg[0].shape[0]))
              for side in range(2) for seg in segs
              for c0 in range(0, seg[0].shape[0], min(KEY_CHUNK, seg[0].shape[0]))]
    scores = [_dot_nt(seg[side][c0:c0 + n, :], qh[side]) for side, seg, c0, n in pieces]
    mx = [jnp.max(s, axis=0, keepdims=True) for s in scores]
    ps = [jnp.exp2(s - m).astype(BF16) for s, m in zip(scores, mx)]
    os_ = [_dot(seg[2 + side][:, c0:c0 + n], e) for (side, seg, c0, n), e in zip(pieces, ps)]
    outs = []
    for side in range(2):
        idx = [i for i, pc in enumerate(pieces) if pc[0] == side]
        m = functools.reduce(jnp.maximum, [mx[i] for i in idx])
        o = functools.reduce(jnp.add, [os_[i] * jnp.exp2(mx[i] - m) for i in idx])
        outs.append(o[0:HEAD_DIM] / o[HEAD_DIM:HEAD_DIM + 1])
    o_ref[...] = jnp.concatenate(outs, axis=0).T.astype(o_ref.dtype)


def _attention(q, kv_segs, k_heads):
    b, lq, _ = q.shape
    tq = min(Q_TILE, lq)
    ka = (lambda pr: pr) if k_heads == 2 else (lambda pr: 2 * pr)
    kb = (lambda pr: pr) if k_heads == 2 else (lambda pr: 2 * pr + 1)
    in_specs = [pl.BlockSpec((None, tq, 2 * LANES), lambda bb, pr, i: (bb, i, pr))]
    args = [q]
    for k, vt in kv_segs:
        lk = k.shape[1]
        in_specs += [pl.BlockSpec((None, lk, LANES), lambda bb, pr, i, f=ka: (bb, 0, f(pr))),
                     pl.BlockSpec((None, lk, LANES), lambda bb, pr, i, f=kb: (bb, 0, f(pr))),
                     pl.BlockSpec((None, None, LANES, lk), lambda bb, pr, i, f=ka: (bb, f(pr), 0, 0)),
                     pl.BlockSpec((None, None, LANES, lk), lambda bb, pr, i, f=kb: (bb, f(pr), 0, 0))]
        args += [k, k, vt, vt]
    return pl.pallas_call(
        functools.partial(_attn_kernel, n_seg=len(kv_segs)),
        grid=(b, 2, lq // tq),
        in_specs=in_specs,
        out_specs=pl.BlockSpec((None, tq, LANES), lambda bb, pr, i: (bb, i, pr)),
        out_shape=jax.ShapeDtypeStruct((b, lq, 2 * LANES), BF16),
        compiler_params=_cparams(("arbitrary", "arbitrary", "arbitrary")),
        name="attention",
    )(*args)


def _rw_prep_kernel(p_ref, prev_ref, next_ref, mu_ref, kk_ref_w, ka_ref, rk_ref, w0_ref, w2_ref,
                    a0_ref, a2_ref, g2_ref, seg_ref,
                    r_ref, v_ref, kk_ref, lw_ref, kd_ref, bd_ref, bonus_ref, g_ref):
    i = pl.program_id(1)
    n = pl.num_programs(1)
    x = p_ref[...]
    prev_row = jnp.where(i > 0, prev_ref[7:8, :], 0.0)
    next_row = jnp.where(i < n - 1, next_ref[0:1, :], 0.0)
    up, dn = _shift_rows(x, prev_row, next_row)
    xs = x + mu_ref[...] * (0.5 * (up + dn) - x)
    c = RW_C
    r, k, v = xs[:, 0:c], xs[:, c:2 * c], xs[:, 2 * c:3 * c]
    xw, xa, xg = xs[:, 3 * c:3 * c + 128], xs[:, 3 * c + 128:3 * c + 256], xs[:, 3 * c + 256:3 * c + 384]
    seg = seg_ref[...]
    kk = k * kk_ref_w[...]
    nrm = jnp.sqrt(_segsum(kk * kk, seg))
    kk = kk / jnp.maximum(nrm, 1e-12)
    u = w0_ref[...] + _dot(jnp.tanh(xw).astype(BF16), w2_ref[...])
    z = -u
    softplus = jnp.maximum(z, 0.0) + jnp.log(1.0 + jnp.exp(-jnp.abs(z)))
    lw_ref[...] = -jnp.exp(-softplus - 0.5)
    a = _sigmoid(a0_ref[...] + _dot(xa.astype(BF16), a2_ref[...]))
    bonus = jnp.zeros_like(r)
    for d in range(2):
        a_d = a[:, d * c:(d + 1) * c]
        k_d = k * (1.0 + (a_d - 1.0) * ka_ref[...])
        kd_ref[:, d * c:(d + 1) * c] = k_d
        bd_ref[:, d * c:(d + 1) * c] = kk * a_d
        bonus = bonus + _segsum(r * k_d * rk_ref[...], seg)
    r_ref[...] = r
    v_ref[...] = v
    kk_ref[...] = kk
    bonus_ref[...] = bonus * v
    g_ref[...] = _dot(_sigmoid(xg).astype(BF16), g2_ref[...])


def _rw_prep(p, mu, k_k, k_a, r_k, w0, w2bd, a0, a2bd, g2):
    b, length, cols = p.shape
    tm = min(ROW_TILE, length)
    nb = tm // 8
    last = length // 8 - 1
    seg = jnp.asarray(_seg_matrix(RW_C, RW_N, 1.0), BF16)
    consts = (mu, k_k, k_a, r_k, w0, w2bd, a0, a2bd, g2, seg)
    outw = (RW_C, RW_C, RW_C, 2 * RW_C, 2 * RW_C, 2 * RW_C, RW_C, RW_C)
    return pl.pallas_call(
        _rw_prep_kernel,
        grid=(b, length // tm),
        in_specs=[pl.BlockSpec((None, tm, cols), lambda bb, i: (bb, i, 0)),
                  pl.BlockSpec((None, 8, cols), lambda bb, i: (bb, jnp.maximum(i * nb - 1, 0), 0)),
                  pl.BlockSpec((None, 8, cols), lambda bb, i: (bb, jnp.minimum((i + 1) * nb, last), 0))]
                 + [_const_spec(a.shape) for a in consts],
        out_specs=[pl.BlockSpec((None, tm, w), lambda bb, i: (bb, i, 0)) for w in outw],
        out_shape=[jax.ShapeDtypeStruct((b, length, w), F32) for w in outw],
        compiler_params=_cparams(("arbitrary", "arbitrary")),
        name="rwkv_prep",
    )(p, p, p, *consts)


@functools.lru_cache(maxsize=None)
def _rw_masks():
    cs, n = RW_CHUNK, RW_C
    i = np.arange(n)
    same = (i[:, None] // cs) == (i[None, :] // cs)
    rel = (i[None, :] % cs) - (i[:, None] % cs)
    masks = np.stack([same & (rel < 0), same & (rel <= 0), same & (rel > 0), same & (rel >= 0),
                      same, i[:, None] == i[None, :]]).astype(np.float32)
    t = np.arange(cs)
    tri = np.stack([t[None, :] <= t[:, None], t[None, :] >= t[:, None]]).astype(np.float32)
    tr, tc = (i % cs)[:, None], (i % cs)[None, :]
    lvl = []
    for sgn in (1, -1):
        earlier = (tc - tr) * sgn < 0
        rows = [same & earlier & (tr // RW_BASE == tc // RW_BASE)]
        for k in range(RW_LEVELS):
            s = RW_BASE << k
            rows.append(same & earlier & (tr // (2 * s) == tc // (2 * s)) & (tr // s != tc // s))
        lvl.append(np.stack(rows))
    return masks, tri, np.stack(lvl).astype(np.float32)


def _rw_scan_kernel(rf_ref, rb_ref, vf_ref, vb_ref, kkf_ref, kkb_ref, lwf_ref, lwb_ref, kdf_ref, kdb_ref,
                    bf_ref, bb_ref, s0_ref, mask_ref, tri_ref, lvl_ref, yf_ref, yb_ref, sfin_ref, s_ref,
                    *, n_chunks, n_batch):
    cs = RW_CHUNK

    @pl.when(pl.program_id(1) == 0)
    def _():
        s_ref[...] = s0_ref[...]

    same_head = mask_ref[4].astype(BF16)
    eye = mask_ref[5]
    dirs = ((rf_ref, vf_ref, kkf_ref, lwf_ref, kdf_ref, bf_ref, yf_ref),
            (rb_ref, vb_ref, kkb_ref, lwb_ref, kdb_ref, bb_ref, yb_ref))

    def tile(x):
        xb = x.astype(BF16)
        return jnp.concatenate([xb, xb, xb, xb], axis=0)

    def bd(x):
        return tile(x) * same_head

    def spread(a, half):
        swapped = pltpu.roll(a, cs, 1)
        low = lax.broadcasted_iota(jnp.int32, a.shape, 1) < cs
        h = jnp.where(low, swapped, a) if half else jnp.where(low, a, swapped)
        return jnp.concatenate([h, h], axis=1)

    streams = [(bi, d) for bi in range(n_batch) for d in range(2)]

    def each(f, *cols):
        return [f(*a) for a in zip(*cols)] if cols else [f(bi, d) for bi, d in streams]

    bf = lambda xs: [x.astype(BF16) for x in xs]

    def body(c, carry):
        def load(bi, d):
            cc = c if d == 0 else n_chunks - 1 - c
            sl = pl.ds(pl.multiple_of(cc * cs, cs), cs)
            return [ref[bi, sl, :] for ref in dirs[d][:6]] + [sl]

        r, v, kk, lw, kd, b, sl = zip(*each(load))
        strict = [mask_ref[2 * d] for _, d in streams]
        incl = [mask_ref[2 * d + 1] for _, d in streams]
        cl = [functools.reduce(jnp.add, [_dot(tri_ref[d], p) for p in _pieces(x, 3)])
              for (_, d), x in zip(streams, lw)]
        w_inv = [jnp.exp(-x) for x in cl]
        w_all = [jnp.exp(jnp.sum(x, axis=0, keepdims=True)) for x in lw]
        kt = each(lambda a, w: a * w, kd, w_inv)
        bt = each(lambda a, w: a * w, b, w_inv)
        qk_s = each(lambda a, x, y: bd(a * jnp.exp(x - y)), kk, cl, lw)
        rt_s = each(lambda a, x: bd(a * jnp.exp(x)), r, cl)
        kb = each(lambda a, t: jnp.concatenate([a.astype(BF16), t.astype(BF16)], axis=0), kt, bt)
        aq = each(_dot_nt, qk_s, kb)
        ar = each(_dot_nt, rt_s, kb)
        a_kk = each(lambda a, m: (spread(a, 0) * m).astype(BF16), aq, strict)
        a_kb = each(lambda a, m: spread(a, 1) * m, aq, strict)
        a_rk = each(lambda a, m: (spread(a, 0) * m).astype(BF16), ar, incl)
        a_rb = each(lambda a, m: (spread(a, 1) * m).astype(BF16), ar, incl)
        dmask = lambda k: [lvl_ref[d, k] for _, d in streams]
        l0 = each(lambda a, m: a * m, a_kb, dmask(0))
        p0 = [eye - a for a in l0]
        sq = each(lambda a: _dot(a, a), bf(l0))
        t_inv = each(lambda p, pb, s: p + _dot(pb, s), p0, bf(p0), bf(sq))
        for k in range(1, RW_LEVELS + 1):
            cm = bf(each(lambda a, m: a * m, a_kb, dmask(k)))
            tb = bf(t_inv)
            tc = bf(each(_dot, tb, cm))
            t_inv = each(lambda t, x, y: t - _dot(x, y), t_inv, tc, tb)
        t_inv = bf(t_inv)
        v_s = each(bd, v)
        akv = bf(each(_dot, a_kk, v_s))
        x1 = bf(each(_dot, t_inv, qk_s))
        x2 = each(_dot, t_inv, akv)
        bh_s = each(lambda a, w: bd(a * w), bt, w_all)
        kh_s = each(lambda a, w: bd(a * w), kt, w_all)
        y_v = each(_dot, a_rk, v_s)
        s_v = each(_dot_tn, v_s, kh_s)
        s0f = [s_ref[bi, d] for bi, d in streams]
        s0 = bf(s0f)
        u = bf(each(lambda x, s, y: _dot_nt(x, s) + y, x1, s0, x2))
        y_bd = each(lambda rt, s, yv, ab, u_: _dot_nt(rt, s) + yv - _dot(ab, u_), rt_s, s0, y_v, a_rb, u)
        s1 = each(lambda s, w, sv, u_, bh: s * w + sv - _dot_tn(u_, bh), s0f, w_all, s_v, u, bh_s)
        for (bi, d), y, s, rows in zip(streams, y_bd, s1, sl):
            dirs[d][6][bi, rows, :] = y[0:cs] + y[cs:2 * cs] + y[2 * cs:3 * cs] + y[3 * cs:4 * cs]
            s_ref[bi, d] = s
        return carry

    lax.fori_loop(0, n_chunks, body, 0)
    sfin_ref[...] = s_ref[...]


def _rw_scan(r, v, kk, lw, kd, bdir, s0):
    b, length, c = r.shape
    tc = min(RW_BLOCK, length)
    nblk = length // tc
    nb = RW_SEQS
    masks, tri, lvl = _rw_masks()
    masks, tri, lvl = jnp.asarray(masks), jnp.asarray(tri, BF16), jnp.asarray(lvl)
    fwd = lambda col: pl.BlockSpec((nb, tc, c), lambda g, i: (g, i, col))
    bwd = lambda col: pl.BlockSpec((nb, tc, c), lambda g, i: (g, nblk - 1 - i, col))
    state = pl.BlockSpec((nb, 2, c, c), lambda g, i: (g, 0, 0, 0))
    return pl.pallas_call(
        functools.partial(_rw_scan_kernel, n_chunks=tc // RW_CHUNK, n_batch=nb),
        grid=(b // nb, nblk),
        in_specs=[fwd(0), bwd(0)] * 3 + [fwd(0), bwd(1)] * 3
                 + [state, _const_spec(masks.shape), _const_spec(tri.shape), _const_spec(lvl.shape)],
        out_specs=[fwd(0), bwd(0), state],
        out_shape=[jax.ShapeDtypeStruct((b, length, c), F32), jax.ShapeDtypeStruct((b, length, c), F32),
                   jax.ShapeDtypeStruct((b, 2, c, c), F32)],
        scratch_shapes=[pltpu.VMEM((nb, 2, c, c), F32)],
        compiler_params=_cparams(("arbitrary", "arbitrary")),
        name="rwkv_scan",
    )(r, r, v, v, kk, kk, lw, lw, kd, kd, bdir, bdir, s0, masks, tri, lvl)


def _rw_readout_kernel(yf_ref, yb_ref, bonus_ref, g_ref, lnw_ref, lnb_ref, seg_ref, o_ref):
    y = yf_ref[...] + yb_ref[...]
    seg = seg_ref[...]
    mean = _segsum(y, seg)
    yc = y - mean
    var = _segsum(yc * yc, seg)
    yn = yc * lax.rsqrt(var + RW_GN_EPS) * lnw_ref[...] + lnb_ref[...]
    o_ref[...] = ((yn + bonus_ref[...]) * g_ref[...]).astype(o_ref.dtype)


def _rw_readout(y_f, y_b, bonus, g, ln_w, ln_b):
    b, length, c = y_f.shape
    tm = min(ROW_TILE, length)
    seg = jnp.asarray(_seg_matrix(RW_C, RW_N, 1.0 / RW_N), BF16)
    rowspec = pl.BlockSpec((None, tm, c), lambda bb, i: (bb, i, 0))
    return pl.pallas_call(
        _rw_readout_kernel,
        grid=(b, length // tm),
        in_specs=[rowspec, rowspec, rowspec, rowspec,
                  _const_spec((1, c)), _const_spec((1, c)), _const_spec((c, c))],
        out_specs=rowspec,
        out_shape=jax.ShapeDtypeStruct((b, length, c), BF16),
        compiler_params=_cparams(("arbitrary", "arbitrary")),
        name="rwkv_readout",
    )(y_f, y_b, bonus, g, ln_w, ln_b, seg)


def _blockdiag2(w):
    k, n = w.shape[1:]
    z = jnp.zeros((k, n), w.dtype)
    return jnp.concatenate([jnp.concatenate([w[0], z], axis=1), jnp.concatenate([z, w[1]], axis=1)], axis=0)


def _mla_head_pad(w, width):
    k = w.shape[0]
    w = w.reshape(k, MLA_HEADS, width)
    return jnp.pad(w, ((0, 0), (0, 0), (0, LANES - width))).reshape(k, MLA_HEADS * LANES)


def kernel(x, c, ctx, c_ctx, ada_w, ada_b, norm_ffn1, norm_mix, norm_ffn2, ffn1_gate, ffn1_up, ffn1_down, ffn2_gate, ffn2_up, ffn2_down, w_in, w_out, hy_conv_w, hy_conv_b, hy_f_w1, hy_f_b1, hy_f_w2, hy_f_b2, hy_f_w3, hy_f_b3, hy_f_w4, hy_f_freq, hy_bias, gqa_q_norm, gqa_k_norm, mla_cq_norm, mla_ckv_norm, mla_w_uq, mla_w_ukv, mla_q_norm, mla_k_norm, rw_mu, rw_w0, rw_w2, rw_a0, rw_a2, rw_g2, rw_k_k, rw_k_a, rw_r_k, rw_ln_w, rw_ln_b):
    b, length, d = x.shape
    lc = ctx.shape[1]
    depth = ada_w.shape[0]

    c_all = jnp.zeros((16, d), F32).at[:b].set(c).at[b].set(c_ctx)
    mod = _compute_mod(c_all, ada_w, ada_b)

    xc = ctx.reshape(1, b * lc, d)
    for l in range(depth):
        ctx_out = l < depth - 1
        mod_x = mod[l, :b].reshape(b, N_MOD, d)
        mod_c = mod[l, b:b + 1].reshape(1, N_MOD, d)
        row = lambda a: a[l].reshape(1, -1)

        wg1, wu1, wd1 = (w[l].astype(BF16) for w in (ffn1_gate, ffn1_up, ffn1_down))
        wg2, wu2, wd2 = (w[l].astype(BF16) for w in (ffn2_gate, ffn2_up, ffn2_down))
        wi = w_in[l]
        o1, o2, o3 = HY_COLS, HY_COLS + GQA_COLS, HY_COLS + GQA_COLS + MLA_COLS
        wi = jnp.concatenate([wi[:, :o3], jnp.zeros((d, MLA_COLS_PAD - MLA_COLS), F32), wi[:, o3:]],
                             axis=1).astype(BF16)
        wo = w_out[l].astype(BF16)

        x = _ffn(x, mod_x, row(norm_ffn1), wg1, wu1, wd1, 0)
        xc = _ffn(xc, mod_c, row(norm_ffn1), wg1, wu1, wd1, 0)

        hy_x, gq_x, ml_x, rw_x = _inproj(x, mod_x, row(norm_mix), wi)
        hy_c, gq_c, ml_c, rw_c = (t.reshape(b, lc, -1) for t in _inproj(xc, mod_c, row(norm_mix), wi))

        w1p = jnp.zeros((LANES, HY_ORDER), F32).at[:HY_EMB].set(hy_f_w1[l])
        filt = (w1p, row(hy_f_b1), hy_f_w2[l], row(hy_f_b2), hy_f_w3[l], row(hy_f_b3), hy_f_w4[l],
                row(hy_f_freq))
        spec_x = _hy_spectrum(_hy_filters(length, *filt), row(hy_bias))
        y_hy_x = _hy_conv(hy_x, hy_conv_w[l], hy_conv_b[l], spec_x)

        gq = jnp.tile(row(gqa_q_norm), (1, 2))
        gk = jnp.tile(row(gqa_k_norm), (1, 2))
        q_l, k_l, v_l = _gqa_prep(gq_x, gq, gk, True)
        q_c, k_c, v_c = _gqa_prep(gq_c, gq, gk, False)
        y_gq_x = _attention(q_l, [(k_c, v_c), (k_l, v_l)], GQA_KV_HEADS)

        wuq = _mla_head_pad(mla_w_uq[l], MLA_QK).astype(BF16)
        wukv = mla_w_ukv[l].reshape(MLA_KV_RANK, MLA_HEADS, MLA_NOPE + MLA_V)
        wuk = _mla_head_pad(wukv[:, :, :MLA_NOPE].reshape(MLA_KV_RANK, -1), MLA_NOPE).astype(BF16)
        wuv = wukv[:, :, MLA_NOPE:].reshape(MLA_KV_RANK, -1).astype(BF16)
        pad_n = lambda g: jnp.pad(g[l], (0, LANES - MLA_QK)).reshape(1, LANES)
        mla_w = (row(mla_cq_norm), row(mla_ckv_norm), wuq, wuk, wuv, pad_n(mla_q_norm), pad_n(mla_k_norm))
        mq_l, mk_l, mv_l = _mla_prep(ml_x, *mla_w, True)
        mq_c, mk_c, mv_c = _mla_prep(ml_c, *mla_w, False)
        y_ml_x = _attention(mq_l, [(mk_c, mv_c), (mk_l, mv_l)], MLA_HEADS)

        rw_w = (row(rw_mu), row(rw_k_k), row(rw_k_a), rw_r_k[l].reshape(1, RW_C),
                rw_w0[l].reshape(1, 2 * RW_C), _blockdiag2(rw_w2[l]).astype(BF16),
                rw_a0[l].reshape(1, 2 * RW_C), _blockdiag2(rw_a2[l]).astype(BF16), rw_g2[l].astype(BF16))
        pc = _rw_prep(rw_c, *rw_w)
        px = _rw_prep(rw_x, *rw_w)
        zeros = jnp.zeros((b, 2, RW_C, RW_C), F32)
        yf_c, yb_c, s_ctx = _rw_scan(*pc[:6], zeros)
        yf_x, yb_x, _ = _rw_scan(*px[:6], s_ctx)
        y_rw_x = _rw_readout(yf_x, yb_x, px[6], px[7], row(rw_ln_w), row(rw_ln_b))

        x = _outproj(x, mod_x, (y_hy_x, y_gq_x, y_ml_x, y_rw_x), wo)
        if ctx_out:
            spec_c = _hy_spectrum(_hy_filters(lc, *filt), row(hy_bias))
            y_hy_c = _hy_conv(hy_c, hy_conv_w[l], hy_conv_b[l], spec_c)
            y_gq_c = _attention(q_c, [(k_c, v_c)], GQA_KV_HEADS)
            y_ml_c = _attention(mq_c, [(mk_c, mv_c)], MLA_HEADS)
            y_rw_c = _rw_readout(yf_c, yb_c, pc[6], pc[7], row(rw_ln_w), row(rw_ln_b))
            flat = lambda t: t.reshape(1, b * lc, -1)
            xc = _outproj(xc, mod_c, tuple(flat(t) for t in (y_hy_c, y_gq_c, y_ml_c, y_rw_c)), wo)

        x = _ffn(x, mod_x, row(norm_ffn2), wg2, wu2, wd2, 2)
        if ctx_out:
            xc = _ffn(xc, mod_c, row(norm_ffn2), wg2, wu2, wd2, 2)
    return x
```

```python
import functools
import math

import numpy as np
import jax
import jax.numpy as jnp
from jax import lax
from jax.experimental import pallas as pl
from jax.experimental.pallas import tpu as pltpu

F32 = jnp.float32
BF16 = jnp.bfloat16
HI = lax.Precision.HIGHEST

D_MODEL = 1024
GRID_W = 64
HEAD_DIM = 64
D_FF = 2816
N_MOD = 9
NORM_EPS = 1e-6
LOG2E = math.log2(math.e)
ROPE_THETA = 10000.0

HY_CH = 256
HY_EMB = 33
HY_ORDER = 64
HY_FAST_PCT = 0.3
HY_SLOW_PCT = 1.5
HY_TARGET = 1e-2

GQA_HEADS = 4
GQA_KV_HEADS = 2
MLA_HEADS = 4
MLA_NOPE = 64
MLA_ROPE = 32
MLA_QK = MLA_NOPE + MLA_ROPE
MLA_V = 64
MLA_Q_RANK = 256
MLA_KV_RANK = 128

RW_HEADS = 4
RW_N = 64
RW_C = RW_HEADS * RW_N
RW_DECAY_LORA = 64
RW_AAA_LORA = 64
RW_GATE_LORA = 128
RW_GN_EPS = 64e-5

HY_COLS = 3 * HY_CH
GQA_COLS = (GQA_HEADS + 2 * GQA_KV_HEADS) * HEAD_DIM
MLA_COLS = MLA_Q_RANK + MLA_KV_RANK + MLA_ROPE
MLA_COLS_PAD = 512
RW_COLS = 3 * RW_C + 2 * RW_DECAY_LORA + 2 * RW_AAA_LORA + RW_GATE_LORA
D_IN_PAD = HY_COLS + GQA_COLS + MLA_COLS_PAD + RW_COLS

LANES = 128
ROW_TILE = 512
Q_TILE = 512
KEY_CHUNK = 256
VT_ROWS = HEAD_DIM + 16
RW_CHUNK = 64
RW_BASE = 4
RW_LEVELS = 4
RW_BLOCK = 128
RW_SEQS = 4
VMEM_LIMIT = 56 * 2 ** 20


def _cparams(sem):
    return pltpu.CompilerParams(dimension_semantics=sem, vmem_limit_bytes=VMEM_LIMIT)


def _const_spec(shape):
    nd = len(shape)
    return pl.BlockSpec(shape, lambda *_: (0,) * nd, pipeline_mode=pl.Buffered(1))


def _dot(a, b, precision=None):
    return jnp.dot(a, b, preferred_element_type=F32, precision=precision)


def _dot_nt(a, b, precision=None):
    return lax.dot_general(a, b, (((1,), (1,)), ((), ())), preferred_element_type=F32,
                           precision=precision)


def _dot_tn(a, b, precision=None):
    return lax.dot_general(a, b, (((0,), (0,)), ((), ())), preferred_element_type=F32,
                           precision=precision)


def _pieces(x, n):
    out = []
    for _ in range(n):
        p = x.astype(BF16)
        out.append(p)
        x = x - p.astype(F32)
    return out


def _segsum(x, seg):
    return functools.reduce(jnp.add, [_dot(p, seg) for p in _pieces(x, 2)])


def _sigmoid(x):
    return 1.0 / (1.0 + jnp.exp(-x))


def _adaln(x, g, mod_ref, i):
    shift = mod_ref[pl.ds(3 * i, 1), :]
    scale = mod_ref[pl.ds(3 * i + 1, 1), :]
    r = lax.rsqrt(jnp.mean(x * x, axis=-1, keepdims=True) + NORM_EPS)
    return (x * r) * (g * (1.0 + scale)) + shift


def _mod_kernel(c_ref, w_ref, b_ref, o_ref):
    c = c_ref[...]
    s = c * _sigmoid(c)
    o_ref[...] = _dot(s, w_ref[...], HI) + b_ref[...]


def _compute_mod(c_all, ada_w, ada_b):
    depth, d, n = ada_w.shape
    rows = c_all.shape[0]
    tn = 1024
    return pl.pallas_call(
        _mod_kernel,
        grid=(depth, n // tn),
        in_specs=[
            pl.BlockSpec((rows, d), lambda l, j: (0, 0)),
            pl.BlockSpec((None, d, tn), lambda l, j: (l, 0, j)),
            pl.BlockSpec((None, 1, tn), lambda l, j: (l, 0, j)),
        ],
        out_specs=pl.BlockSpec((None, rows, tn), lambda l, j: (l, 0, j)),
        out_shape=jax.ShapeDtypeStruct((depth, rows, n), F32),
        compiler_params=_cparams(("arbitrary", "arbitrary")),
        name="mod",
    )(c_all, ada_w, ada_b.reshape(depth, 1, n))


def _ffn_kernel(x_ref, mod_ref, g_ref, wg_ref, wu_ref, wd_ref, o_ref, *, sub):
    x = x_ref[...]
    h = _adaln(x, g_ref[...], mod_ref, sub).astype(BF16)
    a = _dot(h, wg_ref[...])
    u = _dot(h, wu_ref[...])
    z = (a * _sigmoid(a) * u).astype(BF16)
    y = _dot(z, wd_ref[...])
    gate = mod_ref[pl.ds(3 * sub + 2, 1), :]
    o_ref[...] = x + (0.5 * gate) * y


def _ffn(x, mod, g, wg, wu, wd, sub):
    bm, rows, d = x.shape
    tm = min(ROW_TILE, rows)
    f = wg.shape[1]
    return pl.pallas_call(
        functools.partial(_ffn_kernel, sub=sub),
        grid=(bm, rows // tm),
        in_specs=[
            pl.BlockSpec((None, tm, d), lambda b, i: (b, i, 0)),
            pl.BlockSpec((None, N_MOD, d), lambda b, i: (b, 0, 0)),
            _const_spec((1, d)),
            _const_spec((d, f)),
            _const_spec((d, f)),
            _const_spec((f, d)),
        ],
        out_specs=pl.BlockSpec((None, tm, d), lambda b, i: (b, i, 0)),
        out_shape=jax.ShapeDtypeStruct(x.shape, F32),
        compiler_params=_cparams(("arbitrary", "arbitrary")),
        name="ffn",
    )(x, mod, g, wg, wu, wd)


_IN_OFFS = (0, HY_COLS, HY_COLS + GQA_COLS, HY_COLS + GQA_COLS + MLA_COLS_PAD, D_IN_PAD)


def _inproj_kernel(x_ref, mod_ref, g_ref, w_ref, hy_ref, gq_ref, ml_ref, rw_ref):
    h = _adaln(x_ref[...], g_ref[...], mod_ref, 1).astype(BF16)
    p = _dot(h, w_ref[...])
    for ref, lo, hi in zip((hy_ref, gq_ref, ml_ref, rw_ref), _IN_OFFS[:-1], _IN_OFFS[1:]):
        ref[...] = p[:, lo:hi]


def _inproj(x, mod, g, w):
    bm, rows, d = x.shape
    tm = min(ROW_TILE, rows)
    widths = [hi - lo for lo, hi in zip(_IN_OFFS[:-1], _IN_OFFS[1:])]
    return pl.pallas_call(
        _inproj_kernel,
        grid=(bm, rows // tm),
        in_specs=[
            pl.BlockSpec((None, tm, d), lambda b, i: (b, i, 0)),
            pl.BlockSpec((None, N_MOD, d), lambda b, i: (b, 0, 0)),
            _const_spec((1, d)),
            _const_spec((d, D_IN_PAD)),
        ],
        out_specs=[pl.BlockSpec((None, tm, wd), lambda b, i: (b, i, 0)) for wd in widths],
        out_shape=[jax.ShapeDtypeStruct((bm, rows, wd), F32) for wd in widths],
        compiler_params=_cparams(("arbitrary", "arbitrary")),
        name="inproj",
    )(x, mod, g, w)


def _mix_ffn_kernel(x_ref, mod_ref, yh_ref, yg_ref, ym_ref, yf_ref, yb_ref, bonus_ref, gate_ref,
                    lnw_ref, lnb_ref, seg_ref, wo_ref, g_ref, wg_ref, wu_ref, wd_ref, o_ref):
    y = yf_ref[...] + yb_ref[...]
    seg = seg_ref[...]
    yc = y - _segsum(y, seg)
    var = _segsum(yc * yc, seg)
    yn = yc * lax.rsqrt(var + RW_GN_EPS) * lnw_ref[...] + lnb_ref[...]
    y_rw = ((yn + bonus_ref[...]) * gate_ref[...]).astype(BF16)
    ycat = jnp.concatenate([yh_ref[...], yg_ref[...], ym_ref[...], y_rw], axis=-1)
    x = x_ref[...] + mod_ref[pl.ds(5, 1), :] * _dot(ycat, wo_ref[...])
    h = _adaln(x, g_ref[...], mod_ref, 2).astype(BF16)
    a = _dot(h, wg_ref[...])
    u = _dot(h, wu_ref[...])
    z = (a * _sigmoid(a) * u).astype(BF16)
    o_ref[...] = x + (0.5 * mod_ref[pl.ds(8, 1), :]) * _dot(z, wd_ref[...])


def _mix_ffn(x, mod, ys, rw, ln_w, ln_b, wo, g, wg, wu, wd):
    bm, rows, d = x.shape
    tm = min(ROW_TILE, rows)
    c = RW_C
    seg = jnp.asarray(_seg_matrix(c, RW_N, 1.0 / RW_N), BF16)
    row_spec = pl.BlockSpec((None, tm, d), lambda b, i: (b, i, 0))
    y_spec = pl.BlockSpec((None, tm, c), lambda b, i: (b, i, 0))
    consts = (ln_w, ln_b, seg, wo, g, wg, wu, wd)
    return pl.pallas_call(
        _mix_ffn_kernel,
        grid=(bm, rows // tm),
        in_specs=[row_spec, pl.BlockSpec((None, N_MOD, d), lambda b, i: (b, 0, 0))] + [y_spec] * 7
                 + [_const_spec(a.shape) for a in consts],
        out_specs=row_spec,
        out_shape=jax.ShapeDtypeStruct(x.shape, F32),
        compiler_params=_cparams(("arbitrary", "arbitrary")),
        name="mix_ffn",
    )(x, mod, *ys, *rw, *consts)


def _hy_features(length):
    t01 = np.linspace(0.0, 1.0, length, dtype=np.float32)[:, None]
    bands = (HY_EMB - 1) // 2
    w_ang = (np.float32(2.0 * math.pi) * np.arange(length, dtype=np.float32)[:, None]
             / np.float32(length)).astype(np.float32)
    f = np.linspace(1e-4, bands - 1, bands, dtype=np.float32)[None]
    arg = (f * w_ang).astype(np.float32)
    z = np.concatenate([t01, np.cos(arg), -np.sin(arg)], axis=-1).astype(np.float32)
    zp = np.zeros((length, LANES), np.float32)
    zp[:, :HY_EMB] = z
    return zp


def _hy_deltas():
    max_decay = math.log(HY_TARGET) / HY_FAST_PCT
    min_decay = math.log(HY_TARGET) / HY_SLOW_PCT
    d = np.abs(np.linspace(min_decay, max_decay, HY_CH, dtype=np.float32))
    return np.tile(d, 2)[None].astype(np.float32)


def _hyfilt_kernel(z_ref, w1_ref, b1_ref, w2_ref, b2_ref, w3_ref, b3_ref, w4_ref, fr_ref,
                   dl_ref, o_ref):
    z = z_ref[...]
    fr = fr_ref[...]
    h = jnp.sin(fr * (_dot(z, w1_ref[...], HI) + b1_ref[...]))
    h = jnp.sin(fr * (_dot(h, w2_ref[...], HI) + b2_ref[...]))
    h = jnp.sin(fr * (_dot(h, w3_ref[...], HI) + b3_ref[...]))
    h = _dot(h, w4_ref[...], HI)
    o_ref[...] = h * jnp.exp(-z[:, 0:1] * dl_ref[...])


def _hy_filters(length, w1p, b1, w2, b2, w3, b3, w4, freq):
    z = jnp.asarray(_hy_features(length))
    dl = jnp.asarray(_hy_deltas())
    tl = min(ROW_TILE, length)
    consts = (w1p, b1, w2, b2, w3, b3, w4, freq, dl)
    return pl.pallas_call(
        _hyfilt_kernel,
        grid=(length // tl,),
        in_specs=[pl.BlockSpec((tl, LANES), lambda i: (i, 0))] + [_const_spec(a.shape) for a in consts],
        out_specs=pl.BlockSpec((tl, 2 * HY_CH), lambda i: (i, 0)),
        out_shape=jax.ShapeDtypeStruct((length, 2 * HY_CH), F32),
        compiler_params=_cparams(("arbitrary",)),
        name="hyena_filters",
    )(z, *consts)


FFT_SLAB_PAD = 8
FFT_GROUP = 8


def _fft_factors(length):
    return (128, 64) if length >= 2048 else (2 * length, 1)


def _fft_layout(n1, n2):
    return (n1, n2, 2 * n2 + FFT_SLAB_PAD) if n2 > 1 else (1, n1, 2 * n1)


def _stack3(t):
    hi = t.astype(jnp.bfloat16)
    lo = (t - hi.astype(np.float64)).astype(jnp.bfloat16)
    return np.concatenate([hi, lo, hi], axis=-1)


def _rows3(d):
    hi = d.astype(BF16)
    lo = (d - hi.astype(F32)).astype(BF16)
    return jnp.concatenate([hi, hi, lo], axis=0)


@functools.lru_cache(maxsize=None)
def _fft_tables(length):
    n1, n2 = _fft_factors(length)
    n = n1 * n2
    assert n == 2 * length
    a_n2 = np.arange(n2)[:, None, None]
    a_k1 = np.arange(n1)[None, :, None]
    a_n1 = np.arange(n1 // 2)[None, None, :]
    ang = 2.0 * np.pi * ((a_n1 * a_k1 % n1) / n1 + (a_n2 * a_k1 % n) / n)
    t_re, t_im = np.cos(ang), -np.sin(ang)
    ta = np.concatenate([t_re, t_im], axis=1)
    tai = np.concatenate([np.transpose(t_re, (0, 2, 1)), np.transpose(t_im, (0, 2, 1))], axis=2) / n
    jj = np.arange(n2)
    ang2 = 2.0 * np.pi * (np.outer(jj, jj) % n2) / n2
    c, s = np.cos(ang2), np.sin(ang2)
    fb = np.block([[c, s], [-s, c]])
    fbi = np.block([[c, -s], [s, c]])
    return tuple(_stack3(t) for t in (ta, tai, fb, fbi))


def _fft_stage_a(src_ref, fbuf_ref, ta_ref, n1, n2):
    if n2 == 1:
        fbuf_ref[...] = _dot(ta_ref[0], _rows3(src_ref[...]))
        return
    pitch = _fft_layout(n1, n2)[2]

    def body(i, carry):
        js = [i * FFT_GROUP + g for g in range(FFT_GROUP)]
        rows = [_rows3(src_ref[pl.ds(j, n1 // 2, stride=n2), :]) for j in js]
        ts = [_dot(ta_ref[j], r) for j, r in zip(js, rows)]
        for j, t in zip(js, ts):
            fbuf_ref[pl.ds(j, n1, stride=pitch), :] = t[:n1]
            fbuf_ref[pl.ds(n2 + j, n1, stride=pitch), :] = t[n1:]
        return carry

    lax.fori_loop(0, n2 // FFT_GROUP, body, 0)


def _spec_kernel(hf_ref, hb_ref, bias_ref, ta_ref, fb_ref, o_ref, fbuf_ref, *, n1, n2):
    nslab, hs, pitch = _fft_layout(n1, n2)
    if pitch > 2 * hs:
        o_ref[...] = jnp.zeros(o_ref.shape, F32)
    _fft_stage_a(hf_ref, o_ref, ta_ref, n1, n2)
    _fft_stage_a(hb_ref, fbuf_ref, ta_ref, n1, n2)
    is_re = lax.broadcasted_iota(jnp.int32, (2 * hs, 1), 0) < hs
    sgn = jnp.where(is_re, 1.0, -1.0).astype(F32)
    skip = jnp.where(is_re, bias_ref[...], 0.0)

    grp = FFT_GROUP if n2 > 1 else 1

    def stage_b(i, carry):
        sls = [pl.ds(pl.multiple_of((i * grp + g) * pitch, 8), 2 * hs) for g in range(grp)]
        a = [o_ref[sl, :] for sl in sls]
        b = [fbuf_ref[sl, :] for sl in sls]
        if n2 > 1:
            a = [_dot(fb_ref[...], _rows3(x)) for x in a]
            b = [_dot(fb_ref[...], _rows3(x)) for x in b]
        for sl, x, y in zip(sls, a, b):
            o_ref[sl, :] = x + sgn * y + skip
        return carry

    lax.fori_loop(0, nslab // grp, stage_b, 0)


def _hy_spectrum(h, bias):
    length = h.shape[0]
    n1, n2 = _fft_factors(length)
    nslab, _, pitch = _fft_layout(n1, n2)
    ta, _, fb, _ = (jnp.asarray(t) for t in _fft_tables(length))
    nblk = HY_CH // LANES
    return pl.pallas_call(
        functools.partial(_spec_kernel, n1=n1, n2=n2),
        grid=(nblk,),
        in_specs=[pl.BlockSpec((length, LANES), lambda j: (0, j)),
                  pl.BlockSpec((length, LANES), lambda j: (0, nblk + j)),
                  pl.BlockSpec((1, LANES), lambda j: (0, j)),
                  _const_spec(ta.shape), _const_spec(fb.shape)],
        out_specs=pl.BlockSpec((nslab * pitch, LANES), lambda j: (0, j)),
        out_shape=jax.ShapeDtypeStruct((nslab * pitch, HY_CH), F32),
        scratch_shapes=[pltpu.VMEM((nslab * pitch, LANES), F32)],
        compiler_params=_cparams(("arbitrary",)),
        name="hyena_spectrum",
    )(h, h, bias, ta, fb)


def _shift_rows(x, prev_row, next_row):
    n = x.shape[0]
    row = lax.broadcasted_iota(jnp.int32, x.shape, 0)
    up = jnp.where(row == 0, prev_row, pltpu.roll(x, 1, 0))
    dn = jnp.where(row == n - 1, next_row, pltpu.roll(x, n - 1, 0))
    return up, dn


def _hyconv_kernel(x1_ref, x2_ref, v_ref, cw1_ref, cw2_ref, cwv_ref, cb_ref, g_ref,
                   ta_ref, tai_ref, fb_ref, fbi_ref, o_ref, u_ref, fbuf_ref, *, n1, n2):
    def conv3(ref, w_ref, b):
        x = ref[...]
        up, dn = _shift_rows(x, 0.0, 0.0)
        return up * w_ref[0:1, :] + x * w_ref[1:2, :] + dn * w_ref[2:3, :] + b

    cb = cb_ref[...]
    u_ref[...] = conv3(x1_ref, cw1_ref, cb[0:1, :]) * conv3(v_ref, cwv_ref, cb[2:3, :])

    _fft_stage_a(u_ref, fbuf_ref, ta_ref, n1, n2)
    nslab, hs, pitch = _fft_layout(n1, n2)

    grp = FFT_GROUP if n2 > 1 else 1

    def cmul(a, g):
        ar, ai, gr, gi = a[:hs], a[hs:], g[:hs], g[hs:]
        return jnp.concatenate([ar * gr - ai * gi, ar * gi + ai * gr], axis=0)

    def stage_b(i, carry):
        sls = [pl.ds(pl.multiple_of((i * grp + g) * pitch, 8), 2 * hs) for g in range(grp)]
        a = [fbuf_ref[sl, :] for sl in sls]
        if n2 > 1:
            a = [_dot(fb_ref[...], _rows3(x)) for x in a]
        a = [cmul(x, g_ref[sl, :]) for x, sl in zip(a, sls)]
        if n2 > 1:
            a = [_dot(fbi_ref[...], _rows3(x)) for x in a]
        for sl, x in zip(sls, a):
            fbuf_ref[sl, :] = x
        return carry

    lax.fori_loop(0, nslab // grp, stage_b, 0)

    if n2 == 1:
        u_ref[...] = _dot(tai_ref[0], _rows3(fbuf_ref[...]))
    else:
        def stage_a_inv(i, carry):
            js = [i * FFT_GROUP + g for g in range(FFT_GROUP)]
            cs = [_rows3(jnp.concatenate([fbuf_ref[pl.ds(j, n1, stride=pitch), :],
                                          fbuf_ref[pl.ds(n2 + j, n1, stride=pitch), :]], axis=0))
                  for j in js]
            ys = [_dot(tai_ref[j], c) for j, c in zip(js, cs)]
            for j, y in zip(js, ys):
                u_ref[pl.ds(j, n1 // 2, stride=n2), :] = y
            return carry

        lax.fori_loop(0, n2 // FFT_GROUP, stage_a_inv, 0)

    o_ref[...] = (conv3(x2_ref, cw2_ref, cb[1:2, :]) * u_ref[...]).astype(o_ref.dtype)


def _hy_conv(p, conv_w, conv_b, spec):
    b, length, _ = p.shape
    n1, n2 = _fft_factors(length)
    nslab, _, pitch = _fft_layout(n1, n2)
    ta, tai, fb, fbi = (jnp.asarray(t) for t in _fft_tables(length))
    nblk = HY_CH // LANES
    cb3 = conv_b.reshape(3, HY_CH)
    col = lambda g: pl.BlockSpec((None, length, LANES), lambda j, i, g=g: (i, 0, g * nblk + j))
    wcol = lambda g: pl.BlockSpec((3, LANES), lambda j, i, g=g: (0, g * nblk + j))
    return pl.pallas_call(
        functools.partial(_hyconv_kernel, n1=n1, n2=n2),
        grid=(nblk, b),
        in_specs=[col(0), col(1), col(2), wcol(0), wcol(1), wcol(2),
                  pl.BlockSpec((3, LANES), lambda j, i: (0, j)),
                  pl.BlockSpec((nslab * pitch, LANES), lambda j, i: (0, j), pipeline_mode=pl.Buffered(1)),
                  _const_spec(ta.shape), _const_spec(tai.shape), _const_spec(fb.shape),
                  _const_spec(fbi.shape)],
        out_specs=pl.BlockSpec((None, length, LANES), lambda j, i: (i, 0, j)),
        out_shape=jax.ShapeDtypeStruct((b, length, HY_CH), BF16),
        scratch_shapes=[pltpu.VMEM((length, LANES), F32), pltpu.VMEM((nslab * pitch, LANES), F32)],
        compiler_params=_cparams(("arbitrary", "arbitrary")),
        name="hyena_conv",
    )(p, p, p, conv_w, conv_w, conv_w, cb3, spec, ta, tai, fb, fbi)


def _rope_tables(length, d_rot, lane_lo, head_w):
    rows = length // GRID_W
    row = np.repeat(np.arange(rows, dtype=np.float32), GRID_W)
    colv = np.tile(np.arange(GRID_W, dtype=np.float32), rows)
    n_freq = d_rot // 4
    inv = (np.float32(ROPE_THETA) ** (-np.arange(n_freq, dtype=np.float32) / np.float32(n_freq))).astype(np.float32)
    ang = np.concatenate([row[:, None] * inv, colv[:, None] * inv], axis=-1).astype(np.float32)
    cos_t = np.ones((length, LANES), np.float32)
    sin_t = np.zeros((length, LANES), np.float32)
    c, s = np.cos(ang), np.sin(ang)
    for base in range(0, LANES, head_w):
        for i in range(d_rot // 2):
            cos_t[:, base + lane_lo + 2 * i] = c[:, i]
            cos_t[:, base + lane_lo + 2 * i + 1] = c[:, i]
            sin_t[:, base + lane_lo + 2 * i] = -s[:, i]
            sin_t[:, base + lane_lo + 2 * i + 1] = s[:, i]
    return cos_t, sin_t


def _rope(x, cos_t, sin_t):
    lane = lax.broadcasted_iota(jnp.int32, x.shape, 1)
    w = x.shape[1]
    partner = jnp.where(jnp.bitwise_and(lane, 1) == 0, pltpu.roll(x, w - 1, 1), pltpu.roll(x, 1, 1))
    return x * cos_t + partner * sin_t


def _store_vt(vt_ref, h, vt):
    row = lax.broadcasted_iota(jnp.int32, (VT_ROWS - HEAD_DIM, vt.shape[1]), 0)
    vt_ref[h, 0:HEAD_DIM, :] = vt.astype(BF16)
    vt_ref[h, HEAD_DIM:VT_ROWS, :] = jnp.where(row == 0, 1.0, 0.0).astype(BF16)


def _gqa_prep_kernel(p_ref, cos_ref, sin_ref, gq_ref, gk_ref, seg_ref, q_ref, k_ref, vt_ref, *, rope):
    p = p_ref[...]
    seg = seg_ref[...]
    lane = lax.broadcasted_iota(jnp.int32, (p.shape[0], LANES), 1)
    low = lane < HEAD_DIM

    def hnorm(x, g):
        ms = _segsum(x * x, seg)
        return x * lax.rsqrt(ms + NORM_EPS) * g

    scale = HEAD_DIM ** -0.5 * LOG2E
    for c in range(2):
        q = hnorm(p[:, c * LANES:(c + 1) * LANES], gq_ref[...])
        if rope:
            q = _rope(q, cos_ref[...], sin_ref[...])
        q = q * scale
        qs = pltpu.roll(q, HEAD_DIM, 1)
        q_ref[:, (2 * c) * LANES:(2 * c + 1) * LANES] = jnp.where(low, q, 0.0).astype(BF16)
        q_ref[:, (2 * c + 1) * LANES:(2 * c + 2) * LANES] = jnp.where(low, qs, 0.0).astype(BF16)
    k = hnorm(p[:, 256:384], gk_ref[...])
    if rope:
        k = _rope(k, cos_ref[...], sin_ref[...])
    ks = pltpu.roll(k, HEAD_DIM, 1)
    k_ref[:, 0:LANES] = jnp.where(low, k, 0.0).astype(BF16)
    k_ref[:, LANES:2 * LANES] = jnp.where(low, ks, 0.0).astype(BF16)
    vt = p[:, 384:512].T
    _store_vt(vt_ref, 0, vt[0:HEAD_DIM])
    _store_vt(vt_ref, 1, vt[HEAD_DIM:2 * HEAD_DIM])


def _seg_matrix(width, seg, value):
    i = np.arange(width)
    return ((i[:, None] // seg) == (i[None, :] // seg)).astype(np.float32) * np.float32(value)


def _gqa_prep(p, gq, gk, rope):
    b, length, _ = p.shape
    tm = min(ROW_TILE, length)
    if rope:
        cos_t, sin_t = (jnp.asarray(t) for t in _rope_tables(length, HEAD_DIM, 0, HEAD_DIM))
    else:
        cos_t = sin_t = jnp.zeros((length, LANES), F32)
    seg = jnp.asarray(_seg_matrix(LANES, HEAD_DIM, 1.0 / HEAD_DIM), BF16)
    tab = pl.BlockSpec((tm, LANES), lambda i, bb: (i, 0))
    outw = (512, 256)
    return pl.pallas_call(
        functools.partial(_gqa_prep_kernel, rope=rope),
        grid=(length // tm, b),
        in_specs=[pl.BlockSpec((None, tm, GQA_COLS), lambda i, bb: (bb, i, 0)), tab, tab,
                  _const_spec((1, LANES)), _const_spec((1, LANES)), _const_spec((LANES, LANES))],
        out_specs=[pl.BlockSpec((None, tm, w), lambda i, bb: (bb, i, 0)) for w in outw]
                  + [pl.BlockSpec((None, GQA_KV_HEADS, VT_ROWS, tm), lambda i, bb: (bb, 0, 0, i))],
        out_shape=[jax.ShapeDtypeStruct((b, length, w), BF16) for w in outw]
                  + [jax.ShapeDtypeStruct((b, GQA_KV_HEADS, VT_ROWS, length), BF16)],
        compiler_params=_cparams(("arbitrary", "arbitrary")),
        name="gqa_prep",
    )(p, cos_t, sin_t, gq, gk, seg)


def _mla_prep_kernel(p_ref, cos_ref, sin_ref, cqn_ref, ckvn_ref, wuq_ref, wuk_ref, wuv_ref,
                     qn_ref, kn_ref, q_ref, k_ref, vt_ref, *, rope):
    p = p_ref[...]

    def rms(x, g):
        return x * lax.rsqrt(jnp.mean(x * x, axis=-1, keepdims=True) + NORM_EPS) * g

    cq = rms(p[:, 0:MLA_Q_RANK], cqn_ref[...]).astype(BF16)
    ckv = rms(p[:, MLA_Q_RANK:MLA_Q_RANK + MLA_KV_RANK], ckvn_ref[...]).astype(BF16)
    q = _dot(cq, wuq_ref[...])
    kn = _dot(ckv, wuk_ref[...])
    v = _dot(ckv, wuv_ref[...])
    for c in range(MLA_HEADS // 2):
        vt = v[:, c * LANES:(c + 1) * LANES].T
        _store_vt(vt_ref, 2 * c, vt[0:MLA_V])
        _store_vt(vt_ref, 2 * c + 1, vt[MLA_V:2 * MLA_V])
    lane = lax.broadcasted_iota(jnp.int32, (p.shape[0], LANES), 1)
    in_rope = jnp.logical_and(lane >= MLA_NOPE, lane < MLA_QK)
    kr = jnp.where(in_rope, pltpu.roll(p[:, 384:512], MLA_NOPE, 1), 0.0)
    scale = MLA_QK ** -0.5 * LOG2E

    def hnorm(x, g):
        ms = jnp.sum(x * x, axis=-1, keepdims=True) * (1.0 / MLA_QK)
        return x * lax.rsqrt(ms + NORM_EPS) * g

    for h in range(MLA_HEADS):
        sl = slice(h * LANES, (h + 1) * LANES)
        qh = hnorm(q[:, sl], qn_ref[...])
        kh = hnorm(kn[:, sl] + kr, kn_ref[...])
        if rope:
            qh = _rope(qh, cos_ref[...], sin_ref[...])
            kh = _rope(kh, cos_ref[...], sin_ref[...])
        q_ref[:, sl] = (qh * scale).astype(BF16)
        k_ref[:, sl] = kh.astype(BF16)


def _mla_prep(p, cqn, ckvn, wuq, wuk, wuv, qn, kn, rope):
    b, length, _ = p.shape
    tm = min(ROW_TILE, length)
    if rope:
        cos_t, sin_t = (jnp.asarray(t) for t in _rope_tables(length, MLA_ROPE, MLA_NOPE, LANES))
    else:
        cos_t = sin_t = jnp.zeros((length, LANES), F32)
    tab = pl.BlockSpec((tm, LANES), lambda i, bb: (i, 0))
    consts = (cqn, ckvn, wuq, wuk, wuv, qn, kn)
    outw = (512, 512)
    return pl.pallas_call(
        functools.partial(_mla_prep_kernel, rope=rope),
        grid=(length // tm, b),
        in_specs=[pl.BlockSpec((None, tm, MLA_COLS_PAD), lambda i, bb: (bb, i, 0)), tab, tab]
                 + [_const_spec(a.shape) for a in consts],
        out_specs=[pl.BlockSpec((None, tm, w), lambda i, bb: (bb, i, 0)) for w in outw]
                  + [pl.BlockSpec((None, MLA_HEADS, VT_ROWS, tm), lambda i, bb: (bb, 0, 0, i))],
        out_shape=[jax.ShapeDtypeStruct((b, length, w), BF16) for w in outw]
                  + [jax.ShapeDtypeStruct((b, MLA_HEADS, VT_ROWS, length), BF16)],
        compiler_params=_cparams(("arbitrary", "arbitrary")),
        name="mla_prep",
    )(p, cos_t, sin_t, *consts)


def _attn_kernel(*refs, n_seg):
    q_ref, o_ref = refs[0], refs[-1]
    segs = [refs[1 + 4 * s:5 + 4 * s] for s in range(n_seg)]
    q = q_ref[...]
    qh = [q[:, side * LANES:(side + 1) * LANES] for side in range(2)]
    pieces = [(side, seg, c0, min(KEY_CHUNK, seg[0].shape[0]))
              for side in range(2) for seg in segs
              for c0 in range(0, seg[0].shape[0], min(KEY_CHUNK, seg[0].shape[0]))]
    scores = [_dot_nt(seg[side][c0:c0 + n, :], qh[side]) for side, seg, c0, n in pieces]
    mx = [jnp.max(s, axis=0, keepdims=True) for s in scores]
    ps = [jnp.exp2(s - m).astype(BF16) for s, m in zip(scores, mx)]
    os_ = [_dot(seg[2 + side][:, c0:c0 + n], e) for (side, seg, c0, n), e in zip(pieces, ps)]
    outs = []
    for side in range(2):
        idx = [i for i, pc in enumerate(pieces) if pc[0] == side]
        m = functools.reduce(jnp.maximum, [mx[i] for i in idx])
        o = functools.reduce(jnp.add, [os_[i] * jnp.exp2(mx[i] - m) for i in idx])
        outs.append(o[0:HEAD_DIM] / o[HEAD_DIM:HEAD_DIM + 1])
    o_ref[...] = jnp.concatenate(outs, axis=0).T.astype(o_ref.dtype)


def _attention(q, kv_segs, k_heads):
    b, lq, _ = q.shape
    tq = min(Q_TILE, lq)
    ka = (lambda pr: pr) if k_heads == 2 else (lambda pr: 2 * pr)
    kb = (lambda pr: pr) if k_heads == 2 else (lambda pr: 2 * pr + 1)
    in_specs = [pl.BlockSpec((None, tq, 2 * LANES), lambda bb, pr, i: (bb, i, pr))]
    args = [q]
    for k, vt in kv_segs:
        lk = k.shape[1]
        in_specs += [pl.BlockSpec((None, lk, LANES), lambda bb, pr, i, f=ka: (bb, 0, f(pr))),
                     pl.BlockSpec((None, lk, LANES), lambda bb, pr, i, f=kb: (bb, 0, f(pr))),
                     pl.BlockSpec((None, None, VT_ROWS, lk), lambda bb, pr, i, f=ka: (bb, f(pr), 0, 0)),
                     pl.BlockSpec((None, None, VT_ROWS, lk), lambda bb, pr, i, f=kb: (bb, f(pr), 0, 0))]
        args += [k, k, vt, vt]
    return pl.pallas_call(
        functools.partial(_attn_kernel, n_seg=len(kv_segs)),
        grid=(b, 2, lq // tq),
        in_specs=in_specs,
        out_specs=pl.BlockSpec((None, tq, LANES), lambda bb, pr, i: (bb, i, pr)),
        out_shape=jax.ShapeDtypeStruct((b, lq, 2 * LANES), BF16),
        compiler_params=_cparams(("arbitrary", "arbitrary", "arbitrary")),
        name="attention",
    )(*args)


def _rw_prep_kernel(p_ref, prev_ref, next_ref, mu_ref, kk_ref_w, ka_ref, rk_ref, w0_ref, w2_ref,
                    a0_ref, a2_ref, g2_ref, seg_ref,
                    r_ref, v_ref, kk_ref, lw_ref, kd_ref, bd_ref, bonus_ref, g_ref):
    i = pl.program_id(1)
    n = pl.num_programs(1)
    x = p_ref[...]
    prev_row = jnp.where(i > 0, prev_ref[7:8, :], 0.0)
    next_row = jnp.where(i < n - 1, next_ref[0:1, :], 0.0)
    up, dn = _shift_rows(x, prev_row, next_row)
    xs = x + mu_ref[...] * (0.5 * (up + dn) - x)
    c = RW_C
    r, k, v = xs[:, 0:c], xs[:, c:2 * c], xs[:, 2 * c:3 * c]
    xw, xa, xg = xs[:, 3 * c:3 * c + 128], xs[:, 3 * c + 128:3 * c + 256], xs[:, 3 * c + 256:3 * c + 384]
    seg = seg_ref[...]
    kk = k * kk_ref_w[...]
    nrm = jnp.sqrt(_segsum(kk * kk, seg))
    kk = kk / jnp.maximum(nrm, 1e-12)
    u = w0_ref[...] + _dot(jnp.tanh(xw).astype(BF16), w2_ref[...])
    z = -u
    softplus = jnp.maximum(z, 0.0) + jnp.log(1.0 + jnp.exp(-jnp.abs(z)))
    lw_ref[...] = -jnp.exp(-softplus - 0.5)
    a = _sigmoid(a0_ref[...] + _dot(xa.astype(BF16), a2_ref[...]))
    bonus = jnp.zeros_like(r)
    for d in range(2):
        a_d = a[:, d * c:(d + 1) * c]
        k_d = k * (1.0 + (a_d - 1.0) * ka_ref[...])
        kd_ref[:, d * c:(d + 1) * c] = k_d
        bd_ref[:, d * c:(d + 1) * c] = kk * a_d
        bonus = bonus + _segsum(r * k_d * rk_ref[...], seg)
    r_ref[...] = r
    v_ref[...] = v
    kk_ref[...] = kk
    bonus_ref[...] = bonus * v
    g_ref[...] = _dot(_sigmoid(xg).astype(BF16), g2_ref[...])


def _rw_prep(p, mu, k_k, k_a, r_k, w0, w2bd, a0, a2bd, g2):
    b, length, cols = p.shape
    tm = min(ROW_TILE, length)
    nb = tm // 8
    last = length // 8 - 1
    seg = jnp.asarray(_seg_matrix(RW_C, RW_N, 1.0), BF16)
    consts = (mu, k_k, k_a, r_k, w0, w2bd, a0, a2bd, g2, seg)
    outw = (RW_C, RW_C, RW_C, 2 * RW_C, 2 * RW_C, 2 * RW_C, RW_C, RW_C)
    return pl.pallas_call(
        _rw_prep_kernel,
        grid=(b, length // tm),
        in_specs=[pl.BlockSpec((None, tm, cols), lambda bb, i: (bb, i, 0)),
                  pl.BlockSpec((None, 8, cols), lambda bb, i: (bb, jnp.maximum(i * nb - 1, 0), 0)),
                  pl.BlockSpec((None, 8, cols), lambda bb, i: (bb, jnp.minimum((i + 1) * nb, last), 0))]
                 + [_const_spec(a.shape) for a in consts],
        out_specs=[pl.BlockSpec((None, tm, w), lambda bb, i: (bb, i, 0)) for w in outw],
        out_shape=[jax.ShapeDtypeStruct((b, length, w), F32) for w in outw],
        compiler_params=_cparams(("arbitrary", "arbitrary")),
        name="rwkv_prep",
    )(p, p, p, *consts)


@functools.lru_cache(maxsize=None)
def _rw_masks():
    cs, n = RW_CHUNK, RW_C
    i = np.arange(n)
    same = (i[:, None] // cs) == (i[None, :] // cs)
    rel = (i[None, :] % cs) - (i[:, None] % cs)
    masks = np.stack([same & (rel < 0), same & (rel <= 0), same & (rel > 0), same & (rel >= 0),
                      same, i[:, None] == i[None, :]]).astype(np.float32)
    t = np.arange(cs)
    tri = np.stack([t[None, :] <= t[:, None], t[None, :] >= t[:, None]]).astype(np.float32)
    tr, tc = (i % cs)[:, None], (i % cs)[None, :]
    lvl = []
    for sgn in (1, -1):
        earlier = (tc - tr) * sgn < 0
        rows = [same & earlier & (tr // RW_BASE == tc // RW_BASE)]
        for k in range(RW_LEVELS):
            s = RW_BASE << k
            rows.append(same & earlier & (tr // (2 * s) == tc // (2 * s)) & (tr // s != tc // s))
        lvl.append(np.stack(rows))
    return masks, tri, np.stack(lvl).astype(np.float32)


def _rw_scan_kernel(rf_ref, rb_ref, vf_ref, vb_ref, kkf_ref, kkb_ref, lwf_ref, lwb_ref, kdf_ref, kdb_ref,
                    bf_ref, bb_ref, s0_ref, mask_ref, tri_ref, lvl_ref, yf_ref, yb_ref, sfin_ref, s_ref,
                    *, n_chunks, n_batch):
    cs = RW_CHUNK

    @pl.when(pl.program_id(1) == 0)
    def _():
        s_ref[...] = s0_ref[...]

    same_head = mask_ref[4].astype(BF16)
    eye = mask_ref[5]
    dirs = ((rf_ref, vf_ref, kkf_ref, lwf_ref, kdf_ref, bf_ref, yf_ref),
            (rb_ref, vb_ref, kkb_ref, lwb_ref, kdb_ref, bb_ref, yb_ref))

    def tile(x):
        xb = x.astype(BF16)
        return jnp.concatenate([xb, xb, xb, xb], axis=0)

    def bd(x):
        return tile(x) * same_head

    def spread(a, half):
        swapped = pltpu.roll(a, cs, 1)
        low = lax.broadcasted_iota(jnp.int32, a.shape, 1) < cs
        h = jnp.where(low, swapped, a) if half else jnp.where(low, a, swapped)
        return jnp.concatenate([h, h], axis=1)

    streams = [(bi, d) for bi in range(n_batch) for d in range(2)]

    def each(f, *cols):
        return [f(*a) for a in zip(*cols)] if cols else [f(bi, d) for bi, d in streams]

    bf = lambda xs: [x.astype(BF16) for x in xs]

    def body(c, carry):
        def load(bi, d):
            cc = c if d == 0 else n_chunks - 1 - c
            sl = pl.ds(pl.multiple_of(cc * cs, cs), cs)
            return [ref[bi, sl, :] for ref in dirs[d][:6]] + [sl]

        r, v, kk, lw, kd, b, sl = zip(*each(load))
        strict = [mask_ref[2 * d] for _, d in streams]
        incl = [mask_ref[2 * d + 1] for _, d in streams]
        cl = [functools.reduce(jnp.add, [_dot(tri_ref[d], p) for p in _pieces(x, 3)])
              for (_, d), x in zip(streams, lw)]
        w_inv = [jnp.exp(-x) for x in cl]
        w_all = [jnp.exp(jnp.sum(x, axis=0, keepdims=True)) for x in lw]
        kt = each(lambda a, w: a * w, kd, w_inv)
        bt = each(lambda a, w: a * w, b, w_inv)
        qk_s = each(lambda a, x, y: bd(a * jnp.exp(x - y)), kk, cl, lw)
        rt_s = each(lambda a, x: bd(a * jnp.exp(x)), r, cl)
        kb = each(lambda a, t: jnp.concatenate([a.astype(BF16), t.astype(BF16)], axis=0), kt, bt)
        aq = each(_dot_nt, qk_s, kb)
        ar = each(_dot_nt, rt_s, kb)
        a_kk = each(lambda a, m: (spread(a, 0) * m).astype(BF16), aq, strict)
        a_kb = each(lambda a, m: spread(a, 1) * m, aq, strict)
        a_rk = each(lambda a, m: (spread(a, 0) * m).astype(BF16), ar, incl)
        a_rb = each(lambda a, m: (spread(a, 1) * m).astype(BF16), ar, incl)
        dmask = lambda k: [lvl_ref[d, k] for _, d in streams]
        l0 = each(lambda a, m: a * m, a_kb, dmask(0))
        p0 = [eye - a for a in l0]
        sq = each(lambda a: _dot(a, a), bf(l0))
        t_inv = each(lambda p, pb, s: p + _dot(pb, s), p0, bf(p0), bf(sq))
        for k in range(1, RW_LEVELS + 1):
            cm = bf(each(lambda a, m: a * m, a_kb, dmask(k)))
            tb = bf(t_inv)
            tc = bf(each(_dot, tb, cm))
            t_inv = each(lambda t, x, y: t - _dot(x, y), t_inv, tc, tb)
        t_inv = bf(t_inv)
        v_s = each(bd, v)
        akv = bf(each(_dot, a_kk, v_s))
        x1 = bf(each(_dot, t_inv, qk_s))
        x2 = each(_dot, t_inv, akv)
        bh_s = each(lambda a, w: bd(a * w), bt, w_all)
        kh_s = each(lambda a, w: bd(a * w), kt, w_all)
        y_v = each(_dot, a_rk, v_s)
        s_v = each(_dot_tn, v_s, kh_s)
        s0f = [s_ref[bi, d] for bi, d in streams]
        s0 = bf(s0f)
        u = bf(each(lambda x, s, y: _dot_nt(x, s) + y, x1, s0, x2))
        y_bd = each(lambda rt, s, yv, ab, u_: _dot_nt(rt, s) + yv - _dot(ab, u_), rt_s, s0, y_v, a_rb, u)
        s1 = each(lambda s, w, sv, u_, bh: s * w + sv - _dot_tn(u_, bh), s0f, w_all, s_v, u, bh_s)
        for (bi, d), y, s, rows in zip(streams, y_bd, s1, sl):
            dirs[d][6][bi, rows, :] = y[0:cs] + y[cs:2 * cs] + y[2 * cs:3 * cs] + y[3 * cs:4 * cs]
            s_ref[bi, d] = s
        return carry

    lax.fori_loop(0, n_chunks, body, 0)
    sfin_ref[...] = s_ref[...]


def _rw_scan(r, v, kk, lw, kd, bdir, s0):
    b, length, c = r.shape
    tc = min(RW_BLOCK, length)
    nblk = length // tc
    nb = RW_SEQS
    masks, tri, lvl = _rw_masks()
    masks, tri, lvl = jnp.asarray(masks), jnp.asarray(tri, BF16), jnp.asarray(lvl)
    fwd = lambda col: pl.BlockSpec((nb, tc, c), lambda g, i: (g, i, col))
    bwd = lambda col: pl.BlockSpec((nb, tc, c), lambda g, i: (g, nblk - 1 - i, col))
    state = pl.BlockSpec((nb, 2, c, c), lambda g, i: (g, 0, 0, 0))
    return pl.pallas_call(
        functools.partial(_rw_scan_kernel, n_chunks=tc // RW_CHUNK, n_batch=nb),
        grid=(b // nb, nblk),
        in_specs=[fwd(0), bwd(0)] * 3 + [fwd(0), bwd(1)] * 3
                 + [state, _const_spec(masks.shape), _const_spec(tri.shape), _const_spec(lvl.shape)],
        out_specs=[fwd(0), bwd(0), state],
        out_shape=[jax.ShapeDtypeStruct((b, length, c), F32), jax.ShapeDtypeStruct((b, length, c), F32),
                   jax.ShapeDtypeStruct((b, 2, c, c), F32)],
        scratch_shapes=[pltpu.VMEM((nb, 2, c, c), F32)],
        compiler_params=_cparams(("arbitrary", "arbitrary")),
        name="rwkv_scan",
    )(r, r, v, v, kk, kk, lw, lw, kd, kd, bdir, bdir, s0, masks, tri, lvl)


def _blockdiag2(w):
    k, n = w.shape[1:]
    z = jnp.zeros((k, n), w.dtype)
    return jnp.concatenate([jnp.concatenate([w[0], z], axis=1), jnp.concatenate([z, w[1]], axis=1)], axis=0)


def _mla_head_pad(w, width):
    k = w.shape[0]
    w = w.reshape(k, MLA_HEADS, width)
    return jnp.pad(w, ((0, 0), (0, 0), (0, LANES - width))).reshape(k, MLA_HEADS * LANES)


def kernel(x, c, ctx, c_ctx, ada_w, ada_b, norm_ffn1, norm_mix, norm_ffn2, ffn1_gate, ffn1_up, ffn1_down, ffn2_gate, ffn2_up, ffn2_down, w_in, w_out, hy_conv_w, hy_conv_b, hy_f_w1, hy_f_b1, hy_f_w2, hy_f_b2, hy_f_w3, hy_f_b3, hy_f_w4, hy_f_freq, hy_bias, gqa_q_norm, gqa_k_norm, mla_cq_norm, mla_ckv_norm, mla_w_uq, mla_w_ukv, mla_q_norm, mla_k_norm, rw_mu, rw_w0, rw_w2, rw_a0, rw_a2, rw_g2, rw_k_k, rw_k_a, rw_r_k, rw_ln_w, rw_ln_b):
    b, length, d = x.shape
    lc = ctx.shape[1]
    depth = ada_w.shape[0]

    c_all = jnp.zeros((16, d), F32).at[:b].set(c).at[b].set(c_ctx)
    mod = _compute_mod(c_all, ada_w, ada_b)

    xc = ctx.reshape(1, b * lc, d)
    for l in range(depth):
        ctx_out = l < depth - 1
        mod_x = mod[l, :b].reshape(b, N_MOD, d)
        mod_c = mod[l, b:b + 1].reshape(1, N_MOD, d)
        row = lambda a: a[l].reshape(1, -1)

        wg1, wu1, wd1 = (w[l].astype(BF16) for w in (ffn1_gate, ffn1_up, ffn1_down))
        wg2, wu2, wd2 = (w[l].astype(BF16) for w in (ffn2_gate, ffn2_up, ffn2_down))
        wi = w_in[l]
        o1, o2, o3 = HY_COLS, HY_COLS + GQA_COLS, HY_COLS + GQA_COLS + MLA_COLS
        wi = jnp.concatenate([wi[:, :o3], jnp.zeros((d, MLA_COLS_PAD - MLA_COLS), F32), wi[:, o3:]],
                             axis=1).astype(BF16)
        wo = w_out[l].astype(BF16)

        x = _ffn(x, mod_x, row(norm_ffn1), wg1, wu1, wd1, 0)
        xc = _ffn(xc, mod_c, row(norm_ffn1), wg1, wu1, wd1, 0)

        hy_x, gq_x, ml_x, rw_x = _inproj(x, mod_x, row(norm_mix), wi)
        hy_c, gq_c, ml_c, rw_c = (t.reshape(b, lc, -1) for t in _inproj(xc, mod_c, row(norm_mix), wi))

        w1p = jnp.zeros((LANES, HY_ORDER), F32).at[:HY_EMB].set(hy_f_w1[l])
        filt = (w1p, row(hy_f_b1), hy_f_w2[l], row(hy_f_b2), hy_f_w3[l], row(hy_f_b3), hy_f_w4[l],
                row(hy_f_freq))
        spec_x = _hy_spectrum(_hy_filters(length, *filt), row(hy_bias))
        y_hy_x = _hy_conv(hy_x, hy_conv_w[l], hy_conv_b[l], spec_x)

        gq = jnp.tile(row(gqa_q_norm), (1, 2))
        gk = jnp.tile(row(gqa_k_norm), (1, 2))
        q_l, k_l, v_l = _gqa_prep(gq_x, gq, gk, True)
        q_c, k_c, v_c = _gqa_prep(gq_c, gq, gk, False)
        y_gq_x = _attention(q_l, [(k_c, v_c), (k_l, v_l)], GQA_KV_HEADS)

        wuq = _mla_head_pad(mla_w_uq[l], MLA_QK).astype(BF16)
        wukv = mla_w_ukv[l].reshape(MLA_KV_RANK, MLA_HEADS, MLA_NOPE + MLA_V)
        wuk = _mla_head_pad(wukv[:, :, :MLA_NOPE].reshape(MLA_KV_RANK, -1), MLA_NOPE).astype(BF16)
        wuv = wukv[:, :, MLA_NOPE:].reshape(MLA_KV_RANK, -1).astype(BF16)
        pad_n = lambda g: jnp.pad(g[l], (0, LANES - MLA_QK)).reshape(1, LANES)
        mla_w = (row(mla_cq_norm), row(mla_ckv_norm), wuq, wuk, wuv, pad_n(mla_q_norm), pad_n(mla_k_norm))
        mq_l, mk_l, mv_l = _mla_prep(ml_x, *mla_w, True)
        mq_c, mk_c, mv_c = _mla_prep(ml_c, *mla_w, False)
        y_ml_x = _attention(mq_l, [(mk_c, mv_c), (mk_l, mv_l)], MLA_HEADS)

        rw_w = (row(rw_mu), row(rw_k_k), row(rw_k_a), rw_r_k[l].reshape(1, RW_C),
                rw_w0[l].reshape(1, 2 * RW_C), _blockdiag2(rw_w2[l]).astype(BF16),
                rw_a0[l].reshape(1, 2 * RW_C), _blockdiag2(rw_a2[l]).astype(BF16), rw_g2[l].astype(BF16))
        pc = _rw_prep(rw_c, *rw_w)
        px = _rw_prep(rw_x, *rw_w)
        zeros = jnp.zeros((b, 2, RW_C, RW_C), F32)
        yf_c, yb_c, s_ctx = _rw_scan(*pc[:6], zeros)
        yf_x, yb_x, _ = _rw_scan(*px[:6], s_ctx)

        tail = (row(rw_ln_w), row(rw_ln_b), wo, row(norm_ffn2), wg2, wu2, wd2)
        x = _mix_ffn(x, mod_x, (y_hy_x, y_gq_x, y_ml_x), (yf_x, yb_x, px[6], px[7]), *tail)
        if ctx_out:
            spec_c = _hy_spectrum(_hy_filters(lc, *filt), row(hy_bias))
            y_hy_c = _hy_conv(hy_c, hy_conv_w[l], hy_conv_b[l], spec_c)
            y_gq_c = _attention(q_c, [(k_c, v_c)], GQA_KV_HEADS)
            y_ml_c = _attention(mq_c, [(mk_c, mv_c)], MLA_HEADS)
            flat = lambda ts: tuple(t.reshape(1, b * lc, -1) for t in ts)
            xc = _mix_ffn(xc, mod_c, flat((y_hy_c, y_gq_c, y_ml_c)), flat((yf_c, yb_c, pc[6], pc[7])), *tail)
    return x
```

```python
import functools
import math

import numpy as np
import jax
import jax.numpy as jnp
from jax import lax
from jax.experimental import pallas as pl
from jax.experimental.pallas import tpu as pltpu

F32 = jnp.float32
BF16 = jnp.bfloat16
HI = lax.Precision.HIGHEST

D_MODEL = 1024
GRID_W = 64
HEAD_DIM = 64
D_FF = 2816
N_MOD = 9
NORM_EPS = 1e-6
LOG2E = math.log2(math.e)
ROPE_THETA = 10000.0

HY_CH = 256
HY_EMB = 33
HY_ORDER = 64
HY_FAST_PCT = 0.3
HY_SLOW_PCT = 1.5
HY_TARGET = 1e-2

GQA_HEADS = 4
GQA_KV_HEADS = 2
MLA_HEADS = 4
MLA_NOPE = 64
MLA_ROPE = 32
MLA_QK = MLA_NOPE + MLA_ROPE
MLA_V = 64
MLA_Q_RANK = 256
MLA_KV_RANK = 128

RW_HEADS = 4
RW_N = 64
RW_C = RW_HEADS * RW_N
RW_DECAY_LORA = 64
RW_AAA_LORA = 64
RW_GATE_LORA = 128
RW_GN_EPS = 64e-5

HY_COLS = 3 * HY_CH
GQA_COLS = (GQA_HEADS + 2 * GQA_KV_HEADS) * HEAD_DIM
MLA_COLS = MLA_Q_RANK + MLA_KV_RANK + MLA_ROPE
MLA_COLS_PAD = 512
RW_COLS = 3 * RW_C + 2 * RW_DECAY_LORA + 2 * RW_AAA_LORA + RW_GATE_LORA
D_IN_PAD = HY_COLS + GQA_COLS + MLA_COLS_PAD + RW_COLS

LANES = 128
ROW_TILE = 512
Q_TILE = 256
ATTN_HEADS = 4
KEY_CHUNK = 256
VT_ROWS = 2 * HEAD_DIM
RW_CHUNK = 64
RW_BASE = 4
RW_LEVELS = 4
RW_BLOCK = 128
RW_SEQS = 4
VMEM_LIMIT = 56 * 2 ** 20


def _cparams(sem):
    return pltpu.CompilerParams(dimension_semantics=sem, vmem_limit_bytes=VMEM_LIMIT)


def _const_spec(shape):
    nd = len(shape)
    return pl.BlockSpec(shape, lambda *_: (0,) * nd, pipeline_mode=pl.Buffered(1))


def _dot(a, b, precision=None):
    return jnp.dot(a, b, preferred_element_type=F32, precision=precision)


def _dot_nt(a, b, precision=None):
    return lax.dot_general(a, b, (((1,), (1,)), ((), ())), preferred_element_type=F32,
                           precision=precision)


def _dot_tn(a, b, precision=None):
    return lax.dot_general(a, b, (((0,), (0,)), ((), ())), preferred_element_type=F32,
                           precision=precision)


def _pieces(x, n):
    out = []
    for _ in range(n):
        p = x.astype(BF16)
        out.append(p)
        x = x - p.astype(F32)
    return out


def _segsum(x, seg):
    return functools.reduce(jnp.add, [_dot(p, seg) for p in _pieces(x, 2)])


def _sigmoid(x):
    return 1.0 / (1.0 + jnp.exp(-x))


def _adaln(x, g, mod_ref, i):
    shift = mod_ref[pl.ds(3 * i, 1), :]
    scale = mod_ref[pl.ds(3 * i + 1, 1), :]
    r = lax.rsqrt(jnp.mean(x * x, axis=-1, keepdims=True) + NORM_EPS)
    return (x * r) * (g * (1.0 + scale)) + shift


def _mod_kernel(c_ref, w_ref, b_ref, o_ref):
    c = c_ref[...]
    s = c * _sigmoid(c)
    o_ref[...] = _dot(s, w_ref[...], HI) + b_ref[...]


def _compute_mod(c_all, ada_w, ada_b):
    depth, d, n = ada_w.shape
    rows = c_all.shape[0]
    tn = 1024
    return pl.pallas_call(
        _mod_kernel,
        grid=(depth, n // tn),
        in_specs=[
            pl.BlockSpec((rows, d), lambda l, j: (0, 0)),
            pl.BlockSpec((None, d, tn), lambda l, j: (l, 0, j)),
            pl.BlockSpec((None, 1, tn), lambda l, j: (l, 0, j)),
        ],
        out_specs=pl.BlockSpec((None, rows, tn), lambda l, j: (l, 0, j)),
        out_shape=jax.ShapeDtypeStruct((depth, rows, n), F32),
        compiler_params=_cparams(("arbitrary", "arbitrary")),
        name="mod",
    )(c_all, ada_w, ada_b.reshape(depth, 1, n))


def _ffn_kernel(x_ref, mod_ref, g_ref, wg_ref, wu_ref, wd_ref, o_ref, *, sub):
    x = x_ref[...]
    h = _adaln(x, g_ref[...], mod_ref, sub).astype(BF16)
    a = _dot(h, wg_ref[...])
    u = _dot(h, wu_ref[...])
    z = (a * _sigmoid(a) * u).astype(BF16)
    y = _dot(z, wd_ref[...])
    gate = mod_ref[pl.ds(3 * sub + 2, 1), :]
    o_ref[...] = x + (0.5 * gate) * y


def _ffn(x, mod, g, wg, wu, wd, sub):
    bm, rows, d = x.shape
    tm = min(ROW_TILE, rows)
    f = wg.shape[1]
    return pl.pallas_call(
        functools.partial(_ffn_kernel, sub=sub),
        grid=(bm, rows // tm),
        in_specs=[
            pl.BlockSpec((None, tm, d), lambda b, i: (b, i, 0)),
            pl.BlockSpec((None, N_MOD, d), lambda b, i: (b, 0, 0)),
            _const_spec((1, d)),
            _const_spec((d, f)),
            _const_spec((d, f)),
            _const_spec((f, d)),
        ],
        out_specs=pl.BlockSpec((None, tm, d), lambda b, i: (b, i, 0)),
        out_shape=jax.ShapeDtypeStruct(x.shape, F32),
        compiler_params=_cparams(("arbitrary", "arbitrary")),
        name="ffn",
    )(x, mod, g, wg, wu, wd)


_IN_OFFS = (0, HY_COLS, HY_COLS + GQA_COLS, HY_COLS + GQA_COLS + MLA_COLS_PAD, D_IN_PAD)


def _inproj_kernel(x_ref, mod_ref, g_ref, w_ref, hy_ref, gq_ref, ml_ref, rw_ref):
    h = _adaln(x_ref[...], g_ref[...], mod_ref, 1).astype(BF16)
    p = _dot(h, w_ref[...])
    for ref, lo, hi in zip((hy_ref, gq_ref, ml_ref, rw_ref), _IN_OFFS[:-1], _IN_OFFS[1:]):
        ref[...] = p[:, lo:hi]


def _inproj(x, mod, g, w):
    bm, rows, d = x.shape
    tm = min(ROW_TILE, rows)
    widths = [hi - lo for lo, hi in zip(_IN_OFFS[:-1], _IN_OFFS[1:])]
    return pl.pallas_call(
        _inproj_kernel,
        grid=(bm, rows // tm),
        in_specs=[
            pl.BlockSpec((None, tm, d), lambda b, i: (b, i, 0)),
            pl.BlockSpec((None, N_MOD, d), lambda b, i: (b, 0, 0)),
            _const_spec((1, d)),
            _const_spec((d, D_IN_PAD)),
        ],
        out_specs=[pl.BlockSpec((None, tm, wd), lambda b, i: (b, i, 0)) for wd in widths],
        out_shape=[jax.ShapeDtypeStruct((bm, rows, wd), F32) for wd in widths],
        compiler_params=_cparams(("arbitrary", "arbitrary")),
        name="inproj",
    )(x, mod, g, w)


def _mix_ffn_kernel(x_ref, mod_ref, yh_ref, yg_ref, ym_ref, yf_ref, yb_ref, bonus_ref, gate_ref,
                    lnw_ref, lnb_ref, seg_ref, wo_ref, g_ref, wg_ref, wu_ref, wd_ref, o_ref):
    y = yf_ref[...] + yb_ref[...]
    seg = seg_ref[...]
    yc = y - _segsum(y, seg)
    var = _segsum(yc * yc, seg)
    yn = yc * lax.rsqrt(var + RW_GN_EPS) * lnw_ref[...] + lnb_ref[...]
    y_rw = ((yn + bonus_ref[...]) * gate_ref[...]).astype(BF16)
    ycat = jnp.concatenate([yh_ref[...], yg_ref[...], ym_ref[...], y_rw], axis=-1)
    x = x_ref[...] + mod_ref[pl.ds(5, 1), :] * _dot(ycat, wo_ref[...])
    h = _adaln(x, g_ref[...], mod_ref, 2).astype(BF16)
    a = _dot(h, wg_ref[...])
    u = _dot(h, wu_ref[...])
    z = (a * _sigmoid(a) * u).astype(BF16)
    o_ref[...] = x + (0.5 * mod_ref[pl.ds(8, 1), :]) * _dot(z, wd_ref[...])


def _mix_ffn(x, mod, ys, rw, ln_w, ln_b, wo, g, wg, wu, wd):
    bm, rows, d = x.shape
    tm = min(ROW_TILE, rows)
    c = RW_C
    seg = jnp.asarray(_seg_matrix(c, RW_N, 1.0 / RW_N), BF16)
    row_spec = pl.BlockSpec((None, tm, d), lambda b, i: (b, i, 0))
    y_spec = pl.BlockSpec((None, tm, c), lambda b, i: (b, i, 0))
    consts = (ln_w, ln_b, seg, wo, g, wg, wu, wd)
    return pl.pallas_call(
        _mix_ffn_kernel,
        grid=(bm, rows // tm),
        in_specs=[row_spec, pl.BlockSpec((None, N_MOD, d), lambda b, i: (b, 0, 0))] + [y_spec] * 7
                 + [_const_spec(a.shape) for a in consts],
        out_specs=row_spec,
        out_shape=jax.ShapeDtypeStruct(x.shape, F32),
        compiler_params=_cparams(("arbitrary", "arbitrary")),
        name="mix_ffn",
    )(x, mod, *ys, *rw, *consts)


def _hy_features(length):
    t01 = np.linspace(0.0, 1.0, length, dtype=np.float32)[:, None]
    bands = (HY_EMB - 1) // 2
    w_ang = (np.float32(2.0 * math.pi) * np.arange(length, dtype=np.float32)[:, None]
             / np.float32(length)).astype(np.float32)
    f = np.linspace(1e-4, bands - 1, bands, dtype=np.float32)[None]
    arg = (f * w_ang).astype(np.float32)
    z = np.concatenate([t01, np.cos(arg), -np.sin(arg)], axis=-1).astype(np.float32)
    zp = np.zeros((length, LANES), np.float32)
    zp[:, :HY_EMB] = z
    return zp


def _hy_deltas():
    max_decay = math.log(HY_TARGET) / HY_FAST_PCT
    min_decay = math.log(HY_TARGET) / HY_SLOW_PCT
    d = np.abs(np.linspace(min_decay, max_decay, HY_CH, dtype=np.float32))
    return np.tile(d, 2)[None].astype(np.float32)


def _hyfilt_kernel(z_ref, w1_ref, b1_ref, w2_ref, b2_ref, w3_ref, b3_ref, w4_ref, fr_ref,
                   dl_ref, o_ref):
    z = z_ref[...]
    fr = fr_ref[...]
    h = jnp.sin(fr * (_dot(z, w1_ref[...], HI) + b1_ref[...]))
    h = jnp.sin(fr * (_dot(h, w2_ref[...], HI) + b2_ref[...]))
    h = jnp.sin(fr * (_dot(h, w3_ref[...], HI) + b3_ref[...]))
    h = _dot(h, w4_ref[...], HI)
    o_ref[...] = h * jnp.exp(-z[:, 0:1] * dl_ref[...])


def _hy_filters(length, w1p, b1, w2, b2, w3, b3, w4, freq):
    z = jnp.asarray(_hy_features(length))
    dl = jnp.asarray(_hy_deltas())
    tl = min(ROW_TILE, length)
    consts = (w1p, b1, w2, b2, w3, b3, w4, freq, dl)
    return pl.pallas_call(
        _hyfilt_kernel,
        grid=(length // tl,),
        in_specs=[pl.BlockSpec((tl, LANES), lambda i: (i, 0))] + [_const_spec(a.shape) for a in consts],
        out_specs=pl.BlockSpec((tl, 2 * HY_CH), lambda i: (i, 0)),
        out_shape=jax.ShapeDtypeStruct((length, 2 * HY_CH), F32),
        compiler_params=_cparams(("arbitrary",)),
        name="hyena_filters",
    )(z, *consts)


FFT_SLAB_PAD = 8
FFT_GROUP = 16


def _fft_factors(length):
    return (128, 64) if length >= 2048 else (2 * length, 1)


def _fft_layout(n1, n2):
    return (n1, n2, 2 * n2 + FFT_SLAB_PAD) if n2 > 1 else (1, n1, 2 * n1)


def _stack3(t):
    hi = t.astype(jnp.bfloat16)
    lo = (t - hi.astype(np.float64)).astype(jnp.bfloat16)
    return np.concatenate([hi, lo, hi], axis=-1)


def _rows3(d):
    hi = d.astype(BF16)
    lo = (d - hi.astype(F32)).astype(BF16)
    return jnp.concatenate([hi, hi, lo], axis=0)


@functools.lru_cache(maxsize=None)
def _fft_tables(length):
    n1, n2 = _fft_factors(length)
    n = n1 * n2
    assert n == 2 * length
    a_n2 = np.arange(n2)[:, None, None]
    a_k1 = np.arange(n1)[None, :, None]
    a_n1 = np.arange(n1 // 2)[None, None, :]
    ang = 2.0 * np.pi * ((a_n1 * a_k1 % n1) / n1 + (a_n2 * a_k1 % n) / n)
    t_re, t_im = np.cos(ang), -np.sin(ang)
    ta = np.concatenate([t_re, t_im], axis=1)
    tai = np.concatenate([np.transpose(t_re, (0, 2, 1)), np.transpose(t_im, (0, 2, 1))], axis=2) / n
    jj = np.arange(n2)
    ang2 = 2.0 * np.pi * (np.outer(jj, jj) % n2) / n2
    c, s = np.cos(ang2), np.sin(ang2)
    fb = np.block([[c, s], [-s, c]])
    fbi = np.block([[c, -s], [s, c]])
    return tuple(_stack3(t) for t in (ta, tai, fb, fbi))


def _fft_stage_a(src_ref, fbuf_ref, ta_ref, n1, n2):
    if n2 == 1:
        fbuf_ref[...] = _dot(ta_ref[0], _rows3(src_ref[...]))
        return
    pitch = _fft_layout(n1, n2)[2]

    def body(i, carry):
        js = [i * FFT_GROUP + g for g in range(FFT_GROUP)]
        rows = [_rows3(src_ref[pl.ds(j, n1 // 2, stride=n2), :]) for j in js]
        ts = [_dot(ta_ref[j], r) for j, r in zip(js, rows)]
        for j, t in zip(js, ts):
            fbuf_ref[pl.ds(j, n1, stride=pitch), :] = t[:n1]
            fbuf_ref[pl.ds(n2 + j, n1, stride=pitch), :] = t[n1:]
        return carry

    lax.fori_loop(0, n2 // FFT_GROUP, body, 0)


def _spec_kernel(hf_ref, hb_ref, bias_ref, ta_ref, fb_ref, o_ref, fbuf_ref, *, n1, n2):
    nslab, hs, pitch = _fft_layout(n1, n2)
    if pitch > 2 * hs:
        o_ref[...] = jnp.zeros(o_ref.shape, F32)
    _fft_stage_a(hf_ref, o_ref, ta_ref, n1, n2)
    _fft_stage_a(hb_ref, fbuf_ref, ta_ref, n1, n2)
    is_re = lax.broadcasted_iota(jnp.int32, (2 * hs, 1), 0) < hs
    sgn = jnp.where(is_re, 1.0, -1.0).astype(F32)
    skip = jnp.where(is_re, bias_ref[...], 0.0)

    grp = FFT_GROUP if n2 > 1 else 1

    def stage_b(i, carry):
        sls = [pl.ds(pl.multiple_of((i * grp + g) * pitch, 8), 2 * hs) for g in range(grp)]
        a = [o_ref[sl, :] for sl in sls]
        b = [fbuf_ref[sl, :] for sl in sls]
        if n2 > 1:
            a = [_dot(fb_ref[...], _rows3(x)) for x in a]
            b = [_dot(fb_ref[...], _rows3(x)) for x in b]
        for sl, x, y in zip(sls, a, b):
            o_ref[sl, :] = x + sgn * y + skip
        return carry

    lax.fori_loop(0, nslab // grp, stage_b, 0)


def _hy_spectrum(h, bias):
    length = h.shape[0]
    n1, n2 = _fft_factors(length)
    nslab, _, pitch = _fft_layout(n1, n2)
    ta, _, fb, _ = (jnp.asarray(t) for t in _fft_tables(length))
    nblk = HY_CH // LANES
    return pl.pallas_call(
        functools.partial(_spec_kernel, n1=n1, n2=n2),
        grid=(nblk,),
        in_specs=[pl.BlockSpec((length, LANES), lambda j: (0, j)),
                  pl.BlockSpec((length, LANES), lambda j: (0, nblk + j)),
                  pl.BlockSpec((1, LANES), lambda j: (0, j)),
                  _const_spec(ta.shape), _const_spec(fb.shape)],
        out_specs=pl.BlockSpec((nslab * pitch, LANES), lambda j: (0, j)),
        out_shape=jax.ShapeDtypeStruct((nslab * pitch, HY_CH), F32),
        scratch_shapes=[pltpu.VMEM((nslab * pitch, LANES), F32)],
        compiler_params=_cparams(("arbitrary",)),
        name="hyena_spectrum",
    )(h, h, bias, ta, fb)


def _shift_rows(x, prev_row, next_row):
    n = x.shape[0]
    row = lax.broadcasted_iota(jnp.int32, x.shape, 0)
    up = jnp.where(row == 0, prev_row, pltpu.roll(x, 1, 0))
    dn = jnp.where(row == n - 1, next_row, pltpu.roll(x, n - 1, 0))
    return up, dn


def _hyconv_kernel(x1_ref, x2_ref, v_ref, cw1_ref, cw2_ref, cwv_ref, cb_ref, g_ref,
                   ta_ref, tai_ref, fb_ref, fbi_ref, o_ref, u_ref, fbuf_ref, *, n1, n2):
    def conv3(ref, w_ref, b):
        x = ref[...]
        up, dn = _shift_rows(x, 0.0, 0.0)
        return up * w_ref[0:1, :] + x * w_ref[1:2, :] + dn * w_ref[2:3, :] + b

    cb = cb_ref[...]
    u_ref[...] = conv3(x1_ref, cw1_ref, cb[0:1, :]) * conv3(v_ref, cwv_ref, cb[2:3, :])

    _fft_stage_a(u_ref, fbuf_ref, ta_ref, n1, n2)
    nslab, hs, pitch = _fft_layout(n1, n2)

    grp = FFT_GROUP if n2 > 1 else 1

    def cmul(a, g):
        ar, ai, gr, gi = a[:hs], a[hs:], g[:hs], g[hs:]
        return jnp.concatenate([ar * gr - ai * gi, ar * gi + ai * gr], axis=0)

    def stage_b(i, carry):
        sls = [pl.ds(pl.multiple_of((i * grp + g) * pitch, 8), 2 * hs) for g in range(grp)]
        a = [fbuf_ref[sl, :] for sl in sls]
        if n2 > 1:
            a = [_dot(fb_ref[...], _rows3(x)) for x in a]
        a = [cmul(x, g_ref[sl, :]) for x, sl in zip(a, sls)]
        if n2 > 1:
            a = [_dot(fbi_ref[...], _rows3(x)) for x in a]
        for sl, x in zip(sls, a):
            fbuf_ref[sl, :] = x
        return carry

    lax.fori_loop(0, nslab // grp, stage_b, 0)

    if n2 == 1:
        u_ref[...] = _dot(tai_ref[0], _rows3(fbuf_ref[...]))
    else:
        def stage_a_inv(i, carry):
            js = [i * FFT_GROUP + g for g in range(FFT_GROUP)]
            cs = [_rows3(jnp.concatenate([fbuf_ref[pl.ds(j, n1, stride=pitch), :],
                                          fbuf_ref[pl.ds(n2 + j, n1, stride=pitch), :]], axis=0))
                  for j in js]
            ys = [_dot(tai_ref[j], c) for j, c in zip(js, cs)]
            for j, y in zip(js, ys):
                u_ref[pl.ds(j, n1 // 2, stride=n2), :] = y
            return carry

        lax.fori_loop(0, n2 // FFT_GROUP, stage_a_inv, 0)

    o_ref[...] = (conv3(x2_ref, cw2_ref, cb[1:2, :]) * u_ref[...]).astype(o_ref.dtype)


def _hy_conv(p, conv_w, conv_b, spec):
    b, length, _ = p.shape
    n1, n2 = _fft_factors(length)
    nslab, _, pitch = _fft_layout(n1, n2)
    ta, tai, fb, fbi = (jnp.asarray(t) for t in _fft_tables(length))
    nblk = HY_CH // LANES
    cb3 = conv_b.reshape(3, HY_CH)
    col = lambda g: pl.BlockSpec((None, length, LANES), lambda j, i, g=g: (i, 0, g * nblk + j))
    wcol = lambda g: pl.BlockSpec((3, LANES), lambda j, i, g=g: (0, g * nblk + j))
    return pl.pallas_call(
        functools.partial(_hyconv_kernel, n1=n1, n2=n2),
        grid=(nblk, b),
        in_specs=[col(0), col(1), col(2), wcol(0), wcol(1), wcol(2),
                  pl.BlockSpec((3, LANES), lambda j, i: (0, j)),
                  pl.BlockSpec((nslab * pitch, LANES), lambda j, i: (0, j), pipeline_mode=pl.Buffered(1)),
                  _const_spec(ta.shape), _const_spec(tai.shape), _const_spec(fb.shape),
                  _const_spec(fbi.shape)],
        out_specs=pl.BlockSpec((None, length, LANES), lambda j, i: (i, 0, j)),
        out_shape=jax.ShapeDtypeStruct((b, length, HY_CH), BF16),
        scratch_shapes=[pltpu.VMEM((length, LANES), F32), pltpu.VMEM((nslab * pitch, LANES), F32)],
        compiler_params=_cparams(("arbitrary", "arbitrary")),
        name="hyena_conv",
    )(p, p, p, conv_w, conv_w, conv_w, cb3, spec, ta, tai, fb, fbi)


def _rope_tables(length, d_rot, lane_lo, head_w):
    rows = length // GRID_W
    row = np.repeat(np.arange(rows, dtype=np.float32), GRID_W)
    colv = np.tile(np.arange(GRID_W, dtype=np.float32), rows)
    n_freq = d_rot // 4
    inv = (np.float32(ROPE_THETA) ** (-np.arange(n_freq, dtype=np.float32) / np.float32(n_freq))).astype(np.float32)
    ang = np.concatenate([row[:, None] * inv, colv[:, None] * inv], axis=-1).astype(np.float32)
    cos_t = np.ones((length, LANES), np.float32)
    sin_t = np.zeros((length, LANES), np.float32)
    c, s = np.cos(ang), np.sin(ang)
    for base in range(0, LANES, head_w):
        for i in range(d_rot // 2):
            cos_t[:, base + lane_lo + 2 * i] = c[:, i]
            cos_t[:, base + lane_lo + 2 * i + 1] = c[:, i]
            sin_t[:, base + lane_lo + 2 * i] = -s[:, i]
            sin_t[:, base + lane_lo + 2 * i + 1] = s[:, i]
    return cos_t, sin_t


def _rope_angles(length, d_rot):
    rows = length // GRID_W
    row = np.repeat(np.arange(rows, dtype=np.float32), GRID_W)
    colv = np.tile(np.arange(GRID_W, dtype=np.float32), rows)
    n_freq = d_rot // 4
    inv = (np.float32(ROPE_THETA) ** (-np.arange(n_freq, dtype=np.float32) / np.float32(n_freq))).astype(np.float32)
    return np.concatenate([row[:, None] * inv, colv[:, None] * inv], axis=-1).astype(np.float32)


def _mla_lanes():
    lanes = np.empty(MLA_QK, np.int64)
    n = np.arange(MLA_NOPE)
    lanes[:MLA_NOPE] = np.where(n < 32, n, n + 16)
    i = np.arange(MLA_ROPE // 2)
    lanes[MLA_NOPE + 2 * i] = 32 + i
    lanes[MLA_NOPE + 2 * i + 1] = 96 + i
    return lanes


def _mla_rope_tables(length):
    ang = _rope_angles(length, MLA_ROPE)
    cos_t = np.ones((length, LANES), np.float32)
    sin_t = np.zeros((length, LANES), np.float32)
    n_pairs = MLA_ROPE // 2
    cos_t[:, 32:32 + n_pairs] = cos_t[:, 96:96 + n_pairs] = np.cos(ang)
    sin_t[:, 32:32 + n_pairs] = -np.sin(ang)
    sin_t[:, 96:96 + n_pairs] = np.sin(ang)
    return cos_t, sin_t


def _rope(x, cos_t, sin_t):
    lane = lax.broadcasted_iota(jnp.int32, x.shape, 1)
    w = x.shape[1]
    partner = jnp.where(jnp.bitwise_and(lane, 1) == 0, pltpu.roll(x, w - 1, 1), pltpu.roll(x, 1, 1))
    return x * cos_t + partner * sin_t


def _store_vt(vt_ref, h, vt):
    row = lax.broadcasted_iota(jnp.int32, (VT_ROWS - HEAD_DIM, vt.shape[1]), 0)
    vt_ref[h, 0:HEAD_DIM, :] = vt.astype(BF16)
    vt_ref[h, HEAD_DIM:VT_ROWS, :] = jnp.where(row == 0, 1.0, 0.0).astype(BF16)


def _gqa_prep_kernel(p_ref, cos_ref, sin_ref, gq_ref, gk_ref, seg_ref, q_ref, k_ref, vt_ref, *, rope):
    p = p_ref[...]
    seg = seg_ref[...]
    lane = lax.broadcasted_iota(jnp.int32, (p.shape[0], LANES), 1)
    low = lane < HEAD_DIM

    def hnorm(x, g):
        ms = _segsum(x * x, seg)
        return x * lax.rsqrt(ms + NORM_EPS) * g

    scale = HEAD_DIM ** -0.5 * LOG2E
    for c in range(2):
        q = hnorm(p[:, c * LANES:(c + 1) * LANES], gq_ref[...])
        if rope:
            q = _rope(q, cos_ref[...], sin_ref[...])
        q = q * scale
        qs = pltpu.roll(q, HEAD_DIM, 1)
        q_ref[:, (2 * c) * LANES:(2 * c + 1) * LANES] = jnp.where(low, q, 0.0).astype(BF16)
        q_ref[:, (2 * c + 1) * LANES:(2 * c + 2) * LANES] = jnp.where(low, qs, 0.0).astype(BF16)
    k = hnorm(p[:, 256:384], gk_ref[...])
    if rope:
        k = _rope(k, cos_ref[...], sin_ref[...])
    ks = pltpu.roll(k, HEAD_DIM, 1)
    k_ref[:, 0:LANES] = jnp.where(low, k, 0.0).astype(BF16)
    k_ref[:, LANES:2 * LANES] = jnp.where(low, ks, 0.0).astype(BF16)
    vt = p[:, 384:512].T
    _store_vt(vt_ref, 0, vt[0:HEAD_DIM])
    _store_vt(vt_ref, 1, vt[HEAD_DIM:2 * HEAD_DIM])


def _seg_matrix(width, seg, value):
    i = np.arange(width)
    return ((i[:, None] // seg) == (i[None, :] // seg)).astype(np.float32) * np.float32(value)


def _gqa_prep(p, gq, gk, rope):
    b, length, _ = p.shape
    tm = min(ROW_TILE, length)
    if rope:
        cos_t, sin_t = (jnp.asarray(t) for t in _rope_tables(length, HEAD_DIM, 0, HEAD_DIM))
    else:
        cos_t = sin_t = jnp.zeros((length, LANES), F32)
    seg = jnp.asarray(_seg_matrix(LANES, HEAD_DIM, 1.0 / HEAD_DIM), BF16)
    tab = pl.BlockSpec((tm, LANES), lambda i, bb: (i, 0))
    outw = (512, 256)
    return pl.pallas_call(
        functools.partial(_gqa_prep_kernel, rope=rope),
        grid=(length // tm, b),
        in_specs=[pl.BlockSpec((None, tm, GQA_COLS), lambda i, bb: (bb, i, 0)), tab, tab,
                  _const_spec((1, LANES)), _const_spec((1, LANES)), _const_spec((LANES, LANES))],
        out_specs=[pl.BlockSpec((None, tm, w), lambda i, bb: (bb, i, 0)) for w in outw]
                  + [pl.BlockSpec((None, GQA_KV_HEADS, VT_ROWS, tm), lambda i, bb: (bb, 0, 0, i))],
        out_shape=[jax.ShapeDtypeStruct((b, length, w), BF16) for w in outw]
                  + [jax.ShapeDtypeStruct((b, GQA_KV_HEADS, VT_ROWS, length), BF16)],
        compiler_params=_cparams(("arbitrary", "arbitrary")),
        name="gqa_prep",
    )(p, cos_t, sin_t, gq, gk, seg)


def _mla_prep_kernel(p_ref, cos_ref, sin_ref, cqn_ref, ckvn_ref, wuq_ref, wuk_ref, wuv_ref,
                     qn_ref, kn_ref, ones_ref, q_ref, k_ref, vt_ref, *, rope):
    p = p_ref[...]

    def rms(x, g, width):
        ms = _segsum(x * x, ones_ref[0:x.shape[1], 0:x.shape[1]]) * (1.0 / width)
        return x * lax.rsqrt(ms + NORM_EPS) * g

    cq = rms(p[:, 0:MLA_Q_RANK], cqn_ref[...], MLA_Q_RANK).astype(BF16)
    ckv = rms(p[:, MLA_Q_RANK:MLA_Q_RANK + MLA_KV_RANK], ckvn_ref[...], MLA_KV_RANK).astype(BF16)
    q = _dot(cq, wuq_ref[...])
    kn = _dot(ckv, wuk_ref[...])
    v = _dot(ckv, wuv_ref[...])
    for c in range(MLA_HEADS // 2):
        vt = v[:, c * LANES:(c + 1) * LANES].T
        _store_vt(vt_ref, 2 * c, vt[0:MLA_V])
        _store_vt(vt_ref, 2 * c + 1, vt[MLA_V:2 * MLA_V])
    kr = p[:, 384:512]
    scale = MLA_QK ** -0.5 * LOG2E

    def rot(x):
        return x * cos_ref[...] + pltpu.roll(x, LANES // 2, 1) * sin_ref[...]

    for h in range(MLA_HEADS):
        sl = slice(h * LANES, (h + 1) * LANES)
        qh = rms(q[:, sl], qn_ref[...], MLA_QK)
        kh = rms(kn[:, sl] + kr, kn_ref[...], MLA_QK)
        if rope:
            qh, kh = rot(qh), rot(kh)
        q_ref[:, sl] = (qh * scale).astype(BF16)
        k_ref[:, sl] = kh.astype(BF16)


def _mla_prep(p, cqn, ckvn, wuq, wuk, wuv, qn, kn, rope):
    b, length, _ = p.shape
    tm = min(ROW_TILE, length)
    if rope:
        cos_t, sin_t = (jnp.asarray(t) for t in _mla_rope_tables(length))
    else:
        cos_t = sin_t = jnp.zeros((length, LANES), F32)
    tab = pl.BlockSpec((tm, LANES), lambda i, bb: (i, 0))
    ones = jnp.ones((MLA_Q_RANK, MLA_Q_RANK), BF16)
    consts = (cqn, ckvn, wuq, wuk, wuv, qn, kn, ones)
    outw = (512, 512)
    return pl.pallas_call(
        functools.partial(_mla_prep_kernel, rope=rope),
        grid=(length // tm, b),
        in_specs=[pl.BlockSpec((None, tm, MLA_COLS_PAD), lambda i, bb: (bb, i, 0)), tab, tab]
                 + [_const_spec(a.shape) for a in consts],
        out_specs=[pl.BlockSpec((None, tm, w), lambda i, bb: (bb, i, 0)) for w in outw]
                  + [pl.BlockSpec((None, MLA_HEADS, VT_ROWS, tm), lambda i, bb: (bb, 0, 0, i))],
        out_shape=[jax.ShapeDtypeStruct((b, length, w), BF16) for w in outw]
                  + [jax.ShapeDtypeStruct((b, MLA_HEADS, VT_ROWS, length), BF16)],
        compiler_params=_cparams(("arbitrary", "arbitrary")),
        name="mla_prep",
    )(p, cos_t, sin_t, *consts)


def _attn_kernel(*refs, n_seg):
    nh = ATTN_HEADS
    q_ref, o_ref = refs[0], refs[-1]
    segs = [refs[1 + 2 * nh * s:1 + 2 * nh * (s + 1)] for s in range(n_seg)]
    q = q_ref[...]
    qh = [q[:, h * LANES:(h + 1) * LANES] for h in range(nh)]
    pieces = [(h, seg, c0, min(KEY_CHUNK, seg[0].shape[0]))
              for h in range(nh) for seg in segs
              for c0 in range(0, seg[0].shape[0], min(KEY_CHUNK, seg[0].shape[0]))]
    scores = [_dot_nt(seg[h][c0:c0 + n, :], qh[h]) for h, seg, c0, n in pieces]
    mx = [jnp.max(s, axis=0, keepdims=True) for s in scores]
    ps = [jnp.exp2(s - m).astype(BF16) for s, m in zip(scores, mx)]
    os_ = [_dot(seg[nh + h][:, c0:c0 + n], e) for (h, seg, c0, n), e in zip(pieces, ps)]
    outs = []
    for h in range(nh):
        idx = [i for i, pc in enumerate(pieces) if pc[0] == h]
        m = functools.reduce(jnp.maximum, [mx[i] for i in idx])
        o = functools.reduce(jnp.add, [os_[i] * jnp.exp2(mx[i] - m) for i in idx])
        outs.append(o[0:HEAD_DIM] / o[HEAD_DIM:HEAD_DIM + 1])
    o_ref[...] = jnp.concatenate(outs, axis=0).T.astype(o_ref.dtype)


def _attention(q, kv_segs, k_heads):
    b, lq, _ = q.shape
    tq = min(Q_TILE, lq)
    nh = ATTN_HEADS
    kv_of = [h * k_heads // nh for h in range(nh)]
    in_specs = [pl.BlockSpec((None, tq, nh * LANES), lambda bb, i: (bb, i, 0))]
    args = [q]
    for k, vt in kv_segs:
        lk = k.shape[1]
        in_specs += [pl.BlockSpec((None, lk, LANES), lambda bb, i, j=j: (bb, 0, j)) for j in kv_of]
        in_specs += [pl.BlockSpec((None, None, VT_ROWS, lk), lambda bb, i, j=j: (bb, j, 0, 0)) for j in kv_of]
        args += [k] * nh + [vt] * nh
    return pl.pallas_call(
        functools.partial(_attn_kernel, n_seg=len(kv_segs)),
        grid=(b, lq // tq),
        in_specs=in_specs,
        out_specs=pl.BlockSpec((None, tq, nh * HEAD_DIM), lambda bb, i: (bb, i, 0)),
        out_shape=jax.ShapeDtypeStruct((b, lq, nh * HEAD_DIM), BF16),
        compiler_params=_cparams(("arbitrary", "arbitrary")),
        name="attention",
    )(*args)


def _rw_prep_kernel(p_ref, prev_ref, next_ref, mu_ref, kk_ref_w, ka_ref, rk_ref, w0_ref, w2_ref,
                    a0_ref, a2_ref, g2_ref, seg_ref,
                    r_ref, v_ref, kk_ref, lw_ref, kd_ref, bd_ref, bonus_ref, g_ref):
    i = pl.program_id(1)
    n = pl.num_programs(1)
    x = p_ref[...]
    prev_row = jnp.where(i > 0, prev_ref[7:8, :], 0.0)
    next_row = jnp.where(i < n - 1, next_ref[0:1, :], 0.0)
    up, dn = _shift_rows(x, prev_row, next_row)
    xs = x + mu_ref[...] * (0.5 * (up + dn) - x)
    c = RW_C
    r, k, v = xs[:, 0:c], xs[:, c:2 * c], xs[:, 2 * c:3 * c]
    xw, xa, xg = xs[:, 3 * c:3 * c + 128], xs[:, 3 * c + 128:3 * c + 256], xs[:, 3 * c + 256:3 * c + 384]
    seg = seg_ref[...]
    kk = k * kk_ref_w[...]
    nrm = jnp.sqrt(_segsum(kk * kk, seg))
    kk = kk / jnp.maximum(nrm, 1e-12)
    u = w0_ref[...] + _dot(jnp.tanh(xw).astype(BF16), w2_ref[...])
    z = -u
    softplus = jnp.maximum(z, 0.0) + jnp.log(1.0 + jnp.exp(-jnp.abs(z)))
    lw_ref[...] = -jnp.exp(-softplus - 0.5)
    a = _sigmoid(a0_ref[...] + _dot(xa.astype(BF16), a2_ref[...]))
    bonus = jnp.zeros_like(r)
    for d in range(2):
        a_d = a[:, d * c:(d + 1) * c]
        k_d = k * (1.0 + (a_d - 1.0) * ka_ref[...])
        kd_ref[:, d * c:(d + 1) * c] = k_d
        bd_ref[:, d * c:(d + 1) * c] = kk * a_d
        bonus = bonus + _segsum(r * k_d * rk_ref[...], seg)
    r_ref[...] = r
    v_ref[...] = v
    kk_ref[...] = kk
    bonus_ref[...] = bonus * v
    g_ref[...] = _dot(_sigmoid(xg).astype(BF16), g2_ref[...])


def _rw_prep(p, mu, k_k, k_a, r_k, w0, w2bd, a0, a2bd, g2):
    b, length, cols = p.shape
    tm = min(ROW_TILE, length)
    nb = tm // 8
    last = length // 8 - 1
    seg = jnp.asarray(_seg_matrix(RW_C, RW_N, 1.0), BF16)
    consts = (mu, k_k, k_a, r_k, w0, w2bd, a0, a2bd, g2, seg)
    outw = (RW_C, RW_C, RW_C, 2 * RW_C, 2 * RW_C, 2 * RW_C, RW_C, RW_C)
    return pl.pallas_call(
        _rw_prep_kernel,
        grid=(b, length // tm),
        in_specs=[pl.BlockSpec((None, tm, cols), lambda bb, i: (bb, i, 0)),
                  pl.BlockSpec((None, 8, cols), lambda bb, i: (bb, jnp.maximum(i * nb - 1, 0), 0)),
                  pl.BlockSpec((None, 8, cols), lambda bb, i: (bb, jnp.minimum((i + 1) * nb, last), 0))]
                 + [_const_spec(a.shape) for a in consts],
        out_specs=[pl.BlockSpec((None, tm, w), lambda bb, i: (bb, i, 0)) for w in outw],
        out_shape=[jax.ShapeDtypeStruct((b, length, w), F32) for w in outw],
        compiler_params=_cparams(("arbitrary", "arbitrary")),
        name="rwkv_prep",
    )(p, p, p, *consts)


@functools.lru_cache(maxsize=None)
def _rw_masks():
    cs, n = RW_CHUNK, RW_C
    i = np.arange(n)
    same = (i[:, None] // cs) == (i[None, :] // cs)
    rel = (i[None, :] % cs) - (i[:, None] % cs)
    masks = np.stack([same & (rel < 0), same & (rel <= 0), same & (rel > 0), same & (rel >= 0),
                      same, i[:, None] == i[None, :]]).astype(np.float32)
    t = np.arange(cs)
    tri = np.stack([t[None, :] <= t[:, None], t[None, :] >= t[:, None]]).astype(np.float32)
    tr, tc = (i % cs)[:, None], (i % cs)[None, :]
    lvl = []
    for sgn in (1, -1):
        earlier = (tc - tr) * sgn < 0
        rows = [same & earlier & (tr // RW_BASE == tc // RW_BASE)]
        for k in range(RW_LEVELS):
            s = RW_BASE << k
            rows.append(same & earlier & (tr // (2 * s) == tc // (2 * s)) & (tr // s != tc // s))
        lvl.append(np.stack(rows))
    return masks, tri, np.stack(lvl).astype(np.float32)


def _rw_scan_kernel(rf_ref, rb_ref, vf_ref, vb_ref, kkf_ref, kkb_ref, lwf_ref, lwb_ref, kdf_ref, kdb_ref,
                    bf_ref, bb_ref, s0_ref, mask_ref, tri_ref, lvl_ref, yf_ref, yb_ref, sfin_ref, s_ref,
                    *, n_chunks, n_batch):
    cs = RW_CHUNK

    @pl.when(pl.program_id(1) == 0)
    def _():
        s_ref[...] = s0_ref[...]

    same_head = mask_ref[4].astype(BF16)
    eye = mask_ref[5]
    dirs = ((rf_ref, vf_ref, kkf_ref, lwf_ref, kdf_ref, bf_ref, yf_ref),
            (rb_ref, vb_ref, kkb_ref, lwb_ref, kdb_ref, bb_ref, yb_ref))

    def tile(x):
        xb = x.astype(BF16)
        return jnp.concatenate([xb, xb, xb, xb], axis=0)

    def bd(x):
        return tile(x) * same_head

    def spread(a, half):
        swapped = pltpu.roll(a, cs, 1)
        low = lax.broadcasted_iota(jnp.int32, a.shape, 1) < cs
        h = jnp.where(low, swapped, a) if half else jnp.where(low, a, swapped)
        return jnp.concatenate([h, h], axis=1)

    streams = [(bi, d) for bi in range(n_batch) for d in range(2)]

    def each(f, *cols):
        return [f(*a) for a in zip(*cols)] if cols else [f(bi, d) for bi, d in streams]

    bf = lambda xs: [x.astype(BF16) for x in xs]

    def body(c, carry):
        def load(bi, d):
            cc = c if d == 0 else n_chunks - 1 - c
            sl = pl.ds(pl.multiple_of(cc * cs, cs), cs)
            return [ref[bi, sl, :] for ref in dirs[d][:6]] + [sl]

        r, v, kk, lw, kd, b, sl = zip(*each(load))
        strict = [mask_ref[2 * d] for _, d in streams]
        incl = [mask_ref[2 * d + 1] for _, d in streams]
        cl = [functools.reduce(jnp.add, [_dot(tri_ref[d], p) for p in _pieces(x, 3)])
              for (_, d), x in zip(streams, lw)]
        w_inv = [jnp.exp(-x) for x in cl]
        w_all = [jnp.exp(jnp.sum(x, axis=0, keepdims=True)) for x in lw]
        kt = each(lambda a, w: a * w, kd, w_inv)
        bt = each(lambda a, w: a * w, b, w_inv)
        qk_s = each(lambda a, x, y: bd(a * jnp.exp(x - y)), kk, cl, lw)
        rt_s = each(lambda a, x: bd(a * jnp.exp(x)), r, cl)
        kb = each(lambda a, t: jnp.concatenate([a.astype(BF16), t.astype(BF16)], axis=0), kt, bt)
        aq = each(_dot_nt, qk_s, kb)
        ar = each(_dot_nt, rt_s, kb)
        a_kk = each(lambda a, m: (spread(a, 0) * m).astype(BF16), aq, strict)
        a_kb = each(lambda a, m: spread(a, 1) * m, aq, strict)
        a_rk = each(lambda a, m: (spread(a, 0) * m).astype(BF16), ar, incl)
        a_rb = each(lambda a, m: (spread(a, 1) * m).astype(BF16), ar, incl)
        dmask = lambda k: [lvl_ref[d, k] for _, d in streams]
        l0 = each(lambda a, m: a * m, a_kb, dmask(0))
        p0 = [eye - a for a in l0]
        sq = each(lambda a: _dot(a, a), bf(l0))
        t_inv = each(lambda p, pb, s: p + _dot(pb, s), p0, bf(p0), bf(sq))
        for k in range(1, RW_LEVELS + 1):
            cm = bf(each(lambda a, m: a * m, a_kb, dmask(k)))
            tb = bf(t_inv)
            tc = bf(each(_dot, tb, cm))
            t_inv = each(lambda t, x, y: t - _dot(x, y), t_inv, tc, tb)
        t_inv = bf(t_inv)
        v_s = each(bd, v)
        akv = bf(each(_dot, a_kk, v_s))
        x1 = bf(each(_dot, t_inv, qk_s))
        x2 = each(_dot, t_inv, akv)
        bh_s = each(lambda a, w: bd(a * w), bt, w_all)
        kh_s = each(lambda a, w: bd(a * w), kt, w_all)
        y_v = each(_dot, a_rk, v_s)
        s_v = each(_dot_tn, v_s, kh_s)
        s0f = [s_ref[bi, d] for bi, d in streams]
        s0 = bf(s0f)
        u = bf(each(lambda x, s, y: _dot_nt(x, s) + y, x1, s0, x2))
        y_bd = each(lambda rt, s, yv, ab, u_: _dot_nt(rt, s) + yv - _dot(ab, u_), rt_s, s0, y_v, a_rb, u)
        s1 = each(lambda s, w, sv, u_, bh: s * w + sv - _dot_tn(u_, bh), s0f, w_all, s_v, u, bh_s)
        for (bi, d), y, s, rows in zip(streams, y_bd, s1, sl):
            dirs[d][6][bi, rows, :] = y[0:cs] + y[cs:2 * cs] + y[2 * cs:3 * cs] + y[3 * cs:4 * cs]
            s_ref[bi, d] = s
        return carry

    lax.fori_loop(0, n_chunks, body, 0)
    sfin_ref[...] = s_ref[...]


def _rw_scan(r, v, kk, lw, kd, bdir, s0):
    b, length, c = r.shape
    tc = min(RW_BLOCK, length)
    nblk = length // tc
    nb = RW_SEQS
    masks, tri, lvl = _rw_masks()
    masks, tri, lvl = jnp.asarray(masks), jnp.asarray(tri, BF16), jnp.asarray(lvl)
    fwd = lambda col: pl.BlockSpec((nb, tc, c), lambda g, i: (g, i, col))
    bwd = lambda col: pl.BlockSpec((nb, tc, c), lambda g, i: (g, nblk - 1 - i, col))
    state = pl.BlockSpec((nb, 2, c, c), lambda g, i: (g, 0, 0, 0))
    return pl.pallas_call(
        functools.partial(_rw_scan_kernel, n_chunks=tc // RW_CHUNK, n_batch=nb),
        grid=(b // nb, nblk),
        in_specs=[fwd(0), bwd(0)] * 3 + [fwd(0), bwd(1)] * 3
                 + [state, _const_spec(masks.shape), _const_spec(tri.shape), _const_spec(lvl.shape)],
        out_specs=[fwd(0), bwd(0), state],
        out_shape=[jax.ShapeDtypeStruct((b, length, c), F32), jax.ShapeDtypeStruct((b, length, c), F32),
                   jax.ShapeDtypeStruct((b, 2, c, c), F32)],
        scratch_shapes=[pltpu.VMEM((nb, 2, c, c), F32)],
        compiler_params=_cparams(("arbitrary", "arbitrary")),
        name="rwkv_scan",
    )(r, r, v, v, kk, kk, lw, lw, kd, kd, bdir, bdir, s0, masks, tri, lvl)


def _blockdiag2(w):
    k, n = w.shape[1:]
    z = jnp.zeros((k, n), w.dtype)
    return jnp.concatenate([jnp.concatenate([w[0], z], axis=1), jnp.concatenate([z, w[1]], axis=1)], axis=0)


def _mla_spread(w, lanes):
    k = w.shape[0]
    w = w.reshape(k, -1, len(lanes))
    return jnp.zeros((k, w.shape[1], LANES), w.dtype).at[:, :, lanes].set(w).reshape(k, -1)


def kernel(x, c, ctx, c_ctx, ada_w, ada_b, norm_ffn1, norm_mix, norm_ffn2, ffn1_gate, ffn1_up, ffn1_down, ffn2_gate, ffn2_up, ffn2_down, w_in, w_out, hy_conv_w, hy_conv_b, hy_f_w1, hy_f_b1, hy_f_w2, hy_f_b2, hy_f_w3, hy_f_b3, hy_f_w4, hy_f_freq, hy_bias, gqa_q_norm, gqa_k_norm, mla_cq_norm, mla_ckv_norm, mla_w_uq, mla_w_ukv, mla_q_norm, mla_k_norm, rw_mu, rw_w0, rw_w2, rw_a0, rw_a2, rw_g2, rw_k_k, rw_k_a, rw_r_k, rw_ln_w, rw_ln_b):
    b, length, d = x.shape
    lc = ctx.shape[1]
    depth = ada_w.shape[0]

    c_all = jnp.zeros((16, d), F32).at[:b].set(c).at[b].set(c_ctx)
    mod = _compute_mod(c_all, ada_w, ada_b)

    xc = ctx.reshape(1, b * lc, d)
    for l in range(depth):
        ctx_out = l < depth - 1
        mod_x = mod[l, :b].reshape(b, N_MOD, d)
        mod_c = mod[l, b:b + 1].reshape(1, N_MOD, d)
        row = lambda a: a[l].reshape(1, -1)

        wg1, wu1, wd1 = (w[l].astype(BF16) for w in (ffn1_gate, ffn1_up, ffn1_down))
        wg2, wu2, wd2 = (w[l].astype(BF16) for w in (ffn2_gate, ffn2_up, ffn2_down))
        wi = w_in[l]
        o1, o2, o3 = HY_COLS, HY_COLS + GQA_COLS, HY_COLS + GQA_COLS + MLA_COLS
        ml_lanes = _mla_lanes()
        o_kr = o3 - MLA_ROPE
        wi = jnp.concatenate([wi[:, :o_kr], _mla_spread(wi[:, o_kr:o3], ml_lanes[MLA_NOPE:]), wi[:, o3:]],
                             axis=1).astype(BF16)
        wo = w_out[l].astype(BF16)

        x = _ffn(x, mod_x, row(norm_ffn1), wg1, wu1, wd1, 0)
        xc = _ffn(xc, mod_c, row(norm_ffn1), wg1, wu1, wd1, 0)

        hy_x, gq_x, ml_x, rw_x = _inproj(x, mod_x, row(norm_mix), wi)
        hy_c, gq_c, ml_c, rw_c = (t.reshape(b, lc, -1) for t in _inproj(xc, mod_c, row(norm_mix), wi))

        w1p = jnp.zeros((LANES, HY_ORDER), F32).at[:HY_EMB].set(hy_f_w1[l])
        filt = (w1p, row(hy_f_b1), hy_f_w2[l], row(hy_f_b2), hy_f_w3[l], row(hy_f_b3), hy_f_w4[l],
                row(hy_f_freq))
        spec_x = _hy_spectrum(_hy_filters(length, *filt), row(hy_bias))
        y_hy_x = _hy_conv(hy_x, hy_conv_w[l], hy_conv_b[l], spec_x)

        gq = jnp.tile(row(gqa_q_norm), (1, 2))
        gk = jnp.tile(row(gqa_k_norm), (1, 2))
        q_l, k_l, v_l = _gqa_prep(gq_x, gq, gk, True)
        q_c, k_c, v_c = _gqa_prep(gq_c, gq, gk, False)
        y_gq_x = _attention(q_l, [(k_c, v_c), (k_l, v_l)], GQA_KV_HEADS)

        wuq = _mla_spread(mla_w_uq[l], ml_lanes).astype(BF16)
        wukv = mla_w_ukv[l].reshape(MLA_KV_RANK, MLA_HEADS, MLA_NOPE + MLA_V)
        wuk = _mla_spread(wukv[:, :, :MLA_NOPE].reshape(MLA_KV_RANK, -1), ml_lanes[:MLA_NOPE]).astype(BF16)
        wuv = wukv[:, :, MLA_NOPE:].reshape(MLA_KV_RANK, -1).astype(BF16)
        pad_n = lambda g: _mla_spread(g[l].reshape(1, MLA_QK), ml_lanes)
        mla_w = (row(mla_cq_norm), row(mla_ckv_norm), wuq, wuk, wuv, pad_n(mla_q_norm), pad_n(mla_k_norm))
        mq_l, mk_l, mv_l = _mla_prep(ml_x, *mla_w, True)
        mq_c, mk_c, mv_c = _mla_prep(ml_c, *mla_w, False)
        y_ml_x = _attention(mq_l, [(mk_c, mv_c), (mk_l, mv_l)], MLA_HEADS)

        rw_w = (row(rw_mu), row(rw_k_k), row(rw_k_a), rw_r_k[l].reshape(1, RW_C),
                rw_w0[l].reshape(1, 2 * RW_C), _blockdiag2(rw_w2[l]).astype(BF16),
                rw_a0[l].reshape(1, 2 * RW_C), _blockdiag2(rw_a2[l]).astype(BF16), rw_g2[l].astype(BF16))
        pc = _rw_prep(rw_c, *rw_w)
        px = _rw_prep(rw_x, *rw_w)
        zeros = jnp.zeros((b, 2, RW_C, RW_C), F32)
        yf_c, yb_c, s_ctx = _rw_scan(*pc[:6], zeros)
        yf_x, yb_x, _ = _rw_scan(*px[:6], s_ctx)

        tail = (row(rw_ln_w), row(rw_ln_b), wo, row(norm_ffn2), wg2, wu2, wd2)
        x = _mix_ffn(x, mod_x, (y_hy_x, y_gq_x, y_ml_x), (yf_x, yb_x, px[6], px[7]), *tail)
        if ctx_out:
            spec_c = _hy_spectrum(_hy_filters(lc, *filt), row(hy_bias))
            y_hy_c = _hy_conv(hy_c, hy_conv_w[l], hy_conv_b[l], spec_c)
            y_gq_c = _attention(q_c, [(k_c, v_c)], GQA_KV_HEADS)
            y_ml_c = _attention(mq_c, [(mk_c, mv_c)], MLA_HEADS)
            flat = lambda ts: tuple(t.reshape(1, b * lc, -1) for t in ts)
            xc = _mix_ffn(xc, mod_c, flat((y_hy_c, y_gq_c, y_ml_c)), flat((yf_c, yb_c, pc[6], pc[7])), *tail)
    return x
```

```python
import functools
import math

import numpy as np
import jax
import jax.numpy as jnp
from jax import lax
from jax.experimental import pallas as pl
from jax.experimental.pallas import tpu as pltpu

F32 = jnp.float32
BF16 = jnp.bfloat16
HI = lax.Precision.HIGHEST

D_MODEL = 1024
GRID_W = 64
HEAD_DIM = 64
D_FF = 2816
N_MOD = 9
NORM_EPS = 1e-6
LOG2E = math.log2(math.e)
ROPE_THETA = 10000.0

HY_CH = 256
HY_EMB = 33
HY_ORDER = 64
HY_FAST_PCT = 0.3
HY_SLOW_PCT = 1.5
HY_TARGET = 1e-2

GQA_HEADS = 4
GQA_KV_HEADS = 2
MLA_HEADS = 4
MLA_NOPE = 64
MLA_ROPE = 32
MLA_QK = MLA_NOPE + MLA_ROPE
MLA_V = 64
MLA_Q_RANK = 256
MLA_KV_RANK = 128

RW_HEADS = 4
RW_N = 64
RW_C = RW_HEADS * RW_N
RW_DECAY_LORA = 64
RW_AAA_LORA = 64
RW_GATE_LORA = 128
RW_GN_EPS = 64e-5

HY_COLS = 3 * HY_CH
GQA_COLS = (GQA_HEADS + 2 * GQA_KV_HEADS) * HEAD_DIM
MLA_COLS = MLA_Q_RANK + MLA_KV_RANK + MLA_ROPE
MLA_COLS_PAD = 512
RW_COLS = 3 * RW_C + 2 * RW_DECAY_LORA + 2 * RW_AAA_LORA + RW_GATE_LORA
D_IN_PAD = HY_COLS + GQA_COLS + MLA_COLS_PAD + RW_COLS

LANES = 128
ROW_TILE = 512
Q_TILE = 512
ATTN_HEADS = 4
KEY_CHUNK = 256
VT_ROWS = 2 * HEAD_DIM
RW_CHUNK = 64
RW_BASE = 4
RW_LEVELS = 4
RW_BLOCK = 128
RW_SEQS = 4
VMEM_LIMIT = 56 * 2 ** 20


def _cparams(sem):
    return pltpu.CompilerParams(dimension_semantics=sem, vmem_limit_bytes=VMEM_LIMIT)


def _const_spec(shape):
    nd = len(shape)
    return pl.BlockSpec(shape, lambda *_: (0,) * nd, pipeline_mode=pl.Buffered(1))


def _dot(a, b, precision=None):
    return jnp.dot(a, b, preferred_element_type=F32, precision=precision)


def _dot_nt(a, b, precision=None):
    return lax.dot_general(a, b, (((1,), (1,)), ((), ())), preferred_element_type=F32,
                           precision=precision)


def _dot_tn(a, b, precision=None):
    return lax.dot_general(a, b, (((0,), (0,)), ((), ())), preferred_element_type=F32,
                           precision=precision)


def _pieces(x, n):
    out = []
    for _ in range(n):
        p = x.astype(BF16)
        out.append(p)
        x = x - p.astype(F32)
    return out


def _segsum(x, seg):
    return functools.reduce(jnp.add, [_dot(p, seg) for p in _pieces(x, 2)])


def _sigmoid(x):
    return 1.0 / (1.0 + jnp.exp(-x))


def _adaln(x, g, mod_ref, i):
    shift = mod_ref[pl.ds(3 * i, 1), :]
    scale = mod_ref[pl.ds(3 * i + 1, 1), :]
    r = lax.rsqrt(jnp.mean(x * x, axis=-1, keepdims=True) + NORM_EPS)
    return (x * r) * (g * (1.0 + scale)) + shift


def _mod_kernel(c_ref, w_ref, b_ref, o_ref):
    c = c_ref[...]
    s = c * _sigmoid(c)
    o_ref[...] = _dot(s, w_ref[...], HI) + b_ref[...]


def _compute_mod(c_all, ada_w, ada_b):
    depth, d, n = ada_w.shape
    rows = c_all.shape[0]
    tn = 1024
    return pl.pallas_call(
        _mod_kernel,
        grid=(depth, n // tn),
        in_specs=[
            pl.BlockSpec((rows, d), lambda l, j: (0, 0)),
            pl.BlockSpec((None, d, tn), lambda l, j: (l, 0, j)),
            pl.BlockSpec((None, 1, tn), lambda l, j: (l, 0, j)),
        ],
        out_specs=pl.BlockSpec((None, rows, tn), lambda l, j: (l, 0, j)),
        out_shape=jax.ShapeDtypeStruct((depth, rows, n), F32),
        compiler_params=_cparams(("arbitrary", "arbitrary")),
        name="mod",
    )(c_all, ada_w, ada_b.reshape(depth, 1, n))


def _ffn_kernel(x_ref, mod_ref, g_ref, wg_ref, wu_ref, wd_ref, o_ref, *, sub):
    x = x_ref[...]
    h = _adaln(x, g_ref[...], mod_ref, sub).astype(BF16)
    a = _dot(h, wg_ref[...])
    u = _dot(h, wu_ref[...])
    z = (a * _sigmoid(a) * u).astype(BF16)
    y = _dot(z, wd_ref[...])
    gate = mod_ref[pl.ds(3 * sub + 2, 1), :]
    o_ref[...] = x + (0.5 * gate) * y


def _ffn(x, mod, g, wg, wu, wd, sub):
    bm, rows, d = x.shape
    tm = min(ROW_TILE, rows)
    f = wg.shape[1]
    return pl.pallas_call(
        functools.partial(_ffn_kernel, sub=sub),
        grid=(bm, rows // tm),
        in_specs=[
            pl.BlockSpec((None, tm, d), lambda b, i: (b, i, 0)),
            pl.BlockSpec((None, N_MOD, d), lambda b, i: (b, 0, 0)),
            _const_spec((1, d)),
            _const_spec((d, f)),
            _const_spec((d, f)),
            _const_spec((f, d)),
        ],
        out_specs=pl.BlockSpec((None, tm, d), lambda b, i: (b, i, 0)),
        out_shape=jax.ShapeDtypeStruct(x.shape, F32),
        compiler_params=_cparams(("arbitrary", "arbitrary")),
        name="ffn",
    )(x, mod, g, wg, wu, wd)


_IN_OFFS = (0, HY_COLS, HY_COLS + GQA_COLS, HY_COLS + GQA_COLS + MLA_COLS_PAD, D_IN_PAD)


def _inproj_kernel(x_ref, mod_ref, g_ref, w_ref, hy_ref, gq_ref, ml_ref, rw_ref):
    h = _adaln(x_ref[...], g_ref[...], mod_ref, 1).astype(BF16)
    p = _dot(h, w_ref[...])
    for ref, lo, hi in zip((hy_ref, gq_ref, ml_ref, rw_ref), _IN_OFFS[:-1], _IN_OFFS[1:]):
        ref[...] = p[:, lo:hi]


def _inproj(x, mod, g, w):
    bm, rows, d = x.shape
    tm = min(ROW_TILE, rows)
    widths = [hi - lo for lo, hi in zip(_IN_OFFS[:-1], _IN_OFFS[1:])]
    return pl.pallas_call(
        _inproj_kernel,
        grid=(bm, rows // tm),
        in_specs=[
            pl.BlockSpec((None, tm, d), lambda b, i: (b, i, 0)),
            pl.BlockSpec((None, N_MOD, d), lambda b, i: (b, 0, 0)),
            _const_spec((1, d)),
            _const_spec((d, D_IN_PAD)),
        ],
        out_specs=[pl.BlockSpec((None, tm, wd), lambda b, i: (b, i, 0)) for wd in widths],
        out_shape=[jax.ShapeDtypeStruct((bm, rows, wd), F32) for wd in widths],
        compiler_params=_cparams(("arbitrary", "arbitrary")),
        name="inproj",
    )(x, mod, g, w)


def _mix_ffn_kernel(x_ref, mod_ref, yh_ref, yg_ref, ym_ref, yf_ref, yb_ref, bonus_ref, gate_ref,
                    lnw_ref, lnb_ref, seg_ref, wo_ref, g_ref, wg_ref, wu_ref, wd_ref, o_ref):
    y = yf_ref[...] + yb_ref[...]
    seg = seg_ref[...]
    yc = y - _segsum(y, seg)
    var = _segsum(yc * yc, seg)
    yn = yc * lax.rsqrt(var + RW_GN_EPS) * lnw_ref[...] + lnb_ref[...]
    y_rw = ((yn + bonus_ref[...]) * gate_ref[...]).astype(BF16)
    ycat = jnp.concatenate([yh_ref[...], yg_ref[...], ym_ref[...], y_rw], axis=-1)
    x = x_ref[...] + mod_ref[pl.ds(5, 1), :] * _dot(ycat, wo_ref[...])
    h = _adaln(x, g_ref[...], mod_ref, 2).astype(BF16)
    a = _dot(h, wg_ref[...])
    u = _dot(h, wu_ref[...])
    z = (a * _sigmoid(a) * u).astype(BF16)
    o_ref[...] = x + (0.5 * mod_ref[pl.ds(8, 1), :]) * _dot(z, wd_ref[...])


def _mix_ffn(x, mod, ys, rw, ln_w, ln_b, wo, g, wg, wu, wd):
    bm, rows, d = x.shape
    tm = min(ROW_TILE, rows)
    c = RW_C
    seg = jnp.asarray(_seg_matrix(c, RW_N, 1.0 / RW_N), BF16)
    row_spec = pl.BlockSpec((None, tm, d), lambda b, i: (b, i, 0))
    y_spec = pl.BlockSpec((None, tm, c), lambda b, i: (b, i, 0))
    consts = (ln_w, ln_b, seg, wo, g, wg, wu, wd)
    return pl.pallas_call(
        _mix_ffn_kernel,
        grid=(bm, rows // tm),
        in_specs=[row_spec, pl.BlockSpec((None, N_MOD, d), lambda b, i: (b, 0, 0))] + [y_spec] * 7
                 + [_const_spec(a.shape) for a in consts],
        out_specs=row_spec,
        out_shape=jax.ShapeDtypeStruct(x.shape, F32),
        compiler_params=_cparams(("arbitrary", "arbitrary")),
        name="mix_ffn",
    )(x, mod, *ys, *rw, *consts)


def _hy_features(length):
    t01 = np.linspace(0.0, 1.0, length, dtype=np.float32)[:, None]
    bands = (HY_EMB - 1) // 2
    w_ang = (np.float32(2.0 * math.pi) * np.arange(length, dtype=np.float32)[:, None]
             / np.float32(length)).astype(np.float32)
    f = np.linspace(1e-4, bands - 1, bands, dtype=np.float32)[None]
    arg = (f * w_ang).astype(np.float32)
    z = np.concatenate([t01, np.cos(arg), -np.sin(arg)], axis=-1).astype(np.float32)
    zp = np.zeros((length, LANES), np.float32)
    zp[:, :HY_EMB] = z
    return zp


def _hy_deltas():
    max_decay = math.log(HY_TARGET) / HY_FAST_PCT
    min_decay = math.log(HY_TARGET) / HY_SLOW_PCT
    d = np.abs(np.linspace(min_decay, max_decay, HY_CH, dtype=np.float32))
    return np.tile(d, 2)[None].astype(np.float32)


def _hyfilt_kernel(z_ref, w1_ref, b1_ref, w2_ref, b2_ref, w3_ref, b3_ref, w4_ref, fr_ref,
                   dl_ref, o_ref):
    z = z_ref[...]
    fr = fr_ref[...]
    h = jnp.sin(fr * (_dot(z, w1_ref[...], HI) + b1_ref[...]))
    h = jnp.sin(fr * (_dot(h, w2_ref[...], HI) + b2_ref[...]))
    h = jnp.sin(fr * (_dot(h, w3_ref[...], HI) + b3_ref[...]))
    h = _dot(h, w4_ref[...], HI)
    o_ref[...] = h * jnp.exp(-z[:, 0:1] * dl_ref[...])


def _hy_filters(length, w1p, b1, w2, b2, w3, b3, w4, freq):
    z = jnp.asarray(_hy_features(length))
    dl = jnp.asarray(_hy_deltas())
    tl = min(ROW_TILE, length)
    consts = (w1p, b1, w2, b2, w3, b3, w4, freq, dl)
    return pl.pallas_call(
        _hyfilt_kernel,
        grid=(length // tl,),
        in_specs=[pl.BlockSpec((tl, LANES), lambda i: (i, 0))] + [_const_spec(a.shape) for a in consts],
        out_specs=pl.BlockSpec((tl, 2 * HY_CH), lambda i: (i, 0)),
        out_shape=jax.ShapeDtypeStruct((length, 2 * HY_CH), F32),
        compiler_params=_cparams(("arbitrary",)),
        name="hyena_filters",
    )(z, *consts)


FFT_SLAB_PAD = 8
FFT_GROUP = 16


def _fft_factors(length):
    return (128, 64) if length >= 2048 else (2 * length, 1)


def _fft_layout(n1, n2):
    return (n1, n2, 2 * n2 + FFT_SLAB_PAD) if n2 > 1 else (1, n1, 2 * n1)


def _stack3(t):
    hi = t.astype(jnp.bfloat16)
    lo = (t - hi.astype(np.float64)).astype(jnp.bfloat16)
    return np.concatenate([hi, lo, hi], axis=-1)


def _rows3(d):
    hi = d.astype(BF16)
    lo = (d - hi.astype(F32)).astype(BF16)
    return jnp.concatenate([hi, hi, lo], axis=0)


@functools.lru_cache(maxsize=None)
def _fft_tables(length):
    n1, n2 = _fft_factors(length)
    n = n1 * n2
    assert n == 2 * length
    a_n2 = np.arange(n2)[:, None, None]
    a_k1 = np.arange(n1)[None, :, None]
    a_n1 = np.arange(n1 // 2)[None, None, :]
    ang = 2.0 * np.pi * ((a_n1 * a_k1 % n1) / n1 + (a_n2 * a_k1 % n) / n)
    t_re, t_im = np.cos(ang), -np.sin(ang)
    ta = np.concatenate([t_re, t_im], axis=1)
    tai = np.concatenate([np.transpose(t_re, (0, 2, 1)), np.transpose(t_im, (0, 2, 1))], axis=2) / n
    jj = np.arange(n2)
    ang2 = 2.0 * np.pi * (np.outer(jj, jj) % n2) / n2
    c, s = np.cos(ang2), np.sin(ang2)
    fb = np.block([[c, s], [-s, c]])
    fbi = np.block([[c, -s], [s, c]])
    return tuple(_stack3(t) for t in (ta, tai, fb, fbi))


def _fft_stage_a(src_ref, fbuf_ref, ta_ref, n1, n2):
    if n2 == 1:
        fbuf_ref[...] = _dot(ta_ref[0], _rows3(src_ref[...]))
        return
    pitch = _fft_layout(n1, n2)[2]

    def body(i, carry):
        js = [i * FFT_GROUP + g for g in range(FFT_GROUP)]
        rows = [_rows3(src_ref[pl.ds(j, n1 // 2, stride=n2), :]) for j in js]
        ts = [_dot(ta_ref[j], r) for j, r in zip(js, rows)]
        for j, t in zip(js, ts):
            fbuf_ref[pl.ds(j, n1, stride=pitch), :] = t[:n1]
            fbuf_ref[pl.ds(n2 + j, n1, stride=pitch), :] = t[n1:]
        return carry

    lax.fori_loop(0, n2 // FFT_GROUP, body, 0)


def _spec_kernel(hf_ref, hb_ref, bias_ref, ta_ref, fb_ref, o_ref, fbuf_ref, *, n1, n2):
    nslab, hs, pitch = _fft_layout(n1, n2)
    if pitch > 2 * hs:
        o_ref[...] = jnp.zeros(o_ref.shape, F32)
    _fft_stage_a(hf_ref, o_ref, ta_ref, n1, n2)
    _fft_stage_a(hb_ref, fbuf_ref, ta_ref, n1, n2)
    is_re = lax.broadcasted_iota(jnp.int32, (2 * hs, 1), 0) < hs
    sgn = jnp.where(is_re, 1.0, -1.0).astype(F32)
    skip = jnp.where(is_re, bias_ref[...], 0.0)

    grp = FFT_GROUP if n2 > 1 else 1

    def stage_b(i, carry):
        sls = [pl.ds(pl.multiple_of((i * grp + g) * pitch, 8), 2 * hs) for g in range(grp)]
        a = [o_ref[sl, :] for sl in sls]
        b = [fbuf_ref[sl, :] for sl in sls]
        if n2 > 1:
            a = [_dot(fb_ref[...], _rows3(x)) for x in a]
            b = [_dot(fb_ref[...], _rows3(x)) for x in b]
        for sl, x, y in zip(sls, a, b):
            o_ref[sl, :] = x + sgn * y + skip
        return carry

    lax.fori_loop(0, nslab // grp, stage_b, 0)


def _hy_spectrum(h, bias):
    length = h.shape[0]
    n1, n2 = _fft_factors(length)
    nslab, _, pitch = _fft_layout(n1, n2)
    ta, _, fb, _ = (jnp.asarray(t) for t in _fft_tables(length))
    nblk = HY_CH // LANES
    return pl.pallas_call(
        functools.partial(_spec_kernel, n1=n1, n2=n2),
        grid=(nblk,),
        in_specs=[pl.BlockSpec((length, LANES), lambda j: (0, j)),
                  pl.BlockSpec((length, LANES), lambda j: (0, nblk + j)),
                  pl.BlockSpec((1, LANES), lambda j: (0, j)),
                  _const_spec(ta.shape), _const_spec(fb.shape)],
        out_specs=pl.BlockSpec((nslab * pitch, LANES), lambda j: (0, j)),
        out_shape=jax.ShapeDtypeStruct((nslab * pitch, HY_CH), F32),
        scratch_shapes=[pltpu.VMEM((nslab * pitch, LANES), F32)],
        compiler_params=_cparams(("arbitrary",)),
        name="hyena_spectrum",
    )(h, h, bias, ta, fb)


def _shift_rows(x, prev_row, next_row):
    n = x.shape[0]
    row = lax.broadcasted_iota(jnp.int32, x.shape, 0)
    up = jnp.where(row == 0, prev_row, pltpu.roll(x, 1, 0))
    dn = jnp.where(row == n - 1, next_row, pltpu.roll(x, n - 1, 0))
    return up, dn


def _hyconv_kernel(x1_ref, x2_ref, v_ref, cw1_ref, cw2_ref, cwv_ref, cb_ref, g_ref,
                   ta_ref, tai_ref, fb_ref, fbi_ref, o_ref, u_ref, fbuf_ref, *, n1, n2):
    def conv3(ref, w_ref, b):
        x = ref[...]
        up, dn = _shift_rows(x, 0.0, 0.0)
        return up * w_ref[0:1, :] + x * w_ref[1:2, :] + dn * w_ref[2:3, :] + b

    cb = cb_ref[...]
    u_ref[...] = conv3(x1_ref, cw1_ref, cb[0:1, :]) * conv3(v_ref, cwv_ref, cb[2:3, :])

    _fft_stage_a(u_ref, fbuf_ref, ta_ref, n1, n2)
    nslab, hs, pitch = _fft_layout(n1, n2)

    grp = FFT_GROUP if n2 > 1 else 1

    def cmul(a, g):
        ar, ai, gr, gi = a[:hs], a[hs:], g[:hs], g[hs:]
        return jnp.concatenate([ar * gr - ai * gi, ar * gi + ai * gr], axis=0)

    def stage_b(i, carry):
        sls = [pl.ds(pl.multiple_of((i * grp + g) * pitch, 8), 2 * hs) for g in range(grp)]
        a = [fbuf_ref[sl, :] for sl in sls]
        if n2 > 1:
            a = [_dot(fb_ref[...], _rows3(x)) for x in a]
        a = [cmul(x, g_ref[sl, :]) for x, sl in zip(a, sls)]
        if n2 > 1:
            a = [_dot(fbi_ref[...], _rows3(x)) for x in a]
        for sl, x in zip(sls, a):
            fbuf_ref[sl, :] = x
        return carry

    lax.fori_loop(0, nslab // grp, stage_b, 0)

    if n2 == 1:
        u_ref[...] = _dot(tai_ref[0], _rows3(fbuf_ref[...]))
    else:
        def stage_a_inv(i, carry):
            js = [i * FFT_GROUP + g for g in range(FFT_GROUP)]
            cs = [_rows3(jnp.concatenate([fbuf_ref[pl.ds(j, n1, stride=pitch), :],
                                          fbuf_ref[pl.ds(n2 + j, n1, stride=pitch), :]], axis=0))
                  for j in js]
            ys = [_dot(tai_ref[j], c) for j, c in zip(js, cs)]
            for j, y in zip(js, ys):
                u_ref[pl.ds(j, n1 // 2, stride=n2), :] = y
            return carry

        lax.fori_loop(0, n2 // FFT_GROUP, stage_a_inv, 0)

    o_ref[...] = (conv3(x2_ref, cw2_ref, cb[1:2, :]) * u_ref[...]).astype(o_ref.dtype)


def _hy_conv(p, conv_w, conv_b, spec):
    b, length, _ = p.shape
    n1, n2 = _fft_factors(length)
    nslab, _, pitch = _fft_layout(n1, n2)
    ta, tai, fb, fbi = (jnp.asarray(t) for t in _fft_tables(length))
    nblk = HY_CH // LANES
    cb3 = conv_b.reshape(3, HY_CH)
    col = lambda g: pl.BlockSpec((None, length, LANES), lambda j, i, g=g: (i, 0, g * nblk + j))
    wcol = lambda g: pl.BlockSpec((3, LANES), lambda j, i, g=g: (0, g * nblk + j))
    return pl.pallas_call(
        functools.partial(_hyconv_kernel, n1=n1, n2=n2),
        grid=(nblk, b),
        in_specs=[col(0), col(1), col(2), wcol(0), wcol(1), wcol(2),
                  pl.BlockSpec((3, LANES), lambda j, i: (0, j)),
                  pl.BlockSpec((nslab * pitch, LANES), lambda j, i: (0, j), pipeline_mode=pl.Buffered(1)),
                  _const_spec(ta.shape), _const_spec(tai.shape), _const_spec(fb.shape),
                  _const_spec(fbi.shape)],
        out_specs=pl.BlockSpec((None, length, LANES), lambda j, i: (i, 0, j)),
        out_shape=jax.ShapeDtypeStruct((b, length, HY_CH), BF16),
        scratch_shapes=[pltpu.VMEM((length, LANES), F32), pltpu.VMEM((nslab * pitch, LANES), F32)],
        compiler_params=_cparams(("arbitrary", "arbitrary")),
        name="hyena_conv",
    )(p, p, p, conv_w, conv_w, conv_w, cb3, spec, ta, tai, fb, fbi)


def _rope_tables(length, d_rot, lane_lo, head_w):
    rows = length // GRID_W
    row = np.repeat(np.arange(rows, dtype=np.float32), GRID_W)
    colv = np.tile(np.arange(GRID_W, dtype=np.float32), rows)
    n_freq = d_rot // 4
    inv = (np.float32(ROPE_THETA) ** (-np.arange(n_freq, dtype=np.float32) / np.float32(n_freq))).astype(np.float32)
    ang = np.concatenate([row[:, None] * inv, colv[:, None] * inv], axis=-1).astype(np.float32)
    cos_t = np.ones((length, LANES), np.float32)
    sin_t = np.zeros((length, LANES), np.float32)
    c, s = np.cos(ang), np.sin(ang)
    for base in range(0, LANES, head_w):
        for i in range(d_rot // 2):
            cos_t[:, base + lane_lo + 2 * i] = c[:, i]
            cos_t[:, base + lane_lo + 2 * i + 1] = c[:, i]
            sin_t[:, base + lane_lo + 2 * i] = -s[:, i]
            sin_t[:, base + lane_lo + 2 * i + 1] = s[:, i]
    return cos_t, sin_t


def _rope_angles(length, d_rot):
    rows = length // GRID_W
    row = np.repeat(np.arange(rows, dtype=np.float32), GRID_W)
    colv = np.tile(np.arange(GRID_W, dtype=np.float32), rows)
    n_freq = d_rot // 4
    inv = (np.float32(ROPE_THETA) ** (-np.arange(n_freq, dtype=np.float32) / np.float32(n_freq))).astype(np.float32)
    return np.concatenate([row[:, None] * inv, colv[:, None] * inv], axis=-1).astype(np.float32)


def _mla_lanes():
    lanes = np.empty(MLA_QK, np.int64)
    n = np.arange(MLA_NOPE)
    lanes[:MLA_NOPE] = np.where(n < 32, n, n + 16)
    i = np.arange(MLA_ROPE // 2)
    lanes[MLA_NOPE + 2 * i] = 32 + i
    lanes[MLA_NOPE + 2 * i + 1] = 96 + i
    return lanes


def _mla_rope_tables(length):
    ang = _rope_angles(length, MLA_ROPE)
    cos_t = np.ones((length, LANES), np.float32)
    sin_t = np.zeros((length, LANES), np.float32)
    n_pairs = MLA_ROPE // 2
    cos_t[:, 32:32 + n_pairs] = cos_t[:, 96:96 + n_pairs] = np.cos(ang)
    sin_t[:, 32:32 + n_pairs] = -np.sin(ang)
    sin_t[:, 96:96 + n_pairs] = np.sin(ang)
    return cos_t, sin_t


def _rope(x, cos_t, sin_t):
    lane = lax.broadcasted_iota(jnp.int32, x.shape, 1)
    w = x.shape[1]
    partner = jnp.where(jnp.bitwise_and(lane, 1) == 0, pltpu.roll(x, w - 1, 1), pltpu.roll(x, 1, 1))
    return x * cos_t + partner * sin_t


def _store_vt(vt_ref, h, vt):
    row = lax.broadcasted_iota(jnp.int32, (VT_ROWS - HEAD_DIM, vt.shape[1]), 0)
    vt_ref[h, 0:HEAD_DIM, :] = vt.astype(BF16)
    vt_ref[h, HEAD_DIM:VT_ROWS, :] = jnp.where(row == 0, 1.0, 0.0).astype(BF16)


def _gqa_prep_kernel(p_ref, cos_ref, sin_ref, gq_ref, gk_ref, seg_ref, q_ref, k_ref, vt_ref, *, rope):
    p = p_ref[...]
    seg = seg_ref[...]
    lane = lax.broadcasted_iota(jnp.int32, (p.shape[0], LANES), 1)
    low = lane < HEAD_DIM

    def hnorm(x, g):
        ms = _segsum(x * x, seg)
        return x * lax.rsqrt(ms + NORM_EPS) * g

    scale = HEAD_DIM ** -0.5 * LOG2E
    for c in range(2):
        q = hnorm(p[:, c * LANES:(c + 1) * LANES], gq_ref[...])
        if rope:
            q = _rope(q, cos_ref[...], sin_ref[...])
        q = q * scale
        qs = pltpu.roll(q, HEAD_DIM, 1)
        q_ref[:, (2 * c) * LANES:(2 * c + 1) * LANES] = jnp.where(low, q, 0.0).astype(BF16)
        q_ref[:, (2 * c + 1) * LANES:(2 * c + 2) * LANES] = jnp.where(low, qs, 0.0).astype(BF16)
    k = hnorm(p[:, 256:384], gk_ref[...])
    if rope:
        k = _rope(k, cos_ref[...], sin_ref[...])
    ks = pltpu.roll(k, HEAD_DIM, 1)
    k_ref[:, 0:LANES] = jnp.where(low, k, 0.0).astype(BF16)
    k_ref[:, LANES:2 * LANES] = jnp.where(low, ks, 0.0).astype(BF16)
    vt = p[:, 384:512].T
    _store_vt(vt_ref, 0, vt[0:HEAD_DIM])
    _store_vt(vt_ref, 1, vt[HEAD_DIM:2 * HEAD_DIM])


def _seg_matrix(width, seg, value):
    i = np.arange(width)
    return ((i[:, None] // seg) == (i[None, :] // seg)).astype(np.float32) * np.float32(value)


def _gqa_prep(p, gq, gk, rope):
    b, length, _ = p.shape
    tm = min(ROW_TILE, length)
    if rope:
        cos_t, sin_t = (jnp.asarray(t) for t in _rope_tables(length, HEAD_DIM, 0, HEAD_DIM))
    else:
        cos_t = sin_t = jnp.zeros((length, LANES), F32)
    seg = jnp.asarray(_seg_matrix(LANES, HEAD_DIM, 1.0 / HEAD_DIM), BF16)
    tab = pl.BlockSpec((tm, LANES), lambda i, bb: (i, 0))
    outw = (512, 256)
    return pl.pallas_call(
        functools.partial(_gqa_prep_kernel, rope=rope),
        grid=(length // tm, b),
        in_specs=[pl.BlockSpec((None, tm, GQA_COLS), lambda i, bb: (bb, i, 0)), tab, tab,
                  _const_spec((1, LANES)), _const_spec((1, LANES)), _const_spec((LANES, LANES))],
        out_specs=[pl.BlockSpec((None, tm, w), lambda i, bb: (bb, i, 0)) for w in outw]
                  + [pl.BlockSpec((None, GQA_KV_HEADS, VT_ROWS, tm), lambda i, bb: (bb, 0, 0, i))],
        out_shape=[jax.ShapeDtypeStruct((b, length, w), BF16) for w in outw]
                  + [jax.ShapeDtypeStruct((b, GQA_KV_HEADS, VT_ROWS, length), BF16)],
        compiler_params=_cparams(("arbitrary", "arbitrary")),
        name="gqa_prep",
    )(p, cos_t, sin_t, gq, gk, seg)


def _mla_prep_kernel(p_ref, cos_ref, sin_ref, cqn_ref, ckvn_ref, wuq_ref, wuk_ref, wuv_ref,
                     qn_ref, kn_ref, ones_ref, q_ref, k_ref, vt_ref, *, rope):
    p = p_ref[...]

    def rms(x, g, width):
        ms = _segsum(x * x, ones_ref[0:x.shape[1], 0:x.shape[1]]) * (1.0 / width)
        return x * lax.rsqrt(ms + NORM_EPS) * g

    cq = rms(p[:, 0:MLA_Q_RANK], cqn_ref[...], MLA_Q_RANK).astype(BF16)
    ckv = rms(p[:, MLA_Q_RANK:MLA_Q_RANK + MLA_KV_RANK], ckvn_ref[...], MLA_KV_RANK).astype(BF16)
    q = _dot(cq, wuq_ref[...])
    kn = _dot(ckv, wuk_ref[...])
    v = _dot(ckv, wuv_ref[...])
    for c in range(MLA_HEADS // 2):
        vt = v[:, c * LANES:(c + 1) * LANES].T
        _store_vt(vt_ref, 2 * c, vt[0:MLA_V])
        _store_vt(vt_ref, 2 * c + 1, vt[MLA_V:2 * MLA_V])
    kr = p[:, 384:512]
    scale = MLA_QK ** -0.5 * LOG2E

    def rot(x):
        return x * cos_ref[...] + pltpu.roll(x, LANES // 2, 1) * sin_ref[...]

    for h in range(MLA_HEADS):
        sl = slice(h * LANES, (h + 1) * LANES)
        qh = rms(q[:, sl], qn_ref[...], MLA_QK)
        kh = rms(kn[:, sl] + kr, kn_ref[...], MLA_QK)
        if rope:
            qh, kh = rot(qh), rot(kh)
        q_ref[:, sl] = (qh * scale).astype(BF16)
        k_ref[:, sl] = kh.astype(BF16)


def _mla_prep(p, cqn, ckvn, wuq, wuk, wuv, qn, kn, rope):
    b, length, _ = p.shape
    tm = min(ROW_TILE, length)
    if rope:
        cos_t, sin_t = (jnp.asarray(t) for t in _mla_rope_tables(length))
    else:
        cos_t = sin_t = jnp.zeros((length, LANES), F32)
    tab = pl.BlockSpec((tm, LANES), lambda i, bb: (i, 0))
    ones = jnp.ones((MLA_Q_RANK, MLA_Q_RANK), BF16)
    consts = (cqn, ckvn, wuq, wuk, wuv, qn, kn, ones)
    outw = (512, 512)
    return pl.pallas_call(
        functools.partial(_mla_prep_kernel, rope=rope),
        grid=(length // tm, b),
        in_specs=[pl.BlockSpec((None, tm, MLA_COLS_PAD), lambda i, bb: (bb, i, 0)), tab, tab]
                 + [_const_spec(a.shape) for a in consts],
        out_specs=[pl.BlockSpec((None, tm, w), lambda i, bb: (bb, i, 0)) for w in outw]
                  + [pl.BlockSpec((None, MLA_HEADS, VT_ROWS, tm), lambda i, bb: (bb, 0, 0, i))],
        out_shape=[jax.ShapeDtypeStruct((b, length, w), BF16) for w in outw]
                  + [jax.ShapeDtypeStruct((b, MLA_HEADS, VT_ROWS, length), BF16)],
        compiler_params=_cparams(("arbitrary", "arbitrary")),
        name="mla_prep",
    )(p, cos_t, sin_t, *consts)


def _attn_kernel(*refs, n_seg):
    nh = ATTN_HEADS
    q_ref, o_ref = refs[0], refs[-1]
    segs = [refs[1 + 2 * nh * s:1 + 2 * nh * (s + 1)] for s in range(n_seg)]
    q = q_ref[...]
    qh = [q[:, h * LANES:(h + 1) * LANES] for h in range(nh)]
    pieces = [(h, seg, c0, min(KEY_CHUNK, seg[0].shape[0]))
              for h in range(nh) for seg in segs
              for c0 in range(0, seg[0].shape[0], min(KEY_CHUNK, seg[0].shape[0]))]
    scores = [_dot_nt(seg[h][c0:c0 + n, :], qh[h]) for h, seg, c0, n in pieces]
    mx = [jnp.max(s, axis=0, keepdims=True) for s in scores]
    ps = [jnp.exp2(s - m).astype(BF16) for s, m in zip(scores, mx)]
    os_ = [_dot(seg[nh + h][:, c0:c0 + n], e) for (h, seg, c0, n), e in zip(pieces, ps)]
    outs = []
    for h in range(nh):
        idx = [i for i, pc in enumerate(pieces) if pc[0] == h]
        m = functools.reduce(jnp.maximum, [mx[i] for i in idx])
        o = functools.reduce(jnp.add, [os_[i] * jnp.exp2(mx[i] - m) for i in idx])
        outs.append(o[0:HEAD_DIM] / o[HEAD_DIM:HEAD_DIM + 1])
    o_ref[...] = jnp.concatenate(outs, axis=0).T.astype(o_ref.dtype)


def _attention(q, kv_segs, k_heads):
    b, lq, _ = q.shape
    tq = min(Q_TILE, lq)
    nh = ATTN_HEADS
    kv_of = [h * k_heads // nh for h in range(nh)]
    in_specs = [pl.BlockSpec((None, tq, nh * LANES), lambda bb, i: (bb, i, 0))]
    args = [q]
    for k, vt in kv_segs:
        lk = k.shape[1]
        in_specs += [pl.BlockSpec((None, lk, LANES), lambda bb, i, j=j: (bb, 0, j)) for j in kv_of]
        in_specs += [pl.BlockSpec((None, None, VT_ROWS, lk), lambda bb, i, j=j: (bb, j, 0, 0)) for j in kv_of]
        args += [k] * nh + [vt] * nh
    return pl.pallas_call(
        functools.partial(_attn_kernel, n_seg=len(kv_segs)),
        grid=(b, lq // tq),
        in_specs=in_specs,
        out_specs=pl.BlockSpec((None, tq, nh * HEAD_DIM), lambda bb, i: (bb, i, 0)),
        out_shape=jax.ShapeDtypeStruct((b, lq, nh * HEAD_DIM), BF16),
        compiler_params=_cparams(("arbitrary", "arbitrary")),
        name="attention",
    )(*args)


def _rw_prep_kernel(p_ref, prev_ref, next_ref, mu_ref, kk_ref_w, ka_ref, rk_ref, w0_ref, w2_ref,
                    a0_ref, a2_ref, g2_ref, seg_ref,
                    r_ref, v_ref, kk_ref, lw_ref, kd_ref, bd_ref, bonus_ref, g_ref):
    i = pl.program_id(1)
    n = pl.num_programs(1)
    x = p_ref[...]
    prev_row = jnp.where(i > 0, prev_ref[7:8, :], 0.0)
    next_row = jnp.where(i < n - 1, next_ref[0:1, :], 0.0)
    up, dn = _shift_rows(x, prev_row, next_row)
    xs = x + mu_ref[...] * (0.5 * (up + dn) - x)
    c = RW_C
    r, k, v = xs[:, 0:c], xs[:, c:2 * c], xs[:, 2 * c:3 * c]
    xw, xa, xg = xs[:, 3 * c:3 * c + 128], xs[:, 3 * c + 128:3 * c + 256], xs[:, 3 * c + 256:3 * c + 384]
    seg = seg_ref[...]
    kk = k * kk_ref_w[...]
    nrm = jnp.sqrt(_segsum(kk * kk, seg))
    kk = kk / jnp.maximum(nrm, 1e-12)
    u = w0_ref[...] + _dot(jnp.tanh(xw).astype(BF16), w2_ref[...])
    z = -u
    softplus = jnp.maximum(z, 0.0) + jnp.log(1.0 + jnp.exp(-jnp.abs(z)))
    lw_ref[...] = -jnp.exp(-softplus - 0.5)
    a = _sigmoid(a0_ref[...] + _dot(xa.astype(BF16), a2_ref[...]))
    bonus = jnp.zeros_like(r)
    for d in range(2):
        a_d = a[:, d * c:(d + 1) * c]
        k_d = k * (1.0 + (a_d - 1.0) * ka_ref[...])
        kd_ref[:, d * c:(d + 1) * c] = k_d
        bd_ref[:, d * c:(d + 1) * c] = kk * a_d
        bonus = bonus + _segsum(r * k_d * rk_ref[...], seg)
    r_ref[...] = r
    v_ref[...] = v
    kk_ref[...] = kk
    bonus_ref[...] = bonus * v
    g_ref[...] = _dot(_sigmoid(xg).astype(BF16), g2_ref[...])


def _rw_prep(p, mu, k_k, k_a, r_k, w0, w2bd, a0, a2bd, g2):
    b, length, cols = p.shape
    tm = min(ROW_TILE, length)
    nb = tm // 8
    last = length // 8 - 1
    seg = jnp.asarray(_seg_matrix(RW_C, RW_N, 1.0), BF16)
    consts = (mu, k_k, k_a, r_k, w0, w2bd, a0, a2bd, g2, seg)
    outw = (RW_C, RW_C, RW_C, 2 * RW_C, 2 * RW_C, 2 * RW_C, RW_C, RW_C)
    return pl.pallas_call(
        _rw_prep_kernel,
        grid=(b, length // tm),
        in_specs=[pl.BlockSpec((None, tm, cols), lambda bb, i: (bb, i, 0)),
                  pl.BlockSpec((None, 8, cols), lambda bb, i: (bb, jnp.maximum(i * nb - 1, 0), 0)),
                  pl.BlockSpec((None, 8, cols), lambda bb, i: (bb, jnp.minimum((i + 1) * nb, last), 0))]
                 + [_const_spec(a.shape) for a in consts],
        out_specs=[pl.BlockSpec((None, tm, w), lambda bb, i: (bb, i, 0)) for w in outw],
        out_shape=[jax.ShapeDtypeStruct((b, length, w), F32) for w in outw],
        compiler_params=_cparams(("arbitrary", "arbitrary")),
        name="rwkv_prep",
    )(p, p, p, *consts)


@functools.lru_cache(maxsize=None)
def _rw_masks():
    cs, n = RW_CHUNK, RW_C
    i = np.arange(n)
    same = (i[:, None] // cs) == (i[None, :] // cs)
    rel = (i[None, :] % cs) - (i[:, None] % cs)
    masks = np.stack([same & (rel < 0), same & (rel <= 0), same & (rel > 0), same & (rel >= 0),
                      same, i[:, None] == i[None, :]]).astype(np.float32)
    t = np.arange(cs)
    tri = np.stack([t[None, :] <= t[:, None], t[None, :] >= t[:, None]]).astype(np.float32)
    tr, tc = (i % cs)[:, None], (i % cs)[None, :]
    lvl = []
    for sgn in (1, -1):
        earlier = (tc - tr) * sgn < 0
        rows = [same & earlier & (tr // RW_BASE == tc // RW_BASE)]
        for k in range(RW_LEVELS):
            s = RW_BASE << k
            rows.append(same & earlier & (tr // (2 * s) == tc // (2 * s)) & (tr // s != tc // s))
        lvl.append(np.stack(rows))
    return masks, tri, np.stack(lvl).astype(np.float32)


def _rw_scan_kernel(rf_ref, rb_ref, vf_ref, vb_ref, kkf_ref, kkb_ref, lwf_ref, lwb_ref, kdf_ref, kdb_ref,
                    bf_ref, bb_ref, s0_ref, mask_ref, tri_ref, lvl_ref, yf_ref, yb_ref, sfin_ref, s_ref,
                    *, n_chunks, n_batch):
    cs = RW_CHUNK

    @pl.when(pl.program_id(1) == 0)
    def _():
        s_ref[...] = s0_ref[...]

    same_head = mask_ref[4].astype(BF16)
    eye = mask_ref[5]
    dirs = ((rf_ref, vf_ref, kkf_ref, lwf_ref, kdf_ref, bf_ref, yf_ref),
            (rb_ref, vb_ref, kkb_ref, lwb_ref, kdb_ref, bb_ref, yb_ref))

    def tile(x):
        xb = x.astype(BF16)
        return jnp.concatenate([xb, xb, xb, xb], axis=0)

    def bd(x):
        return tile(x) * same_head

    def spread(a, half):
        swapped = pltpu.roll(a, cs, 1)
        low = lax.broadcasted_iota(jnp.int32, a.shape, 1) < cs
        h = jnp.where(low, swapped, a) if half else jnp.where(low, a, swapped)
        return jnp.concatenate([h, h], axis=1)

    streams = [(bi, d) for bi in range(n_batch) for d in range(2)]

    def each(f, *cols):
        return [f(*a) for a in zip(*cols)] if cols else [f(bi, d) for bi, d in streams]

    bf = lambda xs: [x.astype(BF16) for x in xs]

    def body(c, carry):
        def load(bi, d):
            cc = c if d == 0 else n_chunks - 1 - c
            sl = pl.ds(pl.multiple_of(cc * cs, cs), cs)
            return [ref[bi, sl, :] for ref in dirs[d][:6]] + [sl]

        r, v, kk, lw, kd, b, sl = zip(*each(load))
        strict = [mask_ref[2 * d] for _, d in streams]
        incl = [mask_ref[2 * d + 1] for _, d in streams]
        cl = [functools.reduce(jnp.add, [_dot(tri_ref[d], p) for p in _pieces(x, 3)])
              for (_, d), x in zip(streams, lw)]
        w_inv = [jnp.exp(-x) for x in cl]
        w_all = [jnp.exp(jnp.sum(x, axis=0, keepdims=True)) for x in lw]
        kt = each(lambda a, w: a * w, kd, w_inv)
        bt = each(lambda a, w: a * w, b, w_inv)
        qk_s = each(lambda a, x, y: bd(a * jnp.exp(x - y)), kk, cl, lw)
        rt_s = each(lambda a, x: bd(a * jnp.exp(x)), r, cl)
        kb = each(lambda a, t: jnp.concatenate([a.astype(BF16), t.astype(BF16)], axis=0), kt, bt)
        aq = each(_dot_nt, qk_s, kb)
        ar = each(_dot_nt, rt_s, kb)
        a_kk = each(lambda a, m: (spread(a, 0) * m).astype(BF16), aq, strict)
        a_kb = each(lambda a, m: spread(a, 1) * m, aq, strict)
        a_rk = each(lambda a, m: (spread(a, 0) * m).astype(BF16), ar, incl)
        a_rb = each(lambda a, m: (spread(a, 1) * m).astype(BF16), ar, incl)
        dmask = lambda k: [lvl_ref[d, k] for _, d in streams]
        l0 = each(lambda a, m: a * m, a_kb, dmask(0))
        p0 = [eye - a for a in l0]
        sq = each(lambda a: _dot(a, a), bf(l0))
        t_inv = each(lambda p, pb, s: p + _dot(pb, s), p0, bf(p0), bf(sq))
        for k in range(1, RW_LEVELS + 1):
            cm = bf(each(lambda a, m: a * m, a_kb, dmask(k)))
            tb = bf(t_inv)
            tc = bf(each(_dot, tb, cm))
            t_inv = each(lambda t, x, y: t - _dot(x, y), t_inv, tc, tb)
        t_inv = bf(t_inv)
        v_s = each(bd, v)
        akv = bf(each(_dot, a_kk, v_s))
        x1 = bf(each(_dot, t_inv, qk_s))
        x2 = each(_dot, t_inv, akv)
        bh_s = each(lambda a, w: bd(a * w), bt, w_all)
        kh_s = each(lambda a, w: bd(a * w), kt, w_all)
        y_v = each(_dot, a_rk, v_s)
        s_v = each(_dot_tn, v_s, kh_s)
        s0f = [s_ref[bi, d] for bi, d in streams]
        s0 = bf(s0f)
        u = bf(each(lambda x, s, y: _dot_nt(x, s) + y, x1, s0, x2))
        y_bd = each(lambda rt, s, yv, ab, u_: _dot_nt(rt, s) + yv - _dot(ab, u_), rt_s, s0, y_v, a_rb, u)
        s1 = each(lambda s, w, sv, u_, bh: s * w + sv - _dot_tn(u_, bh), s0f, w_all, s_v, u, bh_s)
        for (bi, d), y, s, rows in zip(streams, y_bd, s1, sl):
            dirs[d][6][bi, rows, :] = y[0:cs] + y[cs:2 * cs] + y[2 * cs:3 * cs] + y[3 * cs:4 * cs]
            s_ref[bi, d] = s
        return carry

    lax.fori_loop(0, n_chunks, body, 0)
    sfin_ref[...] = s_ref[...]


def _rw_scan(r, v, kk, lw, kd, bdir, s0):
    b, length, c = r.shape
    tc = min(RW_BLOCK, length)
    nblk = length // tc
    nb = RW_SEQS
    masks, tri, lvl = _rw_masks()
    masks, tri, lvl = jnp.asarray(masks), jnp.asarray(tri, BF16), jnp.asarray(lvl)
    fwd = lambda col: pl.BlockSpec((nb, tc, c), lambda g, i: (g, i, col))
    bwd = lambda col: pl.BlockSpec((nb, tc, c), lambda g, i: (g, nblk - 1 - i, col))
    state = pl.BlockSpec((nb, 2, c, c), lambda g, i: (g, 0, 0, 0))
    return pl.pallas_call(
        functools.partial(_rw_scan_kernel, n_chunks=tc // RW_CHUNK, n_batch=nb),
        grid=(b // nb, nblk),
        in_specs=[fwd(0), bwd(0)] * 3 + [fwd(0), bwd(1)] * 3
                 + [state, _const_spec(masks.shape), _const_spec(tri.shape), _const_spec(lvl.shape)],
        out_specs=[fwd(0), bwd(0), state],
        out_shape=[jax.ShapeDtypeStruct((b, length, c), F32), jax.ShapeDtypeStruct((b, length, c), F32),
                   jax.ShapeDtypeStruct((b, 2, c, c), F32)],
        scratch_shapes=[pltpu.VMEM((nb, 2, c, c), F32)],
        compiler_params=_cparams(("arbitrary", "arbitrary")),
        name="rwkv_scan",
    )(r, r, v, v, kk, kk, lw, lw, kd, kd, bdir, bdir, s0, masks, tri, lvl)


def _blockdiag2(w):
    k, n = w.shape[1:]
    z = jnp.zeros((k, n), w.dtype)
    return jnp.concatenate([jnp.concatenate([w[0], z], axis=1), jnp.concatenate([z, w[1]], axis=1)], axis=0)


def _mla_spread(w, lanes):
    k = w.shape[0]
    w = w.reshape(k, -1, len(lanes))
    return jnp.zeros((k, w.shape[1], LANES), w.dtype).at[:, :, lanes].set(w).reshape(k, -1)


def kernel(x, c, ctx, c_ctx, ada_w, ada_b, norm_ffn1, norm_mix, norm_ffn2, ffn1_gate, ffn1_up, ffn1_down, ffn2_gate, ffn2_up, ffn2_down, w_in, w_out, hy_conv_w, hy_conv_b, hy_f_w1, hy_f_b1, hy_f_w2, hy_f_b2, hy_f_w3, hy_f_b3, hy_f_w4, hy_f_freq, hy_bias, gqa_q_norm, gqa_k_norm, mla_cq_norm, mla_ckv_norm, mla_w_uq, mla_w_ukv, mla_q_norm, mla_k_norm, rw_mu, rw_w0, rw_w2, rw_a0, rw_a2, rw_g2, rw_k_k, rw_k_a, rw_r_k, rw_ln_w, rw_ln_b):
    b, length, d = x.shape
    lc = ctx.shape[1]
    depth = ada_w.shape[0]

    c_all = jnp.zeros((16, d), F32).at[:b].set(c).at[b].set(c_ctx)
    mod = _compute_mod(c_all, ada_w, ada_b)

    xc = ctx.reshape(1, b * lc, d)
    for l in range(depth):
        ctx_out = l < depth - 1
        mod_x = mod[l, :b].reshape(b, N_MOD, d)
        mod_c = mod[l, b:b + 1].reshape(1, N_MOD, d)
        row = lambda a: a[l].reshape(1, -1)

        wg1, wu1, wd1 = (w[l].astype(BF16) for w in (ffn1_gate, ffn1_up, ffn1_down))
        wg2, wu2, wd2 = (w[l].astype(BF16) for w in (ffn2_gate, ffn2_up, ffn2_down))
        wi = w_in[l]
        o1, o2, o3 = HY_COLS, HY_COLS + GQA_COLS, HY_COLS + GQA_COLS + MLA_COLS
        ml_lanes = _mla_lanes()
        o_kr = o3 - MLA_ROPE
        wi = jnp.concatenate([wi[:, :o_kr], _mla_spread(wi[:, o_kr:o3], ml_lanes[MLA_NOPE:]), wi[:, o3:]],
                             axis=1).astype(BF16)
        wo = w_out[l].astype(BF16)

        x = _ffn(x, mod_x, row(norm_ffn1), wg1, wu1, wd1, 0)
        xc = _ffn(xc, mod_c, row(norm_ffn1), wg1, wu1, wd1, 0)

        hy_x, gq_x, ml_x, rw_x = _inproj(x, mod_x, row(norm_mix), wi)
        hy_c, gq_c, ml_c, rw_c = (t.reshape(b, lc, -1) for t in _inproj(xc, mod_c, row(norm_mix), wi))

        w1p = jnp.zeros((LANES, HY_ORDER), F32).at[:HY_EMB].set(hy_f_w1[l])
        filt = (w1p, row(hy_f_b1), hy_f_w2[l], row(hy_f_b2), hy_f_w3[l], row(hy_f_b3), hy_f_w4[l],
                row(hy_f_freq))
        spec_x = _hy_spectrum(_hy_filters(length, *filt), row(hy_bias))
        y_hy_x = _hy_conv(hy_x, hy_conv_w[l], hy_conv_b[l], spec_x)

        gq = jnp.tile(row(gqa_q_norm), (1, 2))
        gk = jnp.tile(row(gqa_k_norm), (1, 2))
        q_l, k_l, v_l = _gqa_prep(gq_x, gq, gk, True)
        q_c, k_c, v_c = _gqa_prep(gq_c, gq, gk, False)
        y_gq_x = _attention(q_l, [(k_c, v_c), (k_l, v_l)], GQA_KV_HEADS)

        wuq = _mla_spread(mla_w_uq[l], ml_lanes).astype(BF16)
        wukv = mla_w_ukv[l].reshape(MLA_KV_RANK, MLA_HEADS, MLA_NOPE + MLA_V)
        wuk = _mla_spread(wukv[:, :, :MLA_NOPE].reshape(MLA_KV_RANK, -1), ml_lanes[:MLA_NOPE]).astype(BF16)
        wuv = wukv[:, :, MLA_NOPE:].reshape(MLA_KV_RANK, -1).astype(BF16)
        pad_n = lambda g: _mla_spread(g[l].reshape(1, MLA_QK), ml_lanes)
        mla_w = (row(mla_cq_norm), row(mla_ckv_norm), wuq, wuk, wuv, pad_n(mla_q_norm), pad_n(mla_k_norm))
        mq_l, mk_l, mv_l = _mla_prep(ml_x, *mla_w, True)
        mq_c, mk_c, mv_c = _mla_prep(ml_c, *mla_w, False)
        y_ml_x = _attention(mq_l, [(mk_c, mv_c), (mk_l, mv_l)], MLA_HEADS)

        rw_w = (row(rw_mu), row(rw_k_k), row(rw_k_a), rw_r_k[l].reshape(1, RW_C),
                rw_w0[l].reshape(1, 2 * RW_C), _blockdiag2(rw_w2[l]).astype(BF16),
                rw_a0[l].reshape(1, 2 * RW_C), _blockdiag2(rw_a2[l]).astype(BF16), rw_g2[l].astype(BF16))
        pc = _rw_prep(rw_c, *rw_w)
        px = _rw_prep(rw_x, *rw_w)
        zeros = jnp.zeros((b, 2, RW_C, RW_C), F32)
        yf_c, yb_c, s_ctx = _rw_scan(*pc[:6], zeros)
        yf_x, yb_x, _ = _rw_scan(*px[:6], s_ctx)

        tail = (row(rw_ln_w), row(rw_ln_b), wo, row(norm_ffn2), wg2, wu2, wd2)
        x = _mix_ffn(x, mod_x, (y_hy_x, y_gq_x, y_ml_x), (yf_x, yb_x, px[6], px[7]), *tail)
        if ctx_out:
            spec_c = _hy_spectrum(_hy_filters(lc, *filt), row(hy_bias))
            y_hy_c = _hy_conv(hy_c, hy_conv_w[l], hy_conv_b[l], spec_c)
            y_gq_c = _attention(q_c, [(k_c, v_c)], GQA_KV_HEADS)
            y_ml_c = _attention(mq_c, [(mk_c, mv_c)], MLA_HEADS)
            flat = lambda ts: tuple(t.reshape(1, b * lc, -1) for t in ts)
            xc = _mix_ffn(xc, mod_c, flat((y_hy_c, y_gq_c, y_ml_c)), flat((yf_c, yb_c, pc[6], pc[7])), *tail)
    return x
```

```python
import functools
import math

import numpy as np
import jax
import jax.numpy as jnp
from jax import lax
from jax.experimental import pallas as pl
from jax.experimental.pallas import tpu as pltpu

F32 = jnp.float32
BF16 = jnp.bfloat16
HI = lax.Precision.HIGHEST

D_MODEL = 1024
GRID_W = 64
HEAD_DIM = 64
D_FF = 2816
N_MOD = 9
NORM_EPS = 1e-6
LOG2E = math.log2(math.e)
ROPE_THETA = 10000.0

HY_CH = 256
HY_EMB = 33
HY_ORDER = 64
HY_FAST_PCT = 0.3
HY_SLOW_PCT = 1.5
HY_TARGET = 1e-2

GQA_HEADS = 4
GQA_KV_HEADS = 2
MLA_HEADS = 4
MLA_NOPE = 64
MLA_ROPE = 32
MLA_QK = MLA_NOPE + MLA_ROPE
MLA_V = 64
MLA_Q_RANK = 256
MLA_KV_RANK = 128

RW_HEADS = 4
RW_N = 64
RW_C = RW_HEADS * RW_N
RW_DECAY_LORA = 64
RW_AAA_LORA = 64
RW_GATE_LORA = 128
RW_GN_EPS = 64e-5

HY_COLS = 3 * HY_CH
GQA_COLS = (GQA_HEADS + 2 * GQA_KV_HEADS) * HEAD_DIM
MLA_COLS = MLA_Q_RANK + MLA_KV_RANK + MLA_ROPE
MLA_COLS_PAD = 512
RW_COLS = 3 * RW_C + 2 * RW_DECAY_LORA + 2 * RW_AAA_LORA + RW_GATE_LORA
D_IN_PAD = HY_COLS + GQA_COLS + MLA_COLS_PAD + RW_COLS

LANES = 128
ROW_TILE = 512
Q_TILE = 256
ATTN_HEADS = 4
KEY_CHUNK = 256
VT_ROWS = 2 * HEAD_DIM
RW_CHUNK = 64
RW_BASE = 4
RW_LEVELS = 4
RW_BLOCK = 128
RW_SEQS = 4
VMEM_LIMIT = 56 * 2 ** 20


def _cparams(sem):
    return pltpu.CompilerParams(dimension_semantics=sem, vmem_limit_bytes=VMEM_LIMIT)


def _const_spec(shape):
    nd = len(shape)
    return pl.BlockSpec(shape, lambda *_: (0,) * nd, pipeline_mode=pl.Buffered(1))


def _dot(a, b, precision=None):
    return jnp.dot(a, b, preferred_element_type=F32, precision=precision)


def _dot_nt(a, b, precision=None):
    return lax.dot_general(a, b, (((1,), (1,)), ((), ())), preferred_element_type=F32,
                           precision=precision)


def _dot_tn(a, b, precision=None):
    return lax.dot_general(a, b, (((0,), (0,)), ((), ())), preferred_element_type=F32,
                           precision=precision)


def _pieces(x, n):
    out = []
    for _ in range(n):
        p = x.astype(BF16)
        out.append(p)
        x = x - p.astype(F32)
    return out


def _segsum(x, seg):
    return functools.reduce(jnp.add, [_dot(p, seg) for p in _pieces(x, 2)])


def _sigmoid(x):
    return 1.0 / (1.0 + jnp.exp(-x))


def _adaln(x, g, mod_ref, i):
    shift = mod_ref[pl.ds(3 * i, 1), :]
    scale = mod_ref[pl.ds(3 * i + 1, 1), :]
    r = lax.rsqrt(jnp.mean(x * x, axis=-1, keepdims=True) + NORM_EPS)
    return (x * r) * (g * (1.0 + scale)) + shift


def _mod_kernel(c_ref, w_ref, b_ref, o_ref):
    c = c_ref[...]
    s = c * _sigmoid(c)
    o_ref[...] = _dot(s, w_ref[...], HI) + b_ref[...]


def _compute_mod(c_all, ada_w, ada_b):
    depth, d, n = ada_w.shape
    rows = c_all.shape[0]
    tn = 1024
    return pl.pallas_call(
        _mod_kernel,
        grid=(depth, n // tn),
        in_specs=[
            pl.BlockSpec((rows, d), lambda l, j: (0, 0)),
            pl.BlockSpec((None, d, tn), lambda l, j: (l, 0, j)),
            pl.BlockSpec((None, 1, tn), lambda l, j: (l, 0, j)),
        ],
        out_specs=pl.BlockSpec((None, rows, tn), lambda l, j: (l, 0, j)),
        out_shape=jax.ShapeDtypeStruct((depth, rows, n), F32),
        compiler_params=_cparams(("arbitrary", "arbitrary")),
        name="mod",
    )(c_all, ada_w, ada_b.reshape(depth, 1, n))


def _ffn_kernel(x_ref, mod_ref, g_ref, wg_ref, wu_ref, wd_ref, o_ref, *, sub):
    x = x_ref[...]
    h = _adaln(x, g_ref[...], mod_ref, sub).astype(BF16)
    a = _dot(h, wg_ref[...])
    u = _dot(h, wu_ref[...])
    z = (a * _sigmoid(a) * u).astype(BF16)
    y = _dot(z, wd_ref[...])
    gate = mod_ref[pl.ds(3 * sub + 2, 1), :]
    o_ref[...] = x + (0.5 * gate) * y


def _ffn(x, mod, g, wg, wu, wd, sub):
    bm, rows, d = x.shape
    tm = min(ROW_TILE, rows)
    f = wg.shape[1]
    return pl.pallas_call(
        functools.partial(_ffn_kernel, sub=sub),
        grid=(bm, rows // tm),
        in_specs=[
            pl.BlockSpec((None, tm, d), lambda b, i: (b, i, 0)),
            pl.BlockSpec((None, N_MOD, d), lambda b, i: (b, 0, 0)),
            _const_spec((1, d)),
            _const_spec((d, f)),
            _const_spec((d, f)),
            _const_spec((f, d)),
        ],
        out_specs=pl.BlockSpec((None, tm, d), lambda b, i: (b, i, 0)),
        out_shape=jax.ShapeDtypeStruct(x.shape, F32),
        compiler_params=_cparams(("arbitrary", "arbitrary")),
        name="ffn",
    )(x, mod, g, wg, wu, wd)


_IN_OFFS = (0, HY_COLS, HY_COLS + GQA_COLS, HY_COLS + GQA_COLS + MLA_COLS_PAD, D_IN_PAD)


def _inproj_kernel(x_ref, mod_ref, g_ref, w_ref, hy_ref, gq_ref, ml_ref, rw_ref):
    h = _adaln(x_ref[...], g_ref[...], mod_ref, 1).astype(BF16)
    p = _dot(h, w_ref[...])
    for ref, lo, hi in zip((hy_ref, gq_ref, ml_ref, rw_ref), _IN_OFFS[:-1], _IN_OFFS[1:]):
        ref[...] = p[:, lo:hi]


def _inproj(x, mod, g, w):
    bm, rows, d = x.shape
    tm = min(ROW_TILE, rows)
    widths = [hi - lo for lo, hi in zip(_IN_OFFS[:-1], _IN_OFFS[1:])]
    return pl.pallas_call(
        _inproj_kernel,
        grid=(bm, rows // tm),
        in_specs=[
            pl.BlockSpec((None, tm, d), lambda b, i: (b, i, 0)),
            pl.BlockSpec((None, N_MOD, d), lambda b, i: (b, 0, 0)),
            _const_spec((1, d)),
            _const_spec((d, D_IN_PAD)),
        ],
        out_specs=[pl.BlockSpec((None, tm, wd), lambda b, i: (b, i, 0)) for wd in widths],
        out_shape=[jax.ShapeDtypeStruct((bm, rows, wd), F32) for wd in widths],
        compiler_params=_cparams(("arbitrary", "arbitrary")),
        name="inproj",
    )(x, mod, g, w)


def _mix_ffn_kernel(x_ref, mod_ref, yh_ref, yg_ref, ym_ref, yf_ref, yb_ref, bonus_ref, gate_ref,
                    lnw_ref, lnb_ref, seg_ref, wo_ref, g_ref, wg_ref, wu_ref, wd_ref, o_ref):
    y = yf_ref[...] + yb_ref[...]
    seg = seg_ref[...]
    yc = y - _segsum(y, seg)
    var = _segsum(yc * yc, seg)
    yn = yc * lax.rsqrt(var + RW_GN_EPS) * lnw_ref[...] + lnb_ref[...]
    y_rw = ((yn + bonus_ref[...]) * gate_ref[...]).astype(BF16)
    ycat = jnp.concatenate([yh_ref[...], yg_ref[...], ym_ref[...], y_rw], axis=-1)
    x = x_ref[...] + mod_ref[pl.ds(5, 1), :] * _dot(ycat, wo_ref[...])
    h = _adaln(x, g_ref[...], mod_ref, 2).astype(BF16)
    a = _dot(h, wg_ref[...])
    u = _dot(h, wu_ref[...])
    z = (a * _sigmoid(a) * u).astype(BF16)
    o_ref[...] = x + (0.5 * mod_ref[pl.ds(8, 1), :]) * _dot(z, wd_ref[...])


def _mix_ffn(x, mod, ys, rw, ln_w, ln_b, wo, g, wg, wu, wd):
    bm, rows, d = x.shape
    tm = min(ROW_TILE, rows)
    c = RW_C
    seg = jnp.asarray(_seg_matrix(c, RW_N, 1.0 / RW_N), BF16)
    row_spec = pl.BlockSpec((None, tm, d), lambda b, i: (b, i, 0))
    y_spec = pl.BlockSpec((None, tm, c), lambda b, i: (b, i, 0))
    consts = (ln_w, ln_b, seg, wo, g, wg, wu, wd)
    return pl.pallas_call(
        _mix_ffn_kernel,
        grid=(bm, rows // tm),
        in_specs=[row_spec, pl.BlockSpec((None, N_MOD, d), lambda b, i: (b, 0, 0))] + [y_spec] * 7
                 + [_const_spec(a.shape) for a in consts],
        out_specs=row_spec,
        out_shape=jax.ShapeDtypeStruct(x.shape, F32),
        compiler_params=_cparams(("arbitrary", "arbitrary")),
        name="mix_ffn",
    )(x, mod, *ys, *rw, *consts)


def _hy_features(length):
    t01 = np.linspace(0.0, 1.0, length, dtype=np.float32)[:, None]
    bands = (HY_EMB - 1) // 2
    w_ang = (np.float32(2.0 * math.pi) * np.arange(length, dtype=np.float32)[:, None]
             / np.float32(length)).astype(np.float32)
    f = np.linspace(1e-4, bands - 1, bands, dtype=np.float32)[None]
    arg = (f * w_ang).astype(np.float32)
    z = np.concatenate([t01, np.cos(arg), -np.sin(arg)], axis=-1).astype(np.float32)
    zp = np.zeros((length, LANES), np.float32)
    zp[:, :HY_EMB] = z
    return zp


def _hy_deltas():
    max_decay = math.log(HY_TARGET) / HY_FAST_PCT
    min_decay = math.log(HY_TARGET) / HY_SLOW_PCT
    d = np.abs(np.linspace(min_decay, max_decay, HY_CH, dtype=np.float32))
    return np.tile(d, 2)[None].astype(np.float32)


def _hyfilt_kernel(z_ref, w1_ref, b1_ref, w2_ref, b2_ref, w3_ref, b3_ref, w4_ref, fr_ref,
                   dl_ref, o_ref):
    z = z_ref[...]
    fr = fr_ref[...]
    h = jnp.sin(fr * (_dot(z, w1_ref[...], HI) + b1_ref[...]))
    h = jnp.sin(fr * (_dot(h, w2_ref[...], HI) + b2_ref[...]))
    h = jnp.sin(fr * (_dot(h, w3_ref[...], HI) + b3_ref[...]))
    h = _dot(h, w4_ref[...], HI)
    o_ref[...] = h * jnp.exp(-z[:, 0:1] * dl_ref[...])


def _hy_filters(length, w1p, b1, w2, b2, w3, b3, w4, freq):
    z = jnp.asarray(_hy_features(length))
    dl = jnp.asarray(_hy_deltas())
    tl = min(ROW_TILE, length)
    consts = (w1p, b1, w2, b2, w3, b3, w4, freq, dl)
    return pl.pallas_call(
        _hyfilt_kernel,
        grid=(length // tl,),
        in_specs=[pl.BlockSpec((tl, LANES), lambda i: (i, 0))] + [_const_spec(a.shape) for a in consts],
        out_specs=pl.BlockSpec((tl, 2 * HY_CH), lambda i: (i, 0)),
        out_shape=jax.ShapeDtypeStruct((length, 2 * HY_CH), F32),
        compiler_params=_cparams(("arbitrary",)),
        name="hyena_filters",
    )(z, *consts)


FFT_SLAB_PAD = 8
FFT_GROUP = 16
FFT_SLAB_GROUP = 12


def _fft_factors(length):
    return (128, 64) if length >= 2048 else (2 * length, 1)


def _fft_layout(n1, n2):
    if n2 == 1:
        return 1, n1, 2 * n1
    kept = n1 // 2 + 1
    return -(-kept // FFT_SLAB_GROUP) * FFT_SLAB_GROUP, n2, 2 * n2 + FFT_SLAB_PAD


def _stack3(t):
    hi = t.astype(jnp.bfloat16)
    lo = (t - hi.astype(np.float64)).astype(jnp.bfloat16)
    return np.concatenate([hi, lo, hi], axis=-1)


def _rows3(d):
    hi = d.astype(BF16)
    lo = (d - hi.astype(F32)).astype(BF16)
    return jnp.concatenate([hi, hi, lo], axis=0)


@functools.lru_cache(maxsize=None)
def _fft_tables(length):
    n1, n2 = _fft_factors(length)
    n = n1 * n2
    assert n == 2 * length
    a_n2 = np.arange(n2)[:, None, None]
    a_k1 = np.arange(n1)[None, :, None]
    a_n1 = np.arange(n1 // 2)[None, None, :]
    ang = 2.0 * np.pi * ((a_n1 * a_k1 % n1) / n1 + (a_n2 * a_k1 % n) / n)
    t_re, t_im = np.cos(ang), -np.sin(ang)
    wgt = np.ones(n1)
    if n2 > 1:
        nk = _fft_layout(n1, n2)[0]
        keep = np.arange(nk) <= n1 // 2
        t_re, t_im = (np.where(keep[None, :, None], t[:, :nk], 0.0) for t in (t_re, t_im))
        wgt = np.where(np.isin(np.arange(nk), (0, n1 // 2)), 1.0, 2.0) * keep
    ta = np.concatenate([t_re, t_im], axis=1)
    t_re_w, t_im_w = t_re * wgt[None, :, None], t_im * wgt[None, :, None]
    tai = np.concatenate([np.transpose(t_re_w, (0, 2, 1)), np.transpose(t_im_w, (0, 2, 1))], axis=2) / n
    jj = np.arange(n2)
    ang2 = 2.0 * np.pi * (np.outer(jj, jj) % n2) / n2
    c, s = np.cos(ang2), np.sin(ang2)
    fb = np.block([[c, s], [-s, c]])
    fbi = np.block([[c, -s], [s, c]])
    return tuple(_stack3(t) for t in (ta, tai, fb, fbi))


def _fft_stage_a(src_ref, fbuf_ref, ta_ref, n1, n2):
    if n2 == 1:
        fbuf_ref[...] = _dot(ta_ref[0], _rows3(src_ref[...]))
        return
    nk, _, pitch = _fft_layout(n1, n2)

    def body(i, carry):
        js = [i * FFT_GROUP + g for g in range(FFT_GROUP)]
        rows = [_rows3(src_ref[pl.ds(j, n1 // 2, stride=n2), :]) for j in js]
        ts = [_dot(ta_ref[j], r) for j, r in zip(js, rows)]
        for j, t in zip(js, ts):
            fbuf_ref[pl.ds(j, nk, stride=pitch), :] = t[:nk]
            fbuf_ref[pl.ds(n2 + j, nk, stride=pitch), :] = t[nk:]
        return carry

    lax.fori_loop(0, n2 // FFT_GROUP, body, 0)


def _spec_kernel(hf_ref, hb_ref, bias_ref, ta_ref, fb_ref, o_ref, fbuf_ref, *, n1, n2):
    nslab, hs, pitch = _fft_layout(n1, n2)
    if pitch > 2 * hs:
        o_ref[...] = jnp.zeros(o_ref.shape, F32)
    _fft_stage_a(hf_ref, o_ref, ta_ref, n1, n2)
    _fft_stage_a(hb_ref, fbuf_ref, ta_ref, n1, n2)
    is_re = lax.broadcasted_iota(jnp.int32, (2 * hs, 1), 0) < hs
    sgn = jnp.where(is_re, 1.0, -1.0).astype(F32)
    skip = jnp.where(is_re, bias_ref[...], 0.0)

    grp = FFT_SLAB_GROUP if n2 > 1 else 1

    def stage_b(i, carry):
        sls = [pl.ds(pl.multiple_of((i * grp + g) * pitch, 8), 2 * hs) for g in range(grp)]
        a = [o_ref[sl, :] for sl in sls]
        b = [fbuf_ref[sl, :] for sl in sls]
        if n2 > 1:
            a = [_dot(fb_ref[...], _rows3(x)) for x in a]
            b = [_dot(fb_ref[...], _rows3(x)) for x in b]
        for sl, x, y in zip(sls, a, b):
            o_ref[sl, :] = x + sgn * y + skip
        return carry

    lax.fori_loop(0, nslab // grp, stage_b, 0)


def _hy_spectrum(h, bias):
    length = h.shape[0]
    n1, n2 = _fft_factors(length)
    nslab, _, pitch = _fft_layout(n1, n2)
    ta, _, fb, _ = (jnp.asarray(t) for t in _fft_tables(length))
    nblk = HY_CH // LANES
    return pl.pallas_call(
        functools.partial(_spec_kernel, n1=n1, n2=n2),
        grid=(nblk,),
        in_specs=[pl.BlockSpec((length, LANES), lambda j: (0, j)),
                  pl.BlockSpec((length, LANES), lambda j: (0, nblk + j)),
                  pl.BlockSpec((1, LANES), lambda j: (0, j)),
                  _const_spec(ta.shape), _const_spec(fb.shape)],
        out_specs=pl.BlockSpec((nslab * pitch, LANES), lambda j: (0, j)),
        out_shape=jax.ShapeDtypeStruct((nslab * pitch, HY_CH), F32),
        scratch_shapes=[pltpu.VMEM((nslab * pitch, LANES), F32)],
        compiler_params=_cparams(("arbitrary",)),
        name="hyena_spectrum",
    )(h, h, bias, ta, fb)


def _shift_rows(x, prev_row, next_row):
    n = x.shape[0]
    row = lax.broadcasted_iota(jnp.int32, x.shape, 0)
    up = jnp.where(row == 0, prev_row, pltpu.roll(x, 1, 0))
    dn = jnp.where(row == n - 1, next_row, pltpu.roll(x, n - 1, 0))
    return up, dn


def _hyconv_kernel(x1_ref, x2_ref, v_ref, cw1_ref, cw2_ref, cwv_ref, cb_ref, g_ref,
                   ta_ref, tai_ref, fb_ref, fbi_ref, o_ref, u_ref, fbuf_ref, *, n1, n2):
    def conv3(ref, w_ref, b):
        x = ref[...]
        up, dn = _shift_rows(x, 0.0, 0.0)
        return up * w_ref[0:1, :] + x * w_ref[1:2, :] + dn * w_ref[2:3, :] + b

    cb = cb_ref[...]
    u_ref[...] = conv3(x1_ref, cw1_ref, cb[0:1, :]) * conv3(v_ref, cwv_ref, cb[2:3, :])

    _fft_stage_a(u_ref, fbuf_ref, ta_ref, n1, n2)
    nslab, hs, pitch = _fft_layout(n1, n2)

    grp = FFT_SLAB_GROUP if n2 > 1 else 1

    def cmul(a, g):
        ar, ai, gr, gi = a[:hs], a[hs:], g[:hs], g[hs:]
        return jnp.concatenate([ar * gr - ai * gi, ar * gi + ai * gr], axis=0)

    def stage_b(i, carry):
        sls = [pl.ds(pl.multiple_of((i * grp + g) * pitch, 8), 2 * hs) for g in range(grp)]
        a = [fbuf_ref[sl, :] for sl in sls]
        if n2 > 1:
            a = [_dot(fb_ref[...], _rows3(x)) for x in a]
        a = [cmul(x, g_ref[sl, :]) for x, sl in zip(a, sls)]
        if n2 > 1:
            a = [_dot(fbi_ref[...], _rows3(x)) for x in a]
        for sl, x in zip(sls, a):
            fbuf_ref[sl, :] = x
        return carry

    lax.fori_loop(0, nslab // grp, stage_b, 0)

    if n2 == 1:
        u_ref[...] = _dot(tai_ref[0], _rows3(fbuf_ref[...]))
    else:
        def stage_a_inv(i, carry):
            js = [i * FFT_GROUP + g for g in range(FFT_GROUP)]
            cs = [_rows3(jnp.concatenate([fbuf_ref[pl.ds(j, nslab, stride=pitch), :],
                                          fbuf_ref[pl.ds(n2 + j, nslab, stride=pitch), :]], axis=0))
                  for j in js]
            ys = [_dot(tai_ref[j], c) for j, c in zip(js, cs)]
            for j, y in zip(js, ys):
                u_ref[pl.ds(j, n1 // 2, stride=n2), :] = y
            return carry

        lax.fori_loop(0, n2 // FFT_GROUP, stage_a_inv, 0)

    o_ref[...] = (conv3(x2_ref, cw2_ref, cb[1:2, :]) * u_ref[...]).astype(o_ref.dtype)


def _hy_conv(p, conv_w, conv_b, spec):
    b, length, _ = p.shape
    n1, n2 = _fft_factors(length)
    nslab, _, pitch = _fft_layout(n1, n2)
    ta, tai, fb, fbi = (jnp.asarray(t) for t in _fft_tables(length))
    nblk = HY_CH // LANES
    cb3 = conv_b.reshape(3, HY_CH)
    col = lambda g: pl.BlockSpec((None, length, LANES), lambda j, i, g=g: (i, 0, g * nblk + j))
    wcol = lambda g: pl.BlockSpec((3, LANES), lambda j, i, g=g: (0, g * nblk + j))
    return pl.pallas_call(
        functools.partial(_hyconv_kernel, n1=n1, n2=n2),
        grid=(nblk, b),
        in_specs=[col(0), col(1), col(2), wcol(0), wcol(1), wcol(2),
                  pl.BlockSpec((3, LANES), lambda j, i: (0, j)),
                  pl.BlockSpec((nslab * pitch, LANES), lambda j, i: (0, j), pipeline_mode=pl.Buffered(1)),
                  _const_spec(ta.shape), _const_spec(tai.shape), _const_spec(fb.shape),
                  _const_spec(fbi.shape)],
        out_specs=pl.BlockSpec((None, length, LANES), lambda j, i: (i, 0, j)),
        out_shape=jax.ShapeDtypeStruct((b, length, HY_CH), BF16),
        scratch_shapes=[pltpu.VMEM((length, LANES), F32), pltpu.VMEM((nslab * pitch, LANES), F32)],
        compiler_params=_cparams(("arbitrary", "arbitrary")),
        name="hyena_conv",
    )(p, p, p, conv_w, conv_w, conv_w, cb3, spec, ta, tai, fb, fbi)


def _rope_tables(length, d_rot, lane_lo, head_w):
    rows = length // GRID_W
    row = np.repeat(np.arange(rows, dtype=np.float32), GRID_W)
    colv = np.tile(np.arange(GRID_W, dtype=np.float32), rows)
    n_freq = d_rot // 4
    inv = (np.float32(ROPE_THETA) ** (-np.arange(n_freq, dtype=np.float32) / np.float32(n_freq))).astype(np.float32)
    ang = np.concatenate([row[:, None] * inv, colv[:, None] * inv], axis=-1).astype(np.float32)
    cos_t = np.ones((length, LANES), np.float32)
    sin_t = np.zeros((length, LANES), np.float32)
    c, s = np.cos(ang), np.sin(ang)
    for base in range(0, LANES, head_w):
        for i in range(d_rot // 2):
            cos_t[:, base + lane_lo + 2 * i] = c[:, i]
            cos_t[:, base + lane_lo + 2 * i + 1] = c[:, i]
            sin_t[:, base + lane_lo + 2 * i] = -s[:, i]
            sin_t[:, base + lane_lo + 2 * i + 1] = s[:, i]
    return cos_t, sin_t


def _rope_angles(length, d_rot):
    rows = length // GRID_W
    row = np.repeat(np.arange(rows, dtype=np.float32), GRID_W)
    colv = np.tile(np.arange(GRID_W, dtype=np.float32), rows)
    n_freq = d_rot // 4
    inv = (np.float32(ROPE_THETA) ** (-np.arange(n_freq, dtype=np.float32) / np.float32(n_freq))).astype(np.float32)
    return np.concatenate([row[:, None] * inv, colv[:, None] * inv], axis=-1).astype(np.float32)


def _mla_lanes():
    lanes = np.empty(MLA_QK, np.int64)
    n = np.arange(MLA_NOPE)
    lanes[:MLA_NOPE] = np.where(n < 32, n, n + 16)
    i = np.arange(MLA_ROPE // 2)
    lanes[MLA_NOPE + 2 * i] = 32 + i
    lanes[MLA_NOPE + 2 * i + 1] = 96 + i
    return lanes


def _mla_rope_tables(length):
    ang = _rope_angles(length, MLA_ROPE)
    cos_t = np.ones((length, LANES), np.float32)
    sin_t = np.zeros((length, LANES), np.float32)
    n_pairs = MLA_ROPE // 2
    cos_t[:, 32:32 + n_pairs] = cos_t[:, 96:96 + n_pairs] = np.cos(ang)
    sin_t[:, 32:32 + n_pairs] = -np.sin(ang)
    sin_t[:, 96:96 + n_pairs] = np.sin(ang)
    return cos_t, sin_t


def _rope(x, cos_t, sin_t):
    lane = lax.broadcasted_iota(jnp.int32, x.shape, 1)
    w = x.shape[1]
    partner = jnp.where(jnp.bitwise_and(lane, 1) == 0, pltpu.roll(x, w - 1, 1), pltpu.roll(x, 1, 1))
    return x * cos_t + partner * sin_t


def _store_vt(vt_ref, h, vt):
    row = lax.broadcasted_iota(jnp.int32, (VT_ROWS - HEAD_DIM, vt.shape[1]), 0)
    vt_ref[h, 0:HEAD_DIM, :] = vt.astype(BF16)
    vt_ref[h, HEAD_DIM:VT_ROWS, :] = jnp.where(row == 0, 1.0, 0.0).astype(BF16)


def _gqa_prep_kernel(p_ref, cos_ref, sin_ref, gq_ref, gk_ref, seg_ref, q_ref, k_ref, vt_ref, *, rope):
    p = p_ref[...]
    seg = seg_ref[...]
    lane = lax.broadcasted_iota(jnp.int32, (p.shape[0], LANES), 1)
    low = lane < HEAD_DIM

    def hnorm(x, g):
        ms = _segsum(x * x, seg)
        return x * lax.rsqrt(ms + NORM_EPS) * g

    scale = HEAD_DIM ** -0.5 * LOG2E
    for c in range(2):
        q = hnorm(p[:, c * LANES:(c + 1) * LANES], gq_ref[...])
        if rope:
            q = _rope(q, cos_ref[...], sin_ref[...])
        q = q * scale
        qs = pltpu.roll(q, HEAD_DIM, 1)
        q_ref[:, (2 * c) * LANES:(2 * c + 1) * LANES] = jnp.where(low, q, 0.0).astype(BF16)
        q_ref[:, (2 * c + 1) * LANES:(2 * c + 2) * LANES] = jnp.where(low, qs, 0.0).astype(BF16)
    k = hnorm(p[:, 256:384], gk_ref[...])
    if rope:
        k = _rope(k, cos_ref[...], sin_ref[...])
    ks = pltpu.roll(k, HEAD_DIM, 1)
    k_ref[:, 0:LANES] = jnp.where(low, k, 0.0).astype(BF16)
    k_ref[:, LANES:2 * LANES] = jnp.where(low, ks, 0.0).astype(BF16)
    vt = p[:, 384:512].T
    _store_vt(vt_ref, 0, vt[0:HEAD_DIM])
    _store_vt(vt_ref, 1, vt[HEAD_DIM:2 * HEAD_DIM])


def _seg_matrix(width, seg, value):
    i = np.arange(width)
    return ((i[:, None] // seg) == (i[None, :] // seg)).astype(np.float32) * np.float32(value)


def _gqa_prep(p, gq, gk, rope):
    b, length, _ = p.shape
    tm = min(ROW_TILE, length)
    if rope:
        cos_t, sin_t = (jnp.asarray(t) for t in _rope_tables(length, HEAD_DIM, 0, HEAD_DIM))
    else:
        cos_t = sin_t = jnp.zeros((length, LANES), F32)
    seg = jnp.asarray(_seg_matrix(LANES, HEAD_DIM, 1.0 / HEAD_DIM), BF16)
    tab = pl.BlockSpec((tm, LANES), lambda i, bb: (i, 0))
    outw = (512, 256)
    return pl.pallas_call(
        functools.partial(_gqa_prep_kernel, rope=rope),
        grid=(length // tm, b),
        in_specs=[pl.BlockSpec((None, tm, GQA_COLS), lambda i, bb: (bb, i, 0)), tab, tab,
                  _const_spec((1, LANES)), _const_spec((1, LANES)), _const_spec((LANES, LANES))],
        out_specs=[pl.BlockSpec((None, tm, w), lambda i, bb: (bb, i, 0)) for w in outw]
                  + [pl.BlockSpec((None, GQA_KV_HEADS, VT_ROWS, tm), lambda i, bb: (bb, 0, 0, i))],
        out_shape=[jax.ShapeDtypeStruct((b, length, w), BF16) for w in outw]
                  + [jax.ShapeDtypeStruct((b, GQA_KV_HEADS, VT_ROWS, length), BF16)],
        compiler_params=_cparams(("arbitrary", "arbitrary")),
        name="gqa_prep",
    )(p, cos_t, sin_t, gq, gk, seg)


def _mla_prep_kernel(p_ref, cos_ref, sin_ref, cqn_ref, ckvn_ref, wuq_ref, wuk_ref, wuv_ref,
                     qn_ref, kn_ref, ones_ref, q_ref, k_ref, vt_ref, *, rope):
    p = p_ref[...]

    def rms(x, g, width):
        ms = _segsum(x * x, ones_ref[0:x.shape[1], 0:x.shape[1]]) * (1.0 / width)
        return x * lax.rsqrt(ms + NORM_EPS) * g

    cq = rms(p[:, 0:MLA_Q_RANK], cqn_ref[...], MLA_Q_RANK).astype(BF16)
    ckv = rms(p[:, MLA_Q_RANK:MLA_Q_RANK + MLA_KV_RANK], ckvn_ref[...], MLA_KV_RANK).astype(BF16)
    q = _dot(cq, wuq_ref[...])
    kn = _dot(ckv, wuk_ref[...])
    v = _dot(ckv, wuv_ref[...])
    for c in range(MLA_HEADS // 2):
        vt = v[:, c * LANES:(c + 1) * LANES].T
        _store_vt(vt_ref, 2 * c, vt[0:MLA_V])
        _store_vt(vt_ref, 2 * c + 1, vt[MLA_V:2 * MLA_V])
    kr = p[:, 384:512]
    scale = MLA_QK ** -0.5 * LOG2E

    def rot(x):
        return x * cos_ref[...] + pltpu.roll(x, LANES // 2, 1) * sin_ref[...]

    for h in range(MLA_HEADS):
        sl = slice(h * LANES, (h + 1) * LANES)
        qh = rms(q[:, sl], qn_ref[...], MLA_QK)
        kh = rms(kn[:, sl] + kr, kn_ref[...], MLA_QK)
        if rope:
            qh, kh = rot(qh), rot(kh)
        q_ref[:, sl] = (qh * scale).astype(BF16)
        k_ref[:, sl] = kh.astype(BF16)


def _mla_prep(p, cqn, ckvn, wuq, wuk, wuv, qn, kn, rope):
    b, length, _ = p.shape
    tm = min(ROW_TILE, length)
    if rope:
        cos_t, sin_t = (jnp.asarray(t) for t in _mla_rope_tables(length))
    else:
        cos_t = sin_t = jnp.zeros((length, LANES), F32)
    tab = pl.BlockSpec((tm, LANES), lambda i, bb: (i, 0))
    ones = jnp.ones((MLA_Q_RANK, MLA_Q_RANK), BF16)
    consts = (cqn, ckvn, wuq, wuk, wuv, qn, kn, ones)
    outw = (512, 512)
    return pl.pallas_call(
        functools.partial(_mla_prep_kernel, rope=rope),
        grid=(length // tm, b),
        in_specs=[pl.BlockSpec((None, tm, MLA_COLS_PAD), lambda i, bb: (bb, i, 0)), tab, tab]
                 + [_const_spec(a.shape) for a in consts],
        out_specs=[pl.BlockSpec((None, tm, w), lambda i, bb: (bb, i, 0)) for w in outw]
                  + [pl.BlockSpec((None, MLA_HEADS, VT_ROWS, tm), lambda i, bb: (bb, 0, 0, i))],
        out_shape=[jax.ShapeDtypeStruct((b, length, w), BF16) for w in outw]
                  + [jax.ShapeDtypeStruct((b, MLA_HEADS, VT_ROWS, length), BF16)],
        compiler_params=_cparams(("arbitrary", "arbitrary")),
        name="mla_prep",
    )(p, cos_t, sin_t, *consts)


def _attn_kernel(*refs, n_seg):
    nh = ATTN_HEADS
    q_ref, o_ref = refs[0], refs[-1]
    segs = [refs[1 + 2 * nh * s:1 + 2 * nh * (s + 1)] for s in range(n_seg)]
    q = q_ref[...]
    qh = [q[:, h * LANES:(h + 1) * LANES] for h in range(nh)]
    pieces = [(h, seg, c0, min(KEY_CHUNK, seg[0].shape[0]))
              for h in range(nh) for seg in segs
              for c0 in range(0, seg[0].shape[0], min(KEY_CHUNK, seg[0].shape[0]))]
    scores = [_dot_nt(seg[h][c0:c0 + n, :], qh[h]) for h, seg, c0, n in pieces]
    mx = [jnp.max(s, axis=0, keepdims=True) for s in scores]
    ps = [jnp.exp2(s - m).astype(BF16) for s, m in zip(scores, mx)]
    os_ = [_dot(seg[nh + h][:, c0:c0 + n], e) for (h, seg, c0, n), e in zip(pieces, ps)]
    outs = []
    for h in range(nh):
        idx = [i for i, pc in enumerate(pieces) if pc[0] == h]
        m = functools.reduce(jnp.maximum, [mx[i] for i in idx])
        o = functools.reduce(jnp.add, [os_[i] * jnp.exp2(mx[i] - m) for i in idx])
        outs.append(o[0:HEAD_DIM] / o[HEAD_DIM:HEAD_DIM + 1])
    o_ref[...] = jnp.concatenate(outs, axis=0).T.astype(o_ref.dtype)


def _attention(q, kv_segs, k_heads):
    b, lq, _ = q.shape
    tq = min(Q_TILE, lq)
    nh = ATTN_HEADS
    kv_of = [h * k_heads // nh for h in range(nh)]
    in_specs = [pl.BlockSpec((None, tq, nh * LANES), lambda bb, i: (bb, i, 0))]
    args = [q]
    for k, vt in kv_segs:
        lk = k.shape[1]
        in_specs += [pl.BlockSpec((None, lk, LANES), lambda bb, i, j=j: (bb, 0, j)) for j in kv_of]
        in_specs += [pl.BlockSpec((None, None, VT_ROWS, lk), lambda bb, i, j=j: (bb, j, 0, 0)) for j in kv_of]
        args += [k] * nh + [vt] * nh
    return pl.pallas_call(
        functools.partial(_attn_kernel, n_seg=len(kv_segs)),
        grid=(b, lq // tq),
        in_specs=in_specs,
        out_specs=pl.BlockSpec((None, tq, nh * HEAD_DIM), lambda bb, i: (bb, i, 0)),
        out_shape=jax.ShapeDtypeStruct((b, lq, nh * HEAD_DIM), BF16),
        compiler_params=_cparams(("arbitrary", "arbitrary")),
        name="attention",
    )(*args)


def _rw_prep_kernel(p_ref, prev_ref, next_ref, mu_ref, kk_ref_w, ka_ref, rk_ref, w0_ref, w2_ref,
                    a0_ref, a2_ref, g2_ref, seg_ref,
                    r_ref, v_ref, kk_ref, lw_ref, kd_ref, bd_ref, bonus_ref, g_ref):
    i = pl.program_id(1)
    n = pl.num_programs(1)
    x = p_ref[...]
    prev_row = jnp.where(i > 0, prev_ref[7:8, :], 0.0)
    next_row = jnp.where(i < n - 1, next_ref[0:1, :], 0.0)
    up, dn = _shift_rows(x, prev_row, next_row)
    xs = x + mu_ref[...] * (0.5 * (up + dn) - x)
    c = RW_C
    r, k, v = xs[:, 0:c], xs[:, c:2 * c], xs[:, 2 * c:3 * c]
    xw, xa, xg = xs[:, 3 * c:3 * c + 128], xs[:, 3 * c + 128:3 * c + 256], xs[:, 3 * c + 256:3 * c + 384]
    seg = seg_ref[...]
    kk = k * kk_ref_w[...]
    nrm = jnp.sqrt(_segsum(kk * kk, seg))
    kk = kk / jnp.maximum(nrm, 1e-12)
    u = w0_ref[...] + _dot(jnp.tanh(xw).astype(BF16), w2_ref[...])
    z = -u
    softplus = jnp.maximum(z, 0.0) + jnp.log(1.0 + jnp.exp(-jnp.abs(z)))
    lw_ref[...] = -jnp.exp(-softplus - 0.5)
    a = _sigmoid(a0_ref[...] + _dot(xa.astype(BF16), a2_ref[...]))
    bonus = jnp.zeros_like(r)
    for d in range(2):
        a_d = a[:, d * c:(d + 1) * c]
        k_d = k * (1.0 + (a_d - 1.0) * ka_ref[...])
        kd_ref[:, d * c:(d + 1) * c] = k_d
        bd_ref[:, d * c:(d + 1) * c] = kk * a_d
        bonus = bonus + _segsum(r * k_d * rk_ref[...], seg)
    r_ref[...] = r
    v_ref[...] = v
    kk_ref[...] = kk
    bonus_ref[...] = bonus * v
    g_ref[...] = _dot(_sigmoid(xg).astype(BF16), g2_ref[...])


def _rw_prep(p, mu, k_k, k_a, r_k, w0, w2bd, a0, a2bd, g2):
    b, length, cols = p.shape
    tm = min(ROW_TILE, length)
    nb = tm // 8
    last = length // 8 - 1
    seg = jnp.asarray(_seg_matrix(RW_C, RW_N, 1.0), BF16)
    consts = (mu, k_k, k_a, r_k, w0, w2bd, a0, a2bd, g2, seg)
    outw = (RW_C, RW_C, RW_C, 2 * RW_C, 2 * RW_C, 2 * RW_C, RW_C, RW_C)
    return pl.pallas_call(
        _rw_prep_kernel,
        grid=(b, length // tm),
        in_specs=[pl.BlockSpec((None, tm, cols), lambda bb, i: (bb, i, 0)),
                  pl.BlockSpec((None, 8, cols), lambda bb, i: (bb, jnp.maximum(i * nb - 1, 0), 0)),
                  pl.BlockSpec((None, 8, cols), lambda bb, i: (bb, jnp.minimum((i + 1) * nb, last), 0))]
                 + [_const_spec(a.shape) for a in consts],
        out_specs=[pl.BlockSpec((None, tm, w), lambda bb, i: (bb, i, 0)) for w in outw],
        out_shape=[jax.ShapeDtypeStruct((b, length, w), F32) for w in outw],
        compiler_params=_cparams(("arbitrary", "arbitrary")),
        name="rwkv_prep",
    )(p, p, p, *consts)


@functools.lru_cache(maxsize=None)
def _rw_masks():
    cs, n = RW_CHUNK, RW_C
    i = np.arange(n)
    same = (i[:, None] // cs) == (i[None, :] // cs)
    rel = (i[None, :] % cs) - (i[:, None] % cs)
    masks = np.stack([same & (rel < 0), same & (rel <= 0), same & (rel > 0), same & (rel >= 0),
                      same, i[:, None] == i[None, :]]).astype(np.float32)
    t = np.arange(cs)
    tri = np.stack([t[None, :] <= t[:, None], t[None, :] >= t[:, None]]).astype(np.float32)
    tr, tc = (i % cs)[:, None], (i % cs)[None, :]
    lvl = []
    for sgn in (1, -1):
        earlier = (tc - tr) * sgn < 0
        rows = [same & earlier & (tr // RW_BASE == tc // RW_BASE)]
        for k in range(RW_LEVELS):
            s = RW_BASE << k
            rows.append(same & earlier & (tr // (2 * s) == tc // (2 * s)) & (tr // s != tc // s))
        lvl.append(np.stack(rows))
    return masks, tri, np.stack(lvl).astype(np.float32)


def _rw_scan_kernel(rf_ref, rb_ref, vf_ref, vb_ref, kkf_ref, kkb_ref, lwf_ref, lwb_ref, kdf_ref, kdb_ref,
                    bf_ref, bb_ref, s0_ref, mask_ref, tri_ref, lvl_ref, yf_ref, yb_ref, sfin_ref, s_ref,
                    *, n_chunks, n_batch):
    cs = RW_CHUNK

    @pl.when(pl.program_id(1) == 0)
    def _():
        s_ref[...] = s0_ref[...]

    same_head = mask_ref[4].astype(BF16)
    eye = mask_ref[5]
    dirs = ((rf_ref, vf_ref, kkf_ref, lwf_ref, kdf_ref, bf_ref, yf_ref),
            (rb_ref, vb_ref, kkb_ref, lwb_ref, kdb_ref, bb_ref, yb_ref))

    def tile(x):
        xb = x.astype(BF16)
        return jnp.concatenate([xb, xb, xb, xb], axis=0)

    def bd(x):
        return tile(x) * same_head

    def spread(a, half):
        swapped = pltpu.roll(a, cs, 1)
        low = lax.broadcasted_iota(jnp.int32, a.shape, 1) < cs
        h = jnp.where(low, swapped, a) if half else jnp.where(low, a, swapped)
        return jnp.concatenate([h, h], axis=1)

    streams = [(bi, d) for bi in range(n_batch) for d in range(2)]

    def each(f, *cols):
        return [f(*a) for a in zip(*cols)] if cols else [f(bi, d) for bi, d in streams]

    bf = lambda xs: [x.astype(BF16) for x in xs]

    def body(c, carry):
        def load(bi, d):
            cc = c if d == 0 else n_chunks - 1 - c
            sl = pl.ds(pl.multiple_of(cc * cs, cs), cs)
            return [ref[bi, sl, :] for ref in dirs[d][:6]] + [sl]

        r, v, kk, lw, kd, b, sl = zip(*each(load))
        strict = [mask_ref[2 * d] for _, d in streams]
        incl = [mask_ref[2 * d + 1] for _, d in streams]
        cl = [functools.reduce(jnp.add, [_dot(tri_ref[d], p) for p in _pieces(x, 3)])
              for (_, d), x in zip(streams, lw)]
        w_inv = [jnp.exp(-x) for x in cl]
        w_all = [jnp.exp(jnp.sum(x, axis=0, keepdims=True)) for x in lw]
        kt = each(lambda a, w: a * w, kd, w_inv)
        bt = each(lambda a, w: a * w, b, w_inv)
        qk_s = each(lambda a, x, y: bd(a * jnp.exp(x - y)), kk, cl, lw)
        rt_s = each(lambda a, x: bd(a * jnp.exp(x)), r, cl)
        kb = each(lambda a, t: jnp.concatenate([a.astype(BF16), t.astype(BF16)], axis=0), kt, bt)
        aq = each(_dot_nt, qk_s, kb)
        ar = each(_dot_nt, rt_s, kb)
        a_kk = each(lambda a, m: (spread(a, 0) * m).astype(BF16), aq, strict)
        a_kb = each(lambda a, m: spread(a, 1) * m, aq, strict)
        a_rk = each(lambda a, m: (spread(a, 0) * m).astype(BF16), ar, incl)
        a_rb = each(lambda a, m: (spread(a, 1) * m).astype(BF16), ar, incl)
        dmask = lambda k: [lvl_ref[d, k] for _, d in streams]
        l0 = each(lambda a, m: a * m, a_kb, dmask(0))
        p0 = [eye - a for a in l0]
        sq = each(lambda a: _dot(a, a), bf(l0))
        t_inv = each(lambda p, pb, s: p + _dot(pb, s), p0, bf(p0), bf(sq))
        for k in range(1, RW_LEVELS + 1):
            cm = bf(each(lambda a, m: a * m, a_kb, dmask(k)))
            tb = bf(t_inv)
            tc = bf(each(_dot, tb, cm))
            t_inv = each(lambda t, x, y: t - _dot(x, y), t_inv, tc, tb)
        t_inv = bf(t_inv)
        v_s = each(bd, v)
        akv = bf(each(_dot, a_kk, v_s))
        x1 = bf(each(_dot, t_inv, qk_s))
        x2 = each(_dot, t_inv, akv)
        bh_s = each(lambda a, w: bd(a * w), bt, w_all)
        kh_s = each(lambda a, w: bd(a * w), kt, w_all)
        y_v = each(_dot, a_rk, v_s)
        s_v = each(_dot_tn, v_s, kh_s)
        s0f = [s_ref[bi, d] for bi, d in streams]
        s0 = bf(s0f)
        u = bf(each(lambda x, s, y: _dot_nt(x, s) + y, x1, s0, x2))
        y_bd = each(lambda rt, s, yv, ab, u_: _dot_nt(rt, s) + yv - _dot(ab, u_), rt_s, s0, y_v, a_rb, u)
        s1 = each(lambda s, w, sv, u_, bh: s * w + sv - _dot_tn(u_, bh), s0f, w_all, s_v, u, bh_s)
        for (bi, d), y, s, rows in zip(streams, y_bd, s1, sl):
            dirs[d][6][bi, rows, :] = y[0:cs] + y[cs:2 * cs] + y[2 * cs:3 * cs] + y[3 * cs:4 * cs]
            s_ref[bi, d] = s
        return carry

    lax.fori_loop(0, n_chunks, body, 0)
    sfin_ref[...] = s_ref[...]


def _rw_scan(r, v, kk, lw, kd, bdir, s0):
    b, length, c = r.shape
    tc = min(RW_BLOCK, length)
    nblk = length // tc
    nb = RW_SEQS
    masks, tri, lvl = _rw_masks()
    masks, tri, lvl = jnp.asarray(masks), jnp.asarray(tri, BF16), jnp.asarray(lvl)
    fwd = lambda col: pl.BlockSpec((nb, tc, c), lambda g, i: (g, i, col))
    bwd = lambda col: pl.BlockSpec((nb, tc, c), lambda g, i: (g, nblk - 1 - i, col))
    state = pl.BlockSpec((nb, 2, c, c), lambda g, i: (g, 0, 0, 0))
    return pl.pallas_call(
        functools.partial(_rw_scan_kernel, n_chunks=tc // RW_CHUNK, n_batch=nb),
        grid=(b // nb, nblk),
        in_specs=[fwd(0), bwd(0)] * 3 + [fwd(0), bwd(1)] * 3
                 + [state, _const_spec(masks.shape), _const_spec(tri.shape), _const_spec(lvl.shape)],
        out_specs=[fwd(0), bwd(0), state],
        out_shape=[jax.ShapeDtypeStruct((b, length, c), F32), jax.ShapeDtypeStruct((b, length, c), F32),
                   jax.ShapeDtypeStruct((b, 2, c, c), F32)],
        scratch_shapes=[pltpu.VMEM((nb, 2, c, c), F32)],
        compiler_params=_cparams(("arbitrary", "arbitrary")),
        name="rwkv_scan",
    )(r, r, v, v, kk, kk, lw, lw, kd, kd, bdir, bdir, s0, masks, tri, lvl)


def _blockdiag2(w):
    k, n = w.shape[1:]
    z = jnp.zeros((k, n), w.dtype)
    return jnp.concatenate([jnp.concatenate([w[0], z], axis=1), jnp.concatenate([z, w[1]], axis=1)], axis=0)


def _mla_spread(w, lanes):
    k = w.shape[0]
    w = w.reshape(k, -1, len(lanes))
    return jnp.zeros((k, w.shape[1], LANES), w.dtype).at[:, :, lanes].set(w).reshape(k, -1)


def kernel(x, c, ctx, c_ctx, ada_w, ada_b, norm_ffn1, norm_mix, norm_ffn2, ffn1_gate, ffn1_up, ffn1_down, ffn2_gate, ffn2_up, ffn2_down, w_in, w_out, hy_conv_w, hy_conv_b, hy_f_w1, hy_f_b1, hy_f_w2, hy_f_b2, hy_f_w3, hy_f_b3, hy_f_w4, hy_f_freq, hy_bias, gqa_q_norm, gqa_k_norm, mla_cq_norm, mla_ckv_norm, mla_w_uq, mla_w_ukv, mla_q_norm, mla_k_norm, rw_mu, rw_w0, rw_w2, rw_a0, rw_a2, rw_g2, rw_k_k, rw_k_a, rw_r_k, rw_ln_w, rw_ln_b):
    b, length, d = x.shape
    lc = ctx.shape[1]
    depth = ada_w.shape[0]

    c_all = jnp.zeros((16, d), F32).at[:b].set(c).at[b].set(c_ctx)
    mod = _compute_mod(c_all, ada_w, ada_b)

    xc = ctx.reshape(1, b * lc, d)
    for l in range(depth):
        ctx_out = l < depth - 1
        mod_x = mod[l, :b].reshape(b, N_MOD, d)
        mod_c = mod[l, b:b + 1].reshape(1, N_MOD, d)
        row = lambda a: a[l].reshape(1, -1)

        wg1, wu1, wd1 = (w[l].astype(BF16) for w in (ffn1_gate, ffn1_up, ffn1_down))
        wg2, wu2, wd2 = (w[l].astype(BF16) for w in (ffn2_gate, ffn2_up, ffn2_down))
        wi = w_in[l]
        o1, o2, o3 = HY_COLS, HY_COLS + GQA_COLS, HY_COLS + GQA_COLS + MLA_COLS
        ml_lanes = _mla_lanes()
        o_kr = o3 - MLA_ROPE
        wi = jnp.concatenate([wi[:, :o_kr], _mla_spread(wi[:, o_kr:o3], ml_lanes[MLA_NOPE:]), wi[:, o3:]],
                             axis=1).astype(BF16)
        wo = w_out[l].astype(BF16)

        x = _ffn(x, mod_x, row(norm_ffn1), wg1, wu1, wd1, 0)
        xc = _ffn(xc, mod_c, row(norm_ffn1), wg1, wu1, wd1, 0)

        hy_x, gq_x, ml_x, rw_x = _inproj(x, mod_x, row(norm_mix), wi)
        hy_c, gq_c, ml_c, rw_c = (t.reshape(b, lc, -1) for t in _inproj(xc, mod_c, row(norm_mix), wi))

        w1p = jnp.zeros((LANES, HY_ORDER), F32).at[:HY_EMB].set(hy_f_w1[l])
        filt = (w1p, row(hy_f_b1), hy_f_w2[l], row(hy_f_b2), hy_f_w3[l], row(hy_f_b3), hy_f_w4[l],
                row(hy_f_freq))
        spec_x = _hy_spectrum(_hy_filters(length, *filt), row(hy_bias))
        y_hy_x = _hy_conv(hy_x, hy_conv_w[l], hy_conv_b[l], spec_x)

        gq = jnp.tile(row(gqa_q_norm), (1, 2))
        gk = jnp.tile(row(gqa_k_norm), (1, 2))
        q_l, k_l, v_l = _gqa_prep(gq_x, gq, gk, True)
        q_c, k_c, v_c = _gqa_prep(gq_c, gq, gk, False)
        y_gq_x = _attention(q_l, [(k_c, v_c), (k_l, v_l)], GQA_KV_HEADS)

        wuq = _mla_spread(mla_w_uq[l], ml_lanes).astype(BF16)
        wukv = mla_w_ukv[l].reshape(MLA_KV_RANK, MLA_HEADS, MLA_NOPE + MLA_V)
        wuk = _mla_spread(wukv[:, :, :MLA_NOPE].reshape(MLA_KV_RANK, -1), ml_lanes[:MLA_NOPE]).astype(BF16)
        wuv = wukv[:, :, MLA_NOPE:].reshape(MLA_KV_RANK, -1).astype(BF16)
        pad_n = lambda g: _mla_spread(g[l].reshape(1, MLA_QK), ml_lanes)
        mla_w = (row(mla_cq_norm), row(mla_ckv_norm), wuq, wuk, wuv, pad_n(mla_q_norm), pad_n(mla_k_norm))
        mq_l, mk_l, mv_l = _mla_prep(ml_x, *mla_w, True)
        mq_c, mk_c, mv_c = _mla_prep(ml_c, *mla_w, False)
        y_ml_x = _attention(mq_l, [(mk_c, mv_c), (mk_l, mv_l)], MLA_HEADS)

        rw_w = (row(rw_mu), row(rw_k_k), row(rw_k_a), rw_r_k[l].reshape(1, RW_C),
                rw_w0[l].reshape(1, 2 * RW_C), _blockdiag2(rw_w2[l]).astype(BF16),
                rw_a0[l].reshape(1, 2 * RW_C), _blockdiag2(rw_a2[l]).astype(BF16), rw_g2[l].astype(BF16))
        pc = _rw_prep(rw_c, *rw_w)
        px = _rw_prep(rw_x, *rw_w)
        zeros = jnp.zeros((b, 2, RW_C, RW_C), F32)
        yf_c, yb_c, s_ctx = _rw_scan(*pc[:6], zeros)
        yf_x, yb_x, _ = _rw_scan(*px[:6], s_ctx)

        tail = (row(rw_ln_w), row(rw_ln_b), wo, row(norm_ffn2), wg2, wu2, wd2)
        x = _mix_ffn(x, mod_x, (y_hy_x, y_gq_x, y_ml_x), (yf_x, yb_x, px[6], px[7]), *tail)
        if ctx_out:
            spec_c = _hy_spectrum(_hy_filters(lc, *filt), row(hy_bias))
            y_hy_c = _hy_conv(hy_c, hy_conv_w[l], hy_conv_b[l], spec_c)
            y_gq_c = _attention(q_c, [(k_c, v_c)], GQA_KV_HEADS)
            y_ml_c = _attention(mq_c, [(mk_c, mv_c)], MLA_HEADS)
            flat = lambda ts: tuple(t.reshape(1, b * lc, -1) for t in ts)
            xc = _mix_ffn(xc, mod_c, flat((y_hy_c, y_gq_c, y_ml_c)), flat((yf_c, yb_c, pc[6], pc[7])), *tail)
    return x
```

```python
import functools
import math

import numpy as np
import jax
import jax.numpy as jnp
from jax import lax
from jax.experimental import pallas as pl
from jax.experimental.pallas import tpu as pltpu

F32 = jnp.float32
BF16 = jnp.bfloat16
HI = lax.Precision.HIGHEST

D_MODEL = 1024
GRID_W = 64
HEAD_DIM = 64
D_FF = 2816
N_MOD = 9
NORM_EPS = 1e-6
LOG2E = math.log2(math.e)
ROPE_THETA = 10000.0

HY_CH = 256
HY_EMB = 33
HY_ORDER = 64
HY_FAST_PCT = 0.3
HY_SLOW_PCT = 1.5
HY_TARGET = 1e-2

GQA_HEADS = 4
GQA_KV_HEADS = 2
MLA_HEADS = 4
MLA_NOPE = 64
MLA_ROPE = 32
MLA_QK = MLA_NOPE + MLA_ROPE
MLA_V = 64
MLA_Q_RANK = 256
MLA_KV_RANK = 128

RW_HEADS = 4
RW_N = 64
RW_C = RW_HEADS * RW_N
RW_DECAY_LORA = 64
RW_AAA_LORA = 64
RW_GATE_LORA = 128
RW_GN_EPS = 64e-5

HY_COLS = 3 * HY_CH
GQA_COLS = (GQA_HEADS + 2 * GQA_KV_HEADS) * HEAD_DIM
MLA_COLS = MLA_Q_RANK + MLA_KV_RANK + MLA_ROPE
MLA_COLS_PAD = 512
RW_COLS = 3 * RW_C + 2 * RW_DECAY_LORA + 2 * RW_AAA_LORA + RW_GATE_LORA
D_IN_PAD = HY_COLS + GQA_COLS + MLA_COLS_PAD + RW_COLS

LANES = 128
ROW_TILE = 512
Q_TILE = 256
ATTN_HEADS = 4
KEY_CHUNK = 256
VT_ROWS = 2 * HEAD_DIM
RW_CHUNK = 64
RW_BASE = 4
RW_LEVELS = 4
RW_BLOCK = 128
RW_SEQS = 4
VMEM_LIMIT = 56 * 2 ** 20


def _cparams(sem):
    return pltpu.CompilerParams(dimension_semantics=sem, vmem_limit_bytes=VMEM_LIMIT)


def _const_spec(shape):
    nd = len(shape)
    return pl.BlockSpec(shape, lambda *_: (0,) * nd, pipeline_mode=pl.Buffered(1))


def _dot(a, b, precision=None):
    return jnp.dot(a, b, preferred_element_type=F32, precision=precision)


def _dot_nt(a, b, precision=None):
    return lax.dot_general(a, b, (((1,), (1,)), ((), ())), preferred_element_type=F32,
                           precision=precision)


def _dot_tn(a, b, precision=None):
    return lax.dot_general(a, b, (((0,), (0,)), ((), ())), preferred_element_type=F32,
                           precision=precision)


def _pieces(x, n):
    out = []
    for _ in range(n):
        p = x.astype(BF16)
        out.append(p)
        x = x - p.astype(F32)
    return out


def _segsum(x, seg):
    return functools.reduce(jnp.add, [_dot(p, seg) for p in _pieces(x, 2)])


def _sigmoid(x):
    return 1.0 / (1.0 + jnp.exp(-x))


def _adaln(x, g, mod_ref, i):
    shift = mod_ref[pl.ds(3 * i, 1), :]
    scale = mod_ref[pl.ds(3 * i + 1, 1), :]
    r = lax.rsqrt(jnp.mean(x * x, axis=-1, keepdims=True) + NORM_EPS)
    return (x * r) * (g * (1.0 + scale)) + shift


def _mod_kernel(c_ref, w_ref, b_ref, o_ref):
    c = c_ref[...]
    s = c * _sigmoid(c)
    o_ref[...] = _dot(s, w_ref[...], HI) + b_ref[...]


def _compute_mod(c_all, ada_w, ada_b):
    depth, d, n = ada_w.shape
    rows = c_all.shape[0]
    tn = 1024
    return pl.pallas_call(
        _mod_kernel,
        grid=(depth, n // tn),
        in_specs=[
            pl.BlockSpec((rows, d), lambda l, j: (0, 0)),
            pl.BlockSpec((None, d, tn), lambda l, j: (l, 0, j)),
            pl.BlockSpec((None, 1, tn), lambda l, j: (l, 0, j)),
        ],
        out_specs=pl.BlockSpec((None, rows, tn), lambda l, j: (l, 0, j)),
        out_shape=jax.ShapeDtypeStruct((depth, rows, n), F32),
        compiler_params=_cparams(("arbitrary", "arbitrary")),
        name="mod",
    )(c_all, ada_w, ada_b.reshape(depth, 1, n))


def _ffn_kernel(x_ref, mod_ref, g_ref, wg_ref, wu_ref, wd_ref, o_ref, *, sub):
    x = x_ref[...]
    h = _adaln(x, g_ref[...], mod_ref, sub).astype(BF16)
    a = _dot(h, wg_ref[...])
    u = _dot(h, wu_ref[...])
    z = (a * _sigmoid(a) * u).astype(BF16)
    y = _dot(z, wd_ref[...])
    gate = mod_ref[pl.ds(3 * sub + 2, 1), :]
    o_ref[...] = x + (0.5 * gate) * y


def _ffn(x, mod, g, wg, wu, wd, sub):
    bm, rows, d = x.shape
    tm = min(ROW_TILE, rows)
    f = wg.shape[1]
    return pl.pallas_call(
        functools.partial(_ffn_kernel, sub=sub),
        grid=(bm, rows // tm),
        in_specs=[
            pl.BlockSpec((None, tm, d), lambda b, i: (b, i, 0)),
            pl.BlockSpec((None, N_MOD, d), lambda b, i: (b, 0, 0)),
            _const_spec((1, d)),
            _const_spec((d, f)),
            _const_spec((d, f)),
            _const_spec((f, d)),
        ],
        out_specs=pl.BlockSpec((None, tm, d), lambda b, i: (b, i, 0)),
        out_shape=jax.ShapeDtypeStruct(x.shape, F32),
        compiler_params=_cparams(("arbitrary", "arbitrary")),
        name="ffn",
    )(x, mod, g, wg, wu, wd)


_IN_OFFS = (0, HY_COLS, HY_COLS + GQA_COLS, HY_COLS + GQA_COLS + MLA_COLS_PAD, D_IN_PAD)


def _inproj_kernel(x_ref, mod_ref, g_ref, w_ref, hy_ref, gq_ref, ml_ref, rw_ref):
    h = _adaln(x_ref[...], g_ref[...], mod_ref, 1).astype(BF16)
    p = _dot(h, w_ref[...])
    for ref, lo, hi in zip((hy_ref, gq_ref, ml_ref, rw_ref), _IN_OFFS[:-1], _IN_OFFS[1:]):
        ref[...] = p[:, lo:hi]


def _inproj(x, mod, g, w):
    bm, rows, d = x.shape
    tm = min(ROW_TILE, rows)
    widths = [hi - lo for lo, hi in zip(_IN_OFFS[:-1], _IN_OFFS[1:])]
    return pl.pallas_call(
        _inproj_kernel,
        grid=(bm, rows // tm),
        in_specs=[
            pl.BlockSpec((None, tm, d), lambda b, i: (b, i, 0)),
            pl.BlockSpec((None, N_MOD, d), lambda b, i: (b, 0, 0)),
            _const_spec((1, d)),
            _const_spec((d, D_IN_PAD)),
        ],
        out_specs=[pl.BlockSpec((None, tm, wd), lambda b, i: (b, i, 0)) for wd in widths],
        out_shape=[jax.ShapeDtypeStruct((bm, rows, wd), F32) for wd in widths],
        compiler_params=_cparams(("arbitrary", "arbitrary")),
        name="inproj",
    )(x, mod, g, w)


def _mix_ffn_kernel(x_ref, mod_ref, yh_ref, yg_ref, ym_ref, yf_ref, yb_ref, bonus_ref, gate_ref,
                    lnw_ref, lnb_ref, seg_ref, wo_ref, g_ref, wg_ref, wu_ref, wd_ref, o_ref):
    y = yf_ref[...] + yb_ref[...]
    seg = seg_ref[...]
    yc = y - _segsum(y, seg)
    var = _segsum(yc * yc, seg)
    yn = yc * lax.rsqrt(var + RW_GN_EPS) * lnw_ref[...] + lnb_ref[...]
    y_rw = ((yn + bonus_ref[...]) * gate_ref[...]).astype(BF16)
    ycat = jnp.concatenate([yh_ref[...], yg_ref[...], ym_ref[...], y_rw], axis=-1)
    x = x_ref[...] + mod_ref[pl.ds(5, 1), :] * _dot(ycat, wo_ref[...])
    h = _adaln(x, g_ref[...], mod_ref, 2).astype(BF16)
    a = _dot(h, wg_ref[...])
    u = _dot(h, wu_ref[...])
    z = (a * _sigmoid(a) * u).astype(BF16)
    o_ref[...] = x + (0.5 * mod_ref[pl.ds(8, 1), :]) * _dot(z, wd_ref[...])


def _mix_ffn(x, mod, ys, rw, ln_w, ln_b, wo, g, wg, wu, wd):
    bm, rows, d = x.shape
    tm = min(ROW_TILE, rows)
    c = RW_C
    seg = jnp.asarray(_seg_matrix(c, RW_N, 1.0 / RW_N), BF16)
    row_spec = pl.BlockSpec((None, tm, d), lambda b, i: (b, i, 0))
    y_spec = pl.BlockSpec((None, tm, c), lambda b, i: (b, i, 0))
    consts = (ln_w, ln_b, seg, wo, g, wg, wu, wd)
    return pl.pallas_call(
        _mix_ffn_kernel,
        grid=(bm, rows // tm),
        in_specs=[row_spec, pl.BlockSpec((None, N_MOD, d), lambda b, i: (b, 0, 0))] + [y_spec] * 7
                 + [_const_spec(a.shape) for a in consts],
        out_specs=row_spec,
        out_shape=jax.ShapeDtypeStruct(x.shape, F32),
        compiler_params=_cparams(("arbitrary", "arbitrary")),
        name="mix_ffn",
    )(x, mod, *ys, *rw, *consts)


def _hy_features(length):
    t01 = np.linspace(0.0, 1.0, length, dtype=np.float32)[:, None]
    bands = (HY_EMB - 1) // 2
    w_ang = (np.float32(2.0 * math.pi) * np.arange(length, dtype=np.float32)[:, None]
             / np.float32(length)).astype(np.float32)
    f = np.linspace(1e-4, bands - 1, bands, dtype=np.float32)[None]
    arg = (f * w_ang).astype(np.float32)
    z = np.concatenate([t01, np.cos(arg), -np.sin(arg)], axis=-1).astype(np.float32)
    zp = np.zeros((length, LANES), np.float32)
    zp[:, :HY_EMB] = z
    return zp


def _hy_deltas():
    max_decay = math.log(HY_TARGET) / HY_FAST_PCT
    min_decay = math.log(HY_TARGET) / HY_SLOW_PCT
    d = np.abs(np.linspace(min_decay, max_decay, HY_CH, dtype=np.float32))
    return np.tile(d, 2)[None].astype(np.float32)


def _hyfilt_kernel(z_ref, w1_ref, b1_ref, w2_ref, b2_ref, w3_ref, b3_ref, w4_ref, fr_ref,
                   dl_ref, o_ref):
    z = z_ref[...]
    fr = fr_ref[...]
    h = jnp.sin(fr * (_dot(z, w1_ref[...], HI) + b1_ref[...]))
    h = jnp.sin(fr * (_dot(h, w2_ref[...], HI) + b2_ref[...]))
    h = jnp.sin(fr * (_dot(h, w3_ref[...], HI) + b3_ref[...]))
    h = _dot(h, w4_ref[...], HI)
    o_ref[...] = h * jnp.exp(-z[:, 0:1] * dl_ref[...])


def _hy_filters(length, w1p, b1, w2, b2, w3, b3, w4, freq):
    z = jnp.asarray(_hy_features(length))
    dl = jnp.asarray(_hy_deltas())
    tl = min(ROW_TILE, length)
    consts = (w1p, b1, w2, b2, w3, b3, w4, freq, dl)
    return pl.pallas_call(
        _hyfilt_kernel,
        grid=(length // tl,),
        in_specs=[pl.BlockSpec((tl, LANES), lambda i: (i, 0))] + [_const_spec(a.shape) for a in consts],
        out_specs=pl.BlockSpec((tl, 2 * HY_CH), lambda i: (i, 0)),
        out_shape=jax.ShapeDtypeStruct((length, 2 * HY_CH), F32),
        compiler_params=_cparams(("arbitrary",)),
        name="hyena_filters",
    )(z, *consts)


FFT_SLAB_PAD = 8
FFT_GROUP = 16
FFT_SLAB_GROUP = 12


def _fft_factors(length):
    return (128, 64) if length >= 2048 else (2 * length, 1)


def _fft_layout(n1, n2):
    if n2 == 1:
        return 1, n1, 2 * n1
    kept = n1 // 2 + 1
    return -(-kept // FFT_SLAB_GROUP) * FFT_SLAB_GROUP, n2, 2 * n2 + FFT_SLAB_PAD


def _stack3(t):
    hi = t.astype(jnp.bfloat16)
    lo = (t - hi.astype(np.float64)).astype(jnp.bfloat16)
    return np.concatenate([hi, lo, hi], axis=-1)


def _rows3(d):
    hi = d.astype(BF16)
    lo = (d - hi.astype(F32)).astype(BF16)
    return jnp.concatenate([hi, hi, lo], axis=0)


@functools.lru_cache(maxsize=None)
def _fft_tables(length):
    n1, n2 = _fft_factors(length)
    n = n1 * n2
    assert n == 2 * length
    a_n2 = np.arange(n2)[:, None, None]
    a_k1 = np.arange(n1)[None, :, None]
    a_n1 = np.arange(n1 // 2)[None, None, :]
    ang = 2.0 * np.pi * ((a_n1 * a_k1 % n1) / n1 + (a_n2 * a_k1 % n) / n)
    t_re, t_im = np.cos(ang), -np.sin(ang)
    wgt = np.ones(n1)
    if n2 > 1:
        nk = _fft_layout(n1, n2)[0]
        keep = np.arange(nk) <= n1 // 2
        t_re, t_im = (np.where(keep[None, :, None], t[:, :nk], 0.0) for t in (t_re, t_im))
        wgt = np.where(np.isin(np.arange(nk), (0, n1 // 2)), 1.0, 2.0) * keep
    ta = np.concatenate([t_re, t_im], axis=1)
    t_re_w, t_im_w = t_re * wgt[None, :, None], t_im * wgt[None, :, None]
    tai = np.concatenate([np.transpose(t_re_w, (0, 2, 1)), np.transpose(t_im_w, (0, 2, 1))], axis=2) / n
    jj = np.arange(n2)
    ang2 = 2.0 * np.pi * (np.outer(jj, jj) % n2) / n2
    c, s = np.cos(ang2), np.sin(ang2)
    fb = np.block([[c, s], [-s, c]])
    fbi = np.block([[c, -s], [s, c]])
    return tuple(_stack3(t) for t in (ta, tai, fb, fbi))


def _fft_stage_a(src_ref, fbuf_ref, ta_ref, n1, n2):
    if n2 == 1:
        fbuf_ref[...] = _dot(ta_ref[0], _rows3(src_ref[...]))
        return
    nk, _, pitch = _fft_layout(n1, n2)

    def body(i, carry):
        js = [i * FFT_GROUP + g for g in range(FFT_GROUP)]
        rows = [_rows3(src_ref[pl.ds(j, n1 // 2, stride=n2), :]) for j in js]
        ts = [_dot(ta_ref[j], r) for j, r in zip(js, rows)]
        for j, t in zip(js, ts):
            fbuf_ref[pl.ds(j, nk, stride=pitch), :] = t[:nk]
            fbuf_ref[pl.ds(n2 + j, nk, stride=pitch), :] = t[nk:]
        return carry

    lax.fori_loop(0, n2 // FFT_GROUP, body, 0)


def _spec_kernel(hf_ref, hb_ref, bias_ref, ta_ref, fb_ref, o_ref, fbuf_ref, *, n1, n2):
    nslab, hs, pitch = _fft_layout(n1, n2)
    if pitch > 2 * hs:
        o_ref[...] = jnp.zeros(o_ref.shape, F32)
    _fft_stage_a(hf_ref, o_ref, ta_ref, n1, n2)
    _fft_stage_a(hb_ref, fbuf_ref, ta_ref, n1, n2)
    is_re = lax.broadcasted_iota(jnp.int32, (2 * hs, 1), 0) < hs
    sgn = jnp.where(is_re, 1.0, -1.0).astype(F32)
    skip = jnp.where(is_re, bias_ref[...], 0.0)

    grp = FFT_SLAB_GROUP if n2 > 1 else 1

    def stage_b(i, carry):
        sls = [pl.ds(pl.multiple_of((i * grp + g) * pitch, 8), 2 * hs) for g in range(grp)]
        a = [o_ref[sl, :] for sl in sls]
        b = [fbuf_ref[sl, :] for sl in sls]
        if n2 > 1:
            a = [_dot(fb_ref[...], _rows3(x)) for x in a]
            b = [_dot(fb_ref[...], _rows3(x)) for x in b]
        for sl, x, y in zip(sls, a, b):
            o_ref[sl, :] = x + sgn * y + skip
        return carry

    lax.fori_loop(0, nslab // grp, stage_b, 0)


def _hy_spectrum(h, bias):
    length = h.shape[0]
    n1, n2 = _fft_factors(length)
    nslab, _, pitch = _fft_layout(n1, n2)
    ta, _, fb, _ = (jnp.asarray(t) for t in _fft_tables(length))
    nblk = HY_CH // LANES
    return pl.pallas_call(
        functools.partial(_spec_kernel, n1=n1, n2=n2),
        grid=(nblk,),
        in_specs=[pl.BlockSpec((length, LANES), lambda j: (0, j)),
                  pl.BlockSpec((length, LANES), lambda j: (0, nblk + j)),
                  pl.BlockSpec((1, LANES), lambda j: (0, j)),
                  _const_spec(ta.shape), _const_spec(fb.shape)],
        out_specs=pl.BlockSpec((nslab * pitch, LANES), lambda j: (0, j)),
        out_shape=jax.ShapeDtypeStruct((nslab * pitch, HY_CH), F32),
        scratch_shapes=[pltpu.VMEM((nslab * pitch, LANES), F32)],
        compiler_params=_cparams(("arbitrary",)),
        name="hyena_spectrum",
    )(h, h, bias, ta, fb)


def _shift_rows(x, prev_row, next_row):
    n = x.shape[0]
    row = lax.broadcasted_iota(jnp.int32, x.shape, 0)
    up = jnp.where(row == 0, prev_row, pltpu.roll(x, 1, 0))
    dn = jnp.where(row == n - 1, next_row, pltpu.roll(x, n - 1, 0))
    return up, dn


def _hyconv_kernel(x1_ref, x2_ref, v_ref, cw1_ref, cw2_ref, cwv_ref, cb_ref, g_ref,
                   ta_ref, tai_ref, fb_ref, fbi_ref, o_ref, u_ref, fbuf_ref, *, n1, n2):
    def conv3(ref, w_ref, b):
        x = ref[...]
        up, dn = _shift_rows(x, 0.0, 0.0)
        return up * w_ref[0:1, :] + x * w_ref[1:2, :] + dn * w_ref[2:3, :] + b

    cb = cb_ref[...]
    u_ref[...] = conv3(x1_ref, cw1_ref, cb[0:1, :]) * conv3(v_ref, cwv_ref, cb[2:3, :])

    _fft_stage_a(u_ref, fbuf_ref, ta_ref, n1, n2)
    nslab, hs, pitch = _fft_layout(n1, n2)

    grp = FFT_SLAB_GROUP if n2 > 1 else 1

    def cmul(a, g):
        ar, ai, gr, gi = a[:hs], a[hs:], g[:hs], g[hs:]
        return jnp.concatenate([ar * gr - ai * gi, ar * gi + ai * gr], axis=0)

    def stage_b(i, carry):
        sls = [pl.ds(pl.multiple_of((i * grp + g) * pitch, 8), 2 * hs) for g in range(grp)]
        a = [fbuf_ref[sl, :] for sl in sls]
        if n2 > 1:
            a = [_dot(fb_ref[...], _rows3(x)) for x in a]
        a = [cmul(x, g_ref[sl, :]) for x, sl in zip(a, sls)]
        if n2 > 1:
            a = [_dot(fbi_ref[...], _rows3(x)) for x in a]
        for sl, x in zip(sls, a):
            fbuf_ref[sl, :] = x
        return carry

    lax.fori_loop(0, nslab // grp, stage_b, 0)

    if n2 == 1:
        u_ref[...] = _dot(tai_ref[0], _rows3(fbuf_ref[...]))
    else:
        def stage_a_inv(i, carry):
            js = [i * FFT_GROUP + g for g in range(FFT_GROUP)]
            cs = [_rows3(jnp.concatenate([fbuf_ref[pl.ds(j, nslab, stride=pitch), :],
                                          fbuf_ref[pl.ds(n2 + j, nslab, stride=pitch), :]], axis=0))
                  for j in js]
            ys = [_dot(tai_ref[j], c) for j, c in zip(js, cs)]
            for j, y in zip(js, ys):
                u_ref[pl.ds(j, n1 // 2, stride=n2), :] = y
            return carry

        lax.fori_loop(0, n2 // FFT_GROUP, stage_a_inv, 0)

    o_ref[...] = (conv3(x2_ref, cw2_ref, cb[1:2, :]) * u_ref[...]).astype(o_ref.dtype)


def _hy_conv(p, conv_w, conv_b, spec):
    b, length, _ = p.shape
    n1, n2 = _fft_factors(length)
    nslab, _, pitch = _fft_layout(n1, n2)
    ta, tai, fb, fbi = (jnp.asarray(t) for t in _fft_tables(length))
    nblk = HY_CH // LANES
    cb3 = conv_b.reshape(3, HY_CH)
    col = lambda g: pl.BlockSpec((None, length, LANES), lambda j, i, g=g: (i, 0, g * nblk + j))
    wcol = lambda g: pl.BlockSpec((3, LANES), lambda j, i, g=g: (0, g * nblk + j))
    return pl.pallas_call(
        functools.partial(_hyconv_kernel, n1=n1, n2=n2),
        grid=(nblk, b),
        in_specs=[col(0), col(1), col(2), wcol(0), wcol(1), wcol(2),
                  pl.BlockSpec((3, LANES), lambda j, i: (0, j)),
                  pl.BlockSpec((nslab * pitch, LANES), lambda j, i: (0, j), pipeline_mode=pl.Buffered(1)),
                  _const_spec(ta.shape), _const_spec(tai.shape), _const_spec(fb.shape),
                  _const_spec(fbi.shape)],
        out_specs=pl.BlockSpec((None, length, LANES), lambda j, i: (i, 0, j)),
        out_shape=jax.ShapeDtypeStruct((b, length, HY_CH), BF16),
        scratch_shapes=[pltpu.VMEM((length, LANES), F32), pltpu.VMEM((nslab * pitch, LANES), F32)],
        compiler_params=_cparams(("arbitrary", "arbitrary")),
        name="hyena_conv",
    )(p, p, p, conv_w, conv_w, conv_w, cb3, spec, ta, tai, fb, fbi)


def _rope_tables(length, d_rot, lane_lo, head_w):
    rows = length // GRID_W
    row = np.repeat(np.arange(rows, dtype=np.float32), GRID_W)
    colv = np.tile(np.arange(GRID_W, dtype=np.float32), rows)
    n_freq = d_rot // 4
    inv = (np.float32(ROPE_THETA) ** (-np.arange(n_freq, dtype=np.float32) / np.float32(n_freq))).astype(np.float32)
    ang = np.concatenate([row[:, None] * inv, colv[:, None] * inv], axis=-1).astype(np.float32)
    cos_t = np.ones((length, LANES), np.float32)
    sin_t = np.zeros((length, LANES), np.float32)
    c, s = np.cos(ang), np.sin(ang)
    for base in range(0, LANES, head_w):
        for i in range(d_rot // 2):
            cos_t[:, base + lane_lo + 2 * i] = c[:, i]
            cos_t[:, base + lane_lo + 2 * i + 1] = c[:, i]
            sin_t[:, base + lane_lo + 2 * i] = -s[:, i]
            sin_t[:, base + lane_lo + 2 * i + 1] = s[:, i]
    return cos_t, sin_t


def _rope_angles(length, d_rot):
    rows = length // GRID_W
    row = np.repeat(np.arange(rows, dtype=np.float32), GRID_W)
    colv = np.tile(np.arange(GRID_W, dtype=np.float32), rows)
    n_freq = d_rot // 4
    inv = (np.float32(ROPE_THETA) ** (-np.arange(n_freq, dtype=np.float32) / np.float32(n_freq))).astype(np.float32)
    return np.concatenate([row[:, None] * inv, colv[:, None] * inv], axis=-1).astype(np.float32)


def _mla_lanes():
    lanes = np.empty(MLA_QK, np.int64)
    n = np.arange(MLA_NOPE)
    lanes[:MLA_NOPE] = np.where(n < 32, n, n + 16)
    i = np.arange(MLA_ROPE // 2)
    lanes[MLA_NOPE + 2 * i] = 32 + i
    lanes[MLA_NOPE + 2 * i + 1] = 96 + i
    return lanes


def _mla_rope_tables(length):
    ang = _rope_angles(length, MLA_ROPE)
    cos_t = np.ones((length, LANES), np.float32)
    sin_t = np.zeros((length, LANES), np.float32)
    n_pairs = MLA_ROPE // 2
    cos_t[:, 32:32 + n_pairs] = cos_t[:, 96:96 + n_pairs] = np.cos(ang)
    sin_t[:, 32:32 + n_pairs] = -np.sin(ang)
    sin_t[:, 96:96 + n_pairs] = np.sin(ang)
    return cos_t, sin_t


def _rope(x, cos_t, sin_t):
    lane = lax.broadcasted_iota(jnp.int32, x.shape, 1)
    w = x.shape[1]
    partner = jnp.where(jnp.bitwise_and(lane, 1) == 0, pltpu.roll(x, w - 1, 1), pltpu.roll(x, 1, 1))
    return x * cos_t + partner * sin_t


def _store_vt(vt_ref, h, vt):
    row = lax.broadcasted_iota(jnp.int32, (VT_ROWS - HEAD_DIM, vt.shape[1]), 0)
    vt_ref[h, 0:HEAD_DIM, :] = vt.astype(BF16)
    vt_ref[h, HEAD_DIM:VT_ROWS, :] = jnp.where(row == 0, 1.0, 0.0).astype(BF16)


def _gqa_prep_kernel(p_ref, cos_ref, sin_ref, gq_ref, gk_ref, seg_ref, q_ref, k_ref, vt_ref, *, rope):
    p = p_ref[...]
    seg = seg_ref[...]
    lane = lax.broadcasted_iota(jnp.int32, (p.shape[0], LANES), 1)
    low = lane < HEAD_DIM

    def hnorm(x, g):
        ms = _segsum(x * x, seg)
        return x * lax.rsqrt(ms + NORM_EPS) * g

    scale = HEAD_DIM ** -0.5 * LOG2E
    for c in range(2):
        q = hnorm(p[:, c * LANES:(c + 1) * LANES], gq_ref[...])
        if rope:
            q = _rope(q, cos_ref[...], sin_ref[...])
        q = q * scale
        qs = pltpu.roll(q, HEAD_DIM, 1)
        q_ref[:, (2 * c) * LANES:(2 * c + 1) * LANES] = jnp.where(low, q, 0.0).astype(BF16)
        q_ref[:, (2 * c + 1) * LANES:(2 * c + 2) * LANES] = jnp.where(low, qs, 0.0).astype(BF16)
    k = hnorm(p[:, 256:384], gk_ref[...])
    if rope:
        k = _rope(k, cos_ref[...], sin_ref[...])
    ks = pltpu.roll(k, HEAD_DIM, 1)
    k_ref[:, 0:LANES] = jnp.where(low, k, 0.0).astype(BF16)
    k_ref[:, LANES:2 * LANES] = jnp.where(low, ks, 0.0).astype(BF16)
    vt = p[:, 384:512].T
    _store_vt(vt_ref, 0, vt[0:HEAD_DIM])
    _store_vt(vt_ref, 1, vt[HEAD_DIM:2 * HEAD_DIM])


def _seg_matrix(width, seg, value):
    i = np.arange(width)
    return ((i[:, None] // seg) == (i[None, :] // seg)).astype(np.float32) * np.float32(value)


def _gqa_prep(p, gq, gk, rope):
    b, length, _ = p.shape
    tm = min(ROW_TILE, length)
    if rope:
        cos_t, sin_t = (jnp.asarray(t) for t in _rope_tables(length, HEAD_DIM, 0, HEAD_DIM))
    else:
        cos_t = sin_t = jnp.zeros((length, LANES), F32)
    seg = jnp.asarray(_seg_matrix(LANES, HEAD_DIM, 1.0 / HEAD_DIM), BF16)
    tab = pl.BlockSpec((tm, LANES), lambda i, bb: (i, 0))
    outw = (512, 256)
    return pl.pallas_call(
        functools.partial(_gqa_prep_kernel, rope=rope),
        grid=(length // tm, b),
        in_specs=[pl.BlockSpec((None, tm, GQA_COLS), lambda i, bb: (bb, i, 0)), tab, tab,
                  _const_spec((1, LANES)), _const_spec((1, LANES)), _const_spec((LANES, LANES))],
        out_specs=[pl.BlockSpec((None, tm, w), lambda i, bb: (bb, i, 0)) for w in outw]
                  + [pl.BlockSpec((None, GQA_KV_HEADS, VT_ROWS, tm), lambda i, bb: (bb, 0, 0, i))],
        out_shape=[jax.ShapeDtypeStruct((b, length, w), BF16) for w in outw]
                  + [jax.ShapeDtypeStruct((b, GQA_KV_HEADS, VT_ROWS, length), BF16)],
        compiler_params=_cparams(("arbitrary", "arbitrary")),
        name="gqa_prep",
    )(p, cos_t, sin_t, gq, gk, seg)


def _mla_prep_kernel(p_ref, cos_ref, sin_ref, cqn_ref, ckvn_ref, wuq_ref, wuk_ref, wuv_ref,
                     qn_ref, kn_ref, ones_ref, q_ref, k_ref, vt_ref, *, rope):
    p = p_ref[...]

    def rms(x, g, width):
        ms = _segsum(x * x, ones_ref[0:x.shape[1], 0:x.shape[1]]) * (1.0 / width)
        return x * lax.rsqrt(ms + NORM_EPS) * g

    cq = rms(p[:, 0:MLA_Q_RANK], cqn_ref[...], MLA_Q_RANK).astype(BF16)
    ckv = rms(p[:, MLA_Q_RANK:MLA_Q_RANK + MLA_KV_RANK], ckvn_ref[...], MLA_KV_RANK).astype(BF16)
    q = _dot(cq, wuq_ref[...])
    kn = _dot(ckv, wuk_ref[...])
    v = _dot(ckv, wuv_ref[...])
    for c in range(MLA_HEADS // 2):
        vt = v[:, c * LANES:(c + 1) * LANES].T
        _store_vt(vt_ref, 2 * c, vt[0:MLA_V])
        _store_vt(vt_ref, 2 * c + 1, vt[MLA_V:2 * MLA_V])
    kr = p[:, 384:512]
    scale = MLA_QK ** -0.5 * LOG2E

    def rot(x):
        return x * cos_ref[...] + pltpu.roll(x, LANES // 2, 1) * sin_ref[...]

    for h in range(MLA_HEADS):
        sl = slice(h * LANES, (h + 1) * LANES)
        qh = rms(q[:, sl], qn_ref[...], MLA_QK)
        kh = rms(kn[:, sl] + kr, kn_ref[...], MLA_QK)
        if rope:
            qh, kh = rot(qh), rot(kh)
        q_ref[:, sl] = (qh * scale).astype(BF16)
        k_ref[:, sl] = kh.astype(BF16)


def _mla_prep(p, cqn, ckvn, wuq, wuk, wuv, qn, kn, rope):
    b, length, _ = p.shape
    tm = min(ROW_TILE, length)
    if rope:
        cos_t, sin_t = (jnp.asarray(t) for t in _mla_rope_tables(length))
    else:
        cos_t = sin_t = jnp.zeros((length, LANES), F32)
    tab = pl.BlockSpec((tm, LANES), lambda i, bb: (i, 0))
    ones = jnp.ones((MLA_Q_RANK, MLA_Q_RANK), BF16)
    consts = (cqn, ckvn, wuq, wuk, wuv, qn, kn, ones)
    outw = (512, 512)
    return pl.pallas_call(
        functools.partial(_mla_prep_kernel, rope=rope),
        grid=(length // tm, b),
        in_specs=[pl.BlockSpec((None, tm, MLA_COLS_PAD), lambda i, bb: (bb, i, 0)), tab, tab]
                 + [_const_spec(a.shape) for a in consts],
        out_specs=[pl.BlockSpec((None, tm, w), lambda i, bb: (bb, i, 0)) for w in outw]
                  + [pl.BlockSpec((None, MLA_HEADS, VT_ROWS, tm), lambda i, bb: (bb, 0, 0, i))],
        out_shape=[jax.ShapeDtypeStruct((b, length, w), BF16) for w in outw]
                  + [jax.ShapeDtypeStruct((b, MLA_HEADS, VT_ROWS, length), BF16)],
        compiler_params=_cparams(("arbitrary", "arbitrary")),
        name="mla_prep",
    )(p, cos_t, sin_t, *consts)


def _attn_kernel(*refs, n_seg):
    nh = ATTN_HEADS
    q_ref, o_ref = refs[0], refs[-1]
    segs = [refs[1 + 2 * nh * s:1 + 2 * nh * (s + 1)] for s in range(n_seg)]
    q = q_ref[...]
    qh = [q[:, h * LANES:(h + 1) * LANES] for h in range(nh)]
    pieces = [(h, seg, c0, min(KEY_CHUNK, seg[0].shape[0]))
              for h in range(nh) for seg in segs
              for c0 in range(0, seg[0].shape[0], min(KEY_CHUNK, seg[0].shape[0]))]
    scores = [_dot_nt(seg[h][c0:c0 + n, :], qh[h]) for h, seg, c0, n in pieces]
    mx = [jnp.max(s, axis=0, keepdims=True) for s in scores]
    ps = [jnp.exp2(s - m).astype(BF16) for s, m in zip(scores, mx)]
    os_ = [_dot(seg[nh + h][:, c0:c0 + n], e) for (h, seg, c0, n), e in zip(pieces, ps)]
    outs = []
    for h in range(nh):
        idx = [i for i, pc in enumerate(pieces) if pc[0] == h]
        m = functools.reduce(jnp.maximum, [mx[i] for i in idx])
        o = functools.reduce(jnp.add, [os_[i] * jnp.exp2(mx[i] - m) for i in idx])
        outs.append(o[0:HEAD_DIM] / o[HEAD_DIM:HEAD_DIM + 1])
    o_ref[...] = jnp.concatenate(outs, axis=0).T.astype(o_ref.dtype)


def _attention(q, kv_segs, k_heads):
    b, lq, _ = q.shape
    tq = min(Q_TILE, lq)
    nh = ATTN_HEADS
    kv_of = [h * k_heads // nh for h in range(nh)]
    in_specs = [pl.BlockSpec((None, tq, nh * LANES), lambda bb, i: (bb, i, 0))]
    args = [q]
    for k, vt in kv_segs:
        lk = k.shape[1]
        in_specs += [pl.BlockSpec((None, lk, LANES), lambda bb, i, j=j: (bb, 0, j)) for j in kv_of]
        in_specs += [pl.BlockSpec((None, None, VT_ROWS, lk), lambda bb, i, j=j: (bb, j, 0, 0)) for j in kv_of]
        args += [k] * nh + [vt] * nh
    return pl.pallas_call(
        functools.partial(_attn_kernel, n_seg=len(kv_segs)),
        grid=(b, lq // tq),
        in_specs=in_specs,
        out_specs=pl.BlockSpec((None, tq, nh * HEAD_DIM), lambda bb, i: (bb, i, 0)),
        out_shape=jax.ShapeDtypeStruct((b, lq, nh * HEAD_DIM), BF16),
        compiler_params=_cparams(("arbitrary", "arbitrary")),
        name="attention",
    )(*args)


def _rw_prep_kernel(p_ref, prev_ref, next_ref, mu_ref, kk_ref_w, ka_ref, rk_ref, w0_ref, w2_ref,
                    a0_ref, a2_ref, g2_ref, seg_ref,
                    r_ref, v_ref, kk_ref, lw_ref, kd_ref, bd_ref, bonus_ref, g_ref):
    i = pl.program_id(1)
    n = pl.num_programs(1)
    x = p_ref[...]
    prev_row = jnp.where(i > 0, prev_ref[7:8, :], 0.0)
    next_row = jnp.where(i < n - 1, next_ref[0:1, :], 0.0)
    up, dn = _shift_rows(x, prev_row, next_row)
    xs = x + mu_ref[...] * (0.5 * (up + dn) - x)
    c = RW_C
    r, k, v = xs[:, 0:c], xs[:, c:2 * c], xs[:, 2 * c:3 * c]
    xw, xa, xg = xs[:, 3 * c:3 * c + 128], xs[:, 3 * c + 128:3 * c + 256], xs[:, 3 * c + 256:3 * c + 384]
    seg = seg_ref[...]
    kk = k * kk_ref_w[...]
    nrm = jnp.sqrt(_segsum(kk * kk, seg))
    kk = kk / jnp.maximum(nrm, 1e-12)
    u = w0_ref[...] + _dot(jnp.tanh(xw).astype(BF16), w2_ref[...])
    z = -u
    softplus = jnp.maximum(z, 0.0) + jnp.log(1.0 + jnp.exp(-jnp.abs(z)))
    lw_ref[...] = -jnp.exp(-softplus - 0.5)
    a = _sigmoid(a0_ref[...] + _dot(xa.astype(BF16), a2_ref[...]))
    bonus = jnp.zeros_like(r)
    for d in range(2):
        a_d = a[:, d * c:(d + 1) * c]
        k_d = k * (1.0 + (a_d - 1.0) * ka_ref[...])
        kd_ref[:, d * c:(d + 1) * c] = k_d
        bd_ref[:, d * c:(d + 1) * c] = kk * a_d
        bonus = bonus + _segsum(r * k_d * rk_ref[...], seg)
    r_ref[...] = r
    v_ref[...] = v
    kk_ref[...] = kk
    bonus_ref[...] = bonus * v
    g_ref[...] = _dot(_sigmoid(xg).astype(BF16), g2_ref[...])


def _rw_prep(p, mu, k_k, k_a, r_k, w0, w2bd, a0, a2bd, g2):
    b, length, cols = p.shape
    tm = min(ROW_TILE, length)
    nb = tm // 8
    last = length // 8 - 1
    seg = jnp.asarray(_seg_matrix(RW_C, RW_N, 1.0), BF16)
    consts = (mu, k_k, k_a, r_k, w0, w2bd, a0, a2bd, g2, seg)
    outw = (RW_C, RW_C, RW_C, 2 * RW_C, 2 * RW_C, 2 * RW_C, RW_C, RW_C)
    return pl.pallas_call(
        _rw_prep_kernel,
        grid=(b, length // tm),
        in_specs=[pl.BlockSpec((None, tm, cols), lambda bb, i: (bb, i, 0)),
                  pl.BlockSpec((None, 8, cols), lambda bb, i: (bb, jnp.maximum(i * nb - 1, 0), 0)),
                  pl.BlockSpec((None, 8, cols), lambda bb, i: (bb, jnp.minimum((i + 1) * nb, last), 0))]
                 + [_const_spec(a.shape) for a in consts],
        out_specs=[pl.BlockSpec((None, tm, w), lambda bb, i: (bb, i, 0)) for w in outw],
        out_shape=[jax.ShapeDtypeStruct((b, length, w), F32) for w in outw],
        compiler_params=_cparams(("arbitrary", "arbitrary")),
        name="rwkv_prep",
    )(p, p, p, *consts)


@functools.lru_cache(maxsize=None)
def _rw_masks():
    cs, n = RW_CHUNK, RW_C
    i = np.arange(n)
    same = (i[:, None] // cs) == (i[None, :] // cs)
    rel = (i[None, :] % cs) - (i[:, None] % cs)
    masks = np.stack([same & (rel < 0), same & (rel <= 0), same & (rel > 0), same & (rel >= 0),
                      same, i[:, None] == i[None, :]]).astype(np.float32)
    t = np.arange(cs)
    tri = np.stack([t[None, :] <= t[:, None], t[None, :] >= t[:, None]]).astype(np.float32)
    tr, tc = (i % cs)[:, None], (i % cs)[None, :]
    lvl = []
    for sgn in (1, -1):
        earlier = (tc - tr) * sgn < 0
        rows = [same & earlier & (tr // RW_BASE == tc // RW_BASE)]
        for k in range(RW_LEVELS):
            s = RW_BASE << k
            rows.append(same & earlier & (tr // (2 * s) == tc // (2 * s)) & (tr // s != tc // s))
        lvl.append(np.stack(rows))
    return masks, tri, np.stack(lvl).astype(np.float32)


def _rw_scan_kernel(rf_ref, rb_ref, vf_ref, vb_ref, kkf_ref, kkb_ref, lwf_ref, lwb_ref, kdf_ref, kdb_ref,
                    bf_ref, bb_ref, s0_ref, mask_ref, tri_ref, lvl_ref, yf_ref, yb_ref, sfin_ref, s_ref,
                    *, n_chunks, n_batch):
    cs = RW_CHUNK

    @pl.when(pl.program_id(1) == 0)
    def _():
        s_ref[...] = s0_ref[...]

    same_head = mask_ref[4].astype(BF16)
    eye = mask_ref[5]
    dirs = ((rf_ref, vf_ref, kkf_ref, lwf_ref, kdf_ref, bf_ref, yf_ref),
            (rb_ref, vb_ref, kkb_ref, lwb_ref, kdb_ref, bb_ref, yb_ref))

    def tile(x):
        xb = x.astype(BF16)
        return jnp.concatenate([xb, xb, xb, xb], axis=0)

    def bd(x):
        return tile(x) * same_head

    def halves(t, size, d):
        lo = [t[g:g + size] for g in range(0, t.shape[0], 2 * size)]
        hi = [t[g + size:g + 2 * size] for g in range(0, t.shape[0], 2 * size)]
        return (lo, hi) if d == 0 else (hi, lo)

    def join(earlier, later, d):
        pairs = zip(earlier, later) if d == 0 else zip(later, earlier)
        return jnp.concatenate([x for pr in pairs for x in pr], axis=0)

    def spread(a, half):
        swapped = pltpu.roll(a, cs, 1)
        low = lax.broadcasted_iota(jnp.int32, a.shape, 1) < cs
        h = jnp.where(low, swapped, a) if half else jnp.where(low, a, swapped)
        return jnp.concatenate([h, h], axis=1)

    streams = [(bi, d) for bi in range(n_batch) for d in range(2)]

    def each(f, *cols):
        return [f(*a) for a in zip(*cols)] if cols else [f(bi, d) for bi, d in streams]

    bf = lambda xs: [x.astype(BF16) for x in xs]

    def body(c, carry):
        def load(bi, d):
            cc = c if d == 0 else n_chunks - 1 - c
            sl = pl.ds(pl.multiple_of(cc * cs, cs), cs)
            return [ref[bi, sl, :] for ref in dirs[d][:6]] + [sl]

        r, v, kk, lw, kd, b, sl = zip(*each(load))
        strict = [mask_ref[2 * d] for _, d in streams]
        incl = [mask_ref[2 * d + 1] for _, d in streams]
        cl = [functools.reduce(jnp.add, [_dot(tri_ref[d], p) for p in _pieces(x, 3)])
              for (_, d), x in zip(streams, lw)]
        w_inv = [jnp.exp(-x) for x in cl]
        w_all = [jnp.exp(jnp.sum(x, axis=0, keepdims=True)) for x in lw]
        kt = each(lambda a, w: a * w, kd, w_inv)
        bt = each(lambda a, w: a * w, b, w_inv)
        qk_s = each(lambda a, x, y: bd(a * jnp.exp(x - y)), kk, cl, lw)
        rt_s = each(lambda a, x: bd(a * jnp.exp(x)), r, cl)
        kb = each(lambda a, t: jnp.concatenate([a.astype(BF16), t.astype(BF16)], axis=0), kt, bt)
        aq = each(_dot_nt, qk_s, kb)
        ar = each(_dot_nt, rt_s, kb)
        a_kk = each(lambda a, m: (spread(a, 0) * m).astype(BF16), aq, strict)
        a_kb = each(lambda a, m: spread(a, 1) * m, aq, strict)
        a_rk = each(lambda a, m: (spread(a, 0) * m).astype(BF16), ar, incl)
        a_rb = each(lambda a, m: (spread(a, 1) * m).astype(BF16), ar, incl)
        dmask = lambda k: [lvl_ref[d, k] for _, d in streams]
        l0 = each(lambda a, m: a * m, a_kb, dmask(0))
        p0 = [eye - a for a in l0]
        sq = each(lambda a: _dot(a, a), bf(l0))
        t_inv = each(lambda p, pb, s: p + _dot(pb, s), p0, bf(p0), bf(sq))
        for k in range(1, RW_LEVELS + 1):
            half = RW_BASE << (k - 1)
            cm = bf(each(lambda a, m: a * m, a_kb, dmask(k)))
            tb = bf(t_inv)
            if half % 8:
                tc = bf(each(_dot, tb, cm))
                t_inv = each(lambda t, x, y: t - _dot(x, y), t_inv, tc, tb)
                continue
            parts = [halves(t, half, d) for (_, d), t in zip(streams, t_inv)]
            later = bf([jnp.concatenate(p[1], axis=0) for p in parts])
            tc = bf(each(_dot, later, cm))
            upd = each(_dot, tc, tb)
            t_inv = [join(p[0], [x - u[i * half:(i + 1) * half] for i, x in enumerate(p[1])], d)
                     for (_, d), p, u in zip(streams, parts, upd)]
        t_inv = bf(t_inv)
        v_s = each(bd, v)
        akv = bf(each(_dot, a_kk, v_s))
        x1 = bf(each(_dot, t_inv, qk_s))
        x2 = each(_dot, t_inv, akv)
        bh_s = each(lambda a, w: bd(a * w), bt, w_all)
        kh_s = each(lambda a, w: bd(a * w), kt, w_all)
        y_v = each(_dot, a_rk, v_s)
        s_v = each(_dot_tn, v_s, kh_s)
        s0f = [s_ref[bi, d] for bi, d in streams]
        s0 = bf(s0f)
        u = bf(each(lambda x, s, y: _dot_nt(x, s) + y, x1, s0, x2))
        y_bd = each(lambda rt, s, yv, ab, u_: _dot_nt(rt, s) + yv - _dot(ab, u_), rt_s, s0, y_v, a_rb, u)
        s1 = each(lambda s, w, sv, u_, bh: s * w + sv - _dot_tn(u_, bh), s0f, w_all, s_v, u, bh_s)
        for (bi, d), y, s, rows in zip(streams, y_bd, s1, sl):
            dirs[d][6][bi, rows, :] = y[0:cs] + y[cs:2 * cs] + y[2 * cs:3 * cs] + y[3 * cs:4 * cs]
            s_ref[bi, d] = s
        return carry

    lax.fori_loop(0, n_chunks, body, 0)
    sfin_ref[...] = s_ref[...]


def _rw_scan(r, v, kk, lw, kd, bdir, s0):
    b, length, c = r.shape
    tc = min(RW_BLOCK, length)
    nblk = length // tc
    nb = RW_SEQS
    masks, tri, lvl = _rw_masks()
    masks, tri, lvl = jnp.asarray(masks), jnp.asarray(tri, BF16), jnp.asarray(lvl)
    fwd = lambda col: pl.BlockSpec((nb, tc, c), lambda g, i: (g, i, col))
    bwd = lambda col: pl.BlockSpec((nb, tc, c), lambda g, i: (g, nblk - 1 - i, col))
    state = pl.BlockSpec((nb, 2, c, c), lambda g, i: (g, 0, 0, 0))
    return pl.pallas_call(
        functools.partial(_rw_scan_kernel, n_chunks=tc // RW_CHUNK, n_batch=nb),
        grid=(b // nb, nblk),
        in_specs=[fwd(0), bwd(0)] * 3 + [fwd(0), bwd(1)] * 3
                 + [state, _const_spec(masks.shape), _const_spec(tri.shape), _const_spec(lvl.shape)],
        out_specs=[fwd(0), bwd(0), state],
        out_shape=[jax.ShapeDtypeStruct((b, length, c), F32), jax.ShapeDtypeStruct((b, length, c), F32),
                   jax.ShapeDtypeStruct((b, 2, c, c), F32)],
        scratch_shapes=[pltpu.VMEM((nb, 2, c, c), F32)],
        compiler_params=_cparams(("arbitrary", "arbitrary")),
        name="rwkv_scan",
    )(r, r, v, v, kk, kk, lw, lw, kd, kd, bdir, bdir, s0, masks, tri, lvl)


def _blockdiag2(w):
    k, n = w.shape[1:]
    z = jnp.zeros((k, n), w.dtype)
    return jnp.concatenate([jnp.concatenate([w[0], z], axis=1), jnp.concatenate([z, w[1]], axis=1)], axis=0)


def _mla_spread(w, lanes):
    k = w.shape[0]
    w = w.reshape(k, -1, len(lanes))
    return jnp.zeros((k, w.shape[1], LANES), w.dtype).at[:, :, lanes].set(w).reshape(k, -1)


def kernel(x, c, ctx, c_ctx, ada_w, ada_b, norm_ffn1, norm_mix, norm_ffn2, ffn1_gate, ffn1_up, ffn1_down, ffn2_gate, ffn2_up, ffn2_down, w_in, w_out, hy_conv_w, hy_conv_b, hy_f_w1, hy_f_b1, hy_f_w2, hy_f_b2, hy_f_w3, hy_f_b3, hy_f_w4, hy_f_freq, hy_bias, gqa_q_norm, gqa_k_norm, mla_cq_norm, mla_ckv_norm, mla_w_uq, mla_w_ukv, mla_q_norm, mla_k_norm, rw_mu, rw_w0, rw_w2, rw_a0, rw_a2, rw_g2, rw_k_k, rw_k_a, rw_r_k, rw_ln_w, rw_ln_b):
    b, length, d = x.shape
    lc = ctx.shape[1]
    depth = ada_w.shape[0]

    c_all = jnp.zeros((16, d), F32).at[:b].set(c).at[b].set(c_ctx)
    mod = _compute_mod(c_all, ada_w, ada_b)

    xc = ctx.reshape(1, b * lc, d)
    for l in range(depth):
        ctx_out = l < depth - 1
        mod_x = mod[l, :b].reshape(b, N_MOD, d)
        mod_c = mod[l, b:b + 1].reshape(1, N_MOD, d)
        row = lambda a: a[l].reshape(1, -1)

        wg1, wu1, wd1 = (w[l].astype(BF16) for w in (ffn1_gate, ffn1_up, ffn1_down))
        wg2, wu2, wd2 = (w[l].astype(BF16) for w in (ffn2_gate, ffn2_up, ffn2_down))
        wi = w_in[l]
        o1, o2, o3 = HY_COLS, HY_COLS + GQA_COLS, HY_COLS + GQA_COLS + MLA_COLS
        ml_lanes = _mla_lanes()
        o_kr = o3 - MLA_ROPE
        wi = jnp.concatenate([wi[:, :o_kr], _mla_spread(wi[:, o_kr:o3], ml_lanes[MLA_NOPE:]), wi[:, o3:]],
                             axis=1).astype(BF16)
        wo = w_out[l].astype(BF16)

        x = _ffn(x, mod_x, row(norm_ffn1), wg1, wu1, wd1, 0)
        xc = _ffn(xc, mod_c, row(norm_ffn1), wg1, wu1, wd1, 0)

        hy_x, gq_x, ml_x, rw_x = _inproj(x, mod_x, row(norm_mix), wi)
        hy_c, gq_c, ml_c, rw_c = (t.reshape(b, lc, -1) for t in _inproj(xc, mod_c, row(norm_mix), wi))

        w1p = jnp.zeros((LANES, HY_ORDER), F32).at[:HY_EMB].set(hy_f_w1[l])
        filt = (w1p, row(hy_f_b1), hy_f_w2[l], row(hy_f_b2), hy_f_w3[l], row(hy_f_b3), hy_f_w4[l],
                row(hy_f_freq))
        spec_x = _hy_spectrum(_hy_filters(length, *filt), row(hy_bias))
        y_hy_x = _hy_conv(hy_x, hy_conv_w[l], hy_conv_b[l], spec_x)

        gq = jnp.tile(row(gqa_q_norm), (1, 2))
        gk = jnp.tile(row(gqa_k_norm), (1, 2))
        q_l, k_l, v_l = _gqa_prep(gq_x, gq, gk, True)
        q_c, k_c, v_c = _gqa_prep(gq_c, gq, gk, False)
        y_gq_x = _attention(q_l, [(k_c, v_c), (k_l, v_l)], GQA_KV_HEADS)

        wuq = _mla_spread(mla_w_uq[l], ml_lanes).astype(BF16)
        wukv = mla_w_ukv[l].reshape(MLA_KV_RANK, MLA_HEADS, MLA_NOPE + MLA_V)
        wuk = _mla_spread(wukv[:, :, :MLA_NOPE].reshape(MLA_KV_RANK, -1), ml_lanes[:MLA_NOPE]).astype(BF16)
        wuv = wukv[:, :, MLA_NOPE:].reshape(MLA_KV_RANK, -1).astype(BF16)
        pad_n = lambda g: _mla_spread(g[l].reshape(1, MLA_QK), ml_lanes)
        mla_w = (row(mla_cq_norm), row(mla_ckv_norm), wuq, wuk, wuv, pad_n(mla_q_norm), pad_n(mla_k_norm))
        mq_l, mk_l, mv_l = _mla_prep(ml_x, *mla_w, True)
        mq_c, mk_c, mv_c = _mla_prep(ml_c, *mla_w, False)
        y_ml_x = _attention(mq_l, [(mk_c, mv_c), (mk_l, mv_l)], MLA_HEADS)

        rw_w = (row(rw_mu), row(rw_k_k), row(rw_k_a), rw_r_k[l].reshape(1, RW_C),
                rw_w0[l].reshape(1, 2 * RW_C), _blockdiag2(rw_w2[l]).astype(BF16),
                rw_a0[l].reshape(1, 2 * RW_C), _blockdiag2(rw_a2[l]).astype(BF16), rw_g2[l].astype(BF16))
        pc = _rw_prep(rw_c, *rw_w)
        px = _rw_prep(rw_x, *rw_w)
        zeros = jnp.zeros((b, 2, RW_C, RW_C), F32)
        yf_c, yb_c, s_ctx = _rw_scan(*pc[:6], zeros)
        yf_x, yb_x, _ = _rw_scan(*px[:6], s_ctx)

        tail = (row(rw_ln_w), row(rw_ln_b), wo, row(norm_ffn2), wg2, wu2, wd2)
        x = _mix_ffn(x, mod_x, (y_hy_x, y_gq_x, y_ml_x), (yf_x, yb_x, px[6], px[7]), *tail)
        if ctx_out:
            spec_c = _hy_spectrum(_hy_filters(lc, *filt), row(hy_bias))
            y_hy_c = _hy_conv(hy_c, hy_conv_w[l], hy_conv_b[l], spec_c)
            y_gq_c = _attention(q_c, [(k_c, v_c)], GQA_KV_HEADS)
            y_ml_c = _attention(mq_c, [(mk_c, mv_c)], MLA_HEADS)
            flat = lambda ts: tuple(t.reshape(1, b * lc, -1) for t in ts)
            xc = _mix_ffn(xc, mod_c, flat((y_hy_c, y_gq_c, y_ml_c)), flat((yf_c, yb_c, pc[6], pc[7])), *tail)
    return x
```

```python
import functools
import math

import numpy as np
import jax
import jax.numpy as jnp
from jax import lax
from jax.experimental import pallas as pl
from jax.experimental.pallas import tpu as pltpu

F32 = jnp.float32
BF16 = jnp.bfloat16
HI = lax.Precision.HIGHEST

D_MODEL = 1024
GRID_W = 64
HEAD_DIM = 64
D_FF = 2816
N_MOD = 9
NORM_EPS = 1e-6
LOG2E = math.log2(math.e)
ROPE_THETA = 10000.0

HY_CH = 256
HY_EMB = 33
HY_ORDER = 64
HY_FAST_PCT = 0.3
HY_SLOW_PCT = 1.5
HY_TARGET = 1e-2

GQA_HEADS = 4
GQA_KV_HEADS = 2
MLA_HEADS = 4
MLA_NOPE = 64
MLA_ROPE = 32
MLA_QK = MLA_NOPE + MLA_ROPE
MLA_V = 64
MLA_Q_RANK = 256
MLA_KV_RANK = 128

RW_HEADS = 4
RW_N = 64
RW_C = RW_HEADS * RW_N
RW_DECAY_LORA = 64
RW_AAA_LORA = 64
RW_GATE_LORA = 128
RW_GN_EPS = 64e-5

HY_COLS = 3 * HY_CH
GQA_COLS = (GQA_HEADS + 2 * GQA_KV_HEADS) * HEAD_DIM
MLA_COLS = MLA_Q_RANK + MLA_KV_RANK + MLA_ROPE
MLA_COLS_PAD = 512
RW_COLS = 3 * RW_C + 2 * RW_DECAY_LORA + 2 * RW_AAA_LORA + RW_GATE_LORA
D_IN_PAD = HY_COLS + GQA_COLS + MLA_COLS_PAD + RW_COLS

LANES = 128
ROW_TILE = 512
Q_TILE = 256
ATTN_HEADS = 4
KEY_CHUNK = 256
VT_ROWS = 2 * HEAD_DIM
RW_CHUNK = 64
RW_BASE = 4
RW_LEVELS = 4
RW_BLOCK = 256
RW_SEQS = 4
VMEM_LIMIT = 56 * 2 ** 20


def _cparams(sem):
    return pltpu.CompilerParams(dimension_semantics=sem, vmem_limit_bytes=VMEM_LIMIT)


def _const_spec(shape):
    nd = len(shape)
    return pl.BlockSpec(shape, lambda *_: (0,) * nd, pipeline_mode=pl.Buffered(1))


def _dot(a, b, precision=None):
    return jnp.dot(a, b, preferred_element_type=F32, precision=precision)


def _dot_nt(a, b, precision=None):
    return lax.dot_general(a, b, (((1,), (1,)), ((), ())), preferred_element_type=F32,
                           precision=precision)


def _dot_tn(a, b, precision=None):
    return lax.dot_general(a, b, (((0,), (0,)), ((), ())), preferred_element_type=F32,
                           precision=precision)


def _pieces(x, n):
    out = []
    for _ in range(n):
        p = x.astype(BF16)
        out.append(p)
        x = x - p.astype(F32)
    return out


def _segsum(x, seg):
    return functools.reduce(jnp.add, [_dot(p, seg) for p in _pieces(x, 2)])


def _sigmoid(x):
    return 1.0 / (1.0 + jnp.exp(-x))


def _adaln(x, g, mod_ref, i):
    shift = mod_ref[pl.ds(3 * i, 1), :]
    scale = mod_ref[pl.ds(3 * i + 1, 1), :]
    r = lax.rsqrt(jnp.mean(x * x, axis=-1, keepdims=True) + NORM_EPS)
    return (x * r) * (g * (1.0 + scale)) + shift


def _mod_kernel(c_ref, w_ref, b_ref, o_ref):
    c = c_ref[...]
    s = c * _sigmoid(c)
    o_ref[...] = _dot(s, w_ref[...], HI) + b_ref[...]


def _compute_mod(c_all, ada_w, ada_b):
    depth, d, n = ada_w.shape
    rows = c_all.shape[0]
    tn = 1024
    return pl.pallas_call(
        _mod_kernel,
        grid=(depth, n // tn),
        in_specs=[
            pl.BlockSpec((rows, d), lambda l, j: (0, 0)),
            pl.BlockSpec((None, d, tn), lambda l, j: (l, 0, j)),
            pl.BlockSpec((None, 1, tn), lambda l, j: (l, 0, j)),
        ],
        out_specs=pl.BlockSpec((None, rows, tn), lambda l, j: (l, 0, j)),
        out_shape=jax.ShapeDtypeStruct((depth, rows, n), F32),
        compiler_params=_cparams(("arbitrary", "arbitrary")),
        name="mod",
    )(c_all, ada_w, ada_b.reshape(depth, 1, n))


def _ffn_kernel(x_ref, mod_ref, g_ref, wg_ref, wu_ref, wd_ref, o_ref, *, sub):
    x = x_ref[...]
    h = _adaln(x, g_ref[...], mod_ref, sub).astype(BF16)
    a = _dot(h, wg_ref[...])
    u = _dot(h, wu_ref[...])
    z = (a * _sigmoid(a) * u).astype(BF16)
    y = _dot(z, wd_ref[...])
    gate = mod_ref[pl.ds(3 * sub + 2, 1), :]
    o_ref[...] = x + (0.5 * gate) * y


def _ffn(x, mod, g, wg, wu, wd, sub):
    bm, rows, d = x.shape
    tm = min(ROW_TILE, rows)
    f = wg.shape[1]
    return pl.pallas_call(
        functools.partial(_ffn_kernel, sub=sub),
        grid=(bm, rows // tm),
        in_specs=[
            pl.BlockSpec((None, tm, d), lambda b, i: (b, i, 0)),
            pl.BlockSpec((None, N_MOD, d), lambda b, i: (b, 0, 0)),
            _const_spec((1, d)),
            _const_spec((d, f)),
            _const_spec((d, f)),
            _const_spec((f, d)),
        ],
        out_specs=pl.BlockSpec((None, tm, d), lambda b, i: (b, i, 0)),
        out_shape=jax.ShapeDtypeStruct(x.shape, F32),
        compiler_params=_cparams(("arbitrary", "arbitrary")),
        name="ffn",
    )(x, mod, g, wg, wu, wd)


_IN_OFFS = (0, HY_COLS, HY_COLS + GQA_COLS, HY_COLS + GQA_COLS + MLA_COLS_PAD, D_IN_PAD)


def _inproj_kernel(x_ref, mod_ref, g_ref, w_ref, hy_ref, gq_ref, ml_ref, rw_ref):
    h = _adaln(x_ref[...], g_ref[...], mod_ref, 1).astype(BF16)
    p = _dot(h, w_ref[...])
    for ref, lo, hi in zip((hy_ref, gq_ref, ml_ref, rw_ref), _IN_OFFS[:-1], _IN_OFFS[1:]):
        ref[...] = p[:, lo:hi]


def _inproj(x, mod, g, w):
    bm, rows, d = x.shape
    tm = min(ROW_TILE, rows)
    widths = [hi - lo for lo, hi in zip(_IN_OFFS[:-1], _IN_OFFS[1:])]
    return pl.pallas_call(
        _inproj_kernel,
        grid=(bm, rows // tm),
        in_specs=[
            pl.BlockSpec((None, tm, d), lambda b, i: (b, i, 0)),
            pl.BlockSpec((None, N_MOD, d), lambda b, i: (b, 0, 0)),
            _const_spec((1, d)),
            _const_spec((d, D_IN_PAD)),
        ],
        out_specs=[pl.BlockSpec((None, tm, wd), lambda b, i: (b, i, 0)) for wd in widths],
        out_shape=[jax.ShapeDtypeStruct((bm, rows, wd), F32) for wd in widths],
        compiler_params=_cparams(("arbitrary", "arbitrary")),
        name="inproj",
    )(x, mod, g, w)


def _mix_ffn_kernel(x_ref, mod_ref, yh_ref, yg_ref, ym_ref, yf_ref, yb_ref, bonus_ref, gate_ref,
                    lnw_ref, lnb_ref, seg_ref, wo_ref, g_ref, wg_ref, wu_ref, wd_ref, o_ref):
    y = yf_ref[...] + yb_ref[...]
    seg = seg_ref[...]
    yc = y - _segsum(y, seg)
    var = _segsum(yc * yc, seg)
    yn = yc * lax.rsqrt(var + RW_GN_EPS) * lnw_ref[...] + lnb_ref[...]
    y_rw = ((yn + bonus_ref[...]) * gate_ref[...]).astype(BF16)
    ycat = jnp.concatenate([yh_ref[...], yg_ref[...], ym_ref[...], y_rw], axis=-1)
    x = x_ref[...] + mod_ref[pl.ds(5, 1), :] * _dot(ycat, wo_ref[...])
    h = _adaln(x, g_ref[...], mod_ref, 2).astype(BF16)
    a = _dot(h, wg_ref[...])
    u = _dot(h, wu_ref[...])
    z = (a * _sigmoid(a) * u).astype(BF16)
    o_ref[...] = x + (0.5 * mod_ref[pl.ds(8, 1), :]) * _dot(z, wd_ref[...])


def _mix_ffn(x, mod, ys, rw, ln_w, ln_b, wo, g, wg, wu, wd):
    bm, rows, d = x.shape
    tm = min(ROW_TILE, rows)
    c = RW_C
    seg = jnp.asarray(_seg_matrix(c, RW_N, 1.0 / RW_N), BF16)
    row_spec = pl.BlockSpec((None, tm, d), lambda b, i: (b, i, 0))
    y_spec = pl.BlockSpec((None, tm, c), lambda b, i: (b, i, 0))
    consts = (ln_w, ln_b, seg, wo, g, wg, wu, wd)
    return pl.pallas_call(
        _mix_ffn_kernel,
        grid=(bm, rows // tm),
        in_specs=[row_spec, pl.BlockSpec((None, N_MOD, d), lambda b, i: (b, 0, 0))] + [y_spec] * 7
                 + [_const_spec(a.shape) for a in consts],
        out_specs=row_spec,
        out_shape=jax.ShapeDtypeStruct(x.shape, F32),
        compiler_params=_cparams(("arbitrary", "arbitrary")),
        name="mix_ffn",
    )(x, mod, *ys, *rw, *consts)


def _hy_features(length):
    t01 = np.linspace(0.0, 1.0, length, dtype=np.float32)[:, None]
    bands = (HY_EMB - 1) // 2
    w_ang = (np.float32(2.0 * math.pi) * np.arange(length, dtype=np.float32)[:, None]
             / np.float32(length)).astype(np.float32)
    f = np.linspace(1e-4, bands - 1, bands, dtype=np.float32)[None]
    arg = (f * w_ang).astype(np.float32)
    z = np.concatenate([t01, np.cos(arg), -np.sin(arg)], axis=-1).astype(np.float32)
    zp = np.zeros((length, LANES), np.float32)
    zp[:, :HY_EMB] = z
    return zp


def _hy_deltas():
    max_decay = math.log(HY_TARGET) / HY_FAST_PCT
    min_decay = math.log(HY_TARGET) / HY_SLOW_PCT
    d = np.abs(np.linspace(min_decay, max_decay, HY_CH, dtype=np.float32))
    return np.tile(d, 2)[None].astype(np.float32)


def _hyfilt_kernel(z_ref, w1_ref, b1_ref, w2_ref, b2_ref, w3_ref, b3_ref, w4_ref, fr_ref,
                   dl_ref, o_ref):
    z = z_ref[...]
    fr = fr_ref[...]
    h = jnp.sin(fr * (_dot(z, w1_ref[...], HI) + b1_ref[...]))
    h = jnp.sin(fr * (_dot(h, w2_ref[...], HI) + b2_ref[...]))
    h = jnp.sin(fr * (_dot(h, w3_ref[...], HI) + b3_ref[...]))
    h = _dot(h, w4_ref[...], HI)
    o_ref[...] = h * jnp.exp(-z[:, 0:1] * dl_ref[...])


def _hy_filters(length, w1p, b1, w2, b2, w3, b3, w4, freq):
    z = jnp.asarray(_hy_features(length))
    dl = jnp.asarray(_hy_deltas())
    tl = min(ROW_TILE, length)
    consts = (w1p, b1, w2, b2, w3, b3, w4, freq, dl)
    return pl.pallas_call(
        _hyfilt_kernel,
        grid=(length // tl,),
        in_specs=[pl.BlockSpec((tl, LANES), lambda i: (i, 0))] + [_const_spec(a.shape) for a in consts],
        out_specs=pl.BlockSpec((tl, 2 * HY_CH), lambda i: (i, 0)),
        out_shape=jax.ShapeDtypeStruct((length, 2 * HY_CH), F32),
        compiler_params=_cparams(("arbitrary",)),
        name="hyena_filters",
    )(z, *consts)


FFT_SLAB_PAD = 8
FFT_GROUP = 16
FFT_SLAB_GROUP = 12


def _fft_factors(length):
    return (128, 64) if length >= 2048 else (2 * length, 1)


def _fft_layout(n1, n2):
    if n2 == 1:
        return 1, n1, 2 * n1
    kept = n1 // 2 + 1
    return -(-kept // FFT_SLAB_GROUP) * FFT_SLAB_GROUP, n2, 2 * n2 + FFT_SLAB_PAD


def _stack3(t):
    hi = t.astype(jnp.bfloat16)
    lo = (t - hi.astype(np.float64)).astype(jnp.bfloat16)
    return np.concatenate([hi, lo, hi], axis=-1)


def _rows3(d):
    hi = d.astype(BF16)
    lo = (d - hi.astype(F32)).astype(BF16)
    return jnp.concatenate([hi, hi, lo], axis=0)


@functools.lru_cache(maxsize=None)
def _fft_tables(length):
    n1, n2 = _fft_factors(length)
    n = n1 * n2
    assert n == 2 * length
    a_n2 = np.arange(n2)[:, None, None]
    a_k1 = np.arange(n1)[None, :, None]
    a_n1 = np.arange(n1 // 2)[None, None, :]
    ang = 2.0 * np.pi * ((a_n1 * a_k1 % n1) / n1 + (a_n2 * a_k1 % n) / n)
    t_re, t_im = np.cos(ang), -np.sin(ang)
    wgt = np.ones(n1)
    if n2 > 1:
        nk = _fft_layout(n1, n2)[0]
        keep = np.arange(nk) <= n1 // 2
        t_re, t_im = (np.where(keep[None, :, None], t[:, :nk], 0.0) for t in (t_re, t_im))
        wgt = np.where(np.isin(np.arange(nk), (0, n1 // 2)), 1.0, 2.0) * keep
    ta = np.concatenate([t_re, t_im], axis=1)
    t_re_w, t_im_w = t_re * wgt[None, :, None], t_im * wgt[None, :, None]
    tai = np.concatenate([np.transpose(t_re_w, (0, 2, 1)), np.transpose(t_im_w, (0, 2, 1))], axis=2) / n
    jj = np.arange(n2)
    ang2 = 2.0 * np.pi * (np.outer(jj, jj) % n2) / n2
    c, s = np.cos(ang2), np.sin(ang2)
    fb = np.block([[c, s], [-s, c]])
    fbi = np.block([[c, -s], [s, c]])
    return tuple(_stack3(t) for t in (ta, tai, fb, fbi))


def _fft_stage_a(src_ref, fbuf_ref, ta_ref, n1, n2):
    if n2 == 1:
        fbuf_ref[...] = _dot(ta_ref[0], _rows3(src_ref[...]))
        return
    nk, _, pitch = _fft_layout(n1, n2)

    def body(i, carry):
        js = [i * FFT_GROUP + g for g in range(FFT_GROUP)]
        rows = [_rows3(src_ref[pl.ds(j, n1 // 2, stride=n2), :]) for j in js]
        ts = [_dot(ta_ref[j], r) for j, r in zip(js, rows)]
        for j, t in zip(js, ts):
            fbuf_ref[pl.ds(j, nk, stride=pitch), :] = t[:nk]
            fbuf_ref[pl.ds(n2 + j, nk, stride=pitch), :] = t[nk:]
        return carry

    lax.fori_loop(0, n2 // FFT_GROUP, body, 0)


def _spec_kernel(hf_ref, hb_ref, bias_ref, ta_ref, fb_ref, o_ref, fbuf_ref, *, n1, n2):
    nslab, hs, pitch = _fft_layout(n1, n2)
    if pitch > 2 * hs:
        o_ref[...] = jnp.zeros(o_ref.shape, F32)
    _fft_stage_a(hf_ref, o_ref, ta_ref, n1, n2)
    _fft_stage_a(hb_ref, fbuf_ref, ta_ref, n1, n2)
    is_re = lax.broadcasted_iota(jnp.int32, (2 * hs, 1), 0) < hs
    sgn = jnp.where(is_re, 1.0, -1.0).astype(F32)
    skip = jnp.where(is_re, bias_ref[...], 0.0)

    grp = FFT_SLAB_GROUP if n2 > 1 else 1

    def stage_b(i, carry):
        sls = [pl.ds(pl.multiple_of((i * grp + g) * pitch, 8), 2 * hs) for g in range(grp)]
        a = [o_ref[sl, :] for sl in sls]
        b = [fbuf_ref[sl, :] for sl in sls]
        if n2 > 1:
            a = [_dot(fb_ref[...], _rows3(x)) for x in a]
            b = [_dot(fb_ref[...], _rows3(x)) for x in b]
        for sl, x, y in zip(sls, a, b):
            o_ref[sl, :] = x + sgn * y + skip
        return carry

    lax.fori_loop(0, nslab // grp, stage_b, 0)


def _hy_spectrum(h, bias):
    length = h.shape[0]
    n1, n2 = _fft_factors(length)
    nslab, _, pitch = _fft_layout(n1, n2)
    ta, _, fb, _ = (jnp.asarray(t) for t in _fft_tables(length))
    nblk = HY_CH // LANES
    return pl.pallas_call(
        functools.partial(_spec_kernel, n1=n1, n2=n2),
        grid=(nblk,),
        in_specs=[pl.BlockSpec((length, LANES), lambda j: (0, j)),
                  pl.BlockSpec((length, LANES), lambda j: (0, nblk + j)),
                  pl.BlockSpec((1, LANES), lambda j: (0, j)),
                  _const_spec(ta.shape), _const_spec(fb.shape)],
        out_specs=pl.BlockSpec((nslab * pitch, LANES), lambda j: (0, j)),
        out_shape=jax.ShapeDtypeStruct((nslab * pitch, HY_CH), F32),
        scratch_shapes=[pltpu.VMEM((nslab * pitch, LANES), F32)],
        compiler_params=_cparams(("arbitrary",)),
        name="hyena_spectrum",
    )(h, h, bias, ta, fb)


def _shift_rows(x, prev_row, next_row):
    n = x.shape[0]
    row = lax.broadcasted_iota(jnp.int32, x.shape, 0)
    up = jnp.where(row == 0, prev_row, pltpu.roll(x, 1, 0))
    dn = jnp.where(row == n - 1, next_row, pltpu.roll(x, n - 1, 0))
    return up, dn


def _hyconv_kernel(x1_ref, x2_ref, v_ref, cw1_ref, cw2_ref, cwv_ref, cb_ref, g_ref,
                   ta_ref, tai_ref, fb_ref, fbi_ref, o_ref, u_ref, fbuf_ref, *, n1, n2):
    def conv3(ref, w_ref, b):
        x = ref[...]
        up, dn = _shift_rows(x, 0.0, 0.0)
        return up * w_ref[0:1, :] + x * w_ref[1:2, :] + dn * w_ref[2:3, :] + b

    cb = cb_ref[...]
    u_ref[...] = conv3(x1_ref, cw1_ref, cb[0:1, :]) * conv3(v_ref, cwv_ref, cb[2:3, :])

    _fft_stage_a(u_ref, fbuf_ref, ta_ref, n1, n2)
    nslab, hs, pitch = _fft_layout(n1, n2)

    grp = FFT_SLAB_GROUP if n2 > 1 else 1

    def cmul(a, g):
        ar, ai, gr, gi = a[:hs], a[hs:], g[:hs], g[hs:]
        return jnp.concatenate([ar * gr - ai * gi, ar * gi + ai * gr], axis=0)

    def stage_b(i, carry):
        sls = [pl.ds(pl.multiple_of((i * grp + g) * pitch, 8), 2 * hs) for g in range(grp)]
        a = [fbuf_ref[sl, :] for sl in sls]
        if n2 > 1:
            a = [_dot(fb_ref[...], _rows3(x)) for x in a]
        a = [cmul(x, g_ref[sl, :]) for x, sl in zip(a, sls)]
        if n2 > 1:
            a = [_dot(fbi_ref[...], _rows3(x)) for x in a]
        for sl, x in zip(sls, a):
            fbuf_ref[sl, :] = x
        return carry

    lax.fori_loop(0, nslab // grp, stage_b, 0)

    if n2 == 1:
        u_ref[...] = _dot(tai_ref[0], _rows3(fbuf_ref[...]))
    else:
        def stage_a_inv(i, carry):
            js = [i * FFT_GROUP + g for g in range(FFT_GROUP)]
            cs = [_rows3(jnp.concatenate([fbuf_ref[pl.ds(j, nslab, stride=pitch), :],
                                          fbuf_ref[pl.ds(n2 + j, nslab, stride=pitch), :]], axis=0))
                  for j in js]
            ys = [_dot(tai_ref[j], c) for j, c in zip(js, cs)]
            for j, y in zip(js, ys):
                u_ref[pl.ds(j, n1 // 2, stride=n2), :] = y
            return carry

        lax.fori_loop(0, n2 // FFT_GROUP, stage_a_inv, 0)

    o_ref[...] = (conv3(x2_ref, cw2_ref, cb[1:2, :]) * u_ref[...]).astype(o_ref.dtype)


def _hy_conv(p, conv_w, conv_b, spec):
    b, length, _ = p.shape
    n1, n2 = _fft_factors(length)
    nslab, _, pitch = _fft_layout(n1, n2)
    ta, tai, fb, fbi = (jnp.asarray(t) for t in _fft_tables(length))
    nblk = HY_CH // LANES
    cb3 = conv_b.reshape(3, HY_CH)
    col = lambda g: pl.BlockSpec((None, length, LANES), lambda j, i, g=g: (i, 0, g * nblk + j))
    wcol = lambda g: pl.BlockSpec((3, LANES), lambda j, i, g=g: (0, g * nblk + j))
    return pl.pallas_call(
        functools.partial(_hyconv_kernel, n1=n1, n2=n2),
        grid=(nblk, b),
        in_specs=[col(0), col(1), col(2), wcol(0), wcol(1), wcol(2),
                  pl.BlockSpec((3, LANES), lambda j, i: (0, j)),
                  pl.BlockSpec((nslab * pitch, LANES), lambda j, i: (0, j), pipeline_mode=pl.Buffered(1)),
                  _const_spec(ta.shape), _const_spec(tai.shape), _const_spec(fb.shape),
                  _const_spec(fbi.shape)],
        out_specs=pl.BlockSpec((None, length, LANES), lambda j, i: (i, 0, j)),
        out_shape=jax.ShapeDtypeStruct((b, length, HY_CH), BF16),
        scratch_shapes=[pltpu.VMEM((length, LANES), F32), pltpu.VMEM((nslab * pitch, LANES), F32)],
        compiler_params=_cparams(("arbitrary", "arbitrary")),
        name="hyena_conv",
    )(p, p, p, conv_w, conv_w, conv_w, cb3, spec, ta, tai, fb, fbi)


def _rope_tables(length, d_rot, lane_lo, head_w):
    rows = length // GRID_W
    row = np.repeat(np.arange(rows, dtype=np.float32), GRID_W)
    colv = np.tile(np.arange(GRID_W, dtype=np.float32), rows)
    n_freq = d_rot // 4
    inv = (np.float32(ROPE_THETA) ** (-np.arange(n_freq, dtype=np.float32) / np.float32(n_freq))).astype(np.float32)
    ang = np.concatenate([row[:, None] * inv, colv[:, None] * inv], axis=-1).astype(np.float32)
    cos_t = np.ones((length, LANES), np.float32)
    sin_t = np.zeros((length, LANES), np.float32)
    c, s = np.cos(ang), np.sin(ang)
    for base in range(0, LANES, head_w):
        for i in range(d_rot // 2):
            cos_t[:, base + lane_lo + 2 * i] = c[:, i]
            cos_t[:, base + lane_lo + 2 * i + 1] = c[:, i]
            sin_t[:, base + lane_lo + 2 * i] = -s[:, i]
            sin_t[:, base + lane_lo + 2 * i + 1] = s[:, i]
    return cos_t, sin_t


def _rope_angles(length, d_rot):
    rows = length // GRID_W
    row = np.repeat(np.arange(rows, dtype=np.float32), GRID_W)
    colv = np.tile(np.arange(GRID_W, dtype=np.float32), rows)
    n_freq = d_rot // 4
    inv = (np.float32(ROPE_THETA) ** (-np.arange(n_freq, dtype=np.float32) / np.float32(n_freq))).astype(np.float32)
    return np.concatenate([row[:, None] * inv, colv[:, None] * inv], axis=-1).astype(np.float32)


def _mla_lanes():
    lanes = np.empty(MLA_QK, np.int64)
    n = np.arange(MLA_NOPE)
    lanes[:MLA_NOPE] = np.where(n < 32, n, n + 16)
    i = np.arange(MLA_ROPE // 2)
    lanes[MLA_NOPE + 2 * i] = 32 + i
    lanes[MLA_NOPE + 2 * i + 1] = 96 + i
    return lanes


def _mla_rope_tables(length):
    ang = _rope_angles(length, MLA_ROPE)
    cos_t = np.ones((length, LANES), np.float32)
    sin_t = np.zeros((length, LANES), np.float32)
    n_pairs = MLA_ROPE // 2
    cos_t[:, 32:32 + n_pairs] = cos_t[:, 96:96 + n_pairs] = np.cos(ang)
    sin_t[:, 32:32 + n_pairs] = -np.sin(ang)
    sin_t[:, 96:96 + n_pairs] = np.sin(ang)
    return cos_t, sin_t


def _rope(x, cos_t, sin_t):
    lane = lax.broadcasted_iota(jnp.int32, x.shape, 1)
    w = x.shape[1]
    partner = jnp.where(jnp.bitwise_and(lane, 1) == 0, pltpu.roll(x, w - 1, 1), pltpu.roll(x, 1, 1))
    return x * cos_t + partner * sin_t


def _store_vt(vt_ref, h, vt):
    row = lax.broadcasted_iota(jnp.int32, (VT_ROWS - HEAD_DIM, vt.shape[1]), 0)
    vt_ref[h, 0:HEAD_DIM, :] = vt.astype(BF16)
    vt_ref[h, HEAD_DIM:VT_ROWS, :] = jnp.where(row == 0, 1.0, 0.0).astype(BF16)


def _gqa_prep_kernel(p_ref, cos_ref, sin_ref, gq_ref, gk_ref, seg_ref, q_ref, k_ref, vt_ref, *, rope):
    p = p_ref[...]
    seg = seg_ref[...]
    lane = lax.broadcasted_iota(jnp.int32, (p.shape[0], LANES), 1)
    low = lane < HEAD_DIM

    def hnorm(x, g):
        ms = _segsum(x * x, seg)
        return x * lax.rsqrt(ms + NORM_EPS) * g

    scale = HEAD_DIM ** -0.5 * LOG2E
    for c in range(2):
        q = hnorm(p[:, c * LANES:(c + 1) * LANES], gq_ref[...])
        if rope:
            q = _rope(q, cos_ref[...], sin_ref[...])
        q = q * scale
        qs = pltpu.roll(q, HEAD_DIM, 1)
        q_ref[:, (2 * c) * LANES:(2 * c + 1) * LANES] = jnp.where(low, q, 0.0).astype(BF16)
        q_ref[:, (2 * c + 1) * LANES:(2 * c + 2) * LANES] = jnp.where(low, qs, 0.0).astype(BF16)
    k = hnorm(p[:, 256:384], gk_ref[...])
    if rope:
        k = _rope(k, cos_ref[...], sin_ref[...])
    ks = pltpu.roll(k, HEAD_DIM, 1)
    k_ref[:, 0:LANES] = jnp.where(low, k, 0.0).astype(BF16)
    k_ref[:, LANES:2 * LANES] = jnp.where(low, ks, 0.0).astype(BF16)
    vt = p[:, 384:512].T
    _store_vt(vt_ref, 0, vt[0:HEAD_DIM])
    _store_vt(vt_ref, 1, vt[HEAD_DIM:2 * HEAD_DIM])


def _seg_matrix(width, seg, value):
    i = np.arange(width)
    return ((i[:, None] // seg) == (i[None, :] // seg)).astype(np.float32) * np.float32(value)


def _gqa_prep(p, gq, gk, rope):
    b, length, _ = p.shape
    tm = min(ROW_TILE, length)
    if rope:
        cos_t, sin_t = (jnp.asarray(t) for t in _rope_tables(length, HEAD_DIM, 0, HEAD_DIM))
    else:
        cos_t = sin_t = jnp.zeros((length, LANES), F32)
    seg = jnp.asarray(_seg_matrix(LANES, HEAD_DIM, 1.0 / HEAD_DIM), BF16)
    tab = pl.BlockSpec((tm, LANES), lambda i, bb: (i, 0))
    outw = (512, 256)
    return pl.pallas_call(
        functools.partial(_gqa_prep_kernel, rope=rope),
        grid=(length // tm, b),
        in_specs=[pl.BlockSpec((None, tm, GQA_COLS), lambda i, bb: (bb, i, 0)), tab, tab,
                  _const_spec((1, LANES)), _const_spec((1, LANES)), _const_spec((LANES, LANES))],
        out_specs=[pl.BlockSpec((None, tm, w), lambda i, bb: (bb, i, 0)) for w in outw]
                  + [pl.BlockSpec((None, GQA_KV_HEADS, VT_ROWS, tm), lambda i, bb: (bb, 0, 0, i))],
        out_shape=[jax.ShapeDtypeStruct((b, length, w), BF16) for w in outw]
                  + [jax.ShapeDtypeStruct((b, GQA_KV_HEADS, VT_ROWS, length), BF16)],
        compiler_params=_cparams(("arbitrary", "arbitrary")),
        name="gqa_prep",
    )(p, cos_t, sin_t, gq, gk, seg)


def _mla_prep_kernel(p_ref, cos_ref, sin_ref, cqn_ref, ckvn_ref, wuq_ref, wuk_ref, wuv_ref,
                     qn_ref, kn_ref, ones_ref, q_ref, k_ref, vt_ref, *, rope):
    p = p_ref[...]

    def rms(x, g, width):
        ms = _segsum(x * x, ones_ref[0:x.shape[1], 0:x.shape[1]]) * (1.0 / width)
        return x * lax.rsqrt(ms + NORM_EPS) * g

    cq = rms(p[:, 0:MLA_Q_RANK], cqn_ref[...], MLA_Q_RANK).astype(BF16)
    ckv = rms(p[:, MLA_Q_RANK:MLA_Q_RANK + MLA_KV_RANK], ckvn_ref[...], MLA_KV_RANK).astype(BF16)
    q = _dot(cq, wuq_ref[...])
    kn = _dot(ckv, wuk_ref[...])
    v = _dot(ckv, wuv_ref[...])
    for c in range(MLA_HEADS // 2):
        vt = v[:, c * LANES:(c + 1) * LANES].T
        _store_vt(vt_ref, 2 * c, vt[0:MLA_V])
        _store_vt(vt_ref, 2 * c + 1, vt[MLA_V:2 * MLA_V])
    kr = p[:, 384:512]
    scale = MLA_QK ** -0.5 * LOG2E

    def rot(x):
        return x * cos_ref[...] + pltpu.roll(x, LANES // 2, 1) * sin_ref[...]

    for h in range(MLA_HEADS):
        sl = slice(h * LANES, (h + 1) * LANES)
        qh = rms(q[:, sl], qn_ref[...], MLA_QK)
        kh = rms(kn[:, sl] + kr, kn_ref[...], MLA_QK)
        if rope:
            qh, kh = rot(qh), rot(kh)
        q_ref[:, sl] = (qh * scale).astype(BF16)
        k_ref[:, sl] = kh.astype(BF16)


def _mla_prep(p, cqn, ckvn, wuq, wuk, wuv, qn, kn, rope):
    b, length, _ = p.shape
    tm = min(ROW_TILE, length)
    if rope:
        cos_t, sin_t = (jnp.asarray(t) for t in _mla_rope_tables(length))
    else:
        cos_t = sin_t = jnp.zeros((length, LANES), F32)
    tab = pl.BlockSpec((tm, LANES), lambda i, bb: (i, 0))
    ones = jnp.ones((MLA_Q_RANK, MLA_Q_RANK), BF16)
    consts = (cqn, ckvn, wuq, wuk, wuv, qn, kn, ones)
    outw = (512, 512)
    return pl.pallas_call(
        functools.partial(_mla_prep_kernel, rope=rope),
        grid=(length // tm, b),
        in_specs=[pl.BlockSpec((None, tm, MLA_COLS_PAD), lambda i, bb: (bb, i, 0)), tab, tab]
                 + [_const_spec(a.shape) for a in consts],
        out_specs=[pl.BlockSpec((None, tm, w), lambda i, bb: (bb, i, 0)) for w in outw]
                  + [pl.BlockSpec((None, MLA_HEADS, VT_ROWS, tm), lambda i, bb: (bb, 0, 0, i))],
        out_shape=[jax.ShapeDtypeStruct((b, length, w), BF16) for w in outw]
                  + [jax.ShapeDtypeStruct((b, MLA_HEADS, VT_ROWS, length), BF16)],
        compiler_params=_cparams(("arbitrary", "arbitrary")),
        name="mla_prep",
    )(p, cos_t, sin_t, *consts)


def _attn_kernel(*refs, n_seg):
    nh = ATTN_HEADS
    q_ref, o_ref = refs[0], refs[-1]
    segs = [refs[1 + 2 * nh * s:1 + 2 * nh * (s + 1)] for s in range(n_seg)]
    q = q_ref[...]
    qh = [q[:, h * LANES:(h + 1) * LANES] for h in range(nh)]
    pieces = [(h, seg, c0, min(KEY_CHUNK, seg[0].shape[0]))
              for h in range(nh) for seg in segs
              for c0 in range(0, seg[0].shape[0], min(KEY_CHUNK, seg[0].shape[0]))]
    scores = [_dot_nt(seg[h][c0:c0 + n, :], qh[h]) for h, seg, c0, n in pieces]
    mx = [jnp.max(s, axis=0, keepdims=True) for s in scores]
    ps = [jnp.exp2(s - m).astype(BF16) for s, m in zip(scores, mx)]
    os_ = [_dot(seg[nh + h][:, c0:c0 + n], e) for (h, seg, c0, n), e in zip(pieces, ps)]
    outs = []
    for h in range(nh):
        idx = [i for i, pc in enumerate(pieces) if pc[0] == h]
        m = functools.reduce(jnp.maximum, [mx[i] for i in idx])
        o = functools.reduce(jnp.add, [os_[i] * jnp.exp2(mx[i] - m) for i in idx])
        outs.append(o[0:HEAD_DIM] / o[HEAD_DIM:HEAD_DIM + 1])
    o_ref[...] = jnp.concatenate(outs, axis=0).T.astype(o_ref.dtype)


def _attention(q, kv_segs, k_heads):
    b, lq, _ = q.shape
    tq = min(Q_TILE, lq)
    nh = ATTN_HEADS
    kv_of = [h * k_heads // nh for h in range(nh)]
    in_specs = [pl.BlockSpec((None, tq, nh * LANES), lambda bb, i: (bb, i, 0))]
    args = [q]
    for k, vt in kv_segs:
        lk = k.shape[1]
        in_specs += [pl.BlockSpec((None, lk, LANES), lambda bb, i, j=j: (bb, 0, j)) for j in kv_of]
        in_specs += [pl.BlockSpec((None, None, VT_ROWS, lk), lambda bb, i, j=j: (bb, j, 0, 0)) for j in kv_of]
        args += [k] * nh + [vt] * nh
    return pl.pallas_call(
        functools.partial(_attn_kernel, n_seg=len(kv_segs)),
        grid=(b, lq // tq),
        in_specs=in_specs,
        out_specs=pl.BlockSpec((None, tq, nh * HEAD_DIM), lambda bb, i: (bb, i, 0)),
        out_shape=jax.ShapeDtypeStruct((b, lq, nh * HEAD_DIM), BF16),
        compiler_params=_cparams(("arbitrary", "arbitrary")),
        name="attention",
    )(*args)


def _rw_prep_kernel(p_ref, prev_ref, next_ref, mu_ref, kk_ref_w, ka_ref, rk_ref, w0_ref, w2_ref,
                    a0_ref, a2_ref, g2_ref, seg_ref,
                    r_ref, v_ref, kk_ref, lw_ref, kd_ref, bd_ref, bonus_ref, g_ref):
    i = pl.program_id(1)
    n = pl.num_programs(1)
    x = p_ref[...]
    prev_row = jnp.where(i > 0, prev_ref[7:8, :], 0.0)
    next_row = jnp.where(i < n - 1, next_ref[0:1, :], 0.0)
    up, dn = _shift_rows(x, prev_row, next_row)
    xs = x + mu_ref[...] * (0.5 * (up + dn) - x)
    c = RW_C
    r, k, v = xs[:, 0:c], xs[:, c:2 * c], xs[:, 2 * c:3 * c]
    xw, xa, xg = xs[:, 3 * c:3 * c + 128], xs[:, 3 * c + 128:3 * c + 256], xs[:, 3 * c + 256:3 * c + 384]
    seg = seg_ref[...]
    kk = k * kk_ref_w[...]
    nrm = jnp.sqrt(_segsum(kk * kk, seg))
    kk = kk / jnp.maximum(nrm, 1e-12)
    u = w0_ref[...] + _dot(jnp.tanh(xw).astype(BF16), w2_ref[...])
    z = -u
    softplus = jnp.maximum(z, 0.0) + jnp.log(1.0 + jnp.exp(-jnp.abs(z)))
    lw_ref[...] = -jnp.exp(-softplus - 0.5)
    a = _sigmoid(a0_ref[...] + _dot(xa.astype(BF16), a2_ref[...]))
    bonus = jnp.zeros_like(r)
    for d in range(2):
        a_d = a[:, d * c:(d + 1) * c]
        k_d = k * (1.0 + (a_d - 1.0) * ka_ref[...])
        kd_ref[:, d * c:(d + 1) * c] = k_d
        bd_ref[:, d * c:(d + 1) * c] = kk * a_d
        bonus = bonus + _segsum(r * k_d * rk_ref[...], seg)
    r_ref[...] = r
    v_ref[...] = v
    kk_ref[...] = kk
    bonus_ref[...] = bonus * v
    g_ref[...] = _dot(_sigmoid(xg).astype(BF16), g2_ref[...])


def _rw_prep(p, mu, k_k, k_a, r_k, w0, w2bd, a0, a2bd, g2):
    b, length, cols = p.shape
    tm = min(ROW_TILE, length)
    nb = tm // 8
    last = length // 8 - 1
    seg = jnp.asarray(_seg_matrix(RW_C, RW_N, 1.0), BF16)
    consts = (mu, k_k, k_a, r_k, w0, w2bd, a0, a2bd, g2, seg)
    outw = (RW_C, RW_C, RW_C, 2 * RW_C, 2 * RW_C, 2 * RW_C, RW_C, RW_C)
    return pl.pallas_call(
        _rw_prep_kernel,
        grid=(b, length // tm),
        in_specs=[pl.BlockSpec((None, tm, cols), lambda bb, i: (bb, i, 0)),
                  pl.BlockSpec((None, 8, cols), lambda bb, i: (bb, jnp.maximum(i * nb - 1, 0), 0)),
                  pl.BlockSpec((None, 8, cols), lambda bb, i: (bb, jnp.minimum((i + 1) * nb, last), 0))]
                 + [_const_spec(a.shape) for a in consts],
        out_specs=[pl.BlockSpec((None, tm, w), lambda bb, i: (bb, i, 0)) for w in outw],
        out_shape=[jax.ShapeDtypeStruct((b, length, w), F32) for w in outw],
        compiler_params=_cparams(("arbitrary", "arbitrary")),
        name="rwkv_prep",
    )(p, p, p, *consts)


@functools.lru_cache(maxsize=None)
def _rw_masks():
    cs, n = RW_CHUNK, RW_C
    i = np.arange(n)
    same = ((i[:, None] // cs) == (i[None, :] // cs)).astype(np.float32)
    t, s = np.arange(cs)[:, None], (i % cs)[None, :]
    smask = []
    for sgn in (1, -1):
        earlier = (s - t) * sgn < 0
        rows = [earlier, (s - t) * sgn <= 0, earlier & (t // RW_BASE == s // RW_BASE)]
        for k in range(RW_LEVELS):
            size = RW_BASE << k
            rows.append(earlier & (t // (2 * size) == s // (2 * size)) & (t // size != s // size))
        smask.append(np.stack(rows))
    eye = (s == t).astype(np.float32)
    tt = np.arange(cs)
    tri = np.stack([tt[None, :] <= tt[:, None], tt[None, :] >= tt[:, None]]).astype(np.float32)
    return same, np.stack(smask).astype(np.float32), eye, tri


def _rw_scan_kernel(rf_ref, rb_ref, vf_ref, vb_ref, kkf_ref, kkb_ref, lwf_ref, lwb_ref, kdf_ref, kdb_ref,
                    bf_ref, bb_ref, s0_ref, same_ref, smask_ref, eye_ref, tri_ref,
                    yf_ref, yb_ref, sfin_ref, s_ref, *, n_chunks, n_batch):
    cs = RW_CHUNK

    @pl.when(pl.program_id(1) == 0)
    def _():
        s_ref[...] = s0_ref[...]

    same_head = same_ref[...].astype(BF16)
    eye = eye_ref[...]
    dirs = ((rf_ref, vf_ref, kkf_ref, lwf_ref, kdf_ref, bf_ref, yf_ref),
            (rb_ref, vb_ref, kkb_ref, lwb_ref, kdb_ref, bb_ref, yb_ref))

    def bd(x):
        xb = x.astype(BF16)
        return jnp.concatenate([xb, xb, xb, xb], axis=0) * same_head

    def halves(t, size, d):
        lo = [t[g:g + size] for g in range(0, t.shape[0], 2 * size)]
        hi = [t[g + size:g + 2 * size] for g in range(0, t.shape[0], 2 * size)]
        return (lo, hi) if d == 0 else (hi, lo)

    def join(earlier, later, d):
        pairs = zip(earlier, later) if d == 0 else zip(later, earlier)
        return jnp.concatenate([x for pr in pairs for x in pr], axis=0)

    streams = [(bi, d) for bi in range(n_batch) for d in range(2)]

    def each(f, *cols):
        return [f(*a) for a in zip(*cols)] if cols else [f(bi, d) for bi, d in streams]

    bf = lambda xs: [x.astype(BF16) for x in xs]
    smask = lambda k: [smask_ref[d, k] for _, d in streams]

    def body(c, carry):
        def load(bi, d):
            cc = c if d == 0 else n_chunks - 1 - c
            sl = pl.ds(pl.multiple_of(cc * cs, cs), cs)
            return [ref[bi, sl, :] for ref in dirs[d][:6]] + [sl]

        r, v, kk, lw, kd, b, sl = zip(*each(load))
        strict, incl = smask(0), smask(1)
        cl = [functools.reduce(jnp.add, [_dot(tri_ref[d], p) for p in _pieces(x, 3)])
              for (_, d), x in zip(streams, lw)]
        w_inv = [jnp.exp(-x) for x in cl]
        w_all = [jnp.exp(jnp.sum(x, axis=0, keepdims=True)) for x in lw]
        kt = each(lambda a, w: a * w, kd, w_inv)
        bt = each(lambda a, w: a * w, b, w_inv)
        qk = each(lambda a, x, y: a * jnp.exp(x - y), kk, cl, lw)
        rt = bf(each(lambda a, x: a * jnp.exp(x), r, cl))
        qk_b, kt_s, bt_s = bf(qk), each(bd, kt), each(bd, bt)
        a_kk = bf(each(lambda q, k_, m: _dot_nt(q, k_) * m, qk_b, kt_s, strict))
        a_kb = each(lambda q, k_, m: _dot_nt(q, k_) * m, qk_b, bt_s, strict)
        a_rk = bf(each(lambda q, k_, m: _dot_nt(q, k_) * m, rt, kt_s, incl))
        a_rb = bf(each(lambda q, k_, m: _dot_nt(q, k_) * m, rt, bt_s, incl))
        l0 = each(lambda a, m: a * m, a_kb, smask(2))
        p0 = [eye - a for a in l0]
        sq = each(lambda a, a_s: _dot(a, a_s), bf(l0), each(bd, l0))
        t_inv = each(lambda p, pb, s_: p + _dot(pb, s_), p0, bf(p0), each(bd, sq))
        for k in range(RW_LEVELS):
            half = RW_BASE << k
            cm = each(lambda a, m: bd(a * m), a_kb, smask(3 + k))
            t_s = each(bd, t_inv)
            if half % 8:
                tc = bf(each(_dot, bf(t_inv), cm))
                t_inv = each(lambda t, x, y: t - _dot(x, y), t_inv, tc, t_s)
                continue
            parts = [halves(t, half, d) for (_, d), t in zip(streams, t_inv)]
            later = bf([jnp.concatenate(p[1], axis=0) for p in parts])
            tc = bf(each(_dot, later, cm))
            upd = each(_dot, tc, t_s)
            t_inv = [join(p[0], [x - u[i * half:(i + 1) * half] for i, x in enumerate(p[1])], d)
                     for (_, d), p, u in zip(streams, parts, upd)]
        t_b = bf(t_inv)
        v_s = each(bd, v)
        akv = each(_dot, a_kk, v_s)
        x1 = bf(each(_dot, t_b, each(bd, qk)))
        x2 = each(_dot, t_b, each(bd, akv))
        y_v = each(_dot, a_rk, v_s)
        bh_s = each(lambda a, w: bd(a * w), bt, w_all)
        kh_s = each(lambda a, w: bd(a * w), kt, w_all)
        s_v = each(_dot_tn, v_s, kh_s)
        s0f = [s_ref[bi, d] for bi, d in streams]
        s0 = bf(s0f)
        u = each(lambda x, s_, y: _dot_nt(x, s_) + y, x1, s0, x2)
        u_s = each(bd, u)
        y = each(lambda q, s_, yv, ab, us: _dot_nt(q, s_) + yv - _dot(ab, us), rt, s0, y_v, a_rb, u_s)
        s1 = each(lambda s_, w, sv, us, bh: s_ * w + sv - _dot_tn(us, bh), s0f, w_all, s_v, u_s, bh_s)
        for (bi, d), y_, s_, rows in zip(streams, y, s1, sl):
            dirs[d][6][bi, rows, :] = y_
            s_ref[bi, d] = s_
        return carry

    lax.fori_loop(0, n_chunks, body, 0)
    sfin_ref[...] = s_ref[...]


def _rw_scan(r, v, kk, lw, kd, bdir, s0):
    b, length, c = r.shape
    tc = min(RW_BLOCK, length)
    nblk = length // tc
    nb = RW_SEQS
    same, smask, eye, tri = _rw_masks()
    consts = (jnp.asarray(same), jnp.asarray(smask), jnp.asarray(eye), jnp.asarray(tri, BF16))
    fwd = lambda col: pl.BlockSpec((nb, tc, c), lambda g, i: (g, i, col))
    bwd = lambda col: pl.BlockSpec((nb, tc, c), lambda g, i: (g, nblk - 1 - i, col))
    state = pl.BlockSpec((nb, 2, c, c), lambda g, i: (g, 0, 0, 0))
    return pl.pallas_call(
        functools.partial(_rw_scan_kernel, n_chunks=tc // RW_CHUNK, n_batch=nb),
        grid=(b // nb, nblk),
        in_specs=[fwd(0), bwd(0)] * 3 + [fwd(0), bwd(1)] * 3
                 + [state] + [_const_spec(a.shape) for a in consts],
        out_specs=[fwd(0), bwd(0), state],
        out_shape=[jax.ShapeDtypeStruct((b, length, c), F32), jax.ShapeDtypeStruct((b, length, c), F32),
                   jax.ShapeDtypeStruct((b, 2, c, c), F32)],
        scratch_shapes=[pltpu.VMEM((nb, 2, c, c), F32)],
        compiler_params=_cparams(("arbitrary", "arbitrary")),
        name="rwkv_scan",
    )(r, r, v, v, kk, kk, lw, lw, kd, kd, bdir, bdir, s0, *consts)


def _blockdiag2(w):
    k, n = w.shape[1:]
    z = jnp.zeros((k, n), w.dtype)
    return jnp.concatenate([jnp.concatenate([w[0], z], axis=1), jnp.concatenate([z, w[1]], axis=1)], axis=0)


def _mla_spread(w, lanes):
    k = w.shape[0]
    w = w.reshape(k, -1, len(lanes))
    return jnp.zeros((k, w.shape[1], LANES), w.dtype).at[:, :, lanes].set(w).reshape(k, -1)


def kernel(x, c, ctx, c_ctx, ada_w, ada_b, norm_ffn1, norm_mix, norm_ffn2, ffn1_gate, ffn1_up, ffn1_down, ffn2_gate, ffn2_up, ffn2_down, w_in, w_out, hy_conv_w, hy_conv_b, hy_f_w1, hy_f_b1, hy_f_w2, hy_f_b2, hy_f_w3, hy_f_b3, hy_f_w4, hy_f_freq, hy_bias, gqa_q_norm, gqa_k_norm, mla_cq_norm, mla_ckv_norm, mla_w_uq, mla_w_ukv, mla_q_norm, mla_k_norm, rw_mu, rw_w0, rw_w2, rw_a0, rw_a2, rw_g2, rw_k_k, rw_k_a, rw_r_k, rw_ln_w, rw_ln_b):
    b, length, d = x.shape
    lc = ctx.shape[1]
    depth = ada_w.shape[0]

    c_all = jnp.zeros((16, d), F32).at[:b].set(c).at[b].set(c_ctx)
    mod = _compute_mod(c_all, ada_w, ada_b)

    xc = ctx.reshape(1, b * lc, d)
    for l in range(depth):
        ctx_out = l < depth - 1
        mod_x = mod[l, :b].reshape(b, N_MOD, d)
        mod_c = mod[l, b:b + 1].reshape(1, N_MOD, d)
        row = lambda a: a[l].reshape(1, -1)

        wg1, wu1, wd1 = (w[l].astype(BF16) for w in (ffn1_gate, ffn1_up, ffn1_down))
        wg2, wu2, wd2 = (w[l].astype(BF16) for w in (ffn2_gate, ffn2_up, ffn2_down))
        wi = w_in[l]
        o1, o2, o3 = HY_COLS, HY_COLS + GQA_COLS, HY_COLS + GQA_COLS + MLA_COLS
        ml_lanes = _mla_lanes()
        o_kr = o3 - MLA_ROPE
        wi = jnp.concatenate([wi[:, :o_kr], _mla_spread(wi[:, o_kr:o3], ml_lanes[MLA_NOPE:]), wi[:, o3:]],
                             axis=1).astype(BF16)
        wo = w_out[l].astype(BF16)

        x = _ffn(x, mod_x, row(norm_ffn1), wg1, wu1, wd1, 0)
        xc = _ffn(xc, mod_c, row(norm_ffn1), wg1, wu1, wd1, 0)

        hy_x, gq_x, ml_x, rw_x = _inproj(x, mod_x, row(norm_mix), wi)
        hy_c, gq_c, ml_c, rw_c = (t.reshape(b, lc, -1) for t in _inproj(xc, mod_c, row(norm_mix), wi))

        w1p = jnp.zeros((LANES, HY_ORDER), F32).at[:HY_EMB].set(hy_f_w1[l])
        filt = (w1p, row(hy_f_b1), hy_f_w2[l], row(hy_f_b2), hy_f_w3[l], row(hy_f_b3), hy_f_w4[l],
                row(hy_f_freq))
        spec_x = _hy_spectrum(_hy_filters(length, *filt), row(hy_bias))
        y_hy_x = _hy_conv(hy_x, hy_conv_w[l], hy_conv_b[l], spec_x)

        gq = jnp.tile(row(gqa_q_norm), (1, 2))
        gk = jnp.tile(row(gqa_k_norm), (1, 2))
        q_l, k_l, v_l = _gqa_prep(gq_x, gq, gk, True)
        q_c, k_c, v_c = _gqa_prep(gq_c, gq, gk, False)
        y_gq_x = _attention(q_l, [(k_c, v_c), (k_l, v_l)], GQA_KV_HEADS)

        wuq = _mla_spread(mla_w_uq[l], ml_lanes).astype(BF16)
        wukv = mla_w_ukv[l].reshape(MLA_KV_RANK, MLA_HEADS, MLA_NOPE + MLA_V)
        wuk = _mla_spread(wukv[:, :, :MLA_NOPE].reshape(MLA_KV_RANK, -1), ml_lanes[:MLA_NOPE]).astype(BF16)
        wuv = wukv[:, :, MLA_NOPE:].reshape(MLA_KV_RANK, -1).astype(BF16)
        pad_n = lambda g: _mla_spread(g[l].reshape(1, MLA_QK), ml_lanes)
        mla_w = (row(mla_cq_norm), row(mla_ckv_norm), wuq, wuk, wuv, pad_n(mla_q_norm), pad_n(mla_k_norm))
        mq_l, mk_l, mv_l = _mla_prep(ml_x, *mla_w, True)
        mq_c, mk_c, mv_c = _mla_prep(ml_c, *mla_w, False)
        y_ml_x = _attention(mq_l, [(mk_c, mv_c), (mk_l, mv_l)], MLA_HEADS)

        rw_w = (row(rw_mu), row(rw_k_k), row(rw_k_a), rw_r_k[l].reshape(1, RW_C),
                rw_w0[l].reshape(1, 2 * RW_C), _blockdiag2(rw_w2[l]).astype(BF16),
                rw_a0[l].reshape(1, 2 * RW_C), _blockdiag2(rw_a2[l]).astype(BF16), rw_g2[l].astype(BF16))
        pc = _rw_prep(rw_c, *rw_w)
        px = _rw_prep(rw_x, *rw_w)
        zeros = jnp.zeros((b, 2, RW_C, RW_C), F32)
        yf_c, yb_c, s_ctx = _rw_scan(*pc[:6], zeros)
        yf_x, yb_x, _ = _rw_scan(*px[:6], s_ctx)

        tail = (row(rw_ln_w), row(rw_ln_b), wo, row(norm_ffn2), wg2, wu2, wd2)
        x = _mix_ffn(x, mod_x, (y_hy_x, y_gq_x, y_ml_x), (yf_x, yb_x, px[6], px[7]), *tail)
        if ctx_out:
            spec_c = _hy_spectrum(_hy_filters(lc, *filt), row(hy_bias))
            y_hy_c = _hy_conv(hy_c, hy_conv_w[l], hy_conv_b[l], spec_c)
            y_gq_c = _attention(q_c, [(k_c, v_c)], GQA_KV_HEADS)
            y_ml_c = _attention(mq_c, [(mk_c, mv_c)], MLA_HEADS)
            flat = lambda ts: tuple(t.reshape(1, b * lc, -1) for t in ts)
            xc = _mix_ffn(xc, mod_c, flat((y_hy_c, y_gq_c, y_ml_c)), flat((yf_c, yb_c, pc[6], pc[7])), *tail)
    return x
```

```python
import functools
import math

import numpy as np
import jax
import jax.numpy as jnp
from jax import lax
from jax.experimental import pallas as pl
from jax.experimental.pallas import tpu as pltpu

F32 = jnp.float32
BF16 = jnp.bfloat16
HI = lax.Precision.HIGHEST

GRID_W = 64
HEAD_DIM = 64
N_MOD = 9
NORM_EPS = 1e-6
LOG2E = math.log2(math.e)
ROPE_THETA = 10000.0

HY_CH = 256
HY_EMB = 33
HY_ORDER = 64
HY_FAST_PCT = 0.3
HY_SLOW_PCT = 1.5
HY_TARGET = 1e-2

GQA_HEADS = 4
GQA_KV_HEADS = 2
MLA_HEADS = 4
MLA_NOPE = 64
MLA_ROPE = 32
MLA_QK = MLA_NOPE + MLA_ROPE
MLA_V = 64
MLA_Q_RANK = 256
MLA_KV_RANK = 128

RW_HEADS = 4
RW_N = 64
RW_C = RW_HEADS * RW_N
RW_DECAY_LORA = 64
RW_AAA_LORA = 64
RW_GATE_LORA = 128
RW_GN_EPS = 64e-5

HY_COLS = 3 * HY_CH
GQA_COLS = (GQA_HEADS + 2 * GQA_KV_HEADS) * HEAD_DIM
MLA_COLS = MLA_Q_RANK + MLA_KV_RANK + MLA_ROPE
MLA_COLS_PAD = 512
RW_COLS = 3 * RW_C + 2 * RW_DECAY_LORA + 2 * RW_AAA_LORA + RW_GATE_LORA
D_IN_PAD = HY_COLS + GQA_COLS + MLA_COLS_PAD + RW_COLS

LANES = 128
ROW_TILE = 512
PREP_TILE = 1024
Q_TILE = 256
ATTN_HEADS = 4
KEY_CHUNK = 256
VT_ROWS = 2 * HEAD_DIM
RW_CHUNK = 64
RW_BASE = 4
RW_LEVELS = 4
RW_BLOCK = 256
RW_SEQS = 4
VMEM_LIMIT = 56 * 2 ** 20


def _cparams(sem):
    return pltpu.CompilerParams(dimension_semantics=sem, vmem_limit_bytes=VMEM_LIMIT)


def _const_spec(shape):
    nd = len(shape)
    return pl.BlockSpec(shape, lambda *_: (0,) * nd, pipeline_mode=pl.Buffered(1))


def _dot(a, b, precision=None):
    return jnp.dot(a, b, preferred_element_type=F32, precision=precision)


def _dot_nt(a, b, precision=None):
    return lax.dot_general(a, b, (((1,), (1,)), ((), ())), preferred_element_type=F32,
                           precision=precision)


def _dot_tn(a, b, precision=None):
    return lax.dot_general(a, b, (((0,), (0,)), ((), ())), preferred_element_type=F32,
                           precision=precision)


def _pieces(x, n):
    out = []
    for _ in range(n):
        p = x.astype(BF16)
        out.append(p)
        x = x - p.astype(F32)
    return out


def _segsum(x, seg):
    return functools.reduce(jnp.add, [_dot(p, seg) for p in _pieces(x, 2)])


def _sigmoid(x):
    return 1.0 / (1.0 + jnp.exp(-x))


def _adaln(x, g, mod_ref, i):
    shift = mod_ref[pl.ds(3 * i, 1), :]
    scale = mod_ref[pl.ds(3 * i + 1, 1), :]
    r = lax.rsqrt(jnp.mean(x * x, axis=-1, keepdims=True) + NORM_EPS)
    return (x * r) * (g * (1.0 + scale)) + shift


def _mod_kernel(c_ref, w_ref, b_ref, o_ref):
    c = c_ref[...]
    s = c * _sigmoid(c)
    o_ref[...] = _dot(s, w_ref[...], HI) + b_ref[...]


def _compute_mod(c_all, ada_w, ada_b):
    depth, d, n = ada_w.shape
    rows = c_all.shape[0]
    tn = 1024
    return pl.pallas_call(
        _mod_kernel,
        grid=(depth, n // tn),
        in_specs=[
            pl.BlockSpec((rows, d), lambda l, j: (0, 0)),
            pl.BlockSpec((None, d, tn), lambda l, j: (l, 0, j)),
            pl.BlockSpec((None, 1, tn), lambda l, j: (l, 0, j)),
        ],
        out_specs=pl.BlockSpec((None, rows, tn), lambda l, j: (l, 0, j)),
        out_shape=jax.ShapeDtypeStruct((depth, rows, n), F32),
        compiler_params=_cparams(("arbitrary", "arbitrary")),
        name="mod",
    )(c_all, ada_w, ada_b.reshape(depth, 1, n))


def _ffn_kernel(x_ref, mod_ref, g_ref, wg_ref, wu_ref, wd_ref, o_ref, *, sub):
    x = x_ref[...]
    h = _adaln(x, g_ref[...], mod_ref, sub).astype(BF16)
    a = _dot(h, wg_ref[...])
    u = _dot(h, wu_ref[...])
    z = (a * _sigmoid(a) * u).astype(BF16)
    y = _dot(z, wd_ref[...])
    gate = mod_ref[pl.ds(3 * sub + 2, 1), :]
    o_ref[...] = x + (0.5 * gate) * y


def _ffn(x, mod, g, wg, wu, wd, sub):
    bm, rows, d = x.shape
    tm = min(ROW_TILE, rows)
    f = wg.shape[1]
    return pl.pallas_call(
        functools.partial(_ffn_kernel, sub=sub),
        grid=(bm, rows // tm),
        in_specs=[
            pl.BlockSpec((None, tm, d), lambda b, i: (b, i, 0)),
            pl.BlockSpec((None, N_MOD, d), lambda b, i: (b, 0, 0)),
            _const_spec((1, d)),
            _const_spec((d, f)),
            _const_spec((d, f)),
            _const_spec((f, d)),
        ],
        out_specs=pl.BlockSpec((None, tm, d), lambda b, i: (b, i, 0)),
        out_shape=jax.ShapeDtypeStruct(x.shape, F32),
        compiler_params=_cparams(("arbitrary", "arbitrary")),
        name="ffn",
    )(x, mod, g, wg, wu, wd)


_IN_OFFS = (0, HY_COLS, HY_COLS + GQA_COLS, HY_COLS + GQA_COLS + MLA_COLS_PAD, D_IN_PAD)


def _inproj_kernel(x_ref, mod_ref, g_ref, w_ref, hy_ref, gq_ref, ml_ref, rw_ref):
    h = _adaln(x_ref[...], g_ref[...], mod_ref, 1).astype(BF16)
    p = _dot(h, w_ref[...])
    for ref, lo, hi in zip((hy_ref, gq_ref, ml_ref, rw_ref), _IN_OFFS[:-1], _IN_OFFS[1:]):
        ref[...] = p[:, lo:hi]


def _inproj(x, mod, g, w):
    bm, rows, d = x.shape
    tm = min(ROW_TILE, rows)
    widths = [hi - lo for lo, hi in zip(_IN_OFFS[:-1], _IN_OFFS[1:])]
    return pl.pallas_call(
        _inproj_kernel,
        grid=(bm, rows // tm),
        in_specs=[
            pl.BlockSpec((None, tm, d), lambda b, i: (b, i, 0)),
            pl.BlockSpec((None, N_MOD, d), lambda b, i: (b, 0, 0)),
            _const_spec((1, d)),
            _const_spec((d, D_IN_PAD)),
        ],
        out_specs=[pl.BlockSpec((None, tm, wd), lambda b, i: (b, i, 0)) for wd in widths],
        out_shape=[jax.ShapeDtypeStruct((bm, rows, wd), F32) for wd in widths],
        compiler_params=_cparams(("arbitrary", "arbitrary")),
        name="inproj",
    )(x, mod, g, w)


def _mix_ffn_kernel(x_ref, mod_ref, yh_ref, yg_ref, ym_ref, yf_ref, yb_ref, bonus_ref, gate_ref,
                    lnw_ref, lnb_ref, seg_ref, wo_ref, g_ref, wg_ref, wu_ref, wd_ref, o_ref):
    y = yf_ref[...] + yb_ref[...]
    seg = seg_ref[...]
    yc = y - _segsum(y, seg)
    var = _segsum(yc * yc, seg)
    yn = yc * lax.rsqrt(var + RW_GN_EPS) * lnw_ref[...] + lnb_ref[...]
    y_rw = ((yn + bonus_ref[...]) * gate_ref[...]).astype(BF16)
    ycat = jnp.concatenate([yh_ref[...], yg_ref[...], ym_ref[...], y_rw], axis=-1)
    x = x_ref[...] + mod_ref[pl.ds(5, 1), :] * _dot(ycat, wo_ref[...])
    h = _adaln(x, g_ref[...], mod_ref, 2).astype(BF16)
    a = _dot(h, wg_ref[...])
    u = _dot(h, wu_ref[...])
    z = (a * _sigmoid(a) * u).astype(BF16)
    o_ref[...] = x + (0.5 * mod_ref[pl.ds(8, 1), :]) * _dot(z, wd_ref[...])


def _mix_ffn(x, mod, ys, rw, ln_w, ln_b, wo, g, wg, wu, wd):
    bm, rows, d = x.shape
    tm = min(ROW_TILE, rows)
    c = RW_C
    seg = jnp.asarray(_seg_matrix(c, RW_N, 1.0 / RW_N), BF16)
    row_spec = pl.BlockSpec((None, tm, d), lambda b, i: (b, i, 0))
    y_spec = pl.BlockSpec((None, tm, c), lambda b, i: (b, i, 0))
    consts = (ln_w, ln_b, seg, wo, g, wg, wu, wd)
    return pl.pallas_call(
        _mix_ffn_kernel,
        grid=(bm, rows // tm),
        in_specs=[row_spec, pl.BlockSpec((None, N_MOD, d), lambda b, i: (b, 0, 0))] + [y_spec] * 7
                 + [_const_spec(a.shape) for a in consts],
        out_specs=row_spec,
        out_shape=jax.ShapeDtypeStruct(x.shape, F32),
        compiler_params=_cparams(("arbitrary", "arbitrary")),
        name="mix_ffn",
    )(x, mod, *ys, *rw, *consts)


def _hy_features(length):
    t01 = np.linspace(0.0, 1.0, length, dtype=np.float32)[:, None]
    bands = (HY_EMB - 1) // 2
    w_ang = (np.float32(2.0 * math.pi) * np.arange(length, dtype=np.float32)[:, None]
             / np.float32(length)).astype(np.float32)
    f = np.linspace(1e-4, bands - 1, bands, dtype=np.float32)[None]
    arg = (f * w_ang).astype(np.float32)
    z = np.concatenate([t01, np.cos(arg), -np.sin(arg)], axis=-1).astype(np.float32)
    zp = np.zeros((length, LANES), np.float32)
    zp[:, :HY_EMB] = z
    return zp


def _hy_deltas():
    max_decay = math.log(HY_TARGET) / HY_FAST_PCT
    min_decay = math.log(HY_TARGET) / HY_SLOW_PCT
    d = np.abs(np.linspace(min_decay, max_decay, HY_CH, dtype=np.float32))
    return np.tile(d, 2)[None].astype(np.float32)


def _hyfilt_kernel(z_ref, w1_ref, b1_ref, w2_ref, b2_ref, w3_ref, b3_ref, w4_ref, fr_ref,
                   dl_ref, o_ref):
    z = z_ref[...]
    fr = fr_ref[...]
    h = jnp.sin(fr * (_dot(z, w1_ref[...], HI) + b1_ref[...]))
    h = jnp.sin(fr * (_dot(h, w2_ref[...], HI) + b2_ref[...]))
    h = jnp.sin(fr * (_dot(h, w3_ref[...], HI) + b3_ref[...]))
    h = _dot(h, w4_ref[...], HI)
    o_ref[...] = h * jnp.exp(-z[:, 0:1] * dl_ref[...])


def _hy_filters(length, w1p, b1, w2, b2, w3, b3, w4, freq):
    z = jnp.asarray(_hy_features(length))
    dl = jnp.asarray(_hy_deltas())
    tl = min(ROW_TILE, length)
    consts = (w1p, b1, w2, b2, w3, b3, w4, freq, dl)
    return pl.pallas_call(
        _hyfilt_kernel,
        grid=(length // tl,),
        in_specs=[pl.BlockSpec((tl, LANES), lambda i: (i, 0))] + [_const_spec(a.shape) for a in consts],
        out_specs=pl.BlockSpec((tl, 2 * HY_CH), lambda i: (i, 0)),
        out_shape=jax.ShapeDtypeStruct((length, 2 * HY_CH), F32),
        compiler_params=_cparams(("arbitrary",)),
        name="hyena_filters",
    )(z, *consts)


FFT_SLAB_PAD = 8
FFT_GROUP = 16
FFT_SLAB_GROUP = 12


def _fft_factors(length):
    return (128, 64) if length >= 2048 else (2 * length, 1)


def _fft_layout(n1, n2):
    if n2 == 1:
        return 1, n1, 2 * n1
    kept = n1 // 2 + 1
    return -(-kept // FFT_SLAB_GROUP) * FFT_SLAB_GROUP, n2, 2 * n2 + FFT_SLAB_PAD


def _stack3(t):
    hi = t.astype(jnp.bfloat16)
    lo = (t - hi.astype(np.float64)).astype(jnp.bfloat16)
    return np.concatenate([hi, lo, hi], axis=-1)


def _rows3(d):
    hi = d.astype(BF16)
    lo = (d - hi.astype(F32)).astype(BF16)
    return jnp.concatenate([hi, hi, lo], axis=0)


@functools.lru_cache(maxsize=None)
def _fft_tables(length):
    n1, n2 = _fft_factors(length)
    n = n1 * n2
    assert n == 2 * length
    a_n2 = np.arange(n2)[:, None, None]
    a_k1 = np.arange(n1)[None, :, None]
    a_n1 = np.arange(n1 // 2)[None, None, :]
    ang = 2.0 * np.pi * ((a_n1 * a_k1 % n1) / n1 + (a_n2 * a_k1 % n) / n)
    t_re, t_im = np.cos(ang), -np.sin(ang)
    wgt = np.ones(n1)
    if n2 > 1:
        nk = _fft_layout(n1, n2)[0]
        keep = np.arange(nk) <= n1 // 2
        t_re, t_im = (np.where(keep[None, :, None], t[:, :nk], 0.0) for t in (t_re, t_im))
        wgt = np.where(np.isin(np.arange(nk), (0, n1 // 2)), 1.0, 2.0) * keep
    ta = np.concatenate([t_re, t_im], axis=1)
    t_re_w, t_im_w = t_re * wgt[None, :, None], t_im * wgt[None, :, None]
    tai = np.concatenate([np.transpose(t_re_w, (0, 2, 1)), np.transpose(t_im_w, (0, 2, 1))], axis=2) / n
    jj = np.arange(n2)
    ang2 = 2.0 * np.pi * (np.outer(jj, jj) % n2) / n2
    c, s = np.cos(ang2), np.sin(ang2)
    fb = np.block([[c, s], [-s, c]])
    fbi = np.block([[c, -s], [s, c]])
    return tuple(_stack3(t) for t in (ta, tai, fb, fbi))


def _fft_stage_a(src_ref, fbuf_ref, ta_ref, n1, n2):
    if n2 == 1:
        fbuf_ref[...] = _dot(ta_ref[0], _rows3(src_ref[...]))
        return
    nk, _, pitch = _fft_layout(n1, n2)

    def body(i, carry):
        js = [i * FFT_GROUP + g for g in range(FFT_GROUP)]
        rows = [_rows3(src_ref[pl.ds(j, n1 // 2, stride=n2), :]) for j in js]
        ts = [_dot(ta_ref[j], r) for j, r in zip(js, rows)]
        for j, t in zip(js, ts):
            fbuf_ref[pl.ds(j, nk, stride=pitch), :] = t[:nk]
            fbuf_ref[pl.ds(n2 + j, nk, stride=pitch), :] = t[nk:]
        return carry

    lax.fori_loop(0, n2 // FFT_GROUP, body, 0)


def _spec_kernel(hf_ref, hb_ref, bias_ref, ta_ref, fb_ref, o_ref, fbuf_ref, *, n1, n2):
    nslab, hs, pitch = _fft_layout(n1, n2)
    if pitch > 2 * hs:
        o_ref[...] = jnp.zeros(o_ref.shape, F32)
    _fft_stage_a(hf_ref, o_ref, ta_ref, n1, n2)
    _fft_stage_a(hb_ref, fbuf_ref, ta_ref, n1, n2)
    is_re = lax.broadcasted_iota(jnp.int32, (2 * hs, 1), 0) < hs
    sgn = jnp.where(is_re, 1.0, -1.0).astype(F32)
    skip = jnp.where(is_re, bias_ref[...], 0.0)

    grp = FFT_SLAB_GROUP if n2 > 1 else 1

    def stage_b(i, carry):
        sls = [pl.ds(pl.multiple_of((i * grp + g) * pitch, 8), 2 * hs) for g in range(grp)]
        a = [o_ref[sl, :] for sl in sls]
        b = [fbuf_ref[sl, :] for sl in sls]
        if n2 > 1:
            a = [_dot(fb_ref[...], _rows3(x)) for x in a]
            b = [_dot(fb_ref[...], _rows3(x)) for x in b]
        for sl, x, y in zip(sls, a, b):
            o_ref[sl, :] = x + sgn * y + skip
        return carry

    lax.fori_loop(0, nslab // grp, stage_b, 0)


def _hy_spectrum(h, bias):
    length = h.shape[0]
    n1, n2 = _fft_factors(length)
    nslab, _, pitch = _fft_layout(n1, n2)
    ta, _, fb, _ = (jnp.asarray(t) for t in _fft_tables(length))
    nblk = HY_CH // LANES
    return pl.pallas_call(
        functools.partial(_spec_kernel, n1=n1, n2=n2),
        grid=(nblk,),
        in_specs=[pl.BlockSpec((length, LANES), lambda j: (0, j)),
                  pl.BlockSpec((length, LANES), lambda j: (0, nblk + j)),
                  pl.BlockSpec((1, LANES), lambda j: (0, j)),
                  _const_spec(ta.shape), _const_spec(fb.shape)],
        out_specs=pl.BlockSpec((nslab * pitch, LANES), lambda j: (0, j)),
        out_shape=jax.ShapeDtypeStruct((nslab * pitch, HY_CH), F32),
        scratch_shapes=[pltpu.VMEM((nslab * pitch, LANES), F32)],
        compiler_params=_cparams(("arbitrary",)),
        name="hyena_spectrum",
    )(h, h, bias, ta, fb)


def _shift_rows(x, prev_row, next_row):
    n = x.shape[0]
    row = lax.broadcasted_iota(jnp.int32, x.shape, 0)
    up = jnp.where(row == 0, prev_row, pltpu.roll(x, 1, 0))
    dn = jnp.where(row == n - 1, next_row, pltpu.roll(x, n - 1, 0))
    return up, dn


def _hyconv_kernel(x1_ref, x2_ref, v_ref, cw1_ref, cw2_ref, cwv_ref, cb_ref, g_ref,
                   ta_ref, tai_ref, fb_ref, fbi_ref, o_ref, u_ref, fbuf_ref, *, n1, n2):
    def conv3(ref, w_ref, b):
        x = ref[...]
        up, dn = _shift_rows(x, 0.0, 0.0)
        return up * w_ref[0:1, :] + x * w_ref[1:2, :] + dn * w_ref[2:3, :] + b

    cb = cb_ref[...]
    u_ref[...] = conv3(x1_ref, cw1_ref, cb[0:1, :]) * conv3(v_ref, cwv_ref, cb[2:3, :])

    _fft_stage_a(u_ref, fbuf_ref, ta_ref, n1, n2)
    nslab, hs, pitch = _fft_layout(n1, n2)

    grp = FFT_SLAB_GROUP if n2 > 1 else 1

    def cmul(a, g):
        ar, ai, gr, gi = a[:hs], a[hs:], g[:hs], g[hs:]
        return jnp.concatenate([ar * gr - ai * gi, ar * gi + ai * gr], axis=0)

    def stage_b(i, carry):
        sls = [pl.ds(pl.multiple_of((i * grp + g) * pitch, 8), 2 * hs) for g in range(grp)]
        pairs = [sls[g:g + 2] for g in range(0, grp, 2)]
        a = [jnp.concatenate([fbuf_ref[sl, :] for sl in pr], axis=1) for pr in pairs]
        if n2 > 1:
            a = [_dot(fb_ref[...], _rows3(x)) for x in a]
        a = [cmul(x, jnp.concatenate([g_ref[sl, :] for sl in pr], axis=1)) for x, pr in zip(a, pairs)]
        if n2 > 1:
            a = [_dot(fbi_ref[...], _rows3(x)) for x in a]
        for pr, x in zip(pairs, a):
            for g, sl in enumerate(pr):
                fbuf_ref[sl, :] = x[:, g * LANES:(g + 1) * LANES]
        return carry

    lax.fori_loop(0, nslab // grp, stage_b, 0)

    if n2 == 1:
        u_ref[...] = _dot(tai_ref[0], _rows3(fbuf_ref[...]))
    else:
        def stage_a_inv(i, carry):
            js = [i * FFT_GROUP + g for g in range(FFT_GROUP)]
            cs = [_rows3(jnp.concatenate([fbuf_ref[pl.ds(j, nslab, stride=pitch), :],
                                          fbuf_ref[pl.ds(n2 + j, nslab, stride=pitch), :]], axis=0))
                  for j in js]
            ys = [_dot(tai_ref[j], c) for j, c in zip(js, cs)]
            for j, y in zip(js, ys):
                u_ref[pl.ds(j, n1 // 2, stride=n2), :] = y
            return carry

        lax.fori_loop(0, n2 // FFT_GROUP, stage_a_inv, 0)

    o_ref[...] = (conv3(x2_ref, cw2_ref, cb[1:2, :]) * u_ref[...]).astype(o_ref.dtype)


def _hy_conv(p, conv_w, conv_b, spec):
    b, length, _ = p.shape
    n1, n2 = _fft_factors(length)
    nslab, _, pitch = _fft_layout(n1, n2)
    ta, tai, fb, fbi = (jnp.asarray(t) for t in _fft_tables(length))
    nblk = HY_CH // LANES
    cb3 = conv_b.reshape(3, HY_CH)
    col = lambda g: pl.BlockSpec((None, length, LANES), lambda j, i, g=g: (i, 0, g * nblk + j))
    wcol = lambda g: pl.BlockSpec((3, LANES), lambda j, i, g=g: (0, g * nblk + j))
    return pl.pallas_call(
        functools.partial(_hyconv_kernel, n1=n1, n2=n2),
        grid=(nblk, b),
        in_specs=[col(0), col(1), col(2), wcol(0), wcol(1), wcol(2),
                  pl.BlockSpec((3, LANES), lambda j, i: (0, j)),
                  pl.BlockSpec((nslab * pitch, LANES), lambda j, i: (0, j), pipeline_mode=pl.Buffered(1)),
                  _const_spec(ta.shape), _const_spec(tai.shape), _const_spec(fb.shape),
                  _const_spec(fbi.shape)],
        out_specs=pl.BlockSpec((None, length, LANES), lambda j, i: (i, 0, j)),
        out_shape=jax.ShapeDtypeStruct((b, length, HY_CH), BF16),
        scratch_shapes=[pltpu.VMEM((length, LANES), F32), pltpu.VMEM((nslab * pitch, LANES), F32)],
        compiler_params=_cparams(("arbitrary", "arbitrary")),
        name="hyena_conv",
    )(p, p, p, conv_w, conv_w, conv_w, cb3, spec, ta, tai, fb, fbi)


def _rope_tables(length, d_rot, lane_lo, head_w):
    rows = length // GRID_W
    row = np.repeat(np.arange(rows, dtype=np.float32), GRID_W)
    colv = np.tile(np.arange(GRID_W, dtype=np.float32), rows)
    n_freq = d_rot // 4
    inv = (np.float32(ROPE_THETA) ** (-np.arange(n_freq, dtype=np.float32) / np.float32(n_freq))).astype(np.float32)
    ang = np.concatenate([row[:, None] * inv, colv[:, None] * inv], axis=-1).astype(np.float32)
    cos_t = np.ones((length, LANES), np.float32)
    sin_t = np.zeros((length, LANES), np.float32)
    c, s = np.cos(ang), np.sin(ang)
    for base in range(0, LANES, head_w):
        for i in range(d_rot // 2):
            cos_t[:, base + lane_lo + 2 * i] = c[:, i]
            cos_t[:, base + lane_lo + 2 * i + 1] = c[:, i]
            sin_t[:, base + lane_lo + 2 * i] = -s[:, i]
            sin_t[:, base + lane_lo + 2 * i + 1] = s[:, i]
    return cos_t, sin_t


def _rope_angles(length, d_rot):
    rows = length // GRID_W
    row = np.repeat(np.arange(rows, dtype=np.float32), GRID_W)
    colv = np.tile(np.arange(GRID_W, dtype=np.float32), rows)
    n_freq = d_rot // 4
    inv = (np.float32(ROPE_THETA) ** (-np.arange(n_freq, dtype=np.float32) / np.float32(n_freq))).astype(np.float32)
    return np.concatenate([row[:, None] * inv, colv[:, None] * inv], axis=-1).astype(np.float32)


def _mla_lanes():
    lanes = np.empty(MLA_QK, np.int64)
    n = np.arange(MLA_NOPE)
    lanes[:MLA_NOPE] = np.where(n < 32, n, n + 16)
    i = np.arange(MLA_ROPE // 2)
    lanes[MLA_NOPE + 2 * i] = 32 + i
    lanes[MLA_NOPE + 2 * i + 1] = 96 + i
    return lanes


def _mla_rope_tables(length):
    ang = _rope_angles(length, MLA_ROPE)
    cos_t = np.ones((length, LANES), np.float32)
    sin_t = np.zeros((length, LANES), np.float32)
    n_pairs = MLA_ROPE // 2
    cos_t[:, 32:32 + n_pairs] = cos_t[:, 96:96 + n_pairs] = np.cos(ang)
    sin_t[:, 32:32 + n_pairs] = -np.sin(ang)
    sin_t[:, 96:96 + n_pairs] = np.sin(ang)
    return cos_t, sin_t


def _rope(x, cos_t, sin_t):
    lane = lax.broadcasted_iota(jnp.int32, x.shape, 1)
    w = x.shape[1]
    partner = jnp.where(jnp.bitwise_and(lane, 1) == 0, pltpu.roll(x, w - 1, 1), pltpu.roll(x, 1, 1))
    return x * cos_t + partner * sin_t


def _store_vt(vt_ref, h, vt):
    row = lax.broadcasted_iota(jnp.int32, (VT_ROWS - HEAD_DIM, vt.shape[1]), 0)
    vt_ref[h, 0:HEAD_DIM, :] = vt.astype(BF16)
    vt_ref[h, HEAD_DIM:VT_ROWS, :] = jnp.where(row == 0, 1.0, 0.0).astype(BF16)


def _gqa_prep_kernel(p_ref, cos_ref, sin_ref, gq_ref, gk_ref, seg_ref, q_ref, k_ref, vt_ref, *, rope):
    p = p_ref[...]
    seg = seg_ref[...]
    lane = lax.broadcasted_iota(jnp.int32, (p.shape[0], LANES), 1)
    low = lane < HEAD_DIM

    def hnorm(x, g):
        ms = _segsum(x * x, seg)
        return x * lax.rsqrt(ms + NORM_EPS) * g

    scale = HEAD_DIM ** -0.5 * LOG2E
    for c in range(2):
        q = hnorm(p[:, c * LANES:(c + 1) * LANES], gq_ref[...])
        if rope:
            q = _rope(q, cos_ref[...], sin_ref[...])
        q = q * scale
        qs = pltpu.roll(q, HEAD_DIM, 1)
        q_ref[:, (2 * c) * LANES:(2 * c + 1) * LANES] = jnp.where(low, q, 0.0).astype(BF16)
        q_ref[:, (2 * c + 1) * LANES:(2 * c + 2) * LANES] = jnp.where(low, qs, 0.0).astype(BF16)
    k = hnorm(p[:, 256:384], gk_ref[...])
    if rope:
        k = _rope(k, cos_ref[...], sin_ref[...])
    ks = pltpu.roll(k, HEAD_DIM, 1)
    k_ref[:, 0:LANES] = jnp.where(low, k, 0.0).astype(BF16)
    k_ref[:, LANES:2 * LANES] = jnp.where(low, ks, 0.0).astype(BF16)
    vt = p[:, 384:512].T
    _store_vt(vt_ref, 0, vt[0:HEAD_DIM])
    _store_vt(vt_ref, 1, vt[HEAD_DIM:2 * HEAD_DIM])


def _seg_matrix(width, seg, value):
    i = np.arange(width)
    return ((i[:, None] // seg) == (i[None, :] // seg)).astype(np.float32) * np.float32(value)


def _gqa_prep(p, gq, gk, rope):
    b, length, _ = p.shape
    tm = min(PREP_TILE, length)
    if rope:
        cos_t, sin_t = (jnp.asarray(t) for t in _rope_tables(length, HEAD_DIM, 0, HEAD_DIM))
    else:
        cos_t = sin_t = jnp.zeros((length, LANES), F32)
    seg = jnp.asarray(_seg_matrix(LANES, HEAD_DIM, 1.0 / HEAD_DIM), BF16)
    tab = pl.BlockSpec((tm, LANES), lambda i, bb: (i, 0))
    outw = (512, 256)
    return pl.pallas_call(
        functools.partial(_gqa_prep_kernel, rope=rope),
        grid=(length // tm, b),
        in_specs=[pl.BlockSpec((None, tm, GQA_COLS), lambda i, bb: (bb, i, 0)), tab, tab,
                  _const_spec((1, LANES)), _const_spec((1, LANES)), _const_spec((LANES, LANES))],
        out_specs=[pl.BlockSpec((None, tm, w), lambda i, bb: (bb, i, 0)) for w in outw]
                  + [pl.BlockSpec((None, GQA_KV_HEADS, VT_ROWS, tm), lambda i, bb: (bb, 0, 0, i))],
        out_shape=[jax.ShapeDtypeStruct((b, length, w), BF16) for w in outw]
                  + [jax.ShapeDtypeStruct((b, GQA_KV_HEADS, VT_ROWS, length), BF16)],
        compiler_params=_cparams(("arbitrary", "arbitrary")),
        name="gqa_prep",
    )(p, cos_t, sin_t, gq, gk, seg)


def _mla_prep_kernel(p_ref, cos_ref, sin_ref, cqn_ref, ckvn_ref, wuq_ref, wuk_ref, wuv_ref,
                     qn_ref, kn_ref, ones_ref, q_ref, k_ref, vt_ref, *, rope):
    p = p_ref[...]

    def rms(x, g, width):
        ms = _segsum(x * x, ones_ref[0:x.shape[1], 0:x.shape[1]]) * (1.0 / width)
        return x * lax.rsqrt(ms + NORM_EPS) * g

    cq = rms(p[:, 0:MLA_Q_RANK], cqn_ref[...], MLA_Q_RANK).astype(BF16)
    ckv = rms(p[:, MLA_Q_RANK:MLA_Q_RANK + MLA_KV_RANK], ckvn_ref[...], MLA_KV_RANK).astype(BF16)
    q = _dot(cq, wuq_ref[...])
    kn = _dot(ckv, wuk_ref[...])
    v = _dot(ckv, wuv_ref[...])
    for c in range(MLA_HEADS // 2):
        vt = v[:, c * LANES:(c + 1) * LANES].T
        _store_vt(vt_ref, 2 * c, vt[0:MLA_V])
        _store_vt(vt_ref, 2 * c + 1, vt[MLA_V:2 * MLA_V])
    kr = p[:, 384:512]
    scale = MLA_QK ** -0.5 * LOG2E

    def rot(x):
        return x * cos_ref[...] + pltpu.roll(x, LANES // 2, 1) * sin_ref[...]

    for h in range(MLA_HEADS):
        sl = slice(h * LANES, (h + 1) * LANES)
        qh = rms(q[:, sl], qn_ref[...], MLA_QK)
        kh = rms(kn[:, sl] + kr, kn_ref[...], MLA_QK)
        if rope:
            qh, kh = rot(qh), rot(kh)
        q_ref[:, sl] = (qh * scale).astype(BF16)
        k_ref[:, sl] = kh.astype(BF16)


def _mla_prep(p, cqn, ckvn, wuq, wuk, wuv, qn, kn, rope):
    b, length, _ = p.shape
    tm = min(PREP_TILE, length)
    if rope:
        cos_t, sin_t = (jnp.asarray(t) for t in _mla_rope_tables(length))
    else:
        cos_t = sin_t = jnp.zeros((length, LANES), F32)
    tab = pl.BlockSpec((tm, LANES), lambda i, bb: (i, 0))
    ones = jnp.ones((MLA_Q_RANK, MLA_Q_RANK), BF16)
    consts = (cqn, ckvn, wuq, wuk, wuv, qn, kn, ones)
    outw = (512, 512)
    return pl.pallas_call(
        functools.partial(_mla_prep_kernel, rope=rope),
        grid=(length // tm, b),
        in_specs=[pl.BlockSpec((None, tm, MLA_COLS_PAD), lambda i, bb: (bb, i, 0)), tab, tab]
                 + [_const_spec(a.shape) for a in consts],
        out_specs=[pl.BlockSpec((None, tm, w), lambda i, bb: (bb, i, 0)) for w in outw]
                  + [pl.BlockSpec((None, MLA_HEADS, VT_ROWS, tm), lambda i, bb: (bb, 0, 0, i))],
        out_shape=[jax.ShapeDtypeStruct((b, length, w), BF16) for w in outw]
                  + [jax.ShapeDtypeStruct((b, MLA_HEADS, VT_ROWS, length), BF16)],
        compiler_params=_cparams(("arbitrary", "arbitrary")),
        name="mla_prep",
    )(p, cos_t, sin_t, *consts)


def _attn_kernel(*refs, n_seg):
    nh = ATTN_HEADS
    q_ref, o_ref = refs[0], refs[-1]
    segs = [refs[1 + 2 * nh * s:1 + 2 * nh * (s + 1)] for s in range(n_seg)]
    q = q_ref[...]
    qh = [q[:, h * LANES:(h + 1) * LANES] for h in range(nh)]
    pieces = [(h, seg, c0, min(KEY_CHUNK, seg[0].shape[0]))
              for h in range(nh) for seg in segs
              for c0 in range(0, seg[0].shape[0], min(KEY_CHUNK, seg[0].shape[0]))]
    scores = [_dot_nt(seg[h][c0:c0 + n, :], qh[h]) for h, seg, c0, n in pieces]
    mx = [jnp.max(s, axis=0, keepdims=True) for s in scores]
    ps = [jnp.exp2(s - m).astype(BF16) for s, m in zip(scores, mx)]
    os_ = [_dot(seg[nh + h][:, c0:c0 + n], e) for (h, seg, c0, n), e in zip(pieces, ps)]
    outs = []
    for h in range(nh):
        idx = [i for i, pc in enumerate(pieces) if pc[0] == h]
        m = functools.reduce(jnp.maximum, [mx[i] for i in idx])
        o = functools.reduce(jnp.add, [os_[i] * jnp.exp2(mx[i] - m) for i in idx])
        outs.append(o[0:HEAD_DIM] / o[HEAD_DIM:HEAD_DIM + 1])
    o_ref[...] = jnp.concatenate(outs, axis=0).T.astype(o_ref.dtype)


def _attention(q, kv_segs, k_heads):
    b, lq, _ = q.shape
    tq = min(Q_TILE, lq)
    nh = ATTN_HEADS
    kv_of = [h * k_heads // nh for h in range(nh)]
    in_specs = [pl.BlockSpec((None, tq, nh * LANES), lambda bb, i: (bb, i, 0))]
    args = [q]
    for k, vt in kv_segs:
        lk = k.shape[1]
        in_specs += [pl.BlockSpec((None, lk, LANES), lambda bb, i, j=j: (bb, 0, j)) for j in kv_of]
        in_specs += [pl.BlockSpec((None, None, VT_ROWS, lk), lambda bb, i, j=j: (bb, j, 0, 0)) for j in kv_of]
        args += [k] * nh + [vt] * nh
    return pl.pallas_call(
        functools.partial(_attn_kernel, n_seg=len(kv_segs)),
        grid=(b, lq // tq),
        in_specs=in_specs,
        out_specs=pl.BlockSpec((None, tq, nh * HEAD_DIM), lambda bb, i: (bb, i, 0)),
        out_shape=jax.ShapeDtypeStruct((b, lq, nh * HEAD_DIM), BF16),
        compiler_params=_cparams(("arbitrary", "arbitrary")),
        name="attention",
    )(*args)


def _rw_prep_kernel(p_ref, prev_ref, next_ref, mu_ref, kk_ref_w, ka_ref, rk_ref, w0_ref, w2_ref,
                    a0_ref, a2_ref, g2_ref, seg_ref,
                    r_ref, v_ref, kk_ref, lw_ref, kd_ref, bd_ref, bonus_ref, g_ref):
    i = pl.program_id(1)
    n = pl.num_programs(1)
    x = p_ref[...]
    prev_row = jnp.where(i > 0, prev_ref[7:8, :], 0.0)
    next_row = jnp.where(i < n - 1, next_ref[0:1, :], 0.0)
    up, dn = _shift_rows(x, prev_row, next_row)
    xs = x + mu_ref[...] * (0.5 * (up + dn) - x)
    c = RW_C
    r, k, v = xs[:, 0:c], xs[:, c:2 * c], xs[:, 2 * c:3 * c]
    xw, xa, xg = xs[:, 3 * c:3 * c + 128], xs[:, 3 * c + 128:3 * c + 256], xs[:, 3 * c + 256:3 * c + 384]
    seg = seg_ref[...]
    kk = k * kk_ref_w[...]
    nrm = jnp.sqrt(_segsum(kk * kk, seg))
    kk = kk / jnp.maximum(nrm, 1e-12)
    u = w0_ref[...] + _dot(jnp.tanh(xw).astype(BF16), w2_ref[...])
    z = -u
    softplus = jnp.maximum(z, 0.0) + jnp.log(1.0 + jnp.exp(-jnp.abs(z)))
    lw_ref[...] = -jnp.exp(-softplus - 0.5)
    a = _sigmoid(a0_ref[...] + _dot(xa.astype(BF16), a2_ref[...]))
    k_sum = jnp.zeros_like(r)
    for d in range(2):
        a_d = a[:, d * c:(d + 1) * c]
        k_d = k * (1.0 + (a_d - 1.0) * ka_ref[...])
        kd_ref[:, d * c:(d + 1) * c] = k_d
        bd_ref[:, d * c:(d + 1) * c] = kk * a_d
        k_sum = k_sum + k_d
    bonus = _segsum(r * k_sum * rk_ref[...], seg)
    r_ref[...] = r
    v_ref[...] = v
    kk_ref[...] = kk
    bonus_ref[...] = bonus * v
    g_ref[...] = _dot(_sigmoid(xg).astype(BF16), g2_ref[...])


def _rw_prep(p, mu, k_k, k_a, r_k, w0, w2bd, a0, a2bd, g2):
    b, length, cols = p.shape
    tm = min(PREP_TILE, length)
    nb = tm // 8
    last = length // 8 - 1
    seg = jnp.asarray(_seg_matrix(RW_C, RW_N, 1.0), BF16)
    consts = (mu, k_k, k_a, r_k, w0, w2bd, a0, a2bd, g2, seg)
    outw = (RW_C, RW_C, RW_C, 2 * RW_C, 2 * RW_C, 2 * RW_C, RW_C, RW_C)
    return pl.pallas_call(
        _rw_prep_kernel,
        grid=(b, length // tm),
        in_specs=[pl.BlockSpec((None, tm, cols), lambda bb, i: (bb, i, 0)),
                  pl.BlockSpec((None, 8, cols), lambda bb, i: (bb, jnp.maximum(i * nb - 1, 0), 0)),
                  pl.BlockSpec((None, 8, cols), lambda bb, i: (bb, jnp.minimum((i + 1) * nb, last), 0))]
                 + [_const_spec(a.shape) for a in consts],
        out_specs=[pl.BlockSpec((None, tm, w), lambda bb, i: (bb, i, 0)) for w in outw],
        out_shape=[jax.ShapeDtypeStruct((b, length, w), F32) for w in outw],
        compiler_params=_cparams(("arbitrary", "arbitrary")),
        name="rwkv_prep",
    )(p, p, p, *consts)


@functools.lru_cache(maxsize=None)
def _rw_masks():
    cs, n = RW_CHUNK, RW_C
    i = np.arange(n)
    same = ((i[:, None] // cs) == (i[None, :] // cs)).astype(np.float32)
    t, s = np.arange(cs)[:, None], (i % cs)[None, :]
    smask = []
    for sgn in (1, -1):
        earlier = (s - t) * sgn < 0
        rows = [earlier, (s - t) * sgn <= 0, earlier & (t // RW_BASE == s // RW_BASE)]
        for k in range(RW_LEVELS):
            size = RW_BASE << k
            rows.append(earlier & (t // (2 * size) == s // (2 * size)) & (t // size != s // size))
        smask.append(np.stack(rows))
    eye = (s == t).astype(np.float32)
    tt = np.arange(cs)
    tri = np.stack([tt[None, :] <= tt[:, None], tt[None, :] >= tt[:, None]]).astype(np.float32)
    return same, np.stack(smask).astype(np.float32), eye, tri


def _rw_scan_kernel(rf_ref, rb_ref, vf_ref, vb_ref, kkf_ref, kkb_ref, lwf_ref, lwb_ref, kdf_ref, kdb_ref,
                    bf_ref, bb_ref, s0_ref, same_ref, smask_ref, eye_ref, tri_ref,
                    yf_ref, yb_ref, sfin_ref, s_ref, *, n_chunks, n_batch):
    cs = RW_CHUNK

    @pl.when(pl.program_id(1) == 0)
    def _():
        s_ref[...] = s0_ref[...]

    same_head = same_ref[...].astype(BF16)
    eye = eye_ref[...]
    dirs = ((rf_ref, vf_ref, kkf_ref, lwf_ref, kdf_ref, bf_ref, yf_ref),
            (rb_ref, vb_ref, kkb_ref, lwb_ref, kdb_ref, bb_ref, yb_ref))

    def bd(x):
        xb = x.astype(BF16)
        return jnp.concatenate([xb, xb, xb, xb], axis=0) * same_head

    def halves(t, size, d):
        lo = [t[g:g + size] for g in range(0, t.shape[0], 2 * size)]
        hi = [t[g + size:g + 2 * size] for g in range(0, t.shape[0], 2 * size)]
        return (lo, hi) if d == 0 else (hi, lo)

    def join(earlier, later, d):
        pairs = zip(earlier, later) if d == 0 else zip(later, earlier)
        return jnp.concatenate([x for pr in pairs for x in pr], axis=0)

    streams = [(bi, d) for bi in range(n_batch) for d in range(2)]

    def each(f, *cols):
        return [f(*a) for a in zip(*cols)] if cols else [f(bi, d) for bi, d in streams]

    bf = lambda xs: [x.astype(BF16) for x in xs]
    smask = lambda k: [smask_ref[d, k] for _, d in streams]

    def body(c, carry):
        def load(bi, d):
            cc = c if d == 0 else n_chunks - 1 - c
            sl = pl.ds(pl.multiple_of(cc * cs, cs), cs)
            return [ref[bi, sl, :] for ref in dirs[d][:6]] + [sl]

        r, v, kk, lw, kd, b, sl = zip(*each(load))
        strict, incl = smask(0), smask(1)
        cl = [functools.reduce(jnp.add, [_dot(tri_ref[d], p) for p in _pieces(x, 3)])
              for (_, d), x in zip(streams, lw)]
        w_inv = [jnp.exp(-x) for x in cl]
        w_all = [jnp.exp(jnp.sum(x, axis=0, keepdims=True)) for x in lw]
        kt = each(lambda a, w: a * w, kd, w_inv)
        bt = each(lambda a, w: a * w, b, w_inv)
        qk = each(lambda a, x, y: a * jnp.exp(x - y), kk, cl, lw)
        rt = bf(each(lambda a, x: a * jnp.exp(x), r, cl))
        qk_b, kt_s, bt_s = bf(qk), each(bd, kt), each(bd, bt)
        a_kk = bf(each(lambda q, k_, m: _dot_nt(q, k_) * m, qk_b, kt_s, strict))
        a_kb = each(lambda q, k_, m: _dot_nt(q, k_) * m, qk_b, bt_s, strict)
        a_rk = bf(each(lambda q, k_, m: _dot_nt(q, k_) * m, rt, kt_s, incl))
        a_rb = bf(each(lambda q, k_, m: _dot_nt(q, k_) * m, rt, bt_s, incl))
        l0 = each(lambda a, m: a * m, a_kb, smask(2))
        p0 = [eye - a for a in l0]
        sq = each(lambda a, a_s: _dot(a, a_s), bf(l0), each(bd, l0))
        t_inv = each(lambda p, pb, s_: p + _dot(pb, s_), p0, bf(p0), each(bd, sq))
        for k in range(RW_LEVELS):
            half = RW_BASE << k
            cm = each(lambda a, m: bd(a * m), a_kb, smask(3 + k))
            t_s = each(bd, t_inv)
            if half % 8:
                tc = bf(each(_dot, bf(t_inv), cm))
                t_inv = each(lambda t, x, y: t - _dot(x, y), t_inv, tc, t_s)
                continue
            parts = [halves(t, half, d) for (_, d), t in zip(streams, t_inv)]
            later = bf([jnp.concatenate(p[1], axis=0) for p in parts])
            tc = bf(each(_dot, later, cm))
            upd = each(_dot, tc, t_s)
            t_inv = [join(p[0], [x - u[i * half:(i + 1) * half] for i, x in enumerate(p[1])], d)
                     for (_, d), p, u in zip(streams, parts, upd)]
        t_b = bf(t_inv)
        v_s = each(bd, v)
        akv = each(_dot, a_kk, v_s)
        x1 = bf(each(_dot, t_b, each(bd, qk)))
        x2 = each(_dot, t_b, each(bd, akv))
        y_v = each(_dot, a_rk, v_s)
        bh_s = each(lambda a, w: bd(a * w), bt, w_all)
        kh_s = each(lambda a, w: bd(a * w), kt, w_all)
        s_v = each(_dot_tn, v_s, kh_s)
        s0f = [s_ref[bi, d] for bi, d in streams]
        s0 = bf(s0f)
        u = each(lambda x, s_, y: _dot_nt(x, s_) + y, x1, s0, x2)
        u_s = each(bd, u)
        y = each(lambda q, s_, yv, ab, us: _dot_nt(q, s_) + yv - _dot(ab, us), rt, s0, y_v, a_rb, u_s)
        s1 = each(lambda s_, w, sv, us, bh: s_ * w + sv - _dot_tn(us, bh), s0f, w_all, s_v, u_s, bh_s)
        for (bi, d), y_, s_, rows in zip(streams, y, s1, sl):
            dirs[d][6][bi, rows, :] = y_
            s_ref[bi, d] = s_
        return carry

    lax.fori_loop(0, n_chunks, body, 0)
    sfin_ref[...] = s_ref[...]


def _rw_scan(r, v, kk, lw, kd, bdir, s0):
    b, length, c = r.shape
    tc = min(RW_BLOCK, length)
    nblk = length // tc
    nb = RW_SEQS
    same, smask, eye, tri = _rw_masks()
    consts = (jnp.asarray(same), jnp.asarray(smask), jnp.asarray(eye), jnp.asarray(tri, BF16))
    fwd = lambda col: pl.BlockSpec((nb, tc, c), lambda g, i: (g, i, col))
    bwd = lambda col: pl.BlockSpec((nb, tc, c), lambda g, i: (g, nblk - 1 - i, col))
    state = pl.BlockSpec((nb, 2, c, c), lambda g, i: (g, 0, 0, 0))
    return pl.pallas_call(
        functools.partial(_rw_scan_kernel, n_chunks=tc // RW_CHUNK, n_batch=nb),
        grid=(b // nb, nblk),
        in_specs=[fwd(0), bwd(0)] * 3 + [fwd(0), bwd(1)] * 3
                 + [state] + [_const_spec(a.shape) for a in consts],
        out_specs=[fwd(0), bwd(0), state],
        out_shape=[jax.ShapeDtypeStruct((b, length, c), F32), jax.ShapeDtypeStruct((b, length, c), F32),
                   jax.ShapeDtypeStruct((b, 2, c, c), F32)],
        scratch_shapes=[pltpu.VMEM((nb, 2, c, c), F32)],
        compiler_params=_cparams(("arbitrary", "arbitrary")),
        name="rwkv_scan",
    )(r, r, v, v, kk, kk, lw, lw, kd, kd, bdir, bdir, s0, *consts)


def _blockdiag2(w):
    k, n = w.shape[1:]
    z = jnp.zeros((k, n), w.dtype)
    return jnp.concatenate([jnp.concatenate([w[0], z], axis=1), jnp.concatenate([z, w[1]], axis=1)], axis=0)


def _mla_spread(w, lanes):
    k = w.shape[0]
    w = w.reshape(k, -1, len(lanes))
    return jnp.zeros((k, w.shape[1], LANES), w.dtype).at[:, :, lanes].set(w).reshape(k, -1)


def kernel(x, c, ctx, c_ctx, ada_w, ada_b, norm_ffn1, norm_mix, norm_ffn2, ffn1_gate, ffn1_up, ffn1_down, ffn2_gate, ffn2_up, ffn2_down, w_in, w_out, hy_conv_w, hy_conv_b, hy_f_w1, hy_f_b1, hy_f_w2, hy_f_b2, hy_f_w3, hy_f_b3, hy_f_w4, hy_f_freq, hy_bias, gqa_q_norm, gqa_k_norm, mla_cq_norm, mla_ckv_norm, mla_w_uq, mla_w_ukv, mla_q_norm, mla_k_norm, rw_mu, rw_w0, rw_w2, rw_a0, rw_a2, rw_g2, rw_k_k, rw_k_a, rw_r_k, rw_ln_w, rw_ln_b):
    b, length, d = x.shape
    lc = ctx.shape[1]
    depth = ada_w.shape[0]

    c_all = jnp.zeros((16, d), F32).at[:b].set(c).at[b].set(c_ctx)
    mod = _compute_mod(c_all, ada_w, ada_b)

    xc = ctx.reshape(1, b * lc, d)
    for l in range(depth):
        ctx_out = l < depth - 1
        mod_x = mod[l, :b].reshape(b, N_MOD, d)
        mod_c = mod[l, b:b + 1].reshape(1, N_MOD, d)
        row = lambda a: a[l].reshape(1, -1)

        wg1, wu1, wd1 = (w[l].astype(BF16) for w in (ffn1_gate, ffn1_up, ffn1_down))
        wg2, wu2, wd2 = (w[l].astype(BF16) for w in (ffn2_gate, ffn2_up, ffn2_down))
        wi = w_in[l]
        o1, o2, o3 = HY_COLS, HY_COLS + GQA_COLS, HY_COLS + GQA_COLS + MLA_COLS
        ml_lanes = _mla_lanes()
        o_kr = o3 - MLA_ROPE
        wi = jnp.concatenate([wi[:, :o_kr], _mla_spread(wi[:, o_kr:o3], ml_lanes[MLA_NOPE:]), wi[:, o3:]],
                             axis=1).astype(BF16)
        wo = w_out[l].astype(BF16)

        x = _ffn(x, mod_x, row(norm_ffn1), wg1, wu1, wd1, 0)
        xc = _ffn(xc, mod_c, row(norm_ffn1), wg1, wu1, wd1, 0)

        hy_x, gq_x, ml_x, rw_x = _inproj(x, mod_x, row(norm_mix), wi)
        hy_c, gq_c, ml_c, rw_c = (t.reshape(b, lc, -1) for t in _inproj(xc, mod_c, row(norm_mix), wi))

        w1p = jnp.zeros((LANES, HY_ORDER), F32).at[:HY_EMB].set(hy_f_w1[l])
        filt = (w1p, row(hy_f_b1), hy_f_w2[l], row(hy_f_b2), hy_f_w3[l], row(hy_f_b3), hy_f_w4[l],
                row(hy_f_freq))
        spec_x = _hy_spectrum(_hy_filters(length, *filt), row(hy_bias))
        y_hy_x = _hy_conv(hy_x, hy_conv_w[l], hy_conv_b[l], spec_x)

        gq = jnp.tile(row(gqa_q_norm), (1, 2))
        gk = jnp.tile(row(gqa_k_norm), (1, 2))
        q_l, k_l, v_l = _gqa_prep(gq_x, gq, gk, True)
        q_c, k_c, v_c = _gqa_prep(gq_c, gq, gk, False)
        y_gq_x = _attention(q_l, [(k_c, v_c), (k_l, v_l)], GQA_KV_HEADS)

        wuq = _mla_spread(mla_w_uq[l], ml_lanes).astype(BF16)
        wukv = mla_w_ukv[l].reshape(MLA_KV_RANK, MLA_HEADS, MLA_NOPE + MLA_V)
        wuk = _mla_spread(wukv[:, :, :MLA_NOPE].reshape(MLA_KV_RANK, -1), ml_lanes[:MLA_NOPE]).astype(BF16)
        wuv = wukv[:, :, MLA_NOPE:].reshape(MLA_KV_RANK, -1).astype(BF16)
        pad_n = lambda g: _mla_spread(g[l].reshape(1, MLA_QK), ml_lanes)
        mla_w = (row(mla_cq_norm), row(mla_ckv_norm), wuq, wuk, wuv, pad_n(mla_q_norm), pad_n(mla_k_norm))
        mq_l, mk_l, mv_l = _mla_prep(ml_x, *mla_w, True)
        mq_c, mk_c, mv_c = _mla_prep(ml_c, *mla_w, False)
        y_ml_x = _attention(mq_l, [(mk_c, mv_c), (mk_l, mv_l)], MLA_HEADS)

        rw_w = (row(rw_mu), row(rw_k_k), row(rw_k_a), rw_r_k[l].reshape(1, RW_C),
                rw_w0[l].reshape(1, 2 * RW_C), _blockdiag2(rw_w2[l]).astype(BF16),
                rw_a0[l].reshape(1, 2 * RW_C), _blockdiag2(rw_a2[l]).astype(BF16), rw_g2[l].astype(BF16))
        pc = _rw_prep(rw_c, *rw_w)
        px = _rw_prep(rw_x, *rw_w)
        zeros = jnp.zeros((b, 2, RW_C, RW_C), F32)
        yf_c, yb_c, s_ctx = _rw_scan(*pc[:6], zeros)
        yf_x, yb_x, _ = _rw_scan(*px[:6], s_ctx)

        tail = (row(rw_ln_w), row(rw_ln_b), wo, row(norm_ffn2), wg2, wu2, wd2)
        x = _mix_ffn(x, mod_x, (y_hy_x, y_gq_x, y_ml_x), (yf_x, yb_x, px[6], px[7]), *tail)
        if ctx_out:
            spec_c = _hy_spectrum(_hy_filters(lc, *filt), row(hy_bias))
            y_hy_c = _hy_conv(hy_c, hy_conv_w[l], hy_conv_b[l], spec_c)
            y_gq_c = _attention(q_c, [(k_c, v_c)], GQA_KV_HEADS)
            y_ml_c = _attention(mq_c, [(mk_c, mv_c)], MLA_HEADS)
            flat = lambda ts: tuple(t.reshape(1, b * lc, -1) for t in ts)
            xc = _mix_ffn(xc, mod_c, flat((y_hy_c, y_gq_c, y_ml_c)), flat((yf_c, yb_c, pc[6], pc[7])), *tail)
    return x
```

```python
import functools
import math

import numpy as np
import jax
import jax.numpy as jnp
from jax import lax
from jax.experimental import pallas as pl
from jax.experimental.pallas import tpu as pltpu

F32 = jnp.float32
BF16 = jnp.bfloat16

GRID_W = 64
HEAD_DIM = 64
N_MOD = 9
NORM_EPS = 1e-6
LOG2E = math.log2(math.e)
ROPE_THETA = 10000.0

HY_CH = 256
HY_EMB = 33
HY_ORDER = 64
HY_FAST_PCT = 0.3
HY_SLOW_PCT = 1.5
HY_TARGET = 1e-2

GQA_HEADS = 4
GQA_KV_HEADS = 2
MLA_HEADS = 4
MLA_NOPE = 64
MLA_ROPE = 32
MLA_QK = MLA_NOPE + MLA_ROPE
MLA_V = 64
MLA_Q_RANK = 256
MLA_KV_RANK = 128

RW_HEADS = 4
RW_N = 64
RW_C = RW_HEADS * RW_N
RW_DECAY_LORA = 64
RW_AAA_LORA = 64
RW_GATE_LORA = 128
RW_GN_EPS = 64e-5

HY_COLS = 3 * HY_CH
GQA_COLS = (GQA_HEADS + 2 * GQA_KV_HEADS) * HEAD_DIM
MLA_COLS = MLA_Q_RANK + MLA_KV_RANK + MLA_ROPE
MLA_COLS_PAD = 512
RW_COLS = 3 * RW_C + 2 * RW_DECAY_LORA + 2 * RW_AAA_LORA + RW_GATE_LORA
D_IN_PAD = HY_COLS + GQA_COLS + MLA_COLS_PAD + RW_COLS

LANES = 128
ROW_TILE = 512
PREP_TILE = 1024
Q_TILE = 256
ATTN_HEADS = 4
KEY_CHUNK = 256
VT_ROWS = 2 * HEAD_DIM
RW_CHUNK = 64
RW_BASE = 4
RW_LEVELS = 4
RW_BLOCK = 256
RW_SEQS = 4
VMEM_LIMIT = 56 * 2 ** 20


def _cparams(sem):
    return pltpu.CompilerParams(dimension_semantics=sem, vmem_limit_bytes=VMEM_LIMIT)


def _const_spec(shape):
    nd = len(shape)
    return pl.BlockSpec(shape, lambda *_: (0,) * nd, pipeline_mode=pl.Buffered(1))


def _dot(a, b, precision=None):
    return jnp.dot(a, b, preferred_element_type=F32, precision=precision)


def _dot_nt(a, b, precision=None):
    return lax.dot_general(a, b, (((1,), (1,)), ((), ())), preferred_element_type=F32,
                           precision=precision)


def _dot_tn(a, b, precision=None):
    return lax.dot_general(a, b, (((0,), (0,)), ((), ())), preferred_element_type=F32,
                           precision=precision)


def _pieces(x, n):
    out = []
    for _ in range(n):
        p = x.astype(BF16)
        out.append(p)
        x = x - p.astype(F32)
    return out


def _dot3(a, b):
    (a1, a2), (b1, b2) = _pieces(a, 2), _pieces(b, 2)
    return _dot(a1, b1) + _dot(a1, b2) + _dot(a2, b1)


def _segsum(x, seg):
    return functools.reduce(jnp.add, [_dot(p, seg) for p in _pieces(x, 2)])


def _sigmoid(x):
    return 1.0 / (1.0 + jnp.exp(-x))


def _adaln(x, g, mod_ref, i):
    shift = mod_ref[pl.ds(3 * i, 1), :]
    scale = mod_ref[pl.ds(3 * i + 1, 1), :]
    r = lax.rsqrt(jnp.mean(x * x, axis=-1, keepdims=True) + NORM_EPS)
    return (x * r) * (g * (1.0 + scale)) + shift


def _mod_kernel(c_ref, w_ref, b_ref, o_ref):
    c = c_ref[...]
    s = c * _sigmoid(c)
    o_ref[...] = _dot3(s, w_ref[...]) + b_ref[...]


def _compute_mod(c_all, ada_w, ada_b):
    depth, d, n = ada_w.shape
    rows = c_all.shape[0]
    tn = 1024
    return pl.pallas_call(
        _mod_kernel,
        grid=(depth, n // tn),
        in_specs=[
            pl.BlockSpec((rows, d), lambda l, j: (0, 0)),
            pl.BlockSpec((None, d, tn), lambda l, j: (l, 0, j)),
            pl.BlockSpec((None, 1, tn), lambda l, j: (l, 0, j)),
        ],
        out_specs=pl.BlockSpec((None, rows, tn), lambda l, j: (l, 0, j)),
        out_shape=jax.ShapeDtypeStruct((depth, rows, n), F32),
        compiler_params=_cparams(("arbitrary", "arbitrary")),
        name="mod",
    )(c_all, ada_w, ada_b.reshape(depth, 1, n))


def _ffn_kernel(x_ref, mod_ref, g_ref, wg_ref, wu_ref, wd_ref, o_ref, *, sub):
    x = x_ref[...]
    h = _adaln(x, g_ref[...], mod_ref, sub).astype(BF16)
    a = _dot(h, wg_ref[...])
    u = _dot(h, wu_ref[...])
    z = (a * _sigmoid(a) * u).astype(BF16)
    y = _dot(z, wd_ref[...])
    gate = mod_ref[pl.ds(3 * sub + 2, 1), :]
    o_ref[...] = x + (0.5 * gate) * y


def _ffn(x, mod, g, wg, wu, wd, sub):
    bm, rows, d = x.shape
    tm = min(ROW_TILE, rows)
    f = wg.shape[1]
    return pl.pallas_call(
        functools.partial(_ffn_kernel, sub=sub),
        grid=(bm, rows // tm),
        in_specs=[
            pl.BlockSpec((None, tm, d), lambda b, i: (b, i, 0)),
            pl.BlockSpec((None, N_MOD, d), lambda b, i: (b, 0, 0)),
            _const_spec((1, d)),
            _const_spec((d, f)),
            _const_spec((d, f)),
            _const_spec((f, d)),
        ],
        out_specs=pl.BlockSpec((None, tm, d), lambda b, i: (b, i, 0)),
        out_shape=jax.ShapeDtypeStruct(x.shape, F32),
        compiler_params=_cparams(("arbitrary", "arbitrary")),
        name="ffn",
    )(x, mod, g, wg, wu, wd)


_IN_OFFS = (0, HY_COLS, HY_COLS + GQA_COLS, HY_COLS + GQA_COLS + MLA_COLS_PAD, D_IN_PAD)


def _inproj_kernel(x_ref, mod_ref, g_ref, w_ref, hy_ref, gq_ref, ml_ref, rw_ref):
    h = _adaln(x_ref[...], g_ref[...], mod_ref, 1).astype(BF16)
    p = _dot(h, w_ref[...])
    for ref, lo, hi in zip((hy_ref, gq_ref, ml_ref, rw_ref), _IN_OFFS[:-1], _IN_OFFS[1:]):
        ref[...] = p[:, lo:hi]


def _inproj(x, mod, g, w):
    bm, rows, d = x.shape
    tm = min(ROW_TILE, rows)
    widths = [hi - lo for lo, hi in zip(_IN_OFFS[:-1], _IN_OFFS[1:])]
    return pl.pallas_call(
        _inproj_kernel,
        grid=(bm, rows // tm),
        in_specs=[
            pl.BlockSpec((None, tm, d), lambda b, i: (b, i, 0)),
            pl.BlockSpec((None, N_MOD, d), lambda b, i: (b, 0, 0)),
            _const_spec((1, d)),
            _const_spec((d, D_IN_PAD)),
        ],
        out_specs=[pl.BlockSpec((None, tm, wd), lambda b, i: (b, i, 0)) for wd in widths],
        out_shape=[jax.ShapeDtypeStruct((bm, rows, wd), F32) for wd in widths],
        compiler_params=_cparams(("arbitrary", "arbitrary")),
        name="inproj",
    )(x, mod, g, w)


def _mix_ffn_kernel(x_ref, mod_ref, yh_ref, yg_ref, ym_ref, yf_ref, yb_ref, bonus_ref, gate_ref,
                    lnw_ref, lnb_ref, seg_ref, wo_ref, g_ref, wg_ref, wu_ref, wd_ref, o_ref):
    y = yf_ref[...] + yb_ref[...]
    seg = seg_ref[...]
    yc = y - _segsum(y, seg)
    var = _segsum(yc * yc, seg)
    yn = yc * lax.rsqrt(var + RW_GN_EPS) * lnw_ref[...] + lnb_ref[...]
    y_rw = ((yn + bonus_ref[...]) * gate_ref[...]).astype(BF16)
    ycat = jnp.concatenate([yh_ref[...], yg_ref[...], ym_ref[...], y_rw], axis=-1)
    x = x_ref[...] + mod_ref[pl.ds(5, 1), :] * _dot(ycat, wo_ref[...])
    h = _adaln(x, g_ref[...], mod_ref, 2).astype(BF16)
    a = _dot(h, wg_ref[...])
    u = _dot(h, wu_ref[...])
    z = (a * _sigmoid(a) * u).astype(BF16)
    o_ref[...] = x + (0.5 * mod_ref[pl.ds(8, 1), :]) * _dot(z, wd_ref[...])


def _mix_ffn(x, mod, ys, rw, ln_w, ln_b, wo, g, wg, wu, wd):
    bm, rows, d = x.shape
    tm = min(ROW_TILE, rows)
    c = RW_C
    seg = jnp.asarray(_seg_matrix(c, RW_N, 1.0 / RW_N), BF16)
    row_spec = pl.BlockSpec((None, tm, d), lambda b, i: (b, i, 0))
    y_spec = pl.BlockSpec((None, tm, c), lambda b, i: (b, i, 0))
    consts = (ln_w, ln_b, seg, wo, g, wg, wu, wd)
    return pl.pallas_call(
        _mix_ffn_kernel,
        grid=(bm, rows // tm),
        in_specs=[row_spec, pl.BlockSpec((None, N_MOD, d), lambda b, i: (b, 0, 0))] + [y_spec] * 7
                 + [_const_spec(a.shape) for a in consts],
        out_specs=row_spec,
        out_shape=jax.ShapeDtypeStruct(x.shape, F32),
        compiler_params=_cparams(("arbitrary", "arbitrary")),
        name="mix_ffn",
    )(x, mod, *ys, *rw, *consts)


def _hy_features(length):
    t01 = np.linspace(0.0, 1.0, length, dtype=np.float32)[:, None]
    bands = (HY_EMB - 1) // 2
    w_ang = (np.float32(2.0 * math.pi) * np.arange(length, dtype=np.float32)[:, None]
             / np.float32(length)).astype(np.float32)
    f = np.linspace(1e-4, bands - 1, bands, dtype=np.float32)[None]
    arg = (f * w_ang).astype(np.float32)
    z = np.concatenate([t01, np.cos(arg), -np.sin(arg)], axis=-1).astype(np.float32)
    zp = np.zeros((length, LANES), np.float32)
    zp[:, :HY_EMB] = z
    return zp


def _hy_deltas():
    max_decay = math.log(HY_TARGET) / HY_FAST_PCT
    min_decay = math.log(HY_TARGET) / HY_SLOW_PCT
    d = np.abs(np.linspace(min_decay, max_decay, HY_CH, dtype=np.float32))
    return np.tile(d, 2)[None].astype(np.float32)


def _hyfilt_kernel(z_ref, w1_ref, b1_ref, w2_ref, b2_ref, w3_ref, b3_ref, w4_ref, fr_ref,
                   dl_ref, o_ref):
    z = z_ref[...]
    fr = fr_ref[...]
    h = jnp.sin(fr * (_dot3(z, w1_ref[...]) + b1_ref[...]))
    h = jnp.sin(fr * (_dot3(h, w2_ref[...]) + b2_ref[...]))
    h = jnp.sin(fr * (_dot3(h, w3_ref[...]) + b3_ref[...]))
    h = _dot3(h, w4_ref[...])
    o_ref[...] = h * jnp.exp(-z[:, 0:1] * dl_ref[...])


def _hy_filters(length, w1p, b1, w2, b2, w3, b3, w4, freq):
    z = jnp.asarray(_hy_features(length))
    dl = jnp.asarray(_hy_deltas())
    tl = min(ROW_TILE, length)
    consts = (w1p, b1, w2, b2, w3, b3, w4, freq, dl)
    return pl.pallas_call(
        _hyfilt_kernel,
        grid=(length // tl,),
        in_specs=[pl.BlockSpec((tl, LANES), lambda i: (i, 0))] + [_const_spec(a.shape) for a in consts],
        out_specs=pl.BlockSpec((tl, 2 * HY_CH), lambda i: (i, 0)),
        out_shape=jax.ShapeDtypeStruct((length, 2 * HY_CH), F32),
        compiler_params=_cparams(("arbitrary",)),
        name="hyena_filters",
    )(z, *consts)


FFT_SLAB_PAD = 8
FFT_GROUP = 16
FFT_SLAB_GROUP = 12


def _fft_factors(length):
    return (128, 64) if length >= 2048 else (2 * length, 1)


def _fft_layout(n1, n2):
    if n2 == 1:
        return 1, n1, 2 * n1
    kept = n1 // 2 + 1
    return -(-kept // FFT_SLAB_GROUP) * FFT_SLAB_GROUP, n2, 2 * n2 + FFT_SLAB_PAD


def _stack3(t):
    hi = t.astype(jnp.bfloat16)
    lo = (t - hi.astype(np.float64)).astype(jnp.bfloat16)
    return np.concatenate([hi, lo, hi], axis=-1)


def _rows3(d):
    hi = d.astype(BF16)
    lo = (d - hi.astype(F32)).astype(BF16)
    return jnp.concatenate([hi, hi, lo], axis=0)


@functools.lru_cache(maxsize=None)
def _fft_tables(length):
    n1, n2 = _fft_factors(length)
    n = n1 * n2
    assert n == 2 * length
    a_n2 = np.arange(n2)[:, None, None]
    a_k1 = np.arange(n1)[None, :, None]
    a_n1 = np.arange(n1 // 2)[None, None, :]
    ang = 2.0 * np.pi * ((a_n1 * a_k1 % n1) / n1 + (a_n2 * a_k1 % n) / n)
    t_re, t_im = np.cos(ang), -np.sin(ang)
    wgt = np.ones(n1)
    if n2 > 1:
        nk = _fft_layout(n1, n2)[0]
        keep = np.arange(nk) <= n1 // 2
        t_re, t_im = (np.where(keep[None, :, None], t[:, :nk], 0.0) for t in (t_re, t_im))
        wgt = np.where(np.isin(np.arange(nk), (0, n1 // 2)), 1.0, 2.0) * keep
    ta = np.concatenate([t_re, t_im], axis=1)
    t_re_w, t_im_w = t_re * wgt[None, :, None], t_im * wgt[None, :, None]
    tai = np.concatenate([np.transpose(t_re_w, (0, 2, 1)), np.transpose(t_im_w, (0, 2, 1))], axis=2) / n
    jj = np.arange(n2)
    ang2 = 2.0 * np.pi * (np.outer(jj, jj) % n2) / n2
    c, s = np.cos(ang2), np.sin(ang2)
    fb = np.block([[c, s], [-s, c]])
    fbi = np.block([[c, -s], [s, c]])
    return tuple(_stack3(t) for t in (ta, tai, fb, fbi))


def _fft_stage_a(src_ref, fbuf_ref, ta_ref, n1, n2):
    if n2 == 1:
        fbuf_ref[...] = _dot(ta_ref[0], _rows3(src_ref[...]))
        return
    nk, _, pitch = _fft_layout(n1, n2)

    def body(i, carry):
        js = [i * FFT_GROUP + g for g in range(FFT_GROUP)]
        rows = [_rows3(src_ref[pl.ds(j, n1 // 2, stride=n2), :]) for j in js]
        ts = [_dot(ta_ref[j], r) for j, r in zip(js, rows)]
        for j, t in zip(js, ts):
            fbuf_ref[pl.ds(j, nk, stride=pitch), :] = t[:nk]
            fbuf_ref[pl.ds(n2 + j, nk, stride=pitch), :] = t[nk:]
        return carry

    lax.fori_loop(0, n2 // FFT_GROUP, body, 0)


def _spec_kernel(hf_ref, hb_ref, bias_ref, ta_ref, fb_ref, o_ref, fbuf_ref, *, n1, n2):
    nslab, hs, pitch = _fft_layout(n1, n2)
    if pitch > 2 * hs:
        o_ref[...] = jnp.zeros(o_ref.shape, F32)
    _fft_stage_a(hf_ref, o_ref, ta_ref, n1, n2)
    _fft_stage_a(hb_ref, fbuf_ref, ta_ref, n1, n2)
    is_re = lax.broadcasted_iota(jnp.int32, (2 * hs, 1), 0) < hs
    sgn = jnp.where(is_re, 1.0, -1.0).astype(F32)
    skip = jnp.where(is_re, bias_ref[...], 0.0)

    grp = FFT_SLAB_GROUP if n2 > 1 else 1

    def stage_b(i, carry):
        sls = [pl.ds(pl.multiple_of((i * grp + g) * pitch, 8), 2 * hs) for g in range(grp)]
        a = [o_ref[sl, :] for sl in sls]
        b = [fbuf_ref[sl, :] for sl in sls]
        if n2 > 1:
            a = [_dot(fb_ref[...], _rows3(x)) for x in a]
            b = [_dot(fb_ref[...], _rows3(x)) for x in b]
        for sl, x, y in zip(sls, a, b):
            o_ref[sl, :] = x + sgn * y + skip
        return carry

    lax.fori_loop(0, nslab // grp, stage_b, 0)


def _hy_spectrum(h, bias):
    length = h.shape[0]
    n1, n2 = _fft_factors(length)
    nslab, _, pitch = _fft_layout(n1, n2)
    ta, _, fb, _ = (jnp.asarray(t) for t in _fft_tables(length))
    nblk = HY_CH // LANES
    return pl.pallas_call(
        functools.partial(_spec_kernel, n1=n1, n2=n2),
        grid=(nblk,),
        in_specs=[pl.BlockSpec((length, LANES), lambda j: (0, j)),
                  pl.BlockSpec((length, LANES), lambda j: (0, nblk + j)),
                  pl.BlockSpec((1, LANES), lambda j: (0, j)),
                  _const_spec(ta.shape), _const_spec(fb.shape)],
        out_specs=pl.BlockSpec((nslab * pitch, LANES), lambda j: (0, j)),
        out_shape=jax.ShapeDtypeStruct((nslab * pitch, HY_CH), F32),
        scratch_shapes=[pltpu.VMEM((nslab * pitch, LANES), F32)],
        compiler_params=_cparams(("arbitrary",)),
        name="hyena_spectrum",
    )(h, h, bias, ta, fb)


def _shift_rows(x, prev_row, next_row):
    n = x.shape[0]
    row = lax.broadcasted_iota(jnp.int32, x.shape, 0)
    up = jnp.where(row == 0, prev_row, pltpu.roll(x, 1, 0))
    dn = jnp.where(row == n - 1, next_row, pltpu.roll(x, n - 1, 0))
    return up, dn


def _hyconv_kernel(x1_ref, x2_ref, v_ref, cw1_ref, cw2_ref, cwv_ref, cb_ref, g_ref,
                   ta_ref, tai_ref, fb_ref, fbi_ref, o_ref, u_ref, fbuf_ref, *, n1, n2):
    def conv3(ref, w_ref, b):
        x = ref[...]
        up, dn = _shift_rows(x, 0.0, 0.0)
        return up * w_ref[0:1, :] + x * w_ref[1:2, :] + dn * w_ref[2:3, :] + b

    cb = cb_ref[...]
    u_ref[...] = conv3(x1_ref, cw1_ref, cb[0:1, :]) * conv3(v_ref, cwv_ref, cb[2:3, :])

    _fft_stage_a(u_ref, fbuf_ref, ta_ref, n1, n2)
    nslab, hs, pitch = _fft_layout(n1, n2)

    grp = FFT_SLAB_GROUP if n2 > 1 else 1

    def cmul(a, g):
        ar, ai, gr, gi = a[:hs], a[hs:], g[:hs], g[hs:]
        return jnp.concatenate([ar * gr - ai * gi, ar * gi + ai * gr], axis=0)

    def stage_b(i, carry):
        sls = [pl.ds(pl.multiple_of((i * grp + g) * pitch, 8), 2 * hs) for g in range(grp)]
        pairs = [sls[g:g + 2] for g in range(0, grp, 2)]
        a = [jnp.concatenate([fbuf_ref[sl, :] for sl in pr], axis=1) for pr in pairs]
        if n2 > 1:
            a = [_dot(fb_ref[...], _rows3(x)) for x in a]
        a = [cmul(x, jnp.concatenate([g_ref[sl, :] for sl in pr], axis=1)) for x, pr in zip(a, pairs)]
        if n2 > 1:
            a = [_dot(fbi_ref[...], _rows3(x)) for x in a]
        for pr, x in zip(pairs, a):
            for g, sl in enumerate(pr):
                fbuf_ref[sl, :] = x[:, g * LANES:(g + 1) * LANES]
        return carry

    lax.fori_loop(0, nslab // grp, stage_b, 0)

    if n2 == 1:
        u_ref[...] = _dot(tai_ref[0], _rows3(fbuf_ref[...]))
    else:
        def stage_a_inv(i, carry):
            js = [i * FFT_GROUP + g for g in range(FFT_GROUP)]
            cs = [_rows3(jnp.concatenate([fbuf_ref[pl.ds(j, nslab, stride=pitch), :],
                                          fbuf_ref[pl.ds(n2 + j, nslab, stride=pitch), :]], axis=0))
                  for j in js]
            ys = [_dot(tai_ref[j], c) for j, c in zip(js, cs)]
            for j, y in zip(js, ys):
                u_ref[pl.ds(j, n1 // 2, stride=n2), :] = y
            return carry

        lax.fori_loop(0, n2 // FFT_GROUP, stage_a_inv, 0)

    o_ref[...] = (conv3(x2_ref, cw2_ref, cb[1:2, :]) * u_ref[...]).astype(o_ref.dtype)


def _hy_conv(p, conv_w, conv_b, spec):
    b, length, _ = p.shape
    n1, n2 = _fft_factors(length)
    nslab, _, pitch = _fft_layout(n1, n2)
    ta, tai, fb, fbi = (jnp.asarray(t) for t in _fft_tables(length))
    nblk = HY_CH // LANES
    cb3 = conv_b.reshape(3, HY_CH)
    col = lambda g: pl.BlockSpec((None, length, LANES), lambda j, i, g=g: (i, 0, g * nblk + j))
    wcol = lambda g: pl.BlockSpec((3, LANES), lambda j, i, g=g: (0, g * nblk + j))
    return pl.pallas_call(
        functools.partial(_hyconv_kernel, n1=n1, n2=n2),
        grid=(nblk, b),
        in_specs=[col(0), col(1), col(2), wcol(0), wcol(1), wcol(2),
                  pl.BlockSpec((3, LANES), lambda j, i: (0, j)),
                  pl.BlockSpec((nslab * pitch, LANES), lambda j, i: (0, j), pipeline_mode=pl.Buffered(1)),
                  _const_spec(ta.shape), _const_spec(tai.shape), _const_spec(fb.shape),
                  _const_spec(fbi.shape)],
        out_specs=pl.BlockSpec((None, length, LANES), lambda j, i: (i, 0, j)),
        out_shape=jax.ShapeDtypeStruct((b, length, HY_CH), BF16),
        scratch_shapes=[pltpu.VMEM((length, LANES), F32), pltpu.VMEM((nslab * pitch, LANES), F32)],
        compiler_params=_cparams(("arbitrary", "arbitrary")),
        name="hyena_conv",
    )(p, p, p, conv_w, conv_w, conv_w, cb3, spec, ta, tai, fb, fbi)


def _rope_tables(length, d_rot, lane_lo, head_w):
    rows = length // GRID_W
    row = np.repeat(np.arange(rows, dtype=np.float32), GRID_W)
    colv = np.tile(np.arange(GRID_W, dtype=np.float32), rows)
    n_freq = d_rot // 4
    inv = (np.float32(ROPE_THETA) ** (-np.arange(n_freq, dtype=np.float32) / np.float32(n_freq))).astype(np.float32)
    ang = np.concatenate([row[:, None] * inv, colv[:, None] * inv], axis=-1).astype(np.float32)
    cos_t = np.ones((length, LANES), np.float32)
    sin_t = np.zeros((length, LANES), np.float32)
    c, s = np.cos(ang), np.sin(ang)
    for base in range(0, LANES, head_w):
        for i in range(d_rot // 2):
            cos_t[:, base + lane_lo + 2 * i] = c[:, i]
            cos_t[:, base + lane_lo + 2 * i + 1] = c[:, i]
            sin_t[:, base + lane_lo + 2 * i] = -s[:, i]
            sin_t[:, base + lane_lo + 2 * i + 1] = s[:, i]
    return cos_t, sin_t


def _rope_angles(length, d_rot):
    rows = length // GRID_W
    row = np.repeat(np.arange(rows, dtype=np.float32), GRID_W)
    colv = np.tile(np.arange(GRID_W, dtype=np.float32), rows)
    n_freq = d_rot // 4
    inv = (np.float32(ROPE_THETA) ** (-np.arange(n_freq, dtype=np.float32) / np.float32(n_freq))).astype(np.float32)
    return np.concatenate([row[:, None] * inv, colv[:, None] * inv], axis=-1).astype(np.float32)


def _mla_lanes():
    lanes = np.empty(MLA_QK, np.int64)
    n = np.arange(MLA_NOPE)
    lanes[:MLA_NOPE] = np.where(n < 32, n, n + 16)
    i = np.arange(MLA_ROPE // 2)
    lanes[MLA_NOPE + 2 * i] = 32 + i
    lanes[MLA_NOPE + 2 * i + 1] = 96 + i
    return lanes


def _mla_rope_tables(length):
    ang = _rope_angles(length, MLA_ROPE)
    cos_t = np.ones((length, LANES), np.float32)
    sin_t = np.zeros((length, LANES), np.float32)
    n_pairs = MLA_ROPE // 2
    cos_t[:, 32:32 + n_pairs] = cos_t[:, 96:96 + n_pairs] = np.cos(ang)
    sin_t[:, 32:32 + n_pairs] = -np.sin(ang)
    sin_t[:, 96:96 + n_pairs] = np.sin(ang)
    return cos_t, sin_t


def _rope(x, cos_t, sin_t):
    lane = lax.broadcasted_iota(jnp.int32, x.shape, 1)
    w = x.shape[1]
    partner = jnp.where(jnp.bitwise_and(lane, 1) == 0, pltpu.roll(x, w - 1, 1), pltpu.roll(x, 1, 1))
    return x * cos_t + partner * sin_t


def _store_vt(vt_ref, h, vt):
    row = lax.broadcasted_iota(jnp.int32, (VT_ROWS - HEAD_DIM, vt.shape[1]), 0)
    vt_ref[h, 0:HEAD_DIM, :] = vt.astype(BF16)
    vt_ref[h, HEAD_DIM:VT_ROWS, :] = jnp.where(row == 0, 1.0, 0.0).astype(BF16)


def _gqa_prep_kernel(p_ref, cos_ref, sin_ref, gq_ref, gk_ref, seg_ref, q_ref, k_ref, vt_ref, *, rope):
    p = p_ref[...]
    seg = seg_ref[...]
    lane = lax.broadcasted_iota(jnp.int32, (p.shape[0], LANES), 1)
    low = lane < HEAD_DIM

    def hnorm(x, g):
        ms = _segsum(x * x, seg)
        return x * lax.rsqrt(ms + NORM_EPS) * g

    scale = HEAD_DIM ** -0.5 * LOG2E
    for c in range(2):
        q = hnorm(p[:, c * LANES:(c + 1) * LANES], gq_ref[...])
        if rope:
            q = _rope(q, cos_ref[...], sin_ref[...])
        q = q * scale
        qs = pltpu.roll(q, HEAD_DIM, 1)
        q_ref[:, (2 * c) * LANES:(2 * c + 1) * LANES] = jnp.where(low, q, 0.0).astype(BF16)
        q_ref[:, (2 * c + 1) * LANES:(2 * c + 2) * LANES] = jnp.where(low, qs, 0.0).astype(BF16)
    k = hnorm(p[:, 256:384], gk_ref[...])
    if rope:
        k = _rope(k, cos_ref[...], sin_ref[...])
    ks = pltpu.roll(k, HEAD_DIM, 1)
    k_ref[:, 0:LANES] = jnp.where(low, k, 0.0).astype(BF16)
    k_ref[:, LANES:2 * LANES] = jnp.where(low, ks, 0.0).astype(BF16)
    vt = p[:, 384:512].T
    _store_vt(vt_ref, 0, vt[0:HEAD_DIM])
    _store_vt(vt_ref, 1, vt[HEAD_DIM:2 * HEAD_DIM])


def _seg_matrix(width, seg, value):
    i = np.arange(width)
    return ((i[:, None] // seg) == (i[None, :] // seg)).astype(np.float32) * np.float32(value)


def _gqa_prep(p, gq, gk, rope):
    b, length, _ = p.shape
    tm = min(PREP_TILE, length)
    if rope:
        cos_t, sin_t = (jnp.asarray(t) for t in _rope_tables(length, HEAD_DIM, 0, HEAD_DIM))
    else:
        cos_t = sin_t = jnp.zeros((length, LANES), F32)
    seg = jnp.asarray(_seg_matrix(LANES, HEAD_DIM, 1.0 / HEAD_DIM), BF16)
    tab = pl.BlockSpec((tm, LANES), lambda i, bb: (i, 0))
    outw = (512, 256)
    return pl.pallas_call(
        functools.partial(_gqa_prep_kernel, rope=rope),
        grid=(length // tm, b),
        in_specs=[pl.BlockSpec((None, tm, GQA_COLS), lambda i, bb: (bb, i, 0)), tab, tab,
                  _const_spec((1, LANES)), _const_spec((1, LANES)), _const_spec((LANES, LANES))],
        out_specs=[pl.BlockSpec((None, tm, w), lambda i, bb: (bb, i, 0)) for w in outw]
                  + [pl.BlockSpec((None, GQA_KV_HEADS, VT_ROWS, tm), lambda i, bb: (bb, 0, 0, i))],
        out_shape=[jax.ShapeDtypeStruct((b, length, w), BF16) for w in outw]
                  + [jax.ShapeDtypeStruct((b, GQA_KV_HEADS, VT_ROWS, length), BF16)],
        compiler_params=_cparams(("arbitrary", "arbitrary")),
        name="gqa_prep",
    )(p, cos_t, sin_t, gq, gk, seg)


def _mla_prep_kernel(p_ref, cos_ref, sin_ref, cqn_ref, ckvn_ref, wuq_ref, wuk_ref, wuv_ref,
                     qn_ref, kn_ref, ones_ref, q_ref, k_ref, vt_ref, *, rope):
    p = p_ref[...]

    def rms(x, g, width):
        ms = _segsum(x * x, ones_ref[0:x.shape[1], 0:x.shape[1]]) * (1.0 / width)
        return x * lax.rsqrt(ms + NORM_EPS) * g

    cq = rms(p[:, 0:MLA_Q_RANK], cqn_ref[...], MLA_Q_RANK).astype(BF16)
    ckv = rms(p[:, MLA_Q_RANK:MLA_Q_RANK + MLA_KV_RANK], ckvn_ref[...], MLA_KV_RANK).astype(BF16)
    q = _dot(cq, wuq_ref[...])
    kn = _dot(ckv, wuk_ref[...])
    v = _dot(ckv, wuv_ref[...])
    for c in range(MLA_HEADS // 2):
        vt = v[:, c * LANES:(c + 1) * LANES].T
        _store_vt(vt_ref, 2 * c, vt[0:MLA_V])
        _store_vt(vt_ref, 2 * c + 1, vt[MLA_V:2 * MLA_V])
    kr = p[:, 384:512]
    scale = MLA_QK ** -0.5 * LOG2E

    def rot(x):
        return x * cos_ref[...] + pltpu.roll(x, LANES // 2, 1) * sin_ref[...]

    for h in range(MLA_HEADS):
        sl = slice(h * LANES, (h + 1) * LANES)
        qh = rms(q[:, sl], qn_ref[...], MLA_QK)
        kh = rms(kn[:, sl] + kr, kn_ref[...], MLA_QK)
        if rope:
            qh, kh = rot(qh), rot(kh)
        q_ref[:, sl] = (qh * scale).astype(BF16)
        k_ref[:, sl] = kh.astype(BF16)


def _mla_prep(p, cqn, ckvn, wuq, wuk, wuv, qn, kn, rope):
    b, length, _ = p.shape
    tm = min(PREP_TILE, length)
    if rope:
        cos_t, sin_t = (jnp.asarray(t) for t in _mla_rope_tables(length))
    else:
        cos_t = sin_t = jnp.zeros((length, LANES), F32)
    tab = pl.BlockSpec((tm, LANES), lambda i, bb: (i, 0))
    ones = jnp.ones((MLA_Q_RANK, MLA_Q_RANK), BF16)
    consts = (cqn, ckvn, wuq, wuk, wuv, qn, kn, ones)
    outw = (512, 512)
    return pl.pallas_call(
        functools.partial(_mla_prep_kernel, rope=rope),
        grid=(length // tm, b),
        in_specs=[pl.BlockSpec((None, tm, MLA_COLS_PAD), lambda i, bb: (bb, i, 0)), tab, tab]
                 + [_const_spec(a.shape) for a in consts],
        out_specs=[pl.BlockSpec((None, tm, w), lambda i, bb: (bb, i, 0)) for w in outw]
                  + [pl.BlockSpec((None, MLA_HEADS, VT_ROWS, tm), lambda i, bb: (bb, 0, 0, i))],
        out_shape=[jax.ShapeDtypeStruct((b, length, w), BF16) for w in outw]
                  + [jax.ShapeDtypeStruct((b, MLA_HEADS, VT_ROWS, length), BF16)],
        compiler_params=_cparams(("arbitrary", "arbitrary")),
        name="mla_prep",
    )(p, cos_t, sin_t, *consts)


def _attn_kernel(*refs, n_seg):
    nh = ATTN_HEADS
    q_ref, o_ref = refs[0], refs[-1]
    segs = [refs[1 + 2 * nh * s:1 + 2 * nh * (s + 1)] for s in range(n_seg)]
    q = q_ref[...]
    qh = [q[:, h * LANES:(h + 1) * LANES] for h in range(nh)]
    pieces = [(h, seg, c0, min(KEY_CHUNK, seg[0].shape[0]))
              for h in range(nh) for seg in segs
              for c0 in range(0, seg[0].shape[0], min(KEY_CHUNK, seg[0].shape[0]))]
    scores = [_dot_nt(seg[h][c0:c0 + n, :], qh[h]) for h, seg, c0, n in pieces]
    mx = [jnp.max(s, axis=0, keepdims=True) for s in scores]
    ps = [jnp.exp2(s - m).astype(BF16) for s, m in zip(scores, mx)]
    os_ = [_dot(seg[nh + h][:, c0:c0 + n], e) for (h, seg, c0, n), e in zip(pieces, ps)]
    outs = []
    for h in range(nh):
        idx = [i for i, pc in enumerate(pieces) if pc[0] == h]
        m = functools.reduce(jnp.maximum, [mx[i] for i in idx])
        o = functools.reduce(jnp.add, [os_[i] * jnp.exp2(mx[i] - m) for i in idx])
        outs.append(o[0:HEAD_DIM] / o[HEAD_DIM:HEAD_DIM + 1])
    o_ref[...] = jnp.concatenate(outs, axis=0).T.astype(o_ref.dtype)


def _attention(q, kv_segs, k_heads):
    b, lq, _ = q.shape
    tq = min(Q_TILE, lq)
    nh = ATTN_HEADS
    kv_of = [h * k_heads // nh for h in range(nh)]
    in_specs = [pl.BlockSpec((None, tq, nh * LANES), lambda bb, i: (bb, i, 0))]
    args = [q]
    for k, vt in kv_segs:
        lk = k.shape[1]
        in_specs += [pl.BlockSpec((None, lk, LANES), lambda bb, i, j=j: (bb, 0, j)) for j in kv_of]
        in_specs += [pl.BlockSpec((None, None, VT_ROWS, lk), lambda bb, i, j=j: (bb, j, 0, 0)) for j in kv_of]
        args += [k] * nh + [vt] * nh
    return pl.pallas_call(
        functools.partial(_attn_kernel, n_seg=len(kv_segs)),
        grid=(b, lq // tq),
        in_specs=in_specs,
        out_specs=pl.BlockSpec((None, tq, nh * HEAD_DIM), lambda bb, i: (bb, i, 0)),
        out_shape=jax.ShapeDtypeStruct((b, lq, nh * HEAD_DIM), BF16),
        compiler_params=_cparams(("arbitrary", "arbitrary")),
        name="attention",
    )(*args)


def _rw_prep_kernel(p_ref, prev_ref, next_ref, mu_ref, kk_ref_w, ka_ref, rk_ref, w0_ref, w2_ref,
                    a0_ref, a2_ref, g2_ref, seg_ref,
                    r_ref, v_ref, kk_ref, lw_ref, kd_ref, bd_ref, bonus_ref, g_ref):
    i = pl.program_id(1)
    n = pl.num_programs(1)
    x = p_ref[...]
    prev_row = jnp.where(i > 0, prev_ref[7:8, :], 0.0)
    next_row = jnp.where(i < n - 1, next_ref[0:1, :], 0.0)
    up, dn = _shift_rows(x, prev_row, next_row)
    xs = x + mu_ref[...] * (0.5 * (up + dn) - x)
    c = RW_C
    r, k, v = xs[:, 0:c], xs[:, c:2 * c], xs[:, 2 * c:3 * c]
    xw, xa, xg = xs[:, 3 * c:3 * c + 128], xs[:, 3 * c + 128:3 * c + 256], xs[:, 3 * c + 256:3 * c + 384]
    seg = seg_ref[...]
    kk = k * kk_ref_w[...]
    nrm = jnp.sqrt(_segsum(kk * kk, seg))
    kk = kk / jnp.maximum(nrm, 1e-12)
    u = w0_ref[...] + _dot(jnp.tanh(xw).astype(BF16), w2_ref[...])
    z = -u
    softplus = jnp.maximum(z, 0.0) + jnp.log(1.0 + jnp.exp(-jnp.abs(z)))
    lw_ref[...] = -jnp.exp(-softplus - 0.5)
    a = _sigmoid(a0_ref[...] + _dot(xa.astype(BF16), a2_ref[...]))
    k_sum = jnp.zeros_like(r)
    for d in range(2):
        a_d = a[:, d * c:(d + 1) * c]
        k_d = k * (1.0 + (a_d - 1.0) * ka_ref[...])
        kd_ref[:, d * c:(d + 1) * c] = k_d
        bd_ref[:, d * c:(d + 1) * c] = kk * a_d
        k_sum = k_sum + k_d
    bonus = _segsum(r * k_sum * rk_ref[...], seg)
    r_ref[...] = r
    v_ref[...] = v
    kk_ref[...] = kk
    bonus_ref[...] = bonus * v
    g_ref[...] = _dot(_sigmoid(xg).astype(BF16), g2_ref[...])


def _rw_prep(p, mu, k_k, k_a, r_k, w0, w2bd, a0, a2bd, g2):
    b, length, cols = p.shape
    tm = min(PREP_TILE, length)
    nb = tm // 8
    last = length // 8 - 1
    seg = jnp.asarray(_seg_matrix(RW_C, RW_N, 1.0), BF16)
    consts = (mu, k_k, k_a, r_k, w0, w2bd, a0, a2bd, g2, seg)
    outw = (RW_C, RW_C, RW_C, 2 * RW_C, 2 * RW_C, 2 * RW_C, RW_C, RW_C)
    return pl.pallas_call(
        _rw_prep_kernel,
        grid=(b, length // tm),
        in_specs=[pl.BlockSpec((None, tm, cols), lambda bb, i: (bb, i, 0)),
                  pl.BlockSpec((None, 8, cols), lambda bb, i: (bb, jnp.maximum(i * nb - 1, 0), 0)),
                  pl.BlockSpec((None, 8, cols), lambda bb, i: (bb, jnp.minimum((i + 1) * nb, last), 0))]
                 + [_const_spec(a.shape) for a in consts],
        out_specs=[pl.BlockSpec((None, tm, w), lambda bb, i: (bb, i, 0)) for w in outw],
        out_shape=[jax.ShapeDtypeStruct((b, length, w), F32) for w in outw],
        compiler_params=_cparams(("arbitrary", "arbitrary")),
        name="rwkv_prep",
    )(p, p, p, *consts)


@functools.lru_cache(maxsize=None)
def _rw_masks():
    cs, n = RW_CHUNK, RW_C
    i = np.arange(n)
    same = ((i[:, None] // cs) == (i[None, :] // cs)).astype(np.float32)
    t, s = np.arange(cs)[:, None], (i % cs)[None, :]
    smask = []
    for sgn in (1, -1):
        earlier = (s - t) * sgn < 0
        rows = [earlier, (s - t) * sgn <= 0, earlier & (t // RW_BASE == s // RW_BASE)]
        for k in range(RW_LEVELS):
            size = RW_BASE << k
            rows.append(earlier & (t // (2 * size) == s // (2 * size)) & (t // size != s // size))
        smask.append(np.stack(rows))
    eye = (s == t).astype(np.float32)
    tt = np.arange(cs)
    tri = np.stack([tt[None, :] <= tt[:, None], tt[None, :] >= tt[:, None]]).astype(np.float32)
    return same, np.stack(smask).astype(np.float32), eye, tri


def _rw_scan_kernel(rf_ref, rb_ref, vf_ref, vb_ref, kkf_ref, kkb_ref, lwf_ref, lwb_ref, kdf_ref, kdb_ref,
                    bf_ref, bb_ref, s0_ref, same_ref, smask_ref, eye_ref, tri_ref,
                    yf_ref, yb_ref, sfin_ref, s_ref, *, n_chunks, n_batch):
    cs = RW_CHUNK

    @pl.when(pl.program_id(1) == 0)
    def _():
        s_ref[...] = s0_ref[...]

    same_head = same_ref[...].astype(BF16)
    eye = eye_ref[...]
    dirs = ((rf_ref, vf_ref, kkf_ref, lwf_ref, kdf_ref, bf_ref, yf_ref),
            (rb_ref, vb_ref, kkb_ref, lwb_ref, kdb_ref, bb_ref, yb_ref))

    def bd(x):
        xb = x.astype(BF16)
        return jnp.concatenate([xb, xb, xb, xb], axis=0) * same_head

    def halves(t, size, d):
        lo = [t[g:g + size] for g in range(0, t.shape[0], 2 * size)]
        hi = [t[g + size:g + 2 * size] for g in range(0, t.shape[0], 2 * size)]
        return (lo, hi) if d == 0 else (hi, lo)

    def join(earlier, later, d):
        pairs = zip(earlier, later) if d == 0 else zip(later, earlier)
        return jnp.concatenate([x for pr in pairs for x in pr], axis=0)

    streams = [(bi, d) for bi in range(n_batch) for d in range(2)]

    def each(f, *cols):
        return [f(*a) for a in zip(*cols)] if cols else [f(bi, d) for bi, d in streams]

    bf = lambda xs: [x.astype(BF16) for x in xs]
    smask = lambda k: [smask_ref[d, k] for _, d in streams]

    def body(c, carry):
        def load(bi, d):
            cc = c if d == 0 else n_chunks - 1 - c
            sl = pl.ds(pl.multiple_of(cc * cs, cs), cs)
            return [ref[bi, sl, :] for ref in dirs[d][:6]] + [sl]

        r, v, kk, lw, kd, b, sl = zip(*each(load))
        strict, incl = smask(0), smask(1)
        cl = [functools.reduce(jnp.add, [_dot(tri_ref[d], p) for p in _pieces(x, 3)])
              for (_, d), x in zip(streams, lw)]
        w_inv = [jnp.exp(-x) for x in cl]
        w_all = [jnp.exp(jnp.sum(x, axis=0, keepdims=True)) for x in lw]
        kt = each(lambda a, w: a * w, kd, w_inv)
        bt = each(lambda a, w: a * w, b, w_inv)
        qk = each(lambda a, x, y: a * jnp.exp(x - y), kk, cl, lw)
        rt = bf(each(lambda a, x: a * jnp.exp(x), r, cl))
        qk_b, kt_s, bt_s = bf(qk), each(bd, kt), each(bd, bt)
        a_kk = bf(each(lambda q, k_, m: _dot_nt(q, k_) * m, qk_b, kt_s, strict))
        a_kb = each(lambda q, k_, m: _dot_nt(q, k_) * m, qk_b, bt_s, strict)
        a_rk = bf(each(lambda q, k_, m: _dot_nt(q, k_) * m, rt, kt_s, incl))
        a_rb = bf(each(lambda q, k_, m: _dot_nt(q, k_) * m, rt, bt_s, incl))
        l0 = each(lambda a, m: a * m, a_kb, smask(2))
        p0 = [eye - a for a in l0]
        sq = each(lambda a, a_s: _dot(a, a_s), bf(l0), each(bd, l0))
        t_inv = each(lambda p, pb, s_: p + _dot(pb, s_), p0, bf(p0), each(bd, sq))
        for k in range(RW_LEVELS):
            half = RW_BASE << k
            cm = each(lambda a, m: bd(a * m), a_kb, smask(3 + k))
            t_s = each(bd, t_inv)
            if half % 8:
                tc = bf(each(_dot, bf(t_inv), cm))
                t_inv = each(lambda t, x, y: t - _dot(x, y), t_inv, tc, t_s)
                continue
            parts = [halves(t, half, d) for (_, d), t in zip(streams, t_inv)]
            later = bf([jnp.concatenate(p[1], axis=0) for p in parts])
            tc = bf(each(_dot, later, cm))
            upd = each(_dot, tc, t_s)
            t_inv = [join(p[0], [x - u[i * half:(i + 1) * half] for i, x in enumerate(p[1])], d)
                     for (_, d), p, u in zip(streams, parts, upd)]
        t_b = bf(t_inv)
        v_s = each(bd, v)
        akv = each(_dot, a_kk, v_s)
        x1 = bf(each(_dot, t_b, each(bd, qk)))
        x2 = each(_dot, t_b, each(bd, akv))
        y_v = each(_dot, a_rk, v_s)
        bh_s = each(lambda a, w: bd(a * w), bt, w_all)
        kh_s = each(lambda a, w: bd(a * w), kt, w_all)
        s_v = each(_dot_tn, v_s, kh_s)
        s0f = [s_ref[bi, d] for bi, d in streams]
        s0 = bf(s0f)
        u = each(lambda x, s_, y: _dot_nt(x, s_) + y, x1, s0, x2)
        u_s = each(bd, u)
        y = each(lambda q, s_, yv, ab, us: _dot_nt(q, s_) + yv - _dot(ab, us), rt, s0, y_v, a_rb, u_s)
        s1 = each(lambda s_, w, sv, us, bh: s_ * w + sv - _dot_tn(us, bh), s0f, w_all, s_v, u_s, bh_s)
        for (bi, d), y_, s_, rows in zip(streams, y, s1, sl):
            dirs[d][6][bi, rows, :] = y_
            s_ref[bi, d] = s_
        return carry

    lax.fori_loop(0, n_chunks, body, 0)
    sfin_ref[...] = s_ref[...]


def _rw_scan(r, v, kk, lw, kd, bdir, s0):
    b, length, c = r.shape
    tc = min(RW_BLOCK, length)
    nblk = length // tc
    nb = RW_SEQS
    same, smask, eye, tri = _rw_masks()
    consts = (jnp.asarray(same), jnp.asarray(smask), jnp.asarray(eye), jnp.asarray(tri, BF16))
    fwd = lambda col: pl.BlockSpec((nb, tc, c), lambda g, i: (g, i, col))
    bwd = lambda col: pl.BlockSpec((nb, tc, c), lambda g, i: (g, nblk - 1 - i, col))
    state = pl.BlockSpec((nb, 2, c, c), lambda g, i: (g, 0, 0, 0))
    return pl.pallas_call(
        functools.partial(_rw_scan_kernel, n_chunks=tc // RW_CHUNK, n_batch=nb),
        grid=(b // nb, nblk),
        in_specs=[fwd(0), bwd(0)] * 3 + [fwd(0), bwd(1)] * 3
                 + [state] + [_const_spec(a.shape) for a in consts],
        out_specs=[fwd(0), bwd(0), state],
        out_shape=[jax.ShapeDtypeStruct((b, length, c), F32), jax.ShapeDtypeStruct((b, length, c), F32),
                   jax.ShapeDtypeStruct((b, 2, c, c), F32)],
        scratch_shapes=[pltpu.VMEM((nb, 2, c, c), F32)],
        compiler_params=_cparams(("arbitrary", "arbitrary")),
        name="rwkv_scan",
    )(r, r, v, v, kk, kk, lw, lw, kd, kd, bdir, bdir, s0, *consts)


def _blockdiag2(w):
    k, n = w.shape[1:]
    z = jnp.zeros((k, n), w.dtype)
    return jnp.concatenate([jnp.concatenate([w[0], z], axis=1), jnp.concatenate([z, w[1]], axis=1)], axis=0)


def _mla_spread(w, lanes):
    k = w.shape[0]
    w = w.reshape(k, -1, len(lanes))
    return jnp.zeros((k, w.shape[1], LANES), w.dtype).at[:, :, lanes].set(w).reshape(k, -1)


def kernel(x, c, ctx, c_ctx, ada_w, ada_b, norm_ffn1, norm_mix, norm_ffn2, ffn1_gate, ffn1_up, ffn1_down, ffn2_gate, ffn2_up, ffn2_down, w_in, w_out, hy_conv_w, hy_conv_b, hy_f_w1, hy_f_b1, hy_f_w2, hy_f_b2, hy_f_w3, hy_f_b3, hy_f_w4, hy_f_freq, hy_bias, gqa_q_norm, gqa_k_norm, mla_cq_norm, mla_ckv_norm, mla_w_uq, mla_w_ukv, mla_q_norm, mla_k_norm, rw_mu, rw_w0, rw_w2, rw_a0, rw_a2, rw_g2, rw_k_k, rw_k_a, rw_r_k, rw_ln_w, rw_ln_b):
    b, length, d = x.shape
    lc = ctx.shape[1]
    depth = ada_w.shape[0]

    c_all = jnp.zeros((16, d), F32).at[:b].set(c).at[b].set(c_ctx)
    mod = _compute_mod(c_all, ada_w, ada_b)

    xc = ctx.reshape(1, b * lc, d)
    for l in range(depth):
        ctx_out = l < depth - 1
        mod_x = mod[l, :b].reshape(b, N_MOD, d)
        mod_c = mod[l, b:b + 1].reshape(1, N_MOD, d)
        row = lambda a: a[l].reshape(1, -1)

        wg1, wu1, wd1 = (w[l].astype(BF16) for w in (ffn1_gate, ffn1_up, ffn1_down))
        wg2, wu2, wd2 = (w[l].astype(BF16) for w in (ffn2_gate, ffn2_up, ffn2_down))
        wi = w_in[l]
        o1, o2, o3 = HY_COLS, HY_COLS + GQA_COLS, HY_COLS + GQA_COLS + MLA_COLS
        ml_lanes = _mla_lanes()
        o_kr = o3 - MLA_ROPE
        wi = jnp.concatenate([wi[:, :o_kr], _mla_spread(wi[:, o_kr:o3], ml_lanes[MLA_NOPE:]), wi[:, o3:]],
                             axis=1).astype(BF16)
        wo = w_out[l].astype(BF16)

        x = _ffn(x, mod_x, row(norm_ffn1), wg1, wu1, wd1, 0)
        xc = _ffn(xc, mod_c, row(norm_ffn1), wg1, wu1, wd1, 0)

        hy_x, gq_x, ml_x, rw_x = _inproj(x, mod_x, row(norm_mix), wi)
        hy_c, gq_c, ml_c, rw_c = (t.reshape(b, lc, -1) for t in _inproj(xc, mod_c, row(norm_mix), wi))

        w1p = jnp.zeros((LANES, HY_ORDER), F32).at[:HY_EMB].set(hy_f_w1[l])
        filt = (w1p, row(hy_f_b1), hy_f_w2[l], row(hy_f_b2), hy_f_w3[l], row(hy_f_b3), hy_f_w4[l],
                row(hy_f_freq))
        spec_x = _hy_spectrum(_hy_filters(length, *filt), row(hy_bias))
        y_hy_x = _hy_conv(hy_x, hy_conv_w[l], hy_conv_b[l], spec_x)

        gq = jnp.tile(row(gqa_q_norm), (1, 2))
        gk = jnp.tile(row(gqa_k_norm), (1, 2))
        q_l, k_l, v_l = _gqa_prep(gq_x, gq, gk, True)
        q_c, k_c, v_c = _gqa_prep(gq_c, gq, gk, False)
        y_gq_x = _attention(q_l, [(k_c, v_c), (k_l, v_l)], GQA_KV_HEADS)

        wuq = _mla_spread(mla_w_uq[l], ml_lanes).astype(BF16)
        wukv = mla_w_ukv[l].reshape(MLA_KV_RANK, MLA_HEADS, MLA_NOPE + MLA_V)
        wuk = _mla_spread(wukv[:, :, :MLA_NOPE].reshape(MLA_KV_RANK, -1), ml_lanes[:MLA_NOPE]).astype(BF16)
        wuv = wukv[:, :, MLA_NOPE:].reshape(MLA_KV_RANK, -1).astype(BF16)
        pad_n = lambda g: _mla_spread(g[l].reshape(1, MLA_QK), ml_lanes)
        mla_w = (row(mla_cq_norm), row(mla_ckv_norm), wuq, wuk, wuv, pad_n(mla_q_norm), pad_n(mla_k_norm))
        mq_l, mk_l, mv_l = _mla_prep(ml_x, *mla_w, True)
        mq_c, mk_c, mv_c = _mla_prep(ml_c, *mla_w, False)
        y_ml_x = _attention(mq_l, [(mk_c, mv_c), (mk_l, mv_l)], MLA_HEADS)

        rw_w = (row(rw_mu), row(rw_k_k), row(rw_k_a), rw_r_k[l].reshape(1, RW_C),
                rw_w0[l].reshape(1, 2 * RW_C), _blockdiag2(rw_w2[l]).astype(BF16),
                rw_a0[l].reshape(1, 2 * RW_C), _blockdiag2(rw_a2[l]).astype(BF16), rw_g2[l].astype(BF16))
        pc = _rw_prep(rw_c, *rw_w)
        px = _rw_prep(rw_x, *rw_w)
        zeros = jnp.zeros((b, 2, RW_C, RW_C), F32)
        yf_c, yb_c, s_ctx = _rw_scan(*pc[:6], zeros)
        yf_x, yb_x, _ = _rw_scan(*px[:6], s_ctx)

        tail = (row(rw_ln_w), row(rw_ln_b), wo, row(norm_ffn2), wg2, wu2, wd2)
        x = _mix_ffn(x, mod_x, (y_hy_x, y_gq_x, y_ml_x), (yf_x, yb_x, px[6], px[7]), *tail)
        if ctx_out:
            spec_c = _hy_spectrum(_hy_filters(lc, *filt), row(hy_bias))
            y_hy_c = _hy_conv(hy_c, hy_conv_w[l], hy_conv_b[l], spec_c)
            y_gq_c = _attention(q_c, [(k_c, v_c)], GQA_KV_HEADS)
            y_ml_c = _attention(mq_c, [(mk_c, mv_c)], MLA_HEADS)
            flat = lambda ts: tuple(t.reshape(1, b * lc, -1) for t in ts)
            xc = _mix_ffn(xc, mod_c, flat((y_hy_c, y_gq_c, y_ml_c)), flat((yf_c, yb_c, pc[6], pc[7])), *tail)
    return x
```

```python
import functools
import math

import numpy as np
import jax
import jax.numpy as jnp
from jax import lax
from jax.experimental import pallas as pl
from jax.experimental.pallas import tpu as pltpu

F32 = jnp.float32
BF16 = jnp.bfloat16

GRID_W = 64
HEAD_DIM = 64
N_MOD = 9
NORM_EPS = 1e-6
LOG2E = math.log2(math.e)
ROPE_THETA = 10000.0

HY_CH = 256
HY_EMB = 33
HY_ORDER = 64
HY_FAST_PCT = 0.3
HY_SLOW_PCT = 1.5
HY_TARGET = 1e-2

GQA_HEADS = 4
GQA_KV_HEADS = 2
MLA_HEADS = 4
MLA_NOPE = 64
MLA_ROPE = 32
MLA_QK = MLA_NOPE + MLA_ROPE
MLA_V = 64
MLA_Q_RANK = 256
MLA_KV_RANK = 128

RW_HEADS = 4
RW_N = 64
RW_C = RW_HEADS * RW_N
RW_DECAY_LORA = 64
RW_AAA_LORA = 64
RW_GATE_LORA = 128
RW_GN_EPS = 64e-5

HY_COLS = 3 * HY_CH
GQA_COLS = (GQA_HEADS + 2 * GQA_KV_HEADS) * HEAD_DIM
MLA_COLS = MLA_Q_RANK + MLA_KV_RANK + MLA_ROPE
MLA_COLS_PAD = 512
RW_COLS = 3 * RW_C + 2 * RW_DECAY_LORA + 2 * RW_AAA_LORA + RW_GATE_LORA
D_IN_PAD = HY_COLS + GQA_COLS + MLA_COLS_PAD + RW_COLS

LANES = 128
ROW_TILE = 512
PREP_TILE = 1024
Q_TILE = 256
ATTN_HEADS = 4
KEY_CHUNK = 256
VT_ROWS = 2 * HEAD_DIM
RW_CHUNK = 64
RW_BASE = 4
RW_LEVELS = 4
RW_BLOCK = 256
RW_SEQS = 4
VMEM_LIMIT = 56 * 2 ** 20


def _cparams(sem):
    return pltpu.CompilerParams(dimension_semantics=sem, vmem_limit_bytes=VMEM_LIMIT)


def _const_spec(shape):
    nd = len(shape)
    return pl.BlockSpec(shape, lambda *_: (0,) * nd, pipeline_mode=pl.Buffered(1))


def _dot(a, b, precision=None):
    return jnp.dot(a, b, preferred_element_type=F32, precision=precision)


def _dot_nt(a, b, precision=None):
    return lax.dot_general(a, b, (((1,), (1,)), ((), ())), preferred_element_type=F32,
                           precision=precision)


def _dot_tn(a, b, precision=None):
    return lax.dot_general(a, b, (((0,), (0,)), ((), ())), preferred_element_type=F32,
                           precision=precision)


def _pieces(x, n):
    out = []
    for _ in range(n):
        p = x.astype(BF16)
        out.append(p)
        x = x - p.astype(F32)
    return out


def _dot3(a, b):
    (a1, a2), (b1, b2) = _pieces(a, 2), _pieces(b, 2)
    return _dot(a1, b1) + _dot(a1, b2) + _dot(a2, b1)


def _segsum(x, seg):
    return functools.reduce(jnp.add, [_dot(p, seg) for p in _pieces(x, 2)])


def _sigmoid(x):
    return 1.0 / (1.0 + jnp.exp(-x))


def _adaln(x, g, mod_ref, i):
    shift = mod_ref[pl.ds(3 * i, 1), :]
    scale = mod_ref[pl.ds(3 * i + 1, 1), :]
    r = lax.rsqrt(jnp.mean(x * x, axis=-1, keepdims=True) + NORM_EPS)
    return (x * r) * (g * (1.0 + scale)) + shift


def _mod_kernel(c_ref, w_ref, b_ref, o_ref):
    c = c_ref[...]
    s = c * _sigmoid(c)
    o_ref[...] = _dot3(s, w_ref[...]) + b_ref[...]


def _compute_mod(c_all, ada_w, ada_b):
    depth, d, n = ada_w.shape
    rows = c_all.shape[0]
    tn = 1024
    return pl.pallas_call(
        _mod_kernel,
        grid=(depth, n // tn),
        in_specs=[
            pl.BlockSpec((rows, d), lambda l, j: (0, 0)),
            pl.BlockSpec((None, d, tn), lambda l, j: (l, 0, j)),
            pl.BlockSpec((None, 1, tn), lambda l, j: (l, 0, j)),
        ],
        out_specs=pl.BlockSpec((None, rows, tn), lambda l, j: (l, 0, j)),
        out_shape=jax.ShapeDtypeStruct((depth, rows, n), F32),
        compiler_params=_cparams(("arbitrary", "arbitrary")),
        name="mod",
    )(c_all, ada_w, ada_b.reshape(depth, 1, n))


def _ffn_kernel(x_ref, mod_ref, g_ref, wg_ref, wu_ref, wd_ref, o_ref, *, sub):
    x = x_ref[...]
    h = _adaln(x, g_ref[...], mod_ref, sub).astype(BF16)
    a = _dot(h, wg_ref[...])
    u = _dot(h, wu_ref[...])
    z = (a * _sigmoid(a) * u).astype(BF16)
    y = _dot(z, wd_ref[...])
    gate = mod_ref[pl.ds(3 * sub + 2, 1), :]
    o_ref[...] = x + (0.5 * gate) * y


def _ffn(x, mod, g, wg, wu, wd, sub):
    bm, rows, d = x.shape
    tm = min(ROW_TILE, rows)
    f = wg.shape[1]
    return pl.pallas_call(
        functools.partial(_ffn_kernel, sub=sub),
        grid=(bm, rows // tm),
        in_specs=[
            pl.BlockSpec((None, tm, d), lambda b, i: (b, i, 0)),
            pl.BlockSpec((None, N_MOD, d), lambda b, i: (b, 0, 0)),
            _const_spec((1, d)),
            _const_spec((d, f)),
            _const_spec((d, f)),
            _const_spec((f, d)),
        ],
        out_specs=pl.BlockSpec((None, tm, d), lambda b, i: (b, i, 0)),
        out_shape=jax.ShapeDtypeStruct(x.shape, F32),
        compiler_params=_cparams(("arbitrary", "arbitrary")),
        name="ffn",
    )(x, mod, g, wg, wu, wd)


_IN_OFFS = (0, HY_COLS, HY_COLS + GQA_COLS, HY_COLS + GQA_COLS + MLA_COLS_PAD, D_IN_PAD)


def _inproj_kernel(x_ref, mod_ref, g_ref, w_ref, hy_ref, gq_ref, ml_ref, rw_ref):
    h = _adaln(x_ref[...], g_ref[...], mod_ref, 1).astype(BF16)
    p = _dot(h, w_ref[...])
    for ref, lo, hi in zip((hy_ref, gq_ref, ml_ref, rw_ref), _IN_OFFS[:-1], _IN_OFFS[1:]):
        ref[...] = p[:, lo:hi]


def _inproj(x, mod, g, w):
    bm, rows, d = x.shape
    tm = min(ROW_TILE, rows)
    widths = [hi - lo for lo, hi in zip(_IN_OFFS[:-1], _IN_OFFS[1:])]
    return pl.pallas_call(
        _inproj_kernel,
        grid=(bm, rows // tm),
        in_specs=[
            pl.BlockSpec((None, tm, d), lambda b, i: (b, i, 0)),
            pl.BlockSpec((None, N_MOD, d), lambda b, i: (b, 0, 0)),
            _const_spec((1, d)),
            _const_spec((d, D_IN_PAD)),
        ],
        out_specs=[pl.BlockSpec((None, tm, wd), lambda b, i: (b, i, 0)) for wd in widths],
        out_shape=[jax.ShapeDtypeStruct((bm, rows, wd), F32) for wd in widths],
        compiler_params=_cparams(("arbitrary", "arbitrary")),
        name="inproj",
    )(x, mod, g, w)


def _mix_ffn_kernel(x_ref, mod_ref, yh_ref, yg_ref, ym_ref, yf_ref, yb_ref, bonus_ref, gate_ref,
                    lnw_ref, lnb_ref, seg_ref, wo_ref, g_ref, wg_ref, wu_ref, wd_ref, o_ref):
    y = yf_ref[...] + yb_ref[...]
    seg = seg_ref[...]
    yc = y - _segsum(y, seg)
    var = _segsum(yc * yc, seg)
    yn = yc * lax.rsqrt(var + RW_GN_EPS) * lnw_ref[...] + lnb_ref[...]
    y_rw = ((yn + bonus_ref[...]) * gate_ref[...]).astype(BF16)
    ycat = jnp.concatenate([yh_ref[...], yg_ref[...], ym_ref[...], y_rw], axis=-1)
    x = x_ref[...] + mod_ref[pl.ds(5, 1), :] * _dot(ycat, wo_ref[...])
    h = _adaln(x, g_ref[...], mod_ref, 2).astype(BF16)
    a = _dot(h, wg_ref[...])
    u = _dot(h, wu_ref[...])
    z = (a * _sigmoid(a) * u).astype(BF16)
    o_ref[...] = x + (0.5 * mod_ref[pl.ds(8, 1), :]) * _dot(z, wd_ref[...])


def _mix_ffn(x, mod, ys, rw, ln_w, ln_b, wo, g, wg, wu, wd):
    bm, rows, d = x.shape
    tm = min(ROW_TILE, rows)
    c = RW_C
    seg = jnp.asarray(_seg_matrix(c, RW_N, 1.0 / RW_N), BF16)
    row_spec = pl.BlockSpec((None, tm, d), lambda b, i: (b, i, 0))
    y_spec = pl.BlockSpec((None, tm, c), lambda b, i: (b, i, 0))
    consts = (ln_w, ln_b, seg, wo, g, wg, wu, wd)
    return pl.pallas_call(
        _mix_ffn_kernel,
        grid=(bm, rows // tm),
        in_specs=[row_spec, pl.BlockSpec((None, N_MOD, d), lambda b, i: (b, 0, 0))] + [y_spec] * 7
                 + [_const_spec(a.shape) for a in consts],
        out_specs=row_spec,
        out_shape=jax.ShapeDtypeStruct(x.shape, F32),
        compiler_params=_cparams(("arbitrary", "arbitrary")),
        name="mix_ffn",
    )(x, mod, *ys, *rw, *consts)


def _hy_features(length):
    t01 = np.linspace(0.0, 1.0, length, dtype=np.float32)[:, None]
    bands = (HY_EMB - 1) // 2
    w_ang = (np.float32(2.0 * math.pi) * np.arange(length, dtype=np.float32)[:, None]
             / np.float32(length)).astype(np.float32)
    f = np.linspace(1e-4, bands - 1, bands, dtype=np.float32)[None]
    arg = (f * w_ang).astype(np.float32)
    z = np.concatenate([t01, np.cos(arg), -np.sin(arg)], axis=-1).astype(np.float32)
    zp = np.zeros((length, LANES), np.float32)
    zp[:, :HY_EMB] = z
    return zp


def _hy_deltas():
    max_decay = math.log(HY_TARGET) / HY_FAST_PCT
    min_decay = math.log(HY_TARGET) / HY_SLOW_PCT
    d = np.abs(np.linspace(min_decay, max_decay, HY_CH, dtype=np.float32))
    return np.tile(d, 2)[None].astype(np.float32)


def _hyfilt_kernel(z_ref, w1_ref, b1_ref, w2_ref, b2_ref, w3_ref, b3_ref, w4_ref, fr_ref,
                   dl_ref, o_ref):
    z = z_ref[...]
    fr = fr_ref[...]
    h = jnp.sin(fr * (_dot3(z, w1_ref[...]) + b1_ref[...]))
    h = jnp.sin(fr * (_dot3(h, w2_ref[...]) + b2_ref[...]))
    h = jnp.sin(fr * (_dot3(h, w3_ref[...]) + b3_ref[...]))
    h = _dot3(h, w4_ref[...])
    o_ref[...] = h * jnp.exp(-z[:, 0:1] * dl_ref[...])


def _hy_filters(length, w1p, b1, w2, b2, w3, b3, w4, freq):
    z = jnp.asarray(_hy_features(length))
    dl = jnp.asarray(_hy_deltas())
    tl = min(ROW_TILE, length)
    consts = (w1p, b1, w2, b2, w3, b3, w4, freq, dl)
    return pl.pallas_call(
        _hyfilt_kernel,
        grid=(length // tl,),
        in_specs=[pl.BlockSpec((tl, LANES), lambda i: (i, 0))] + [_const_spec(a.shape) for a in consts],
        out_specs=pl.BlockSpec((tl, 2 * HY_CH), lambda i: (i, 0)),
        out_shape=jax.ShapeDtypeStruct((length, 2 * HY_CH), F32),
        compiler_params=_cparams(("arbitrary",)),
        name="hyena_filters",
    )(z, *consts)


FFT_SLAB_PAD = 8
FFT_GROUP = 16
FFT_SLAB_GROUP = 12


def _fft_factors(length):
    return (128, 64) if length >= 2048 else (2 * length, 1)


def _fft_layout(n1, n2):
    if n2 == 1:
        return 1, n1, 2 * n1
    kept = n1 // 2 + 1
    return -(-kept // FFT_SLAB_GROUP) * FFT_SLAB_GROUP, n2, 2 * n2 + FFT_SLAB_PAD


def _stack3(t):
    hi = t.astype(jnp.bfloat16)
    lo = (t - hi.astype(np.float64)).astype(jnp.bfloat16)
    return np.concatenate([hi, lo, hi], axis=-1)


def _rows3(d):
    hi = d.astype(BF16)
    lo = (d - hi.astype(F32)).astype(BF16)
    return jnp.concatenate([hi, hi, lo], axis=0)


@functools.lru_cache(maxsize=None)
def _fft_tables(length):
    n1, n2 = _fft_factors(length)
    n = n1 * n2
    assert n == 2 * length
    a_n2 = np.arange(n2)[:, None, None]
    a_k1 = np.arange(n1)[None, :, None]
    a_n1 = np.arange(n1 // 2)[None, None, :]
    ang = 2.0 * np.pi * ((a_n1 * a_k1 % n1) / n1 + (a_n2 * a_k1 % n) / n)
    t_re, t_im = np.cos(ang), -np.sin(ang)
    wgt = np.ones(n1)
    if n2 > 1:
        nk = _fft_layout(n1, n2)[0]
        keep = np.arange(nk) <= n1 // 2
        t_re, t_im = (np.where(keep[None, :, None], t[:, :nk], 0.0) for t in (t_re, t_im))
        wgt = np.where(np.isin(np.arange(nk), (0, n1 // 2)), 1.0, 2.0) * keep
    ta = np.concatenate([t_re, t_im], axis=1)
    t_re_w, t_im_w = t_re * wgt[None, :, None], t_im * wgt[None, :, None]
    tai = np.concatenate([np.transpose(t_re_w, (0, 2, 1)), np.transpose(t_im_w, (0, 2, 1))], axis=2) / n
    jj = np.arange(n2)
    ang2 = 2.0 * np.pi * (np.outer(jj, jj) % n2) / n2
    c, s = np.cos(ang2), np.sin(ang2)
    fb = np.block([[c, s], [-s, c]])
    fbi = np.block([[c, -s], [s, c]])
    return tuple(_stack3(t) for t in (ta, tai, fb, fbi))


def _fft_stage_a(src_ref, fbuf_ref, ta_ref, n1, n2):
    if n2 == 1:
        fbuf_ref[...] = _dot(ta_ref[0], _rows3(src_ref[...]))
        return
    nk, _, pitch = _fft_layout(n1, n2)

    def body(i, carry):
        js = [i * FFT_GROUP + g for g in range(FFT_GROUP)]
        rows = [_rows3(src_ref[pl.ds(j, n1 // 2, stride=n2), :]) for j in js]
        ts = [_dot(ta_ref[j], r) for j, r in zip(js, rows)]
        for j, t in zip(js, ts):
            fbuf_ref[pl.ds(j, nk, stride=pitch), :] = t[:nk]
            fbuf_ref[pl.ds(n2 + j, nk, stride=pitch), :] = t[nk:]
        return carry

    lax.fori_loop(0, n2 // FFT_GROUP, body, 0)


def _spec_kernel(hf_ref, hb_ref, bias_ref, ta_ref, fb_ref, o_ref, fbuf_ref, *, n1, n2):
    nslab, hs, pitch = _fft_layout(n1, n2)
    if pitch > 2 * hs:
        o_ref[...] = jnp.zeros(o_ref.shape, F32)
    _fft_stage_a(hf_ref, o_ref, ta_ref, n1, n2)
    _fft_stage_a(hb_ref, fbuf_ref, ta_ref, n1, n2)
    is_re = lax.broadcasted_iota(jnp.int32, (2 * hs, 1), 0) < hs
    sgn = jnp.where(is_re, 1.0, -1.0).astype(F32)
    skip = jnp.where(is_re, bias_ref[...], 0.0)

    grp = FFT_SLAB_GROUP if n2 > 1 else 1

    def stage_b(i, carry):
        sls = [pl.ds(pl.multiple_of((i * grp + g) * pitch, 8), 2 * hs) for g in range(grp)]
        a = [o_ref[sl, :] for sl in sls]
        b = [fbuf_ref[sl, :] for sl in sls]
        if n2 > 1:
            a = [_dot(fb_ref[...], _rows3(x)) for x in a]
            b = [_dot(fb_ref[...], _rows3(x)) for x in b]
        for sl, x, y in zip(sls, a, b):
            o_ref[sl, :] = x + sgn * y + skip
        return carry

    lax.fori_loop(0, nslab // grp, stage_b, 0)


def _hy_spectrum(h, bias):
    length = h.shape[0]
    n1, n2 = _fft_factors(length)
    nslab, _, pitch = _fft_layout(n1, n2)
    ta, _, fb, _ = (jnp.asarray(t) for t in _fft_tables(length))
    nblk = HY_CH // LANES
    return pl.pallas_call(
        functools.partial(_spec_kernel, n1=n1, n2=n2),
        grid=(nblk,),
        in_specs=[pl.BlockSpec((length, LANES), lambda j: (0, j)),
                  pl.BlockSpec((length, LANES), lambda j: (0, nblk + j)),
                  pl.BlockSpec((1, LANES), lambda j: (0, j)),
                  _const_spec(ta.shape), _const_spec(fb.shape)],
        out_specs=pl.BlockSpec((nslab * pitch, LANES), lambda j: (0, j)),
        out_shape=jax.ShapeDtypeStruct((nslab * pitch, HY_CH), F32),
        scratch_shapes=[pltpu.VMEM((nslab * pitch, LANES), F32)],
        compiler_params=_cparams(("arbitrary",)),
        name="hyena_spectrum",
    )(h, h, bias, ta, fb)


def _shift_rows(x, prev_row, next_row):
    n = x.shape[0]
    row = lax.broadcasted_iota(jnp.int32, x.shape, 0)
    up = jnp.where(row == 0, prev_row, pltpu.roll(x, 1, 0))
    dn = jnp.where(row == n - 1, next_row, pltpu.roll(x, n - 1, 0))
    return up, dn


def _hyconv_kernel(x1_ref, x2_ref, v_ref, cw1_ref, cw2_ref, cwv_ref, cb_ref, g_ref,
                   ta_ref, tai_ref, fb_ref, fbi_ref, o_ref, u_ref, fbuf_ref, *, n1, n2):
    def conv3(ref, w_ref, b):
        x = ref[...]
        up, dn = _shift_rows(x, 0.0, 0.0)
        return up * w_ref[0:1, :] + x * w_ref[1:2, :] + dn * w_ref[2:3, :] + b

    cb = cb_ref[...]
    u_ref[...] = conv3(x1_ref, cw1_ref, cb[0:1, :]) * conv3(v_ref, cwv_ref, cb[2:3, :])

    _fft_stage_a(u_ref, fbuf_ref, ta_ref, n1, n2)
    nslab, hs, pitch = _fft_layout(n1, n2)

    grp = FFT_SLAB_GROUP if n2 > 1 else 1

    def cmul(a, g):
        ar, ai, gr, gi = a[:hs], a[hs:], g[:hs], g[hs:]
        return jnp.concatenate([ar * gr - ai * gi, ar * gi + ai * gr], axis=0)

    def stage_b(i, carry):
        sls = [pl.ds(pl.multiple_of((i * grp + g) * pitch, 8), 2 * hs) for g in range(grp)]
        pairs = [sls[g:g + 2] for g in range(0, grp, 2)]
        a = [jnp.concatenate([fbuf_ref[sl, :] for sl in pr], axis=1) for pr in pairs]
        if n2 > 1:
            a = [_dot(fb_ref[...], _rows3(x)) for x in a]
        a = [cmul(x, jnp.concatenate([g_ref[sl, :] for sl in pr], axis=1)) for x, pr in zip(a, pairs)]
        if n2 > 1:
            a = [_dot(fbi_ref[...], _rows3(x)) for x in a]
        for pr, x in zip(pairs, a):
            for g, sl in enumerate(pr):
                fbuf_ref[sl, :] = x[:, g * LANES:(g + 1) * LANES]
        return carry

    lax.fori_loop(0, nslab // grp, stage_b, 0)

    if n2 == 1:
        u_ref[...] = _dot(tai_ref[0], _rows3(fbuf_ref[...]))
    else:
        def stage_a_inv(i, carry):
            js = [i * FFT_GROUP + g for g in range(FFT_GROUP)]
            cs = [_rows3(jnp.concatenate([fbuf_ref[pl.ds(j, nslab, stride=pitch), :],
                                          fbuf_ref[pl.ds(n2 + j, nslab, stride=pitch), :]], axis=0))
                  for j in js]
            ys = [_dot(tai_ref[j], c) for j, c in zip(js, cs)]
            for j, y in zip(js, ys):
                u_ref[pl.ds(j, n1 // 2, stride=n2), :] = y
            return carry

        lax.fori_loop(0, n2 // FFT_GROUP, stage_a_inv, 0)

    o_ref[...] = (conv3(x2_ref, cw2_ref, cb[1:2, :]) * u_ref[...]).astype(o_ref.dtype)


def _hy_conv(p, conv_w, conv_b, spec):
    b, length, _ = p.shape
    n1, n2 = _fft_factors(length)
    nslab, _, pitch = _fft_layout(n1, n2)
    ta, tai, fb, fbi = (jnp.asarray(t) for t in _fft_tables(length))
    nblk = HY_CH // LANES
    cb3 = conv_b.reshape(3, HY_CH)
    col = lambda g: pl.BlockSpec((None, length, LANES), lambda j, i, g=g: (i, 0, g * nblk + j))
    wcol = lambda g: pl.BlockSpec((3, LANES), lambda j, i, g=g: (0, g * nblk + j))
    return pl.pallas_call(
        functools.partial(_hyconv_kernel, n1=n1, n2=n2),
        grid=(nblk, b),
        in_specs=[col(0), col(1), col(2), wcol(0), wcol(1), wcol(2),
                  pl.BlockSpec((3, LANES), lambda j, i: (0, j)),
                  pl.BlockSpec((nslab * pitch, LANES), lambda j, i: (0, j), pipeline_mode=pl.Buffered(1)),
                  _const_spec(ta.shape), _const_spec(tai.shape), _const_spec(fb.shape),
                  _const_spec(fbi.shape)],
        out_specs=pl.BlockSpec((None, length, LANES), lambda j, i: (i, 0, j)),
        out_shape=jax.ShapeDtypeStruct((b, length, HY_CH), BF16),
        scratch_shapes=[pltpu.VMEM((length, LANES), F32), pltpu.VMEM((nslab * pitch, LANES), F32)],
        compiler_params=_cparams(("arbitrary", "arbitrary")),
        name="hyena_conv",
    )(p, p, p, conv_w, conv_w, conv_w, cb3, spec, ta, tai, fb, fbi)


def _rope_tables(length, d_rot, lane_lo, head_w):
    rows = length // GRID_W
    row = np.repeat(np.arange(rows, dtype=np.float32), GRID_W)
    colv = np.tile(np.arange(GRID_W, dtype=np.float32), rows)
    n_freq = d_rot // 4
    inv = (np.float32(ROPE_THETA) ** (-np.arange(n_freq, dtype=np.float32) / np.float32(n_freq))).astype(np.float32)
    ang = np.concatenate([row[:, None] * inv, colv[:, None] * inv], axis=-1).astype(np.float32)
    cos_t = np.ones((length, LANES), np.float32)
    sin_t = np.zeros((length, LANES), np.float32)
    c, s = np.cos(ang), np.sin(ang)
    for base in range(0, LANES, head_w):
        for i in range(d_rot // 2):
            cos_t[:, base + lane_lo + 2 * i] = c[:, i]
            cos_t[:, base + lane_lo + 2 * i + 1] = c[:, i]
            sin_t[:, base + lane_lo + 2 * i] = -s[:, i]
            sin_t[:, base + lane_lo + 2 * i + 1] = s[:, i]
    return cos_t, sin_t


def _rope_angles(length, d_rot):
    rows = length // GRID_W
    row = np.repeat(np.arange(rows, dtype=np.float32), GRID_W)
    colv = np.tile(np.arange(GRID_W, dtype=np.float32), rows)
    n_freq = d_rot // 4
    inv = (np.float32(ROPE_THETA) ** (-np.arange(n_freq, dtype=np.float32) / np.float32(n_freq))).astype(np.float32)
    return np.concatenate([row[:, None] * inv, colv[:, None] * inv], axis=-1).astype(np.float32)


def _mla_lanes():
    lanes = np.empty(MLA_QK, np.int64)
    n = np.arange(MLA_NOPE)
    lanes[:MLA_NOPE] = np.where(n < 32, n, n + 16)
    i = np.arange(MLA_ROPE // 2)
    lanes[MLA_NOPE + 2 * i] = 32 + i
    lanes[MLA_NOPE + 2 * i + 1] = 96 + i
    return lanes


def _mla_rope_tables(length):
    ang = _rope_angles(length, MLA_ROPE)
    cos_t = np.ones((length, LANES), np.float32)
    sin_t = np.zeros((length, LANES), np.float32)
    n_pairs = MLA_ROPE // 2
    cos_t[:, 32:32 + n_pairs] = cos_t[:, 96:96 + n_pairs] = np.cos(ang)
    sin_t[:, 32:32 + n_pairs] = -np.sin(ang)
    sin_t[:, 96:96 + n_pairs] = np.sin(ang)
    return cos_t, sin_t


def _rope(x, cos_t, sin_t):
    lane = lax.broadcasted_iota(jnp.int32, x.shape, 1)
    w = x.shape[1]
    partner = jnp.where(jnp.bitwise_and(lane, 1) == 0, pltpu.roll(x, w - 1, 1), pltpu.roll(x, 1, 1))
    return x * cos_t + partner * sin_t


def _store_vt(vt_ref, h, vt):
    row = lax.broadcasted_iota(jnp.int32, (VT_ROWS - HEAD_DIM, vt.shape[1]), 0)
    vt_ref[h, 0:HEAD_DIM, :] = vt.astype(BF16)
    vt_ref[h, HEAD_DIM:VT_ROWS, :] = jnp.where(row == 0, 1.0, 0.0).astype(BF16)


def _gqa_prep_kernel(p_ref, cos_ref, sin_ref, gq_ref, gk_ref, seg_ref, q_ref, k_ref, vt_ref, *, rope):
    p = p_ref[...]
    seg = seg_ref[...]
    lane = lax.broadcasted_iota(jnp.int32, (p.shape[0], LANES), 1)
    low = lane < HEAD_DIM

    def hnorm(x, g):
        ms = _segsum(x * x, seg)
        return x * lax.rsqrt(ms + NORM_EPS) * g

    scale = HEAD_DIM ** -0.5 * LOG2E
    for c in range(2):
        q = hnorm(p[:, c * LANES:(c + 1) * LANES], gq_ref[...])
        if rope:
            q = _rope(q, cos_ref[...], sin_ref[...])
        q = q * scale
        qs = pltpu.roll(q, HEAD_DIM, 1)
        q_ref[:, (2 * c) * LANES:(2 * c + 1) * LANES] = jnp.where(low, q, 0.0).astype(BF16)
        q_ref[:, (2 * c + 1) * LANES:(2 * c + 2) * LANES] = jnp.where(low, qs, 0.0).astype(BF16)
    k = hnorm(p[:, 256:384], gk_ref[...])
    if rope:
        k = _rope(k, cos_ref[...], sin_ref[...])
    ks = pltpu.roll(k, HEAD_DIM, 1)
    k_ref[:, 0:LANES] = jnp.where(low, k, 0.0).astype(BF16)
    k_ref[:, LANES:2 * LANES] = jnp.where(low, ks, 0.0).astype(BF16)
    vt = p[:, 384:512].T
    _store_vt(vt_ref, 0, vt[0:HEAD_DIM])
    _store_vt(vt_ref, 1, vt[HEAD_DIM:2 * HEAD_DIM])


def _seg_matrix(width, seg, value):
    i = np.arange(width)
    return ((i[:, None] // seg) == (i[None, :] // seg)).astype(np.float32) * np.float32(value)


def _gqa_prep(p, gq, gk, rope):
    b, length, _ = p.shape
    tm = min(PREP_TILE, length)
    if rope:
        cos_t, sin_t = (jnp.asarray(t) for t in _rope_tables(length, HEAD_DIM, 0, HEAD_DIM))
    else:
        cos_t = sin_t = jnp.zeros((length, LANES), F32)
    seg = jnp.asarray(_seg_matrix(LANES, HEAD_DIM, 1.0 / HEAD_DIM), BF16)
    tab = pl.BlockSpec((tm, LANES), lambda i, bb: (i, 0))
    outw = (512, 256)
    return pl.pallas_call(
        functools.partial(_gqa_prep_kernel, rope=rope),
        grid=(length // tm, b),
        in_specs=[pl.BlockSpec((None, tm, GQA_COLS), lambda i, bb: (bb, i, 0)), tab, tab,
                  _const_spec((1, LANES)), _const_spec((1, LANES)), _const_spec((LANES, LANES))],
        out_specs=[pl.BlockSpec((None, tm, w), lambda i, bb: (bb, i, 0)) for w in outw]
                  + [pl.BlockSpec((None, GQA_KV_HEADS, VT_ROWS, tm), lambda i, bb: (bb, 0, 0, i))],
        out_shape=[jax.ShapeDtypeStruct((b, length, w), BF16) for w in outw]
                  + [jax.ShapeDtypeStruct((b, GQA_KV_HEADS, VT_ROWS, length), BF16)],
        compiler_params=_cparams(("arbitrary", "arbitrary")),
        name="gqa_prep",
    )(p, cos_t, sin_t, gq, gk, seg)


def _mla_prep_kernel(p_ref, cos_ref, sin_ref, cqn_ref, ckvn_ref, wuq_ref, wuk_ref, wuv_ref,
                     qn_ref, kn_ref, ones_ref, q_ref, k_ref, vt_ref, *, rope):
    p = p_ref[...]

    def rms(x, g, width):
        ms = _segsum(x * x, ones_ref[0:x.shape[1], 0:x.shape[1]]) * (1.0 / width)
        return x * lax.rsqrt(ms + NORM_EPS) * g

    cq = rms(p[:, 0:MLA_Q_RANK], cqn_ref[...], MLA_Q_RANK).astype(BF16)
    ckv = rms(p[:, MLA_Q_RANK:MLA_Q_RANK + MLA_KV_RANK], ckvn_ref[...], MLA_KV_RANK).astype(BF16)
    q = _dot(cq, wuq_ref[...])
    kn = _dot(ckv, wuk_ref[...])
    v = _dot(ckv, wuv_ref[...])
    for c in range(MLA_HEADS // 2):
        vt = v[:, c * LANES:(c + 1) * LANES].T
        _store_vt(vt_ref, 2 * c, vt[0:MLA_V])
        _store_vt(vt_ref, 2 * c + 1, vt[MLA_V:2 * MLA_V])
    kr = p[:, 384:512]
    scale = MLA_QK ** -0.5 * LOG2E

    def rot(x):
        return x * cos_ref[...] + pltpu.roll(x, LANES // 2, 1) * sin_ref[...]

    for h in range(MLA_HEADS):
        sl = slice(h * LANES, (h + 1) * LANES)
        qh = rms(q[:, sl], qn_ref[...], MLA_QK)
        kh = rms(kn[:, sl] + kr, kn_ref[...], MLA_QK)
        if rope:
            qh, kh = rot(qh), rot(kh)
        q_ref[:, sl] = (qh * scale).astype(BF16)
        k_ref[:, sl] = kh.astype(BF16)


def _mla_prep(p, cqn, ckvn, wuq, wuk, wuv, qn, kn, rope):
    b, length, _ = p.shape
    tm = min(PREP_TILE, length)
    if rope:
        cos_t, sin_t = (jnp.asarray(t) for t in _mla_rope_tables(length))
    else:
        cos_t = sin_t = jnp.zeros((length, LANES), F32)
    tab = pl.BlockSpec((tm, LANES), lambda i, bb: (i, 0))
    ones = jnp.ones((MLA_Q_RANK, MLA_Q_RANK), BF16)
    consts = (cqn, ckvn, wuq, wuk, wuv, qn, kn, ones)
    outw = (512, 512)
    return pl.pallas_call(
        functools.partial(_mla_prep_kernel, rope=rope),
        grid=(length // tm, b),
        in_specs=[pl.BlockSpec((None, tm, MLA_COLS_PAD), lambda i, bb: (bb, i, 0)), tab, tab]
                 + [_const_spec(a.shape) for a in consts],
        out_specs=[pl.BlockSpec((None, tm, w), lambda i, bb: (bb, i, 0)) for w in outw]
                  + [pl.BlockSpec((None, MLA_HEADS, VT_ROWS, tm), lambda i, bb: (bb, 0, 0, i))],
        out_shape=[jax.ShapeDtypeStruct((b, length, w), BF16) for w in outw]
                  + [jax.ShapeDtypeStruct((b, MLA_HEADS, VT_ROWS, length), BF16)],
        compiler_params=_cparams(("arbitrary", "arbitrary")),
        name="mla_prep",
    )(p, cos_t, sin_t, *consts)


def _attn_kernel(*refs, n_seg):
    nh = ATTN_HEADS
    q_ref, o_ref = refs[0], refs[-1]
    segs = [refs[1 + 2 * nh * s:1 + 2 * nh * (s + 1)] for s in range(n_seg)]
    q = q_ref[...]
    qh = [q[:, h * LANES:(h + 1) * LANES] for h in range(nh)]
    pieces = [(h, seg, c0, min(KEY_CHUNK, seg[0].shape[0]))
              for h in range(nh) for seg in segs
              for c0 in range(0, seg[0].shape[0], min(KEY_CHUNK, seg[0].shape[0]))]
    scores = [_dot_nt(seg[h][c0:c0 + n, :], qh[h]) for h, seg, c0, n in pieces]
    mx = [jnp.max(s, axis=0, keepdims=True) for s in scores]
    ps = [jnp.exp2(s - m).astype(BF16) for s, m in zip(scores, mx)]
    os_ = [_dot(seg[nh + h][:, c0:c0 + n], e) for (h, seg, c0, n), e in zip(pieces, ps)]
    outs = []
    for h in range(nh):
        idx = [i for i, pc in enumerate(pieces) if pc[0] == h]
        m = functools.reduce(jnp.maximum, [mx[i] for i in idx])
        o = functools.reduce(jnp.add, [os_[i] * jnp.exp2(mx[i] - m) for i in idx])
        outs.append(o[0:HEAD_DIM] / o[HEAD_DIM:HEAD_DIM + 1])
    o_ref[...] = jnp.concatenate(outs, axis=0).T.astype(o_ref.dtype)


def _attention(q, kv_segs, k_heads):
    b, lq, _ = q.shape
    tq = min(Q_TILE, lq)
    nh = ATTN_HEADS
    kv_of = [h * k_heads // nh for h in range(nh)]
    in_specs = [pl.BlockSpec((None, tq, nh * LANES), lambda bb, i: (bb, i, 0))]
    args = [q]
    for k, vt in kv_segs:
        lk = k.shape[1]
        in_specs += [pl.BlockSpec((None, lk, LANES), lambda bb, i, j=j: (bb, 0, j)) for j in kv_of]
        in_specs += [pl.BlockSpec((None, None, VT_ROWS, lk), lambda bb, i, j=j: (bb, j, 0, 0)) for j in kv_of]
        args += [k] * nh + [vt] * nh
    return pl.pallas_call(
        functools.partial(_attn_kernel, n_seg=len(kv_segs)),
        grid=(b, lq // tq),
        in_specs=in_specs,
        out_specs=pl.BlockSpec((None, tq, nh * HEAD_DIM), lambda bb, i: (bb, i, 0)),
        out_shape=jax.ShapeDtypeStruct((b, lq, nh * HEAD_DIM), BF16),
        compiler_params=_cparams(("arbitrary", "arbitrary")),
        name="attention",
    )(*args)


def _rw_prep_kernel(p_ref, prev_ref, next_ref, mu_ref, kk_ref_w, ka_ref, rk_ref, w0_ref, w2_ref,
                    a0_ref, a2_ref, g2_ref, seg_ref,
                    r_ref, v_ref, kk_ref, lw_ref, kd_ref, bd_ref, bonus_ref, g_ref):
    i = pl.program_id(1)
    n = pl.num_programs(1)
    x = p_ref[...]
    prev_row = jnp.where(i > 0, prev_ref[7:8, :], 0.0)
    next_row = jnp.where(i < n - 1, next_ref[0:1, :], 0.0)
    up, dn = _shift_rows(x, prev_row, next_row)
    xs = x + mu_ref[...] * (0.5 * (up + dn) - x)
    c = RW_C
    r, k, v = xs[:, 0:c], xs[:, c:2 * c], xs[:, 2 * c:3 * c]
    xw, xa, xg = xs[:, 3 * c:3 * c + 128], xs[:, 3 * c + 128:3 * c + 256], xs[:, 3 * c + 256:3 * c + 384]
    seg = seg_ref[...]
    kk = k * kk_ref_w[...]
    nrm = jnp.sqrt(_segsum(kk * kk, seg))
    kk = kk / jnp.maximum(nrm, 1e-12)
    u = w0_ref[...] + _dot(jnp.tanh(xw).astype(BF16), w2_ref[...])
    z = -u
    softplus = jnp.maximum(z, 0.0) + jnp.log(1.0 + jnp.exp(-jnp.abs(z)))
    lw_ref[...] = -jnp.exp(-softplus - 0.5)
    a = _sigmoid(a0_ref[...] + _dot(xa.astype(BF16), a2_ref[...]))
    k_sum = jnp.zeros_like(r)
    for d in range(2):
        a_d = a[:, d * c:(d + 1) * c]
        k_d = k * (1.0 + (a_d - 1.0) * ka_ref[...])
        kd_ref[:, d * c:(d + 1) * c] = k_d
        bd_ref[:, d * c:(d + 1) * c] = kk * a_d
        k_sum = k_sum + k_d
    bonus = _segsum(r * k_sum * rk_ref[...], seg)
    r_ref[...] = r
    v_ref[...] = v
    kk_ref[...] = kk
    bonus_ref[...] = bonus * v
    g_ref[...] = _dot(_sigmoid(xg).astype(BF16), g2_ref[...])


def _rw_prep(p, mu, k_k, k_a, r_k, w0, w2bd, a0, a2bd, g2):
    b, length, cols = p.shape
    tm = min(PREP_TILE, length)
    nb = tm // 8
    last = length // 8 - 1
    seg = jnp.asarray(_seg_matrix(RW_C, RW_N, 1.0), BF16)
    consts = (mu, k_k, k_a, r_k, w0, w2bd, a0, a2bd, g2, seg)
    outw = (RW_C, RW_C, RW_C, 2 * RW_C, 2 * RW_C, 2 * RW_C, RW_C, RW_C)
    return pl.pallas_call(
        _rw_prep_kernel,
        grid=(b, length // tm),
        in_specs=[pl.BlockSpec((None, tm, cols), lambda bb, i: (bb, i, 0)),
                  pl.BlockSpec((None, 8, cols), lambda bb, i: (bb, jnp.maximum(i * nb - 1, 0), 0)),
                  pl.BlockSpec((None, 8, cols), lambda bb, i: (bb, jnp.minimum((i + 1) * nb, last), 0))]
                 + [_const_spec(a.shape) for a in consts],
        out_specs=[pl.BlockSpec((None, tm, w), lambda bb, i: (bb, i, 0)) for w in outw],
        out_shape=[jax.ShapeDtypeStruct((b, length, w), F32) for w in outw],
        compiler_params=_cparams(("arbitrary", "arbitrary")),
        name="rwkv_prep",
    )(p, p, p, *consts)


@functools.lru_cache(maxsize=None)
def _rw_masks():
    cs, n = RW_CHUNK, RW_C
    i = np.arange(n)
    same = ((i[:, None] // cs) == (i[None, :] // cs)).astype(np.float32)
    t, s = np.arange(cs)[:, None], (i % cs)[None, :]
    smask = []
    for sgn in (1, -1):
        earlier = (s - t) * sgn < 0
        rows = [earlier, (s - t) * sgn <= 0, earlier & (t // RW_BASE == s // RW_BASE)]
        for k in range(RW_LEVELS):
            size = RW_BASE << k
            rows.append(earlier & (t // (2 * size) == s // (2 * size)) & (t // size != s // size))
        smask.append(np.stack(rows))
    eye = (s == t).astype(np.float32)
    tt = np.arange(cs)
    tri = np.stack([tt[None, :] <= tt[:, None], tt[None, :] >= tt[:, None]]).astype(np.float32)
    return same, np.stack(smask).astype(np.float32), eye, tri


def _rw_scan_kernel(rf_ref, rb_ref, vf_ref, vb_ref, kkf_ref, kkb_ref, lwf_ref, lwb_ref, kdf_ref, kdb_ref,
                    bf_ref, bb_ref, s0_ref, same_ref, smask_ref, eye_ref, tri_ref,
                    yf_ref, yb_ref, sfin_ref, s_ref, *, n_chunks, n_batch):
    cs = RW_CHUNK

    @pl.when(pl.program_id(1) == 0)
    def _():
        s_ref[...] = s0_ref[...]

    same_head = same_ref[...].astype(BF16)
    eye = eye_ref[...]
    dirs = ((rf_ref, vf_ref, kkf_ref, lwf_ref, kdf_ref, bf_ref, yf_ref),
            (rb_ref, vb_ref, kkb_ref, lwb_ref, kdb_ref, bb_ref, yb_ref))

    def bd(x):
        xb = x.astype(BF16)
        return jnp.concatenate([xb, xb, xb, xb], axis=0) * same_head

    def halves(t, size, d):
        lo = [t[g:g + size] for g in range(0, t.shape[0], 2 * size)]
        hi = [t[g + size:g + 2 * size] for g in range(0, t.shape[0], 2 * size)]
        return (lo, hi) if d == 0 else (hi, lo)

    def join(earlier, later, d):
        pairs = zip(earlier, later) if d == 0 else zip(later, earlier)
        return jnp.concatenate([x for pr in pairs for x in pr], axis=0)

    streams = [(bi, d) for bi in range(n_batch) for d in range(2)]

    def each(f, *cols):
        return [f(*a) for a in zip(*cols)] if cols else [f(bi, d) for bi, d in streams]

    bf = lambda xs: [x.astype(BF16) for x in xs]
    smask = lambda k: [smask_ref[d, k] for _, d in streams]

    def body(c, carry):
        def load(bi, d):
            cc = c if d == 0 else n_chunks - 1 - c
            sl = pl.ds(pl.multiple_of(cc * cs, cs), cs)
            return [ref[bi, sl, :] for ref in dirs[d][:6]] + [sl]

        r, v, kk, lw, kd, b, sl = zip(*each(load))
        strict, incl = smask(0), smask(1)
        cl = [functools.reduce(jnp.add, [_dot(tri_ref[d], p) for p in _pieces(x, 3)])
              for (_, d), x in zip(streams, lw)]
        w_inv = [jnp.exp(-x) for x in cl]
        w_all = [jnp.exp(jnp.sum(x, axis=0, keepdims=True)) for x in lw]
        kt = each(lambda a, w: a * w, kd, w_inv)
        bt = each(lambda a, w: a * w, b, w_inv)
        qk = each(lambda a, x, y: a * jnp.exp(x - y), kk, cl, lw)
        rt = bf(each(lambda a, x: a * jnp.exp(x), r, cl))
        qk_b, kt_s, bt_s = bf(qk), each(bd, kt), each(bd, bt)
        a_kk = bf(each(lambda q, k_, m: _dot_nt(q, k_) * m, qk_b, kt_s, strict))
        a_kb = each(lambda q, k_, m: _dot_nt(q, k_) * m, qk_b, bt_s, strict)
        a_rk = bf(each(lambda q, k_, m: _dot_nt(q, k_) * m, rt, kt_s, incl))
        a_rb = bf(each(lambda q, k_, m: _dot_nt(q, k_) * m, rt, bt_s, incl))
        l0 = each(lambda a, m: a * m, a_kb, smask(2))
        p0 = [eye - a for a in l0]
        sq = each(lambda a, a_s: _dot(a, a_s), bf(l0), each(bd, l0))
        t_inv = each(lambda p, pb, s_: p + _dot(pb, s_), p0, bf(p0), each(bd, sq))
        for k in range(RW_LEVELS):
            half = RW_BASE << k
            cm = each(lambda a, m: bd(a * m), a_kb, smask(3 + k))
            t_s = each(bd, t_inv)
            if half % 8:
                tc = bf(each(_dot, bf(t_inv), cm))
                t_inv = each(lambda t, x, y: t - _dot(x, y), t_inv, tc, t_s)
                continue
            parts = [halves(t, half, d) for (_, d), t in zip(streams, t_inv)]
            later = bf([jnp.concatenate(p[1], axis=0) for p in parts])
            tc = bf(each(_dot, later, cm))
            upd = each(_dot, tc, t_s)
            t_inv = [join(p[0], [x - u[i * half:(i + 1) * half] for i, x in enumerate(p[1])], d)
                     for (_, d), p, u in zip(streams, parts, upd)]
        t_b = bf(t_inv)
        v_s = each(bd, v)
        akv = each(_dot, a_kk, v_s)
        x1 = bf(each(_dot, t_b, each(bd, qk)))
        x2 = each(_dot, t_b, each(bd, akv))
        y_v = each(_dot, a_rk, v_s)
        bh_s = each(lambda a, w: bd(a * w), bt, w_all)
        kh_s = each(lambda a, w: bd(a * w), kt, w_all)
        s_v = each(_dot_tn, v_s, kh_s)
        s0f = [s_ref[bi, d] for bi, d in streams]
        s0 = bf(s0f)
        u = each(lambda x, s_, y: _dot_nt(x, s_) + y, x1, s0, x2)
        u_s = each(bd, u)
        y = each(lambda q, s_, yv, ab, us: _dot_nt(q, s_) + yv - _dot(ab, us), rt, s0, y_v, a_rb, u_s)
        s1 = each(lambda s_, w, sv, us, bh: s_ * w + sv - _dot_tn(us, bh), s0f, w_all, s_v, u_s, bh_s)
        for (bi, d), y_, s_, rows in zip(streams, y, s1, sl):
            dirs[d][6][bi, rows, :] = y_
            s_ref[bi, d] = s_
        return carry

    lax.fori_loop(0, n_chunks, body, 0)

    @pl.when(pl.program_id(1) == pl.num_programs(1) - 1)
    def _():
        sfin_ref[...] = s_ref[...]


def _rw_scan(r, v, kk, lw, kd, bdir, s0):
    b, length, c = r.shape
    tc = min(RW_BLOCK, length)
    nblk = length // tc
    nb = RW_SEQS
    same, smask, eye, tri = _rw_masks()
    consts = (jnp.asarray(same), jnp.asarray(smask), jnp.asarray(eye), jnp.asarray(tri, BF16))
    fwd = lambda col: pl.BlockSpec((nb, tc, c), lambda g, i: (g, i, col))
    bwd = lambda col: pl.BlockSpec((nb, tc, c), lambda g, i: (g, nblk - 1 - i, col))
    state = pl.BlockSpec((nb, 2, c, c), lambda g, i: (g, 0, 0, 0))
    return pl.pallas_call(
        functools.partial(_rw_scan_kernel, n_chunks=tc // RW_CHUNK, n_batch=nb),
        grid=(b // nb, nblk),
        in_specs=[fwd(0), bwd(0)] * 3 + [fwd(0), bwd(1)] * 3
                 + [state] + [_const_spec(a.shape) for a in consts],
        out_specs=[fwd(0), bwd(0), state],
        out_shape=[jax.ShapeDtypeStruct((b, length, c), F32), jax.ShapeDtypeStruct((b, length, c), F32),
                   jax.ShapeDtypeStruct((b, 2, c, c), F32)],
        scratch_shapes=[pltpu.VMEM((nb, 2, c, c), F32)],
        compiler_params=_cparams(("arbitrary", "arbitrary")),
        name="rwkv_scan",
    )(r, r, v, v, kk, kk, lw, lw, kd, kd, bdir, bdir, s0, *consts)


def _blockdiag2(w):
    k, n = w.shape[1:]
    z = jnp.zeros((k, n), w.dtype)
    return jnp.concatenate([jnp.concatenate([w[0], z], axis=1), jnp.concatenate([z, w[1]], axis=1)], axis=0)


def _mla_spread(w, lanes):
    k = w.shape[0]
    w = w.reshape(k, -1, len(lanes))
    return jnp.zeros((k, w.shape[1], LANES), w.dtype).at[:, :, lanes].set(w).reshape(k, -1)


def kernel(x, c, ctx, c_ctx, ada_w, ada_b, norm_ffn1, norm_mix, norm_ffn2, ffn1_gate, ffn1_up, ffn1_down, ffn2_gate, ffn2_up, ffn2_down, w_in, w_out, hy_conv_w, hy_conv_b, hy_f_w1, hy_f_b1, hy_f_w2, hy_f_b2, hy_f_w3, hy_f_b3, hy_f_w4, hy_f_freq, hy_bias, gqa_q_norm, gqa_k_norm, mla_cq_norm, mla_ckv_norm, mla_w_uq, mla_w_ukv, mla_q_norm, mla_k_norm, rw_mu, rw_w0, rw_w2, rw_a0, rw_a2, rw_g2, rw_k_k, rw_k_a, rw_r_k, rw_ln_w, rw_ln_b):
    b, length, d = x.shape
    lc = ctx.shape[1]
    depth = ada_w.shape[0]

    c_all = jnp.zeros((16, d), F32).at[:b].set(c).at[b].set(c_ctx)
    mod = _compute_mod(c_all, ada_w, ada_b)

    xc = ctx.reshape(1, b * lc, d)
    for l in range(depth):
        ctx_out = l < depth - 1
        mod_x = mod[l, :b].reshape(b, N_MOD, d)
        mod_c = mod[l, b:b + 1].reshape(1, N_MOD, d)
        row = lambda a: a[l].reshape(1, -1)

        wg1, wu1, wd1 = (w[l].astype(BF16) for w in (ffn1_gate, ffn1_up, ffn1_down))
        wg2, wu2, wd2 = (w[l].astype(BF16) for w in (ffn2_gate, ffn2_up, ffn2_down))
        wi = w_in[l]
        o1, o2, o3 = HY_COLS, HY_COLS + GQA_COLS, HY_COLS + GQA_COLS + MLA_COLS
        ml_lanes = _mla_lanes()
        o_kr = o3 - MLA_ROPE
        wi = jnp.concatenate([wi[:, :o_kr], _mla_spread(wi[:, o_kr:o3], ml_lanes[MLA_NOPE:]), wi[:, o3:]],
                             axis=1).astype(BF16)
        wo = w_out[l].astype(BF16)

        x = _ffn(x, mod_x, row(norm_ffn1), wg1, wu1, wd1, 0)
        xc = _ffn(xc, mod_c, row(norm_ffn1), wg1, wu1, wd1, 0)

        hy_x, gq_x, ml_x, rw_x = _inproj(x, mod_x, row(norm_mix), wi)
        hy_c, gq_c, ml_c, rw_c = (t.reshape(b, lc, -1) for t in _inproj(xc, mod_c, row(norm_mix), wi))

        w1p = jnp.zeros((LANES, HY_ORDER), F32).at[:HY_EMB].set(hy_f_w1[l])
        filt = (w1p, row(hy_f_b1), hy_f_w2[l], row(hy_f_b2), hy_f_w3[l], row(hy_f_b3), hy_f_w4[l],
                row(hy_f_freq))
        spec_x = _hy_spectrum(_hy_filters(length, *filt), row(hy_bias))
        y_hy_x = _hy_conv(hy_x, hy_conv_w[l], hy_conv_b[l], spec_x)

        gq = jnp.tile(row(gqa_q_norm), (1, 2))
        gk = jnp.tile(row(gqa_k_norm), (1, 2))
        q_l, k_l, v_l = _gqa_prep(gq_x, gq, gk, True)
        q_c, k_c, v_c = _gqa_prep(gq_c, gq, gk, False)
        y_gq_x = _attention(q_l, [(k_c, v_c), (k_l, v_l)], GQA_KV_HEADS)

        wuq = _mla_spread(mla_w_uq[l], ml_lanes).astype(BF16)
        wukv = mla_w_ukv[l].reshape(MLA_KV_RANK, MLA_HEADS, MLA_NOPE + MLA_V)
        wuk = _mla_spread(wukv[:, :, :MLA_NOPE].reshape(MLA_KV_RANK, -1), ml_lanes[:MLA_NOPE]).astype(BF16)
        wuv = wukv[:, :, MLA_NOPE:].reshape(MLA_KV_RANK, -1).astype(BF16)
        pad_n = lambda g: _mla_spread(g[l].reshape(1, MLA_QK), ml_lanes)
        mla_w = (row(mla_cq_norm), row(mla_ckv_norm), wuq, wuk, wuv, pad_n(mla_q_norm), pad_n(mla_k_norm))
        mq_l, mk_l, mv_l = _mla_prep(ml_x, *mla_w, True)
        mq_c, mk_c, mv_c = _mla_prep(ml_c, *mla_w, False)
        y_ml_x = _attention(mq_l, [(mk_c, mv_c), (mk_l, mv_l)], MLA_HEADS)

        rw_w = (row(rw_mu), row(rw_k_k), row(rw_k_a), rw_r_k[l].reshape(1, RW_C),
                rw_w0[l].reshape(1, 2 * RW_C), _blockdiag2(rw_w2[l]).astype(BF16),
                rw_a0[l].reshape(1, 2 * RW_C), _blockdiag2(rw_a2[l]).astype(BF16), rw_g2[l].astype(BF16))
        pc = _rw_prep(rw_c, *rw_w)
        px = _rw_prep(rw_x, *rw_w)
        zeros = jnp.zeros((b, 2, RW_C, RW_C), F32)
        yf_c, yb_c, s_ctx = _rw_scan(*pc[:6], zeros)
        yf_x, yb_x, _ = _rw_scan(*px[:6], s_ctx)

        tail = (row(rw_ln_w), row(rw_ln_b), wo, row(norm_ffn2), wg2, wu2, wd2)
        x = _mix_ffn(x, mod_x, (y_hy_x, y_gq_x, y_ml_x), (yf_x, yb_x, px[6], px[7]), *tail)
        if ctx_out:
            spec_c = _hy_spectrum(_hy_filters(lc, *filt), row(hy_bias))
            y_hy_c = _hy_conv(hy_c, hy_conv_w[l], hy_conv_b[l], spec_c)
            y_gq_c = _attention(q_c, [(k_c, v_c)], GQA_KV_HEADS)
            y_ml_c = _attention(mq_c, [(mk_c, mv_c)], MLA_HEADS)
            flat = lambda ts: tuple(t.reshape(1, b * lc, -1) for t in ts)
            xc = _mix_ffn(xc, mod_c, flat((y_hy_c, y_gq_c, y_ml_c)), flat((yf_c, yb_c, pc[6], pc[7])), *tail)
    return x
```
